```python
import jax, jax.numpy as jnp
from jax import lax
import numpy as np

D_MODEL = 1024
BATCH = 8
SEQ = 4096
DEPTH = 1

HEAD_DIM = 64
MIX_WIDTH = D_MODEL
ATTN_WIDTH = MIX_WIDTH // 2
SGU_WIDTH = MIX_WIDTH - ATTN_WIDTH
N_Q_HEADS = ATTN_WIDTH // HEAD_DIM
N_KV_HEADS = 2
Q_PER_KV = N_Q_HEADS // N_KV_HEADS
KV_WIDTH = N_KV_HEADS * HEAD_DIM
N_SGU_HEADS = 8
SGU_HEAD_DIM = SGU_WIDTH // N_SGU_HEADS
WINDOW = 128
BLOCK = 128
CHUNK = 128
NORM_EPS = 1e-5
NEG_INF = -1e30
SPLIT_SIZES = (ATTN_WIDTH, KV_WIDTH, KV_WIDTH, ATTN_WIDTH, SGU_WIDTH, SGU_WIDTH, SGU_WIDTH)
IN_WIDTH = sum(SPLIT_SIZES)

kernel_name = "hybrid_swa_sink_gmlp_parallel_heads"


def rmsnorm(x, g):
    xf = x.astype(jnp.float32)
    y = xf * lax.rsqrt(jnp.mean(xf * xf, axis=-1, keepdims=True) + NORM_EPS)
    return (y * g.astype(jnp.float32)).astype(x.dtype)


def layernorm(x, g, b):
    xf = x.astype(jnp.float32)
    mu = jnp.mean(xf, axis=-1, keepdims=True)
    xc = xf - mu
    y = xc * lax.rsqrt(jnp.mean(xc * xc, axis=-1, keepdims=True) + NORM_EPS)
    return (y * g.astype(jnp.float32) + b.astype(jnp.float32)).astype(x.dtype)


def banded_sink_attention(q, k, v, sinks):
    B, S = q.shape[0], q.shape[1]
    nb = S // BLOCK
    qb = q.reshape(B, nb, BLOCK, N_KV_HEADS, Q_PER_KV, HEAD_DIM)

    def band(t):
        tb = t.reshape(B, nb, BLOCK, N_KV_HEADS, HEAD_DIM)
        prev = jnp.pad(tb, ((0, 0), (1, 0), (0, 0), (0, 0), (0, 0)))[:, :-1]
        return jnp.concatenate([prev, tb], axis=2)

    kb, vb = band(k), band(v)
    scale = HEAD_DIM ** -0.5
    scores = jnp.einsum('bnqhgd,bnkhd->bnhgqk', qb, kb).astype(jnp.float32) * scale
    qi = jnp.arange(BLOCK)[:, None] + BLOCK
    kj = jnp.arange(2 * BLOCK)[None, :]
    diff = qi - kj
    in_window = (diff >= 0) & (diff < WINDOW)
    key_pos = jnp.arange(nb)[:, None, None] * BLOCK - BLOCK + kj[None]
    valid = in_window[None] & (key_pos >= 0)
    scores = jnp.where(valid[None, :, None, None], scores, NEG_INF)
    sink = sinks.astype(jnp.float32).reshape(N_KV_HEADS, Q_PER_KV)[None, None, :, :, None, None]
    m = jnp.maximum(jnp.max(scores, axis=-1, keepdims=True), sink)
    p = jnp.exp(scores - m)
    probs = p / (jnp.sum(p, axis=-1, keepdims=True) + jnp.exp(sink - m))
    out = jnp.einsum('bnhgqk,bnkhd->bnqhgd', probs.astype(vb.dtype), vb)
    return out.reshape(B, S, ATTN_WIDTH)


def chunked_spatial_gating(u, v, w_s, b_s, ln_g, ln_b):
    B, S = u.shape[0], u.shape[1]
    nc = S // CHUNK
    v = layernorm(v, ln_g, ln_b)
    vc = v.reshape(B, nc, CHUNK, N_SGU_HEADS, SGU_HEAD_DIM)
    causal = jnp.tril(jnp.ones((CHUNK, CHUNK), dtype=bool))
    w = jnp.where(causal[None], w_s, jnp.zeros_like(w_s)).astype(vc.dtype)
    mixed = jnp.einsum('hts,bcshd->bcthd', w, vc) + b_s.T.astype(vc.dtype)[None, None, :, :, None]
    return u * mixed.reshape(B, S, SGU_WIDTH)


def setup_inputs(seed: int = 0) -> dict:
    key = jax.random.key(seed)
    ks = jax.random.split(key, 12)
    f32 = jnp.float32
    x = jax.random.normal(ks[0], (BATCH, SEQ, D_MODEL), f32)
    norm_g = 1.0 + 0.02 * jax.random.normal(ks[1], (DEPTH, D_MODEL), f32)
    w_in = jax.random.normal(ks[2], (DEPTH, D_MODEL, IN_WIDTH), f32) * D_MODEL ** -0.5
    b_in = 0.02 * jax.random.normal(ks[3], (DEPTH, IN_WIDTH), f32)
    attn_sinks = 0.5 * jax.random.normal(ks[4], (DEPTH, N_Q_HEADS), f32)
    sgu_ln_g = 1.0 + 0.02 * jax.random.normal(ks[5], (DEPTH, SGU_WIDTH), f32)
    sgu_ln_b = 0.02 * jax.random.normal(ks[6], (DEPTH, SGU_WIDTH), f32)
    sgu_w = jax.random.normal(ks[7], (DEPTH, N_SGU_HEADS, CHUNK, CHUNK), f32) * CHUNK ** -0.5
    sgu_b = 1.0 + 0.02 * jax.random.normal(ks[8], (DEPTH, N_SGU_HEADS, CHUNK), f32)
    w_out = jax.random.normal(ks[9], (DEPTH, MIX_WIDTH, D_MODEL), f32) * MIX_WIDTH ** -0.5
    b_out = 0.02 * jax.random.normal(ks[10], (DEPTH, D_MODEL), f32)
    final_norm_g = 1.0 + 0.02 * jax.random.normal(ks[11], (D_MODEL,), f32)
    return {"x": x, "norm_g": norm_g, "w_in": w_in, "b_in": b_in,
            "attn_sinks": attn_sinks, "sgu_ln_g": sgu_ln_g, "sgu_ln_b": sgu_ln_b,
            "sgu_w": sgu_w, "sgu_b": sgu_b, "w_out": w_out, "b_out": b_out,
            "final_norm_g": final_norm_g}


def reference(x, norm_g, w_in, b_in, attn_sinks, sgu_ln_g, sgu_ln_b, sgu_w, sgu_b,
              w_out, b_out, final_norm_g):
    split_points = list(np.cumsum(SPLIT_SIZES)[:-1])
    for l in range(DEPTH):
        h = rmsnorm(x, norm_g[l])
        proj = jnp.einsum('bsd,de->bse', h, w_in[l]) + b_in[l]
        q, k, v, z_a, u_s, v_s, z_s = jnp.split(proj, split_points, axis=-1)
        attn = banded_sink_attention(q, k, v, attn_sinks[l]) * jax.nn.silu(z_a)
        u_s = jax.nn.gelu(u_s, approximate=False)
        v_s = jax.nn.gelu(v_s, approximate=False)
        sgu = chunked_spatial_gating(u_s, v_s, sgu_w[l], sgu_b[l], sgu_ln_g[l], sgu_ln_b[l]) * jax.nn.silu(z_s)
        mixed = jnp.concatenate([attn, sgu], axis=-1)
        x = x + jnp.einsum('bse,ed->bsd', mixed, w_out[l]) + b_out[l]
    return rmsnorm(x, final_norm_g)
```

```python
import functools

import jax
import jax.numpy as jnp
from jax import lax
from jax.experimental import pallas as pl
from jax.experimental.pallas import tpu as pltpu

D_MODEL = 1024
HEAD_DIM = 64
ATTN_WIDTH = 512
KV_WIDTH = 128
SGU_WIDTH = 512
N_KV_HEADS = 2
Q_PER_KV = 4
N_SGU_HEADS = 8
BLOCK = 128
NORM_EPS = 1e-5
NEG_INF = -1e30
ATTN_SCALE = HEAD_DIM ** -0.5

OFF_Q = 0
OFF_K = OFF_Q + ATTN_WIDTH
OFF_V = OFF_K + KV_WIDTH
OFF_ZA = OFF_V + KV_WIDTH
OFF_U = OFF_ZA + ATTN_WIDTH
OFF_VS = OFF_U + SGU_WIDTH
OFF_ZS = OFF_VS + SGU_WIDTH
IN_WIDTH = OFF_ZS + SGU_WIDTH

SEQ_TILE = 512
V7X_VMEM_LIMIT_BYTES = 56 * 1024 * 1024

_SQRT_HALF = 0.7071067811865476


def _silu(z):
    return z * (1.0 / (1.0 + jnp.exp(-z)))


def _gelu_exact(z):
    return 0.5 * z * (1.0 + lax.erf(z * _SQRT_HALF))


def _layer_kernel(sinks_ref, x_ref, ng_ref, win_ref, bin_ref, lng_ref, lnb_ref,
                  sw_ref, sbt_ref, wout_ref, bout_ref, fg_ref, o_ref,
                  q_s, k_s, v_s, ga_s, u_s, vl_s, gs_s, mix_s, wp_s, *, seq_tile):
    f32, bf16 = jnp.float32, jnp.bfloat16
    n_sub = seq_tile // BLOCK
    b_idx = pl.program_id(0)
    s_idx = pl.program_id(1)

    @pl.when((b_idx == 0) & (s_idx == 0))
    def _():
        row = lax.broadcasted_iota(jnp.int32, (BLOCK, BLOCK), 0)
        col = lax.broadcasted_iota(jnp.int32, (BLOCK, BLOCK), 1)
        for p in range(N_SGU_HEADS // 2):
            for half in range(2):
                w = jnp.where(row >= col, sw_ref[2 * p + half], 0.0)
                wp_s[p, :, half * BLOCK:(half + 1) * BLOCK] = w.astype(bf16)

    @pl.when(s_idx == 0)
    def _():
        k_s[0:BLOCK, :] = jnp.zeros((BLOCK, KV_WIDTH), bf16)
        v_s[0:BLOCK, :] = jnp.zeros((BLOCK, KV_WIDTH), bf16)

    @pl.when(s_idx > 0)
    def _():
        k_s[0:BLOCK, :] = k_s[seq_tile:seq_tile + BLOCK, :]
        v_s[0:BLOCK, :] = v_s[seq_tile:seq_tile + BLOCK, :]

    x = x_ref[0]
    ms = jnp.mean(x * x, axis=-1, keepdims=True)
    h = ((x * lax.rsqrt(ms + NORM_EPS)) * ng_ref[...]).astype(bf16)

    def proj(off, width):
        return (jnp.dot(h, win_ref[:, off:off + width], preferred_element_type=f32)
                + bin_ref[:, off:off + width])

    q_s[...] = (proj(OFF_Q, ATTN_WIDTH) * ATTN_SCALE).astype(bf16)
    k_s[BLOCK:BLOCK + seq_tile, :] = proj(OFF_K, KV_WIDTH).astype(bf16)
    v_s[BLOCK:BLOCK + seq_tile, :] = proj(OFF_V, KV_WIDTH).astype(bf16)
    ga_s[...] = _silu(proj(OFF_ZA, ATTN_WIDTH))
    u_s[...] = _gelu_exact(proj(OFF_U, SGU_WIDTH))
    gs_s[...] = _silu(proj(OFF_ZS, SGU_WIDTH))

    vg = _gelu_exact(proj(OFF_VS, SGU_WIDTH))
    mu = jnp.mean(vg, axis=-1, keepdims=True)
    vc = vg - mu
    var = jnp.mean(vc * vc, axis=-1, keepdims=True)
    vl_s[...] = ((vc * lax.rsqrt(var + NORM_EPS)) * lng_ref[...] + lnb_ref[...]).astype(bf16)

    qi = lax.broadcasted_iota(jnp.int32, (BLOCK, 2 * BLOCK), 0) + BLOCK
    kj = lax.broadcasted_iota(jnp.int32, (BLOCK, 2 * BLOCK), 1)
    diff = qi - kj
    in_window = (diff >= 0) & (diff < BLOCK)
    lane = lax.broadcasted_iota(jnp.int32, (BLOCK, BLOCK), 1)
    low_half = lane < HEAD_DIM

    sbt = sbt_ref[...]
    sgu_bias = jnp.concatenate(
        [jnp.broadcast_to(sbt[:, hh:hh + 1], (BLOCK, HEAD_DIM)) for hh in range(N_SGU_HEADS)], axis=1)

    for j in range(n_sub):
        rows = slice(j * BLOCK, (j + 1) * BLOCK)
        kb = k_s[j * BLOCK:(j + 2) * BLOCK, :]
        vb = v_s[j * BLOCK:(j + 2) * BLOCK, :]
        if j == 0:
            valid = in_window & ((kj >= BLOCK) | (s_idx > 0))
        else:
            valid = in_window
        valid4 = jnp.concatenate([valid] * Q_PER_KV, axis=0)

        head_out = []
        for hh in range(N_KV_HEADS):
            kh = kb[:, hh * HEAD_DIM:(hh + 1) * HEAD_DIM]
            vh = vb[:, hh * HEAD_DIM:(hh + 1) * HEAD_DIM]
            qh = q_s[rows, hh * Q_PER_KV * HEAD_DIM:(hh + 1) * Q_PER_KV * HEAD_DIM]
            qst = jnp.concatenate(
                [qh[:, g * HEAD_DIM:(g + 1) * HEAD_DIM] for g in range(Q_PER_KV)], axis=0)
            sc = lax.dot_general(qst, kh, (((1,), (1,)), ((), ())),
                                 preferred_element_type=f32)
            sc = jnp.where(valid4, sc, NEG_INF)
            sink = jnp.concatenate(
                [jnp.full((BLOCK, 1), sinks_ref[hh * Q_PER_KV + g], f32) for g in range(Q_PER_KV)],
                axis=0)
            m = jnp.maximum(jnp.max(sc, axis=-1, keepdims=True), sink)
            p = jnp.exp(sc - m)
            denom = jnp.sum(p, axis=-1, keepdims=True) + jnp.exp(sink - m)
            probs = (p * (1.0 / denom)).astype(bf16)
            o = jnp.dot(probs, vh, preferred_element_type=f32)
            head_out.append(jnp.concatenate(
                [o[g * BLOCK:(g + 1) * BLOCK, :] for g in range(Q_PER_KV)], axis=1))
        attn = jnp.concatenate(head_out, axis=1)
        mix_s[rows, 0:ATTN_WIDTH] = (attn * ga_s[rows, :]).astype(bf16)

        mixed = []
        for p_idx in range(N_SGU_HEADS // 2):
            vp = vl_s[rows, p_idx * BLOCK:(p_idx + 1) * BLOCK]
            zero = jnp.zeros_like(vp)
            rhs = jnp.concatenate([jnp.where(low_half, vp, zero),
                                   jnp.where(low_half, zero, vp)], axis=0)
            mixed.append(jnp.dot(wp_s[p_idx], rhs, preferred_element_type=f32))
        mixed = jnp.concatenate(mixed, axis=1) + sgu_bias
        mix_s[rows, ATTN_WIDTH:] = ((u_s[rows, :] * mixed) * gs_s[rows, :]).astype(bf16)

    y = x_ref[0] + jnp.dot(mix_s[...], wout_ref[...], preferred_element_type=f32) + bout_ref[...]
    ms2 = jnp.mean(y * y, axis=-1, keepdims=True)
    o_ref[0] = (y * lax.rsqrt(ms2 + NORM_EPS)) * fg_ref[...]


def _layer_call(x, sinks, norm_g, w_in, b_in, ln_g, ln_b, sgu_w, sgu_bt, w_out, b_out, final_g,
                *, seq_tile=SEQ_TILE):
    batch, seq, d_model = x.shape
    assert d_model == D_MODEL and seq % seq_tile == 0 and seq_tile % BLOCK == 0
    f32, bf16 = jnp.float32, jnp.bfloat16

    def full(shape):
        return pl.BlockSpec(shape, lambda b, s: (0,) * len(shape))

    tile_spec = pl.BlockSpec((1, seq_tile, D_MODEL), lambda b, s: (b, s, 0))
    return pl.pallas_call(
        functools.partial(_layer_kernel, seq_tile=seq_tile),
        grid=(batch, seq // seq_tile),
        in_specs=[
            pl.BlockSpec(memory_space=pltpu.SMEM),
            tile_spec,
            full((1, D_MODEL)),
            full((D_MODEL, IN_WIDTH)),
            full((1, IN_WIDTH)),
            full((1, SGU_WIDTH)),
            full((1, SGU_WIDTH)),
            full((N_SGU_HEADS, BLOCK, BLOCK)),
            full((BLOCK, N_SGU_HEADS)),
            full((D_MODEL, D_MODEL)),
            full((1, D_MODEL)),
            full((1, D_MODEL)),
        ],
        out_specs=tile_spec,
        out_shape=jax.ShapeDtypeStruct(x.shape, x.dtype),
        scratch_shapes=[
            pltpu.VMEM((seq_tile, ATTN_WIDTH), bf16),
            pltpu.VMEM((BLOCK + seq_tile, KV_WIDTH), bf16),
            pltpu.VMEM((BLOCK + seq_tile, KV_WIDTH), bf16),
            pltpu.VMEM((seq_tile, ATTN_WIDTH), f32),
            pltpu.VMEM((seq_tile, SGU_WIDTH), f32),
            pltpu.VMEM((seq_tile, SGU_WIDTH), bf16),
            pltpu.VMEM((seq_tile, SGU_WIDTH), f32),
            pltpu.VMEM((seq_tile, D_MODEL), bf16),
            pltpu.VMEM((N_SGU_HEADS // 2, BLOCK, 2 * BLOCK), bf16),
        ],
        compiler_params=pltpu.CompilerParams(
            dimension_semantics=("arbitrary", "arbitrary"),
            vmem_limit_bytes=V7X_VMEM_LIMIT_BYTES),
        name="hybrid_layer",
    )(sinks, x, norm_g, w_in, b_in, ln_g, ln_b, sgu_w, sgu_bt, w_out, b_out, final_g)


def kernel(x, norm_g, w_in, b_in, attn_sinks, sgu_ln_g, sgu_ln_b, sgu_w, sgu_b, w_out, b_out, final_norm_g):
    depth = norm_g.shape[0]
    bf16 = jnp.bfloat16
    for l in range(depth):
        last = l == depth - 1
        assert last, "the fused call applies the final norm; only depth 1 is supported"
        x = _layer_call(
            x, attn_sinks[l], norm_g[l][None, :], w_in[l].astype(bf16), b_in[l][None, :],
            sgu_ln_g[l][None, :], sgu_ln_b[l][None, :], sgu_w[l], sgu_b[l].T,
            w_out[l].astype(bf16), b_out[l][None, :], final_norm_g[None, :])
    return x
```

```python
import functools

import jax
import jax.numpy as jnp
from jax import lax
from jax.experimental import pallas as pl
from jax.experimental.pallas import tpu as pltpu

D_MODEL = 1024
HEAD_DIM = 64
ATTN_WIDTH = 512
KV_WIDTH = 128
SGU_WIDTH = 512
N_KV_HEADS = 2
Q_PER_KV = 4
N_SGU_HEADS = 8
BLOCK = 128
NORM_EPS = 1e-5
NEG_INF = -1e30
ATTN_SCALE = HEAD_DIM ** -0.5

OFF_Q = 0
OFF_K = OFF_Q + ATTN_WIDTH
OFF_V = OFF_K + KV_WIDTH
OFF_ZA = OFF_V + KV_WIDTH
OFF_U = OFF_ZA + ATTN_WIDTH
OFF_VS = OFF_U + SGU_WIDTH
OFF_ZS = OFF_VS + SGU_WIDTH
IN_WIDTH = OFF_ZS + SGU_WIDTH

SEQ_TILE = 512
V7X_VMEM_LIMIT_BYTES = 56 * 1024 * 1024

_SQRT_HALF = 0.7071067811865476


def _silu(z):
    return z * (1.0 / (1.0 + jnp.exp(-z)))


def _gelu_exact(z):
    return 0.5 * z * (1.0 + lax.erf(z * _SQRT_HALF))


def _layer_kernel(sinks_ref, x_ref, ng_ref, win_ref, bin_ref, lng_ref, lnb_ref,
                  sw_ref, sbt_ref, wout_ref, bout_ref, fg_ref, o_ref,
                  q_s, kd_s, vd_s, ga_s, u_s, vl_s, gs_s, mix_s, wp_s, sb_s, *, seq_tile):
    f32, bf16 = jnp.float32, jnp.bfloat16
    n_sub = seq_tile // BLOCK
    b_idx = pl.program_id(0)
    s_idx = pl.program_id(1)

    lane = lax.broadcasted_iota(jnp.int32, (BLOCK, BLOCK), 1)
    low_half = lane < HEAD_DIM

    @pl.when((b_idx == 0) & (s_idx == 0))
    def _():
        row = lax.broadcasted_iota(jnp.int32, (BLOCK, BLOCK), 0)
        for p in range(N_SGU_HEADS // 2):
            for half in range(2):
                w = jnp.where(row >= lane, sw_ref[2 * p + half], 0.0)
                wp_s[p, :, half * BLOCK:(half + 1) * BLOCK] = w.astype(bf16)
        sbt = sbt_ref[...]
        for hh in range(N_SGU_HEADS):
            sb_s[:, hh * HEAD_DIM:(hh + 1) * HEAD_DIM] = jnp.broadcast_to(
                sbt[:, hh:hh + 1], (BLOCK, HEAD_DIM))

    @pl.when(s_idx == 0)
    def _():
        for hh in range(N_KV_HEADS):
            kd_s[hh, 0:BLOCK, :] = jnp.zeros((BLOCK, KV_WIDTH), bf16)
            vd_s[hh, 0:BLOCK, :] = jnp.zeros((BLOCK, KV_WIDTH), bf16)

    @pl.when(s_idx > 0)
    def _():
        for hh in range(N_KV_HEADS):
            kd_s[hh, 0:BLOCK, :] = kd_s[hh, seq_tile:seq_tile + BLOCK, :]
            vd_s[hh, 0:BLOCK, :] = vd_s[hh, seq_tile:seq_tile + BLOCK, :]

    x = x_ref[0]
    ms = jnp.mean(x * x, axis=-1, keepdims=True)
    h = ((x * lax.rsqrt(ms + NORM_EPS)) * ng_ref[...]).astype(bf16)

    def proj(off, width):
        return (jnp.dot(h, win_ref[:, off:off + width], preferred_element_type=f32)
                + bin_ref[:, off:off + width])

    qkv = proj(OFF_Q, ATTN_WIDTH + 2 * KV_WIDTH)
    q_s[...] = (qkv[:, 0:ATTN_WIDTH] * ATTN_SCALE).astype(bf16)
    half_mask = lax.broadcasted_iota(jnp.int32, (seq_tile, KV_WIDTH), 1) < HEAD_DIM
    for dup_ref, off in ((kd_s, ATTN_WIDTH), (vd_s, ATTN_WIDTH + KV_WIDTH)):
        t = qkv[:, off:off + KV_WIDTH]
        t_sw = pltpu.roll(t, HEAD_DIM, axis=1)
        dup_ref[0, BLOCK:BLOCK + seq_tile, :] = jnp.where(half_mask, t, t_sw).astype(bf16)
        dup_ref[1, BLOCK:BLOCK + seq_tile, :] = jnp.where(half_mask, t_sw, t).astype(bf16)

    scores = []
    for j in range(n_sub):
        rows = slice(j * BLOCK, (j + 1) * BLOCK)
        for hh in range(N_KV_HEADS):
            parts = []
            for g in range(Q_PER_KV):
                c = hh * (Q_PER_KV // 2) + g // 2
                qc = q_s[rows, c * BLOCK:(c + 1) * BLOCK]
                keep = low_half if g % 2 == 0 else jnp.logical_not(low_half)
                parts.append(jnp.where(keep, qc, jnp.zeros_like(qc)))
            qst = jnp.concatenate(parts, axis=0)
            kb = kd_s[hh, j * BLOCK:(j + 2) * BLOCK, :]
            scores.append(lax.dot_general(qst, kb, (((1,), (1,)), ((), ())),
                                          preferred_element_type=f32))

    ga_s[...] = _silu(proj(OFF_ZA, ATTN_WIDTH))
    u_s[...] = _gelu_exact(proj(OFF_U, SGU_WIDTH))
    vg = _gelu_exact(proj(OFF_VS, SGU_WIDTH))
    mu = jnp.mean(vg, axis=-1, keepdims=True)
    vc = vg - mu
    var = jnp.mean(vc * vc, axis=-1, keepdims=True)
    vl_s[...] = ((vc * lax.rsqrt(var + NORM_EPS)) * lng_ref[...] + lnb_ref[...]).astype(bf16)
    gs_s[...] = _silu(proj(OFF_ZS, SGU_WIDTH))

    qi = lax.broadcasted_iota(jnp.int32, (BLOCK, 2 * BLOCK), 0) + BLOCK
    kj = lax.broadcasted_iota(jnp.int32, (BLOCK, 2 * BLOCK), 1)
    diff = qi - kj
    in_window = (diff >= 0) & (diff < BLOCK)
    first_valid = in_window & ((kj >= BLOCK) | (s_idx > 0))

    for j in range(n_sub):
        rows = slice(j * BLOCK, (j + 1) * BLOCK)
        valid = first_valid if j == 0 else in_window
        for hh in range(N_KV_HEADS):
            sc = scores[j * N_KV_HEADS + hh]
            probs = []
            for g in range(Q_PER_KV):
                sg = jnp.where(valid, sc[g * BLOCK:(g + 1) * BLOCK, :], NEG_INF)
                sink = sinks_ref[hh * Q_PER_KV + g]
                m = jnp.maximum(jnp.max(sg, axis=-1, keepdims=True), sink)
                p = jnp.exp(sg - m)
                denom = jnp.sum(p, axis=-1, keepdims=True) + jnp.exp(sink - m)
                probs.append((p * (1.0 / denom)).astype(bf16))
            probs = jnp.concatenate(probs, axis=0)
            vb = vd_s[hh, j * BLOCK:(j + 2) * BLOCK, :]
            o = jnp.dot(probs, vb, preferred_element_type=f32)
            for c2 in range(Q_PER_KV // 2):
                c = hh * (Q_PER_KV // 2) + c2
                blk = jnp.where(low_half, o[(2 * c2) * BLOCK:(2 * c2 + 1) * BLOCK, :],
                                o[(2 * c2 + 1) * BLOCK:(2 * c2 + 2) * BLOCK, :])
                cols = slice(c * BLOCK, (c + 1) * BLOCK)
                mix_s[rows, cols] = (blk * ga_s[rows, cols]).astype(bf16)

        for p_idx in range(N_SGU_HEADS // 2):
            cols = slice(p_idx * BLOCK, (p_idx + 1) * BLOCK)
            vp = vl_s[rows, cols]
            zero = jnp.zeros_like(vp)
            rhs = jnp.concatenate([jnp.where(low_half, vp, zero),
                                   jnp.where(low_half, zero, vp)], axis=0)
            mixed = jnp.dot(wp_s[p_idx], rhs, preferred_element_type=f32) + sb_s[:, cols]
            mix_s[rows, ATTN_WIDTH + p_idx * BLOCK:ATTN_WIDTH + (p_idx + 1) * BLOCK] = (
                (u_s[rows, cols] * mixed) * gs_s[rows, cols]).astype(bf16)

    y = x_ref[0] + jnp.dot(mix_s[...], wout_ref[...], preferred_element_type=f32) + bout_ref[...]
    ms2 = jnp.mean(y * y, axis=-1, keepdims=True)
    o_ref[0] = (y * lax.rsqrt(ms2 + NORM_EPS)) * fg_ref[...]


def _layer_call(x, sinks, norm_g, w_in, b_in, ln_g, ln_b, sgu_w, sgu_bt, w_out, b_out, final_g,
                *, seq_tile=SEQ_TILE):
    batch, seq, d_model = x.shape
    assert d_model == D_MODEL and seq % seq_tile == 0 and seq_tile % BLOCK == 0
    f32, bf16 = jnp.float32, jnp.bfloat16

    def full(shape):
        return pl.BlockSpec(shape, lambda b, s: (0,) * len(shape))

    tile_spec = pl.BlockSpec((1, seq_tile, D_MODEL), lambda b, s: (b, s, 0))
    return pl.pallas_call(
        functools.partial(_layer_kernel, seq_tile=seq_tile),
        grid=(batch, seq // seq_tile),
        in_specs=[
            pl.BlockSpec(memory_space=pltpu.SMEM),
            tile_spec,
            full((1, D_MODEL)),
            full((D_MODEL, IN_WIDTH)),
            full((1, IN_WIDTH)),
            full((1, SGU_WIDTH)),
            full((1, SGU_WIDTH)),
            full((N_SGU_HEADS, BLOCK, BLOCK)),
            full((BLOCK, N_SGU_HEADS)),
            full((D_MODEL, D_MODEL)),
            full((1, D_MODEL)),
            full((1, D_MODEL)),
        ],
        out_specs=tile_spec,
        out_shape=jax.ShapeDtypeStruct(x.shape, x.dtype),
        scratch_shapes=[
            pltpu.VMEM((seq_tile, ATTN_WIDTH), bf16),
            pltpu.VMEM((N_KV_HEADS, BLOCK + seq_tile, KV_WIDTH), bf16),
            pltpu.VMEM((N_KV_HEADS, BLOCK + seq_tile, KV_WIDTH), bf16),
            pltpu.VMEM((seq_tile, ATTN_WIDTH), f32),
            pltpu.VMEM((seq_tile, SGU_WIDTH), f32),
            pltpu.VMEM((seq_tile, SGU_WIDTH), bf16),
            pltpu.VMEM((seq_tile, SGU_WIDTH), f32),
            pltpu.VMEM((seq_tile, D_MODEL), bf16),
            pltpu.VMEM((N_SGU_HEADS // 2, BLOCK, 2 * BLOCK), bf16),
            pltpu.VMEM((BLOCK, SGU_WIDTH), f32),
        ],
        compiler_params=pltpu.CompilerParams(
            dimension_semantics=("arbitrary", "arbitrary"),
            vmem_limit_bytes=V7X_VMEM_LIMIT_BYTES),
        name="hybrid_layer",
    )(sinks, x, norm_g, w_in, b_in, ln_g, ln_b, sgu_w, sgu_bt, w_out, b_out, final_g)


def kernel(x, norm_g, w_in, b_in, attn_sinks, sgu_ln_g, sgu_ln_b, sgu_w, sgu_b, w_out, b_out, final_norm_g):
    depth = norm_g.shape[0]
    bf16 = jnp.bfloat16
    for l in range(depth):
        last = l == depth - 1
        assert last, "the fused call applies the final norm; only depth 1 is supported"
        x = _layer_call(
            x, attn_sinks[l], norm_g[l][None, :], w_in[l].astype(bf16), b_in[l][None, :],
            sgu_ln_g[l][None, :], sgu_ln_b[l][None, :], sgu_w[l], sgu_b[l].T,
            w_out[l].astype(bf16), b_out[l][None, :], final_norm_g[None, :])
    return x
```

```python
import functools

import jax
import jax.numpy as jnp
from jax import lax
from jax.experimental import pallas as pl
from jax.experimental.pallas import tpu as pltpu

D_MODEL = 1024
HEAD_DIM = 64
ATTN_WIDTH = 512
KV_WIDTH = 128
SGU_WIDTH = 512
N_KV_HEADS = 2
Q_PER_KV = 4
N_SGU_HEADS = 8
BLOCK = 128
NORM_EPS = 1e-5
NEG_INF = -1e30
ATTN_SCALE = HEAD_DIM ** -0.5

OFF_Q = 0
OFF_K = OFF_Q + ATTN_WIDTH
OFF_V = OFF_K + KV_WIDTH
OFF_ZA = OFF_V + KV_WIDTH
OFF_U = OFF_ZA + ATTN_WIDTH
OFF_VS = OFF_U + SGU_WIDTH
OFF_ZS = OFF_VS + SGU_WIDTH
IN_WIDTH = OFF_ZS + SGU_WIDTH

SEQ_TILE = 512
MXU_TILE = 256
V7X_VMEM_LIMIT_BYTES = 56 * 1024 * 1024

_SQRT_HALF = 0.7071067811865476


def _silu(z):
    return z * (1.0 / (1.0 + jnp.exp(-z)))


def _gelu_exact(z):
    return 0.5 * z * (1.0 + lax.erf(z * _SQRT_HALF))


def _layer_kernel(sinks_ref, x_ref, ng_ref, win_ref, bin_ref, lng_ref, lnb_ref,
                  sw_ref, sbt_ref, wout_ref, bout_ref, fg_ref, o_ref,
                  q_s, kd_s, vd_s, u_s, vl_s, gs_s, mix_s, wp_s, sb_s, *, seq_tile):
    f32, bf16 = jnp.float32, jnp.bfloat16
    n_sub = seq_tile // BLOCK
    b_idx = pl.program_id(0)
    s_idx = pl.program_id(1)

    lane = lax.broadcasted_iota(jnp.int32, (BLOCK, BLOCK), 1)
    low_half = lane < HEAD_DIM

    @pl.when((b_idx == 0) & (s_idx == 0))
    def _():
        row = lax.broadcasted_iota(jnp.int32, (BLOCK, BLOCK), 0)
        for p in range(N_SGU_HEADS // 2):
            for half in range(2):
                w = jnp.where(row >= lane, sw_ref[2 * p + half], 0.0)
                wp_s[p, :, half * BLOCK:(half + 1) * BLOCK] = w.astype(bf16)
        sbt = sbt_ref[...]
        for hh in range(N_SGU_HEADS):
            sb_s[:, hh * HEAD_DIM:(hh + 1) * HEAD_DIM] = jnp.broadcast_to(
                sbt[:, hh:hh + 1], (BLOCK, HEAD_DIM))

    @pl.when(s_idx == 0)
    def _():
        for hh in range(N_KV_HEADS):
            kd_s[hh, 0:BLOCK, :] = jnp.zeros((BLOCK, KV_WIDTH), bf16)
            vd_s[hh, 0:BLOCK, :] = jnp.zeros((BLOCK, KV_WIDTH), bf16)

    @pl.when(s_idx > 0)
    def _():
        for hh in range(N_KV_HEADS):
            kd_s[hh, 0:BLOCK, :] = kd_s[hh, seq_tile:seq_tile + BLOCK, :]
            vd_s[hh, 0:BLOCK, :] = vd_s[hh, seq_tile:seq_tile + BLOCK, :]

    x = x_ref[0]
    ms = jnp.mean(x * x, axis=-1, keepdims=True)
    h = ((x * lax.rsqrt(ms + NORM_EPS)) * ng_ref[...]).astype(bf16)
    half_t = seq_tile // 2

    def proj_tile(off):
        cols = slice(off, off + MXU_TILE)
        return jnp.concatenate(
            [jnp.dot(h[i * half_t:(i + 1) * half_t], win_ref[:, cols], preferred_element_type=f32)
             for i in range(2)], axis=0) + bin_ref[:, cols]

    for n in range(ATTN_WIDTH // MXU_TILE):
        cols = slice(n * MXU_TILE, (n + 1) * MXU_TILE)
        q_s[:, cols] = (proj_tile(OFF_Q + n * MXU_TILE) * ATTN_SCALE).astype(bf16)
    kv = proj_tile(OFF_K)
    half_mask = lax.broadcasted_iota(jnp.int32, (seq_tile, KV_WIDTH), 1) < HEAD_DIM
    for dup_ref, off in ((kd_s, 0), (vd_s, KV_WIDTH)):
        t = kv[:, off:off + KV_WIDTH]
        t_sw = pltpu.roll(t, HEAD_DIM, axis=1)
        dup_ref[0, BLOCK:BLOCK + seq_tile, :] = jnp.where(half_mask, t, t_sw).astype(bf16)
        dup_ref[1, BLOCK:BLOCK + seq_tile, :] = jnp.where(half_mask, t_sw, t).astype(bf16)

    units = [(j, hh) for j in range(n_sub) for hh in range(N_KV_HEADS)]
    scores = []
    for j, hh in units:
        rows = slice(j * BLOCK, (j + 1) * BLOCK)
        parts = []
        for g in range(Q_PER_KV):
            c = hh * (Q_PER_KV // 2) + g // 2
            qc = q_s[rows, c * BLOCK:(c + 1) * BLOCK]
            keep = low_half if g % 2 == 0 else jnp.logical_not(low_half)
            parts.append(jnp.where(keep, qc, jnp.zeros_like(qc)))
        qst = jnp.concatenate(parts, axis=0)
        kb = kd_s[hh, j * BLOCK:(j + 2) * BLOCK, :]
        scores.append(lax.dot_general(qst, kb, (((1,), (1,)), ((), ())),
                                      preferred_element_type=f32))

    qi = lax.broadcasted_iota(jnp.int32, (BLOCK, 2 * BLOCK), 0) + BLOCK
    kj = lax.broadcasted_iota(jnp.int32, (BLOCK, 2 * BLOCK), 1)
    diff = qi - kj
    in_window = (diff >= 0) & (diff < BLOCK)
    first_valid = in_window & ((kj >= BLOCK) | (s_idx > 0))

    def attend(u):
        j, hh = units[u]
        valid = first_valid if j == 0 else in_window
        sc = scores[u]
        probs = []
        for g in range(Q_PER_KV):
            sg = jnp.where(valid, sc[g * BLOCK:(g + 1) * BLOCK, :], NEG_INF)
            sink = sinks_ref[hh * Q_PER_KV + g]
            m = jnp.maximum(jnp.max(sg, axis=-1, keepdims=True), sink)
            p = jnp.exp(sg - m)
            denom = jnp.sum(p, axis=-1, keepdims=True) + jnp.exp(sink - m)
            probs.append((p * (1.0 / denom)).astype(bf16))
        probs = jnp.concatenate(probs, axis=0)
        vb = vd_s[hh, j * BLOCK:(j + 2) * BLOCK, :]
        o = jnp.dot(probs, vb, preferred_element_type=f32)
        return [jnp.where(low_half, o[(2 * c2) * BLOCK:(2 * c2 + 1) * BLOCK, :],
                          o[(2 * c2 + 1) * BLOCK:(2 * c2 + 2) * BLOCK, :])
                for c2 in range(Q_PER_KV // 2)]

    attn_out = []
    ga_t = []
    for n in range(ATTN_WIDTH // MXU_TILE):
        ga_t.append(_silu(proj_tile(OFF_ZA + n * MXU_TILE)))
        attn_out.append(attend(len(attn_out)))
    for n in range(SGU_WIDTH // MXU_TILE):
        u_s[:, n * MXU_TILE:(n + 1) * MXU_TILE] = _gelu_exact(proj_tile(OFF_U + n * MXU_TILE))
        attn_out.append(attend(len(attn_out)))
    vg = []
    for n in range(SGU_WIDTH // MXU_TILE):
        vg.append(_gelu_exact(proj_tile(OFF_VS + n * MXU_TILE)))
        attn_out.append(attend(len(attn_out)))
    vg = jnp.concatenate(vg, axis=1)
    mu = jnp.mean(vg, axis=-1, keepdims=True)
    vc = vg - mu
    var = jnp.mean(vc * vc, axis=-1, keepdims=True)
    vl_s[...] = ((vc * lax.rsqrt(var + NORM_EPS)) * lng_ref[...] + lnb_ref[...]).astype(bf16)
    for n in range(SGU_WIDTH // MXU_TILE):
        gs_s[:, n * MXU_TILE:(n + 1) * MXU_TILE] = _silu(proj_tile(OFF_ZS + n * MXU_TILE))
        attn_out.append(attend(len(attn_out)))
    assert len(attn_out) == len(units)

    ga = jnp.concatenate(ga_t, axis=1)
    for u, (j, hh) in enumerate(units):
        rows = slice(j * BLOCK, (j + 1) * BLOCK)
        for c2 in range(Q_PER_KV // 2):
            cols = slice((hh * (Q_PER_KV // 2) + c2) * BLOCK, (hh * (Q_PER_KV // 2) + c2 + 1) * BLOCK)
            mix_s[rows, cols] = (attn_out[u][c2] * ga[rows, cols]).astype(bf16)

    for j in range(n_sub):
        rows = slice(j * BLOCK, (j + 1) * BLOCK)
        for p_idx in range(N_SGU_HEADS // 2):
            cols = slice(p_idx * BLOCK, (p_idx + 1) * BLOCK)
            vp = vl_s[rows, cols]
            zero = jnp.zeros_like(vp)
            rhs = jnp.concatenate([jnp.where(low_half, vp, zero),
                                   jnp.where(low_half, zero, vp)], axis=0)
            mixed = jnp.dot(wp_s[p_idx], rhs, preferred_element_type=f32) + sb_s[:, cols]
            mix_s[rows, ATTN_WIDTH + p_idx * BLOCK:ATTN_WIDTH + (p_idx + 1) * BLOCK] = (
                (u_s[rows, cols] * mixed) * gs_s[rows, cols]).astype(bf16)

    y = x_ref[0] + jnp.dot(mix_s[...], wout_ref[...], preferred_element_type=f32) + bout_ref[...]
    ms2 = jnp.mean(y * y, axis=-1, keepdims=True)
    o_ref[0] = (y * lax.rsqrt(ms2 + NORM_EPS)) * fg_ref[...]


def _layer_call(x, sinks, norm_g, w_in, b_in, ln_g, ln_b, sgu_w, sgu_bt, w_out, b_out, final_g,
                *, seq_tile=SEQ_TILE):
    batch, seq, d_model = x.shape
    assert d_model == D_MODEL and seq % seq_tile == 0 and seq_tile % (2 * BLOCK) == 0
    f32, bf16 = jnp.float32, jnp.bfloat16

    def full(shape):
        return pl.BlockSpec(shape, lambda b, s: (0,) * len(shape))

    tile_spec = pl.BlockSpec((1, seq_tile, D_MODEL), lambda b, s: (b, s, 0))
    return pl.pallas_call(
        functools.partial(_layer_kernel, seq_tile=seq_tile),
        grid=(batch, seq // seq_tile),
        in_specs=[
            pl.BlockSpec(memory_space=pltpu.SMEM),
            tile_spec,
            full((1, D_MODEL)),
            full((D_MODEL, IN_WIDTH)),
            full((1, IN_WIDTH)),
            full((1, SGU_WIDTH)),
            full((1, SGU_WIDTH)),
            full((N_SGU_HEADS, BLOCK, BLOCK)),
            full((BLOCK, N_SGU_HEADS)),
            full((D_MODEL, D_MODEL)),
            full((1, D_MODEL)),
            full((1, D_MODEL)),
        ],
        out_specs=tile_spec,
        out_shape=jax.ShapeDtypeStruct(x.shape, x.dtype),
        scratch_shapes=[
            pltpu.VMEM((seq_tile, ATTN_WIDTH), bf16),
            pltpu.VMEM((N_KV_HEADS, BLOCK + seq_tile, KV_WIDTH), bf16),
            pltpu.VMEM((N_KV_HEADS, BLOCK + seq_tile, KV_WIDTH), bf16),
            pltpu.VMEM((seq_tile, SGU_WIDTH), f32),
            pltpu.VMEM((seq_tile, SGU_WIDTH), bf16),
            pltpu.VMEM((seq_tile, SGU_WIDTH), f32),
            pltpu.VMEM((seq_tile, D_MODEL), bf16),
            pltpu.VMEM((N_SGU_HEADS // 2, BLOCK, 2 * BLOCK), bf16),
            pltpu.VMEM((BLOCK, SGU_WIDTH), f32),
        ],
        compiler_params=pltpu.CompilerParams(
            dimension_semantics=("arbitrary", "arbitrary"),
            vmem_limit_bytes=V7X_VMEM_LIMIT_BYTES),
        name="hybrid_layer",
    )(sinks, x, norm_g, w_in, b_in, ln_g, ln_b, sgu_w, sgu_bt, w_out, b_out, final_g)


def kernel(x, norm_g, w_in, b_in, attn_sinks, sgu_ln_g, sgu_ln_b, sgu_w, sgu_b, w_out, b_out, final_norm_g):
    depth = norm_g.shape[0]
    bf16 = jnp.bfloat16
    for l in range(depth):
        last = l == depth - 1
        assert last, "the fused call applies the final norm; only depth 1 is supported"
        x = _layer_call(
            x, attn_sinks[l], norm_g[l][None, :], w_in[l].astype(bf16), b_in[l][None, :],
            sgu_ln_g[l][None, :], sgu_ln_b[l][None, :], sgu_w[l], sgu_b[l].T,
            w_out[l].astype(bf16), b_out[l][None, :], final_norm_g[None, :])
    return x
```

```python
import functools

import jax
import jax.numpy as jnp
from jax import lax
from jax.experimental import pallas as pl
from jax.experimental.pallas import tpu as pltpu

D_MODEL = 1024
HEAD_DIM = 64
ATTN_WIDTH = 512
KV_WIDTH = 128
SGU_WIDTH = 512
N_KV_HEADS = 2
Q_PER_KV = 4
N_SGU_HEADS = 8
BLOCK = 128
NORM_EPS = 1e-5
NEG_INF = -1e30
ATTN_SCALE = HEAD_DIM ** -0.5

OFF_Q = 0
OFF_K = OFF_Q + ATTN_WIDTH
OFF_V = OFF_K + KV_WIDTH
OFF_ZA = OFF_V + KV_WIDTH
OFF_U = OFF_ZA + ATTN_WIDTH
OFF_VS = OFF_U + SGU_WIDTH
OFF_ZS = OFF_VS + SGU_WIDTH
IN_WIDTH = OFF_ZS + SGU_WIDTH

SEQ_TILE = 512
MXU_TILE = 256
V7X_VMEM_LIMIT_BYTES = 56 * 1024 * 1024

_SQRT_HALF = 0.7071067811865476


def _silu(z):
    return z * (1.0 / (1.0 + jnp.exp(-z)))


def _gelu_exact(z):
    return 0.5 * z * (1.0 + lax.erf(z * _SQRT_HALF))


def _layer_kernel(sinks_ref, xf_ref, xn_ref, ng_ref, win_ref, bin_ref, lng_ref, lnb_ref,
                  sw_ref, sbt_ref, wout_ref, bout_ref, fg_ref, o_ref,
                  h_s, xk_s, sc_s, q_s, kd_s, vd_s, u_s, vl_s, gs_s, mix_s, wp_s, sb_s,
                  *, seq_tile, tiles_per_seq):
    f32, bf16 = jnp.float32, jnp.bfloat16
    n_sub = seq_tile // BLOCK
    half_t = seq_tile // 2
    t_idx = pl.program_id(0)
    s_idx = t_idx % tiles_per_seq
    units = [(j, hh) for j in range(n_sub) for hh in range(N_KV_HEADS)]

    lane = lax.broadcasted_iota(jnp.int32, (BLOCK, BLOCK), 1)
    low_half = lane < HEAD_DIM

    def input_norm(x):
        ms = jnp.mean(x * x, axis=-1, keepdims=True)
        return ((x * lax.rsqrt(ms + NORM_EPS)) * ng_ref[...]).astype(bf16)

    def proj_tile(off):
        cols = slice(off, off + MXU_TILE)
        return jnp.concatenate(
            [jnp.dot(h_s[i * half_t:(i + 1) * half_t, :], win_ref[:, cols],
                     preferred_element_type=f32) for i in range(2)], axis=0) + bin_ref[:, cols]

    def project_qkv_and_score(seq_start):
        for dup_ref in (kd_s, vd_s):
            for hh in range(N_KV_HEADS):
                tail = dup_ref[hh, seq_tile:seq_tile + BLOCK, :]
                dup_ref[hh, 0:BLOCK, :] = jnp.where(seq_start, jnp.zeros_like(tail), tail)
        for n in range(ATTN_WIDTH // MXU_TILE):
            cols = slice(n * MXU_TILE, (n + 1) * MXU_TILE)
            q_s[:, cols] = (proj_tile(OFF_Q + n * MXU_TILE) * ATTN_SCALE).astype(bf16)
        kv = proj_tile(OFF_K)
        half_mask = lax.broadcasted_iota(jnp.int32, (seq_tile, KV_WIDTH), 1) < HEAD_DIM
        for dup_ref, off in ((kd_s, 0), (vd_s, KV_WIDTH)):
            t = kv[:, off:off + KV_WIDTH]
            t_sw = pltpu.roll(t, HEAD_DIM, axis=1)
            dup_ref[0, BLOCK:BLOCK + seq_tile, :] = jnp.where(half_mask, t, t_sw).astype(bf16)
            dup_ref[1, BLOCK:BLOCK + seq_tile, :] = jnp.where(half_mask, t_sw, t).astype(bf16)
        for u, (j, hh) in enumerate(units):
            rows = slice(j * BLOCK, (j + 1) * BLOCK)
            parts = []
            for g in range(Q_PER_KV):
                c = hh * (Q_PER_KV // 2) + g // 2
                qc = q_s[rows, c * BLOCK:(c + 1) * BLOCK]
                keep = low_half if g % 2 == 0 else jnp.logical_not(low_half)
                parts.append(jnp.where(keep, qc, jnp.zeros_like(qc)))
            qst = jnp.concatenate(parts, axis=0)
            kb = kd_s[hh, j * BLOCK:(j + 2) * BLOCK, :]
            sc_s[u] = lax.dot_general(qst, kb, (((1,), (1,)), ((), ())),
                                      preferred_element_type=f32)

    @pl.when(t_idx == 0)
    def _():
        row = lax.broadcasted_iota(jnp.int32, (BLOCK, BLOCK), 0)
        for p in range(N_SGU_HEADS // 2):
            for half in range(2):
                w = jnp.where(row >= lane, sw_ref[2 * p + half], 0.0)
                wp_s[p, :, half * BLOCK:(half + 1) * BLOCK] = w.astype(bf16)
        sbt = sbt_ref[...]
        for hh in range(N_SGU_HEADS):
            sb_s[:, hh * HEAD_DIM:(hh + 1) * HEAD_DIM] = jnp.broadcast_to(
                sbt[:, hh:hh + 1], (BLOCK, HEAD_DIM))
        x0 = xf_ref[0]
        xk_s[...] = x0
        h_s[...] = input_norm(x0)
        for dup_ref in (kd_s, vd_s):
            for hh in range(N_KV_HEADS):
                dup_ref[hh, seq_tile:seq_tile + BLOCK, :] = jnp.zeros((BLOCK, KV_WIDTH), bf16)
        project_qkv_and_score(True)

    qi = lax.broadcasted_iota(jnp.int32, (BLOCK, 2 * BLOCK), 0) + BLOCK
    kj = lax.broadcasted_iota(jnp.int32, (BLOCK, 2 * BLOCK), 1)
    diff = qi - kj
    in_window = (diff >= 0) & (diff < BLOCK)
    first_valid = in_window & ((kj >= BLOCK) | (s_idx > 0))

    def attend(u):
        j, hh = units[u]
        valid = first_valid if j == 0 else in_window
        probs = []
        for g in range(Q_PER_KV):
            sg = jnp.where(valid, sc_s[u, g * BLOCK:(g + 1) * BLOCK, :], NEG_INF)
            sink = sinks_ref[hh * Q_PER_KV + g]
            m = jnp.maximum(jnp.max(sg, axis=-1, keepdims=True), sink)
            p = jnp.exp(sg - m)
            denom = jnp.sum(p, axis=-1, keepdims=True) + jnp.exp(sink - m)
            probs.append((p * (1.0 / denom)).astype(bf16))
        probs = jnp.concatenate(probs, axis=0)
        vb = vd_s[hh, j * BLOCK:(j + 2) * BLOCK, :]
        o = jnp.dot(probs, vb, preferred_element_type=f32)
        return [jnp.where(low_half, o[(2 * c2) * BLOCK:(2 * c2 + 1) * BLOCK, :],
                          o[(2 * c2 + 1) * BLOCK:(2 * c2 + 2) * BLOCK, :])
                for c2 in range(Q_PER_KV // 2)]

    attn_out = []
    ga_t = []
    for n in range(ATTN_WIDTH // MXU_TILE):
        ga_t.append(_silu(proj_tile(OFF_ZA + n * MXU_TILE)))
        attn_out.append(attend(len(attn_out)))
    for n in range(SGU_WIDTH // MXU_TILE):
        u_s[:, n * MXU_TILE:(n + 1) * MXU_TILE] = _gelu_exact(proj_tile(OFF_U + n * MXU_TILE))
        attn_out.append(attend(len(attn_out)))
    vg = []
    for n in range(SGU_WIDTH // MXU_TILE):
        vg.append(_gelu_exact(proj_tile(OFF_VS + n * MXU_TILE)))
        attn_out.append(attend(len(attn_out)))
    vg = jnp.concatenate(vg, axis=1)
    mu = jnp.mean(vg, axis=-1, keepdims=True)
    vc = vg - mu
    var = jnp.mean(vc * vc, axis=-1, keepdims=True)
    vl_s[...] = ((vc * lax.rsqrt(var + NORM_EPS)) * lng_ref[...] + lnb_ref[...]).astype(bf16)
    for n in range(SGU_WIDTH // MXU_TILE):
        gs_s[:, n * MXU_TILE:(n + 1) * MXU_TILE] = _silu(proj_tile(OFF_ZS + n * MXU_TILE))
        attn_out.append(attend(len(attn_out)))
    assert len(attn_out) == len(units)

    ga = jnp.concatenate(ga_t, axis=1)
    for u, (j, hh) in enumerate(units):
        rows = slice(j * BLOCK, (j + 1) * BLOCK)
        for c2 in range(Q_PER_KV // 2):
            c = hh * (Q_PER_KV // 2) + c2
            cols = slice(c * BLOCK, (c + 1) * BLOCK)
            mix_s[rows, cols] = (attn_out[u][c2] * ga[rows, cols]).astype(bf16)

    for j in range(n_sub):
        rows = slice(j * BLOCK, (j + 1) * BLOCK)
        for p_idx in range(N_SGU_HEADS // 2):
            cols = slice(p_idx * BLOCK, (p_idx + 1) * BLOCK)
            vp = vl_s[rows, cols]
            zero = jnp.zeros_like(vp)
            rhs = jnp.concatenate([jnp.where(low_half, vp, zero),
                                   jnp.where(low_half, zero, vp)], axis=0)
            mixed = jnp.dot(wp_s[p_idx], rhs, preferred_element_type=f32) + sb_s[:, cols]
            mix_s[rows, ATTN_WIDTH + p_idx * BLOCK:ATTN_WIDTH + (p_idx + 1) * BLOCK] = (
                (u_s[rows, cols] * mixed) * gs_s[rows, cols]).astype(bf16)

    h_s[...] = input_norm(xn_ref[0])

    y = xk_s[...] + jnp.dot(mix_s[...], wout_ref[...], preferred_element_type=f32) + bout_ref[...]
    ms2 = jnp.mean(y * y, axis=-1, keepdims=True)
    o_ref[0] = (y * lax.rsqrt(ms2 + NORM_EPS)) * fg_ref[...]

    xk_s[...] = xn_ref[0]
    project_qkv_and_score((t_idx + 1) % tiles_per_seq == 0)


def _layer_call(x, sinks, norm_g, w_in, b_in, ln_g, ln_b, sgu_w, sgu_bt, w_out, b_out, final_g,
                *, seq_tile=SEQ_TILE):
    batch, seq, d_model = x.shape
    assert d_model == D_MODEL and seq % seq_tile == 0 and seq_tile % (2 * BLOCK) == 0
    f32, bf16 = jnp.float32, jnp.bfloat16
    tiles_per_seq = seq // seq_tile
    n_tiles = batch * tiles_per_seq
    n_units = (seq_tile // BLOCK) * N_KV_HEADS

    def full(shape):
        return pl.BlockSpec(shape, lambda t: (0,) * len(shape))

    def tile_index(t):
        return (t // tiles_per_seq, t % tiles_per_seq, 0)

    tile = (1, seq_tile, D_MODEL)
    return pl.pallas_call(
        functools.partial(_layer_kernel, seq_tile=seq_tile, tiles_per_seq=tiles_per_seq),
        grid=(n_tiles,),
        in_specs=[
            pl.BlockSpec(memory_space=pltpu.SMEM),
            pl.BlockSpec(tile, lambda t: (0, 0, 0)),
            pl.BlockSpec(tile, lambda t: tile_index(jnp.minimum(t + 1, n_tiles - 1))),
            full((1, D_MODEL)),
            full((D_MODEL, IN_WIDTH)),
            full((1, IN_WIDTH)),
            full((1, SGU_WIDTH)),
            full((1, SGU_WIDTH)),
            full((N_SGU_HEADS, BLOCK, BLOCK)),
            full((BLOCK, N_SGU_HEADS)),
            full((D_MODEL, D_MODEL)),
            full((1, D_MODEL)),
            full((1, D_MODEL)),
        ],
        out_specs=pl.BlockSpec(tile, lambda t: tile_index(t)),
        out_shape=jax.ShapeDtypeStruct(x.shape, x.dtype),
        scratch_shapes=[
            pltpu.VMEM((seq_tile, D_MODEL), bf16),
            pltpu.VMEM((seq_tile, D_MODEL), f32),
            pltpu.VMEM((n_units, Q_PER_KV * BLOCK, 2 * BLOCK), f32),
            pltpu.VMEM((seq_tile, ATTN_WIDTH), bf16),
            pltpu.VMEM((N_KV_HEADS, BLOCK + seq_tile, KV_WIDTH), bf16),
            pltpu.VMEM((N_KV_HEADS, BLOCK + seq_tile, KV_WIDTH), bf16),
            pltpu.VMEM((seq_tile, SGU_WIDTH), f32),
            pltpu.VMEM((seq_tile, SGU_WIDTH), bf16),
            pltpu.VMEM((seq_tile, SGU_WIDTH), f32),
            pltpu.VMEM((seq_tile, D_MODEL), bf16),
            pltpu.VMEM((N_SGU_HEADS // 2, BLOCK, 2 * BLOCK), bf16),
            pltpu.VMEM((BLOCK, SGU_WIDTH), f32),
        ],
        compiler_params=pltpu.CompilerParams(
            dimension_semantics=("arbitrary",),
            vmem_limit_bytes=V7X_VMEM_LIMIT_BYTES),
        name="hybrid_layer",
    )(sinks, x, x, norm_g, w_in, b_in, ln_g, ln_b, sgu_w, sgu_bt, w_out, b_out, final_g)


def kernel(x, norm_g, w_in, b_in, attn_sinks, sgu_ln_g, sgu_ln_b, sgu_w, sgu_b, w_out, b_out, final_norm_g):
    depth = norm_g.shape[0]
    bf16 = jnp.bfloat16
    for l in range(depth):
        last = l == depth - 1
        assert last, "the fused call applies the final norm; only depth 1 is supported"
        x = _layer_call(
            x, attn_sinks[l], norm_g[l][None, :], w_in[l].astype(bf16), b_in[l][None, :],
            sgu_ln_g[l][None, :], sgu_ln_b[l][None, :], sgu_w[l], sgu_b[l].T,
            w_out[l].astype(bf16), b_out[l][None, :], final_norm_g[None, :])
    return x
```

```python
import functools

import jax
import jax.numpy as jnp
from jax import lax
from jax.experimental import pallas as pl
from jax.experimental.pallas import tpu as pltpu

D_MODEL = 1024
HEAD_DIM = 64
ATTN_WIDTH = 512
KV_WIDTH = 128
SGU_WIDTH = 512
N_KV_HEADS = 2
Q_PER_KV = 4
N_SGU_HEADS = 8
BLOCK = 128
NORM_EPS = 1e-5
NEG_INF = -1e30
ATTN_SCALE = HEAD_DIM ** -0.5

OFF_Q = 0
OFF_K = OFF_Q + ATTN_WIDTH
OFF_V = OFF_K + KV_WIDTH
OFF_ZA = OFF_V + KV_WIDTH
OFF_U = OFF_ZA + ATTN_WIDTH
OFF_VS = OFF_U + SGU_WIDTH
OFF_ZS = OFF_VS + SGU_WIDTH
IN_WIDTH = OFF_ZS + SGU_WIDTH

SEQ_TILE = 512
MXU_TILE = 256
V7X_VMEM_LIMIT_BYTES = 56 * 1024 * 1024

_SQRT_HALF = 0.7071067811865476


def _silu(z):
    return z * (1.0 / (1.0 + jnp.exp(-z)))


def _gelu_exact(z):
    return 0.5 * z * (1.0 + lax.erf(z * _SQRT_HALF))


def _layer_kernel(sinks_ref, x_ref, ng_ref, win_ref, bin_ref, lng_ref, lnb_ref,
                  sw_ref, sbt_ref, wout_ref, bout_ref, fg_ref, o_ref,
                  q_s, kd_s, vd_s, u_s, vl_s, gs_s, mix_s, wp_s, sb_s, *, seq_tile):
    f32, bf16 = jnp.float32, jnp.bfloat16
    n_sub = seq_tile // BLOCK
    b_idx = pl.program_id(0)
    s_idx = pl.program_id(1)

    lane = lax.broadcasted_iota(jnp.int32, (BLOCK, BLOCK), 1)
    low_half = lane < HEAD_DIM

    @pl.when((b_idx == 0) & (s_idx == 0))
    def _():
        row = lax.broadcasted_iota(jnp.int32, (BLOCK, BLOCK), 0)
        for p in range(N_SGU_HEADS // 2):
            for half in range(2):
                w = jnp.where(row >= lane, sw_ref[2 * p + half], 0.0)
                wp_s[p, :, half * BLOCK:(half + 1) * BLOCK] = w.astype(bf16)
        sbt = sbt_ref[...]
        for hh in range(N_SGU_HEADS):
            sb_s[:, hh * HEAD_DIM:(hh + 1) * HEAD_DIM] = jnp.broadcast_to(
                sbt[:, hh:hh + 1], (BLOCK, HEAD_DIM))

    @pl.when(s_idx == 0)
    def _():
        for hh in range(N_KV_HEADS):
            kd_s[hh, 0:BLOCK, :] = jnp.zeros((BLOCK, KV_WIDTH), bf16)
            vd_s[hh, 0:BLOCK, :] = jnp.zeros((BLOCK, KV_WIDTH), bf16)

    @pl.when(s_idx > 0)
    def _():
        for hh in range(N_KV_HEADS):
            kd_s[hh, 0:BLOCK, :] = kd_s[hh, seq_tile:seq_tile + BLOCK, :]
            vd_s[hh, 0:BLOCK, :] = vd_s[hh, seq_tile:seq_tile + BLOCK, :]

    x = x_ref[0]
    ms = jnp.mean(x * x, axis=-1, keepdims=True)
    h = ((x * lax.rsqrt(ms + NORM_EPS)) * ng_ref[...]).astype(bf16)
    half_t = seq_tile // 2

    def proj_tile(off):
        cols = slice(off, off + MXU_TILE)
        return jnp.concatenate(
            [jnp.dot(h[i * half_t:(i + 1) * half_t], win_ref[:, cols], preferred_element_type=f32)
             for i in range(2)], axis=0) + bin_ref[:, cols]

    for n in range(ATTN_WIDTH // MXU_TILE):
        cols = slice(n * MXU_TILE, (n + 1) * MXU_TILE)
        q_s[:, cols] = (proj_tile(OFF_Q + n * MXU_TILE) * ATTN_SCALE).astype(bf16)
    kv = proj_tile(OFF_K)
    half_mask = lax.broadcasted_iota(jnp.int32, (seq_tile, KV_WIDTH), 1) < HEAD_DIM
    for dup_ref, off in ((kd_s, 0), (vd_s, KV_WIDTH)):
        t = kv[:, off:off + KV_WIDTH]
        t_sw = pltpu.roll(t, HEAD_DIM, axis=1)
        dup_ref[0, BLOCK:BLOCK + seq_tile, :] = jnp.where(half_mask, t, t_sw).astype(bf16)
        dup_ref[1, BLOCK:BLOCK + seq_tile, :] = jnp.where(half_mask, t_sw, t).astype(bf16)

    row = lax.broadcasted_iota(jnp.int32, (BLOCK, BLOCK), 0)
    prev_side = lane > row
    prev_side4 = jnp.concatenate([prev_side] * Q_PER_KV, axis=0)
    has_prev = s_idx > 0
    units = [(j, hh) for j in range(n_sub) for hh in range(N_KV_HEADS)]
    scores = []
    for j, hh in units:
        rows = slice(j * BLOCK, (j + 1) * BLOCK)
        parts = []
        for g in range(Q_PER_KV):
            c = hh * (Q_PER_KV // 2) + g // 2
            qc = q_s[rows, c * BLOCK:(c + 1) * BLOCK]
            keep = low_half if g % 2 == 0 else jnp.logical_not(low_half)
            parts.append(jnp.where(keep, qc, jnp.zeros_like(qc)))
        qst = jnp.concatenate(parts, axis=0)
        kb = kd_s[hh, j * BLOCK:(j + 2) * BLOCK, :]
        sc = lax.dot_general(qst, kb, (((1,), (1,)), ((), ())),
                             preferred_element_type=f32)
        s_prev = sc[:, 0:BLOCK]
        if j == 0:
            s_prev = jnp.where(has_prev, s_prev, NEG_INF)
        scores.append(jnp.where(prev_side4, s_prev, sc[:, BLOCK:2 * BLOCK]))

    def attend(u):
        j, hh = units[u]
        sc = scores[u]
        probs, inv_denoms = [], []
        for g in range(Q_PER_KV):
            sg = sc[g * BLOCK:(g + 1) * BLOCK, :]
            sink = sinks_ref[hh * Q_PER_KV + g]
            m = jnp.maximum(jnp.max(sg, axis=-1, keepdims=True), sink)
            p = jnp.exp(sg - m)
            denom = jnp.sum(p, axis=-1, keepdims=True) + jnp.exp(sink - m)
            inv_denoms.append(1.0 / denom)
            pb = p.astype(bf16)
            zero = jnp.zeros_like(pb)
            probs.append(jnp.concatenate([jnp.where(prev_side, pb, zero),
                                          jnp.where(prev_side, zero, pb)], axis=1))
        probs = jnp.concatenate(probs, axis=0)
        vb = vd_s[hh, j * BLOCK:(j + 2) * BLOCK, :]
        o = jnp.dot(probs, vb, preferred_element_type=f32)
        o = [o[g * BLOCK:(g + 1) * BLOCK, :] * inv_denoms[g] for g in range(Q_PER_KV)]
        return [jnp.where(low_half, o[2 * c2], o[2 * c2 + 1]) for c2 in range(Q_PER_KV // 2)]

    attn_out = []
    ga_t = []
    for n in range(ATTN_WIDTH // MXU_TILE):
        ga_t.append(_silu(proj_tile(OFF_ZA + n * MXU_TILE)))
        attn_out.append(attend(len(attn_out)))
    for n in range(SGU_WIDTH // MXU_TILE):
        u_s[:, n * MXU_TILE:(n + 1) * MXU_TILE] = _gelu_exact(proj_tile(OFF_U + n * MXU_TILE))
        attn_out.append(attend(len(attn_out)))
    vg = []
    for n in range(SGU_WIDTH // MXU_TILE):
        vg.append(_gelu_exact(proj_tile(OFF_VS + n * MXU_TILE)))
        attn_out.append(attend(len(attn_out)))
    vg = jnp.concatenate(vg, axis=1)
    mu = jnp.mean(vg, axis=-1, keepdims=True)
    vc = vg - mu
    var = jnp.mean(vc * vc, axis=-1, keepdims=True)
    vl_s[...] = ((vc * lax.rsqrt(var + NORM_EPS)) * lng_ref[...] + lnb_ref[...]).astype(bf16)
    for n in range(SGU_WIDTH // MXU_TILE):
        gs_s[:, n * MXU_TILE:(n + 1) * MXU_TILE] = _silu(proj_tile(OFF_ZS + n * MXU_TILE))
        attn_out.append(attend(len(attn_out)))
    assert len(attn_out) == len(units)

    ga = jnp.concatenate(ga_t, axis=1)
    for u, (j, hh) in enumerate(units):
        rows = slice(j * BLOCK, (j + 1) * BLOCK)
        for c2 in range(Q_PER_KV // 2):
            cols = slice((hh * (Q_PER_KV // 2) + c2) * BLOCK, (hh * (Q_PER_KV // 2) + c2 + 1) * BLOCK)
            mix_s[rows, cols] = (attn_out[u][c2] * ga[rows, cols]).astype(bf16)

    for j in range(n_sub):
        rows = slice(j * BLOCK, (j + 1) * BLOCK)
        for p_idx in range(N_SGU_HEADS // 2):
            cols = slice(p_idx * BLOCK, (p_idx + 1) * BLOCK)
            vp = vl_s[rows, cols]
            zero = jnp.zeros_like(vp)
            rhs = jnp.concatenate([jnp.where(low_half, vp, zero),
                                   jnp.where(low_half, zero, vp)], axis=0)
            mixed = jnp.dot(wp_s[p_idx], rhs, preferred_element_type=f32) + sb_s[:, cols]
            mix_s[rows, ATTN_WIDTH + p_idx * BLOCK:ATTN_WIDTH + (p_idx + 1) * BLOCK] = (
                (u_s[rows, cols] * mixed) * gs_s[rows, cols]).astype(bf16)

    y = x_ref[0] + jnp.dot(mix_s[...], wout_ref[...], preferred_element_type=f32) + bout_ref[...]
    ms2 = jnp.mean(y * y, axis=-1, keepdims=True)
    o_ref[0] = (y * lax.rsqrt(ms2 + NORM_EPS)) * fg_ref[...]


def _layer_call(x, sinks, norm_g, w_in, b_in, ln_g, ln_b, sgu_w, sgu_bt, w_out, b_out, final_g,
                *, seq_tile=SEQ_TILE):
    batch, seq, d_model = x.shape
    assert d_model == D_MODEL and seq % seq_tile == 0 and seq_tile % (2 * BLOCK) == 0
    f32, bf16 = jnp.float32, jnp.bfloat16

    def full(shape):
        return pl.BlockSpec(shape, lambda b, s: (0,) * len(shape))

    tile_spec = pl.BlockSpec((1, seq_tile, D_MODEL), lambda b, s: (b, s, 0))
    return pl.pallas_call(
        functools.partial(_layer_kernel, seq_tile=seq_tile),
        grid=(batch, seq // seq_tile),
        in_specs=[
            pl.BlockSpec(memory_space=pltpu.SMEM),
            tile_spec,
            full((1, D_MODEL)),
            full((D_MODEL, IN_WIDTH)),
            full((1, IN_WIDTH)),
            full((1, SGU_WIDTH)),
            full((1, SGU_WIDTH)),
            full((N_SGU_HEADS, BLOCK, BLOCK)),
            full((BLOCK, N_SGU_HEADS)),
            full((D_MODEL, D_MODEL)),
            full((1, D_MODEL)),
            full((1, D_MODEL)),
        ],
        out_specs=tile_spec,
        out_shape=jax.ShapeDtypeStruct(x.shape, x.dtype),
        scratch_shapes=[
            pltpu.VMEM((seq_tile, ATTN_WIDTH), bf16),
            pltpu.VMEM((N_KV_HEADS, BLOCK + seq_tile, KV_WIDTH), bf16),
            pltpu.VMEM((N_KV_HEADS, BLOCK + seq_tile, KV_WIDTH), bf16),
            pltpu.VMEM((seq_tile, SGU_WIDTH), f32),
            pltpu.VMEM((seq_tile, SGU_WIDTH), bf16),
            pltpu.VMEM((seq_tile, SGU_WIDTH), f32),
            pltpu.VMEM((seq_tile, D_MODEL), bf16),
            pltpu.VMEM((N_SGU_HEADS // 2, BLOCK, 2 * BLOCK), bf16),
            pltpu.VMEM((BLOCK, SGU_WIDTH), f32),
        ],
        compiler_params=pltpu.CompilerParams(
            dimension_semantics=("arbitrary", "arbitrary"),
            vmem_limit_bytes=V7X_VMEM_LIMIT_BYTES),
        name="hybrid_layer",
    )(sinks, x, norm_g, w_in, b_in, ln_g, ln_b, sgu_w, sgu_bt, w_out, b_out, final_g)


def kernel(x, norm_g, w_in, b_in, attn_sinks, sgu_ln_g, sgu_ln_b, sgu_w, sgu_b, w_out, b_out, final_norm_g):
    depth = norm_g.shape[0]
    bf16 = jnp.bfloat16
    for l in range(depth):
        last = l == depth - 1
        assert last, "the fused call applies the final norm; only depth 1 is supported"
        x = _layer_call(
            x, attn_sinks[l], norm_g[l][None, :], w_in[l].astype(bf16), b_in[l][None, :],
            sgu_ln_g[l][None, :], sgu_ln_b[l][None, :], sgu_w[l], sgu_b[l].T,
            w_out[l].astype(bf16), b_out[l][None, :], final_norm_g[None, :])
    return x
```

```python
import functools

import jax
import jax.numpy as jnp
from jax import lax
from jax.experimental import pallas as pl
from jax.experimental.pallas import tpu as pltpu

D_MODEL = 1024
HEAD_DIM = 64
ATTN_WIDTH = 512
KV_WIDTH = 128
SGU_WIDTH = 512
N_KV_HEADS = 2
Q_PER_KV = 4
N_SGU_HEADS = 8
BLOCK = 128
NORM_EPS = 1e-5
NEG_INF = -1e30
ATTN_SCALE = HEAD_DIM ** -0.5

OFF_Q = 0
OFF_K = OFF_Q + ATTN_WIDTH
OFF_V = OFF_K + KV_WIDTH
OFF_ZA = OFF_V + KV_WIDTH
OFF_U = OFF_ZA + ATTN_WIDTH
OFF_VS = OFF_U + SGU_WIDTH
OFF_ZS = OFF_VS + SGU_WIDTH
IN_WIDTH = OFF_ZS + SGU_WIDTH

SEQ_TILE = 1024
MXU_TILE = 256
V7X_VMEM_LIMIT_BYTES = 56 * 1024 * 1024

_SQRT_HALF = 0.7071067811865476


def _silu(z):
    return z * (1.0 / (1.0 + jnp.exp(-z)))


def _gelu_exact(z):
    return 0.5 * z * (1.0 + lax.erf(z * _SQRT_HALF))


def _layer_kernel(sinks_ref, x_ref, ng_ref, win_ref, bin_ref, lng_ref, lnb_ref,
                  sw_ref, sbt_ref, wout_ref, bout_ref, fg_ref, o_ref,
                  q_s, kd_s, vd_s, u_s, vl_s, gs_s, mix_s, wp_s, sb_s, *, seq_tile):
    f32, bf16 = jnp.float32, jnp.bfloat16
    n_sub = seq_tile // BLOCK
    b_idx = pl.program_id(0)
    s_idx = pl.program_id(1)

    lane = lax.broadcasted_iota(jnp.int32, (BLOCK, BLOCK), 1)
    low_half = lane < HEAD_DIM

    @pl.when((b_idx == 0) & (s_idx == 0))
    def _():
        row = lax.broadcasted_iota(jnp.int32, (BLOCK, BLOCK), 0)
        for p in range(N_SGU_HEADS // 2):
            for half in range(2):
                w = jnp.where(row >= lane, sw_ref[2 * p + half], 0.0)
                wp_s[p, :, half * BLOCK:(half + 1) * BLOCK] = w.astype(bf16)
        sbt = sbt_ref[...]
        for hh in range(N_SGU_HEADS):
            sb_s[:, hh * HEAD_DIM:(hh + 1) * HEAD_DIM] = jnp.broadcast_to(
                sbt[:, hh:hh + 1], (BLOCK, HEAD_DIM))

    @pl.when(s_idx == 0)
    def _():
        for hh in range(N_KV_HEADS):
            kd_s[hh, 0:BLOCK, :] = jnp.zeros((BLOCK, KV_WIDTH), bf16)
            vd_s[hh, 0:BLOCK, :] = jnp.zeros((BLOCK, KV_WIDTH), bf16)

    @pl.when(s_idx > 0)
    def _():
        for hh in range(N_KV_HEADS):
            kd_s[hh, 0:BLOCK, :] = kd_s[hh, seq_tile:seq_tile + BLOCK, :]
            vd_s[hh, 0:BLOCK, :] = vd_s[hh, seq_tile:seq_tile + BLOCK, :]

    x = x_ref[0]
    ms = jnp.mean(x * x, axis=-1, keepdims=True)
    h = ((x * lax.rsqrt(ms + NORM_EPS)) * ng_ref[...]).astype(bf16)
    half_t = seq_tile // 2

    def proj_tile(off):
        cols = slice(off, off + MXU_TILE)
        return jnp.concatenate(
            [jnp.dot(h[i * half_t:(i + 1) * half_t], win_ref[:, cols], preferred_element_type=f32)
             for i in range(2)], axis=0) + bin_ref[:, cols]

    for n in range(ATTN_WIDTH // MXU_TILE):
        cols = slice(n * MXU_TILE, (n + 1) * MXU_TILE)
        q_s[:, cols] = (proj_tile(OFF_Q + n * MXU_TILE) * ATTN_SCALE).astype(bf16)
    kv = proj_tile(OFF_K)
    half_mask = lax.broadcasted_iota(jnp.int32, (seq_tile, KV_WIDTH), 1) < HEAD_DIM
    for dup_ref, off in ((kd_s, 0), (vd_s, KV_WIDTH)):
        t = kv[:, off:off + KV_WIDTH]
        t_sw = pltpu.roll(t, HEAD_DIM, axis=1)
        dup_ref[0, BLOCK:BLOCK + seq_tile, :] = jnp.where(half_mask, t, t_sw).astype(bf16)
        dup_ref[1, BLOCK:BLOCK + seq_tile, :] = jnp.where(half_mask, t_sw, t).astype(bf16)

    row = lax.broadcasted_iota(jnp.int32, (BLOCK, BLOCK), 0)
    prev_side = lane > row
    prev_side4 = jnp.concatenate([prev_side] * Q_PER_KV, axis=0)
    has_prev = s_idx > 0
    units = [(j, hh) for j in range(n_sub) for hh in range(N_KV_HEADS)]
    scores = []
    for j, hh in units:
        rows = slice(j * BLOCK, (j + 1) * BLOCK)
        parts = []
        for g in range(Q_PER_KV):
            c = hh * (Q_PER_KV // 2) + g // 2
            qc = q_s[rows, c * BLOCK:(c + 1) * BLOCK]
            keep = low_half if g % 2 == 0 else jnp.logical_not(low_half)
            parts.append(jnp.where(keep, qc, jnp.zeros_like(qc)))
        qst = jnp.concatenate(parts, axis=0)
        kb = kd_s[hh, j * BLOCK:(j + 2) * BLOCK, :]
        sc = lax.dot_general(qst, kb, (((1,), (1,)), ((), ())),
                             preferred_element_type=f32)
        s_prev = sc[:, 0:BLOCK]
        if j == 0:
            s_prev = jnp.where(has_prev, s_prev, NEG_INF)
        scores.append(jnp.where(prev_side4, s_prev, sc[:, BLOCK:2 * BLOCK]))

    def attend(u):
        j, hh = units[u]
        sc = scores[u]
        probs, inv_denoms = [], []
        for g in range(Q_PER_KV):
            sg = sc[g * BLOCK:(g + 1) * BLOCK, :]
            sink = sinks_ref[hh * Q_PER_KV + g]
            m = jnp.maximum(jnp.max(sg, axis=-1, keepdims=True), sink)
            p = jnp.exp(sg - m)
            denom = jnp.sum(p, axis=-1, keepdims=True) + jnp.exp(sink - m)
            inv_denoms.append(1.0 / denom)
            pb = p.astype(bf16)
            zero = jnp.zeros_like(pb)
            probs.append(jnp.concatenate([jnp.where(prev_side, pb, zero),
                                          jnp.where(prev_side, zero, pb)], axis=1))
        probs = jnp.concatenate(probs, axis=0)
        vb = vd_s[hh, j * BLOCK:(j + 2) * BLOCK, :]
        o = jnp.dot(probs, vb, preferred_element_type=f32)
        o = [o[g * BLOCK:(g + 1) * BLOCK, :] * inv_denoms[g] for g in range(Q_PER_KV)]
        return [jnp.where(low_half, o[2 * c2], o[2 * c2 + 1]) for c2 in range(Q_PER_KV // 2)]

    attn_out = []
    ga_t = []
    units_per_tile = len(units) // ((IN_WIDTH - OFF_ZA) // MXU_TILE)
    for n in range(ATTN_WIDTH // MXU_TILE):
        ga_t.append(_silu(proj_tile(OFF_ZA + n * MXU_TILE)))
        for _ in range(units_per_tile):
            attn_out.append(attend(len(attn_out)))
    for n in range(SGU_WIDTH // MXU_TILE):
        u_s[:, n * MXU_TILE:(n + 1) * MXU_TILE] = _gelu_exact(proj_tile(OFF_U + n * MXU_TILE))
        for _ in range(units_per_tile):
            attn_out.append(attend(len(attn_out)))
    vg = []
    for n in range(SGU_WIDTH // MXU_TILE):
        vg.append(_gelu_exact(proj_tile(OFF_VS + n * MXU_TILE)))
        for _ in range(units_per_tile):
            attn_out.append(attend(len(attn_out)))
    vg = jnp.concatenate(vg, axis=1)
    mu = jnp.mean(vg, axis=-1, keepdims=True)
    vc = vg - mu
    var = jnp.mean(vc * vc, axis=-1, keepdims=True)
    vl_s[...] = ((vc * lax.rsqrt(var + NORM_EPS)) * lng_ref[...] + lnb_ref[...]).astype(bf16)
    for n in range(SGU_WIDTH // MXU_TILE):
        gs_s[:, n * MXU_TILE:(n + 1) * MXU_TILE] = _silu(proj_tile(OFF_ZS + n * MXU_TILE))
        for _ in range(units_per_tile):
            attn_out.append(attend(len(attn_out)))
    assert len(attn_out) == len(units)

    ga = jnp.concatenate(ga_t, axis=1)
    for u, (j, hh) in enumerate(units):
        rows = slice(j * BLOCK, (j + 1) * BLOCK)
        for c2 in range(Q_PER_KV // 2):
            cols = slice((hh * (Q_PER_KV // 2) + c2) * BLOCK, (hh * (Q_PER_KV // 2) + c2 + 1) * BLOCK)
            mix_s[rows, cols] = (attn_out[u][c2] * ga[rows, cols]).astype(bf16)

    for j in range(n_sub):
        rows = slice(j * BLOCK, (j + 1) * BLOCK)
        for p_idx in range(N_SGU_HEADS // 2):
            cols = slice(p_idx * BLOCK, (p_idx + 1) * BLOCK)
            vp = vl_s[rows, cols]
            zero = jnp.zeros_like(vp)
            rhs = jnp.concatenate([jnp.where(low_half, vp, zero),
                                   jnp.where(low_half, zero, vp)], axis=0)
            mixed = jnp.dot(wp_s[p_idx], rhs, preferred_element_type=f32) + sb_s[:, cols]
            mix_s[rows, ATTN_WIDTH + p_idx * BLOCK:ATTN_WIDTH + (p_idx + 1) * BLOCK] = (
                (u_s[rows, cols] * mixed) * gs_s[rows, cols]).astype(bf16)

    y = x_ref[0] + jnp.dot(mix_s[...], wout_ref[...], preferred_element_type=f32) + bout_ref[...]
    ms2 = jnp.mean(y * y, axis=-1, keepdims=True)
    o_ref[0] = (y * lax.rsqrt(ms2 + NORM_EPS)) * fg_ref[...]


def _layer_call(x, sinks, norm_g, w_in, b_in, ln_g, ln_b, sgu_w, sgu_bt, w_out, b_out, final_g,
                *, seq_tile=SEQ_TILE):
    batch, seq, d_model = x.shape
    assert d_model == D_MODEL and seq % seq_tile == 0 and seq_tile % (2 * BLOCK) == 0
    f32, bf16 = jnp.float32, jnp.bfloat16

    def full(shape):
        return pl.BlockSpec(shape, lambda b, s: (0,) * len(shape))

    tile_spec = pl.BlockSpec((1, seq_tile, D_MODEL), lambda b, s: (b, s, 0))
    return pl.pallas_call(
        functools.partial(_layer_kernel, seq_tile=seq_tile),
        grid=(batch, seq // seq_tile),
        in_specs=[
            pl.BlockSpec(memory_space=pltpu.SMEM),
            tile_spec,
            full((1, D_MODEL)),
            full((D_MODEL, IN_WIDTH)),
            full((1, IN_WIDTH)),
            full((1, SGU_WIDTH)),
            full((1, SGU_WIDTH)),
            full((N_SGU_HEADS, BLOCK, BLOCK)),
            full((BLOCK, N_SGU_HEADS)),
            full((D_MODEL, D_MODEL)),
            full((1, D_MODEL)),
            full((1, D_MODEL)),
        ],
        out_specs=tile_spec,
        out_shape=jax.ShapeDtypeStruct(x.shape, x.dtype),
        scratch_shapes=[
            pltpu.VMEM((seq_tile, ATTN_WIDTH), bf16),
            pltpu.VMEM((N_KV_HEADS, BLOCK + seq_tile, KV_WIDTH), bf16),
            pltpu.VMEM((N_KV_HEADS, BLOCK + seq_tile, KV_WIDTH), bf16),
            pltpu.VMEM((seq_tile, SGU_WIDTH), f32),
            pltpu.VMEM((seq_tile, SGU_WIDTH), bf16),
            pltpu.VMEM((seq_tile, SGU_WIDTH), f32),
            pltpu.VMEM((seq_tile, D_MODEL), bf16),
            pltpu.VMEM((N_SGU_HEADS // 2, BLOCK, 2 * BLOCK), bf16),
            pltpu.VMEM((BLOCK, SGU_WIDTH), f32),
        ],
        compiler_params=pltpu.CompilerParams(
            dimension_semantics=("arbitrary", "arbitrary"),
            vmem_limit_bytes=V7X_VMEM_LIMIT_BYTES),
        name="hybrid_layer",
    )(sinks, x, norm_g, w_in, b_in, ln_g, ln_b, sgu_w, sgu_bt, w_out, b_out, final_g)


def kernel(x, norm_g, w_in, b_in, attn_sinks, sgu_ln_g, sgu_ln_b, sgu_w, sgu_b, w_out, b_out, final_norm_g):
    depth = norm_g.shape[0]
    bf16 = jnp.bfloat16
    for l in range(depth):
        last = l == depth - 1
        assert last, "the fused call applies the final norm; only depth 1 is supported"
        x = _layer_call(
            x, attn_sinks[l], norm_g[l][None, :], w_in[l].astype(bf16), b_in[l][None, :],
            sgu_ln_g[l][None, :], sgu_ln_b[l][None, :], sgu_w[l], sgu_b[l].T,
            w_out[l].astype(bf16), b_out[l][None, :], final_norm_g[None, :])
    return x
```

```python
import functools

import jax
import jax.numpy as jnp
from jax import lax
from jax.experimental import pallas as pl
from jax.experimental.pallas import tpu as pltpu

D_MODEL = 1024
HEAD_DIM = 64
ATTN_WIDTH = 512
KV_WIDTH = 128
SGU_WIDTH = 512
N_KV_HEADS = 2
Q_PER_KV = 4
N_SGU_HEADS = 8
BLOCK = 128
NORM_EPS = 1e-5
NEG_INF = -1e30
ATTN_SCALE = HEAD_DIM ** -0.5

OFF_Q = 0
OFF_K = OFF_Q + ATTN_WIDTH
OFF_V = OFF_K + KV_WIDTH
OFF_ZA = OFF_V + KV_WIDTH
OFF_U = OFF_ZA + ATTN_WIDTH
OFF_VS = OFF_U + SGU_WIDTH
OFF_ZS = OFF_VS + SGU_WIDTH
IN_WIDTH = OFF_ZS + SGU_WIDTH

SEQ_TILE = 1024
MXU_TILE = 256
ROW_CHUNK = 256
V7X_VMEM_LIMIT_BYTES = 56 * 1024 * 1024

_SQRT_HALF = 0.7071067811865476


def _silu(z):
    return z * (1.0 / (1.0 + jnp.exp(-z)))


def _gelu_exact(z):
    return 0.5 * z * (1.0 + lax.erf(z * _SQRT_HALF))


def _layer_kernel(sinks_ref, x_ref, ng_ref, win_ref, bin_ref, lng_ref, lnb_ref,
                  sw_ref, sbt_ref, wout_ref, bout_ref, fg_ref, o_ref,
                  q_s, kd_s, vd_s, u_s, vl_s, gs_s, mix_s, wp_s, sb_s, *, seq_tile):
    f32, bf16 = jnp.float32, jnp.bfloat16
    n_sub = seq_tile // BLOCK
    b_idx = pl.program_id(0)
    s_idx = pl.program_id(1)

    lane = lax.broadcasted_iota(jnp.int32, (BLOCK, BLOCK), 1)
    low_half = lane < HEAD_DIM

    @pl.when((b_idx == 0) & (s_idx == 0))
    def _():
        row = lax.broadcasted_iota(jnp.int32, (BLOCK, BLOCK), 0)
        for p in range(N_SGU_HEADS // 2):
            for half in range(2):
                w = jnp.where(row >= lane, sw_ref[2 * p + half], 0.0)
                wp_s[p, :, half * BLOCK:(half + 1) * BLOCK] = w.astype(bf16)
        sbt = sbt_ref[...]
        for hh in range(N_SGU_HEADS):
            sb_s[:, hh * HEAD_DIM:(hh + 1) * HEAD_DIM] = jnp.broadcast_to(
                sbt[:, hh:hh + 1], (BLOCK, HEAD_DIM))

    @pl.when(s_idx == 0)
    def _():
        for hh in range(N_KV_HEADS):
            kd_s[hh, 0:BLOCK, :] = jnp.zeros((BLOCK, KV_WIDTH), bf16)
            vd_s[hh, 0:BLOCK, :] = jnp.zeros((BLOCK, KV_WIDTH), bf16)

    @pl.when(s_idx > 0)
    def _():
        for hh in range(N_KV_HEADS):
            kd_s[hh, 0:BLOCK, :] = kd_s[hh, seq_tile:seq_tile + BLOCK, :]
            vd_s[hh, 0:BLOCK, :] = vd_s[hh, seq_tile:seq_tile + BLOCK, :]

    h_chunks = []
    for r0 in range(0, seq_tile, ROW_CHUNK):
        x = x_ref[0, r0:r0 + ROW_CHUNK, :]
        ms = jnp.mean(x * x, axis=-1, keepdims=True)
        hc = ((x * lax.rsqrt(ms + NORM_EPS)) * ng_ref[...]).astype(bf16)
        h_chunks.append(hc)
        q = jnp.dot(hc, win_ref[:, OFF_Q:OFF_Q + ATTN_WIDTH], preferred_element_type=f32)
        q_s[r0:r0 + ROW_CHUNK, :] = ((q + bin_ref[:, OFF_Q:OFF_Q + ATTN_WIDTH]) * ATTN_SCALE).astype(bf16)
    h = jnp.concatenate(h_chunks, axis=0)
    half_t = seq_tile // 2

    def proj_tile(off):
        cols = slice(off, off + MXU_TILE)
        return jnp.concatenate(
            [jnp.dot(h[i * half_t:(i + 1) * half_t], win_ref[:, cols], preferred_element_type=f32)
             for i in range(2)], axis=0) + bin_ref[:, cols]

    kv = proj_tile(OFF_K)
    half_mask = lax.broadcasted_iota(jnp.int32, (seq_tile, KV_WIDTH), 1) < HEAD_DIM
    for dup_ref, off in ((kd_s, 0), (vd_s, KV_WIDTH)):
        t = kv[:, off:off + KV_WIDTH]
        t_sw = pltpu.roll(t, HEAD_DIM, axis=1)
        dup_ref[0, BLOCK:BLOCK + seq_tile, :] = jnp.where(half_mask, t, t_sw).astype(bf16)
        dup_ref[1, BLOCK:BLOCK + seq_tile, :] = jnp.where(half_mask, t_sw, t).astype(bf16)

    ga_t = [_silu(proj_tile(OFF_ZA))]

    row = lax.broadcasted_iota(jnp.int32, (BLOCK, BLOCK), 0)
    prev_side = lane > row
    prev_side4 = jnp.concatenate([prev_side] * Q_PER_KV, axis=0)
    has_prev = s_idx > 0
    units = [(j, hh) for j in range(n_sub) for hh in range(N_KV_HEADS)]
    scores = []
    for j, hh in units:
        rows = slice(j * BLOCK, (j + 1) * BLOCK)
        parts = []
        for g in range(Q_PER_KV):
            c = hh * (Q_PER_KV // 2) + g // 2
            qc = q_s[rows, c * BLOCK:(c + 1) * BLOCK]
            keep = low_half if g % 2 == 0 else jnp.logical_not(low_half)
            parts.append(jnp.where(keep, qc, jnp.zeros_like(qc)))
        qst = jnp.concatenate(parts, axis=0)
        kb = kd_s[hh, j * BLOCK:(j + 2) * BLOCK, :]
        sc = lax.dot_general(qst, kb, (((1,), (1,)), ((), ())),
                             preferred_element_type=f32)
        s_prev = sc[:, 0:BLOCK]
        if j == 0:
            s_prev = jnp.where(has_prev, s_prev, NEG_INF)
        scores.append(jnp.where(prev_side4, s_prev, sc[:, BLOCK:2 * BLOCK]))

    def attend(u):
        j, hh = units[u]
        sc = scores[u]
        probs, inv_denoms = [], []
        for g in range(Q_PER_KV):
            sg = sc[g * BLOCK:(g + 1) * BLOCK, :]
            sink = sinks_ref[hh * Q_PER_KV + g]
            m = jnp.maximum(jnp.max(sg, axis=-1, keepdims=True), sink)
            p = jnp.exp(sg - m)
            denom = jnp.sum(p, axis=-1, keepdims=True) + jnp.exp(sink - m)
            inv_denoms.append(1.0 / denom)
            pb = p.astype(bf16)
            zero = jnp.zeros_like(pb)
            probs.append(jnp.concatenate([jnp.where(prev_side, pb, zero),
                                          jnp.where(prev_side, zero, pb)], axis=1))
        probs = jnp.concatenate(probs, axis=0)
        vb = vd_s[hh, j * BLOCK:(j + 2) * BLOCK, :]
        o = jnp.dot(probs, vb, preferred_element_type=f32)
        o = [o[g * BLOCK:(g + 1) * BLOCK, :] * inv_denoms[g] for g in range(Q_PER_KV)]
        return [jnp.where(low_half, o[2 * c2], o[2 * c2 + 1]) for c2 in range(Q_PER_KV // 2)]

    attn_out = []
    n_slots = (IN_WIDTH - OFF_ZA) // MXU_TILE - 1
    units_after = [len(units) * (i + 1) // n_slots - len(units) * i // n_slots for i in range(n_slots)]
    slot = iter(units_after)

    def attend_some():
        for _ in range(next(slot)):
            attn_out.append(attend(len(attn_out)))

    for n in range(1, ATTN_WIDTH // MXU_TILE):
        ga_t.append(_silu(proj_tile(OFF_ZA + n * MXU_TILE)))
        attend_some()
    for n in range(SGU_WIDTH // MXU_TILE):
        u_s[:, n * MXU_TILE:(n + 1) * MXU_TILE] = _gelu_exact(proj_tile(OFF_U + n * MXU_TILE))
        attend_some()
    vg = []
    for n in range(SGU_WIDTH // MXU_TILE):
        vg.append(_gelu_exact(proj_tile(OFF_VS + n * MXU_TILE)))
        attend_some()
    vg = jnp.concatenate(vg, axis=1)
    mu = jnp.mean(vg, axis=-1, keepdims=True)
    vc = vg - mu
    var = jnp.mean(vc * vc, axis=-1, keepdims=True)
    vl_s[...] = ((vc * lax.rsqrt(var + NORM_EPS)) * lng_ref[...] + lnb_ref[...]).astype(bf16)
    for n in range(SGU_WIDTH // MXU_TILE):
        gs_s[:, n * MXU_TILE:(n + 1) * MXU_TILE] = _silu(proj_tile(OFF_ZS + n * MXU_TILE))
        attend_some()
    assert len(attn_out) == len(units)

    ga = jnp.concatenate(ga_t, axis=1)
    for u, (j, hh) in enumerate(units):
        rows = slice(j * BLOCK, (j + 1) * BLOCK)
        for c2 in range(Q_PER_KV // 2):
            cols = slice((hh * (Q_PER_KV // 2) + c2) * BLOCK, (hh * (Q_PER_KV // 2) + c2 + 1) * BLOCK)
            mix_s[rows, cols] = (attn_out[u][c2] * ga[rows, cols]).astype(bf16)

    for j in range(n_sub):
        rows = slice(j * BLOCK, (j + 1) * BLOCK)
        for p_idx in range(N_SGU_HEADS // 2):
            cols = slice(p_idx * BLOCK, (p_idx + 1) * BLOCK)
            vp = vl_s[rows, cols]
            zero = jnp.zeros_like(vp)
            rhs = jnp.concatenate([jnp.where(low_half, vp, zero),
                                   jnp.where(low_half, zero, vp)], axis=0)
            mixed = jnp.dot(wp_s[p_idx], rhs, preferred_element_type=f32) + sb_s[:, cols]
            mix_s[rows, ATTN_WIDTH + p_idx * BLOCK:ATTN_WIDTH + (p_idx + 1) * BLOCK] = (
                (u_s[rows, cols] * mixed) * gs_s[rows, cols]).astype(bf16)

    y = x_ref[0] + jnp.dot(mix_s[...], wout_ref[...], preferred_element_type=f32) + bout_ref[...]
    ms2 = jnp.mean(y * y, axis=-1, keepdims=True)
    o_ref[0] = (y * lax.rsqrt(ms2 + NORM_EPS)) * fg_ref[...]


def _layer_call(x, sinks, norm_g, w_in, b_in, ln_g, ln_b, sgu_w, sgu_bt, w_out, b_out, final_g,
                *, seq_tile=SEQ_TILE):
    batch, seq, d_model = x.shape
    assert d_model == D_MODEL and seq % seq_tile == 0 and seq_tile % (2 * BLOCK) == 0
    f32, bf16 = jnp.float32, jnp.bfloat16

    def full(shape):
        return pl.BlockSpec(shape, lambda b, s: (0,) * len(shape))

    tile_spec = pl.BlockSpec((1, seq_tile, D_MODEL), lambda b, s: (b, s, 0))
    return pl.pallas_call(
        functools.partial(_layer_kernel, seq_tile=seq_tile),
        grid=(batch, seq // seq_tile),
        in_specs=[
            pl.BlockSpec(memory_space=pltpu.SMEM),
            tile_spec,
            full((1, D_MODEL)),
            full((D_MODEL, IN_WIDTH)),
            full((1, IN_WIDTH)),
            full((1, SGU_WIDTH)),
            full((1, SGU_WIDTH)),
            full((N_SGU_HEADS, BLOCK, BLOCK)),
            full((BLOCK, N_SGU_HEADS)),
            full((D_MODEL, D_MODEL)),
            full((1, D_MODEL)),
            full((1, D_MODEL)),
        ],
        out_specs=tile_spec,
        out_shape=jax.ShapeDtypeStruct(x.shape, x.dtype),
        scratch_shapes=[
            pltpu.VMEM((seq_tile, ATTN_WIDTH), bf16),
            pltpu.VMEM((N_KV_HEADS, BLOCK + seq_tile, KV_WIDTH), bf16),
            pltpu.VMEM((N_KV_HEADS, BLOCK + seq_tile, KV_WIDTH), bf16),
            pltpu.VMEM((seq_tile, SGU_WIDTH), f32),
            pltpu.VMEM((seq_tile, SGU_WIDTH), bf16),
            pltpu.VMEM((seq_tile, SGU_WIDTH), f32),
            pltpu.VMEM((seq_tile, D_MODEL), bf16),
            pltpu.VMEM((N_SGU_HEADS // 2, BLOCK, 2 * BLOCK), bf16),
            pltpu.VMEM((BLOCK, SGU_WIDTH), f32),
        ],
        compiler_params=pltpu.CompilerParams(
            dimension_semantics=("arbitrary", "arbitrary"),
            vmem_limit_bytes=V7X_VMEM_LIMIT_BYTES),
        name="hybrid_layer",
    )(sinks, x, norm_g, w_in, b_in, ln_g, ln_b, sgu_w, sgu_bt, w_out, b_out, final_g)


def kernel(x, norm_g, w_in, b_in, attn_sinks, sgu_ln_g, sgu_ln_b, sgu_w, sgu_b, w_out, b_out, final_norm_g):
    depth = norm_g.shape[0]
    bf16 = jnp.bfloat16
    for l in range(depth):
        last = l == depth - 1
        assert last, "the fused call applies the final norm; only depth 1 is supported"
        x = _layer_call(
            x, attn_sinks[l], norm_g[l][None, :], w_in[l].astype(bf16), b_in[l][None, :],
            sgu_ln_g[l][None, :], sgu_ln_b[l][None, :], sgu_w[l], sgu_b[l].T,
            w_out[l].astype(bf16), b_out[l][None, :], final_norm_g[None, :])
    return x
```

```python
import functools

import jax
import jax.numpy as jnp
from jax import lax
from jax.experimental import pallas as pl
from jax.experimental.pallas import tpu as pltpu

D_MODEL = 1024
HEAD_DIM = 64
ATTN_WIDTH = 512
KV_WIDTH = 128
SGU_WIDTH = 512
N_KV_HEADS = 2
Q_PER_KV = 4
N_SGU_HEADS = 8
BLOCK = 128
NORM_EPS = 1e-5
NEG_INF = -1e30
ATTN_SCALE = HEAD_DIM ** -0.5

OFF_Q = 0
OFF_K = OFF_Q + ATTN_WIDTH
OFF_V = OFF_K + KV_WIDTH
OFF_ZA = OFF_V + KV_WIDTH
OFF_U = OFF_ZA + ATTN_WIDTH
OFF_VS = OFF_U + SGU_WIDTH
OFF_ZS = OFF_VS + SGU_WIDTH
IN_WIDTH = OFF_ZS + SGU_WIDTH

SEQ_TILE = 512
MXU_TILE = 256
ROW_CHUNK = 256
V7X_VMEM_LIMIT_BYTES = 56 * 1024 * 1024

_SQRT_HALF = 0.7071067811865476


def _silu(z):
    return z * (1.0 / (1.0 + jnp.exp(-z)))


def _gelu_exact(z):
    return 0.5 * z * (1.0 + lax.erf(z * _SQRT_HALF))


def _layer_kernel(sinks_ref, xc_ref, xn_ref, ng_ref, win_ref, bin_ref, lng_ref, lnb_ref,
                  sw_ref, sbt_ref, wout_ref, bout_ref, fg_ref, o_ref,
                  h_s, sc_s, q_s, kd_s, vd_s, u_s, vl_s, gs_s, mix_s, wp_s, sb_s,
                  *, seq_tile, tiles_per_seq):
    f32, bf16 = jnp.float32, jnp.bfloat16
    n_sub = seq_tile // BLOCK
    half_t = seq_tile // 2
    t_idx = pl.program_id(0)
    units = [(j, hh) for j in range(n_sub) for hh in range(N_KV_HEADS)]

    lane = lax.broadcasted_iota(jnp.int32, (BLOCK, BLOCK), 1)
    row = lax.broadcasted_iota(jnp.int32, (BLOCK, BLOCK), 0)
    low_half = lane < HEAD_DIM
    prev_side = lane > row
    prev_side4 = jnp.concatenate([prev_side] * Q_PER_KV, axis=0)

    def proj_tile(off):
        cols = slice(off, off + MXU_TILE)
        return jnp.concatenate(
            [jnp.dot(h_s[i * half_t:(i + 1) * half_t, :], win_ref[:, cols],
                     preferred_element_type=f32) for i in range(2)], axis=0) + bin_ref[:, cols]

    def norm_and_project_q(x_ref):
        for r0 in range(0, seq_tile, ROW_CHUNK):
            x = x_ref[0, r0:r0 + ROW_CHUNK, :]
            ms = jnp.mean(x * x, axis=-1, keepdims=True)
            hc = ((x * lax.rsqrt(ms + NORM_EPS)) * ng_ref[...]).astype(bf16)
            h_s[r0:r0 + ROW_CHUNK, :] = hc
            q = jnp.dot(hc, win_ref[:, OFF_Q:OFF_Q + ATTN_WIDTH], preferred_element_type=f32)
            q_s[r0:r0 + ROW_CHUNK, :] = ((q + bin_ref[:, OFF_Q:OFF_Q + ATTN_WIDTH]) * ATTN_SCALE).astype(bf16)

    def project_kv(seq_start):
        for dup_ref in (kd_s, vd_s):
            for hh in range(N_KV_HEADS):
                tail = dup_ref[hh, seq_tile:seq_tile + BLOCK, :]
                dup_ref[hh, 0:BLOCK, :] = jnp.where(seq_start, jnp.zeros_like(tail), tail)
        kv = proj_tile(OFF_K)
        half_mask = lax.broadcasted_iota(jnp.int32, (seq_tile, KV_WIDTH), 1) < HEAD_DIM
        for dup_ref, off in ((kd_s, 0), (vd_s, KV_WIDTH)):
            t = kv[:, off:off + KV_WIDTH]
            t_sw = pltpu.roll(t, HEAD_DIM, axis=1)
            dup_ref[0, BLOCK:BLOCK + seq_tile, :] = jnp.where(half_mask, t, t_sw).astype(bf16)
            dup_ref[1, BLOCK:BLOCK + seq_tile, :] = jnp.where(half_mask, t_sw, t).astype(bf16)

    def score(seq_start):
        for u, (j, hh) in enumerate(units):
            rows = slice(j * BLOCK, (j + 1) * BLOCK)
            parts = []
            for g in range(Q_PER_KV):
                c = hh * (Q_PER_KV // 2) + g // 2
                qc = q_s[rows, c * BLOCK:(c + 1) * BLOCK]
                keep = low_half if g % 2 == 0 else jnp.logical_not(low_half)
                parts.append(jnp.where(keep, qc, jnp.zeros_like(qc)))
            qst = jnp.concatenate(parts, axis=0)
            kb = kd_s[hh, j * BLOCK:(j + 2) * BLOCK, :]
            sc = lax.dot_general(qst, kb, (((1,), (1,)), ((), ())),
                                 preferred_element_type=f32)
            s_prev = sc[:, 0:BLOCK]
            if j == 0:
                s_prev = jnp.where(seq_start, NEG_INF, s_prev)
            sc_s[u] = jnp.where(prev_side4, s_prev, sc[:, BLOCK:2 * BLOCK])

    @pl.when(t_idx == 0)
    def _():
        for p in range(N_SGU_HEADS // 2):
            for half in range(2):
                w = jnp.where(row >= lane, sw_ref[2 * p + half], 0.0)
                wp_s[p, :, half * BLOCK:(half + 1) * BLOCK] = w.astype(bf16)
        sbt = sbt_ref[...]
        for hh in range(N_SGU_HEADS):
            sb_s[:, hh * HEAD_DIM:(hh + 1) * HEAD_DIM] = jnp.broadcast_to(
                sbt[:, hh:hh + 1], (BLOCK, HEAD_DIM))
        for dup_ref in (kd_s, vd_s):
            for hh in range(N_KV_HEADS):
                dup_ref[hh, seq_tile:seq_tile + BLOCK, :] = jnp.zeros((BLOCK, KV_WIDTH), bf16)
        norm_and_project_q(xc_ref)
        project_kv(True)
        score(True)

    def attend(u):
        j, hh = units[u]
        probs, inv_denoms = [], []
        for g in range(Q_PER_KV):
            sg = sc_s[u, g * BLOCK:(g + 1) * BLOCK, :]
            sink = sinks_ref[hh * Q_PER_KV + g]
            m = jnp.maximum(jnp.max(sg, axis=-1, keepdims=True), sink)
            p = jnp.exp(sg - m)
            denom = jnp.sum(p, axis=-1, keepdims=True) + jnp.exp(sink - m)
            inv_denoms.append(1.0 / denom)
            pb = p.astype(bf16)
            zero = jnp.zeros_like(pb)
            probs.append(jnp.concatenate([jnp.where(prev_side, pb, zero),
                                          jnp.where(prev_side, zero, pb)], axis=1))
        probs = jnp.concatenate(probs, axis=0)
        vb = vd_s[hh, j * BLOCK:(j + 2) * BLOCK, :]
        o = jnp.dot(probs, vb, preferred_element_type=f32)
        o = [o[g * BLOCK:(g + 1) * BLOCK, :] * inv_denoms[g] for g in range(Q_PER_KV)]
        return [jnp.where(low_half, o[2 * c2], o[2 * c2 + 1]) for c2 in range(Q_PER_KV // 2)]

    attn_out = []
    n_slots = (IN_WIDTH - OFF_ZA) // MXU_TILE
    units_after = [len(units) * (i + 1) // n_slots - len(units) * i // n_slots for i in range(n_slots)]
    slot = iter(units_after)

    def attend_some():
        for _ in range(next(slot)):
            attn_out.append(attend(len(attn_out)))

    ga_t = []
    for n in range(ATTN_WIDTH // MXU_TILE):
        ga_t.append(_silu(proj_tile(OFF_ZA + n * MXU_TILE)))
        attend_some()
    for n in range(SGU_WIDTH // MXU_TILE):
        u_s[:, n * MXU_TILE:(n + 1) * MXU_TILE] = _gelu_exact(proj_tile(OFF_U + n * MXU_TILE))
        attend_some()
    vg = []
    for n in range(SGU_WIDTH // MXU_TILE):
        vg.append(_gelu_exact(proj_tile(OFF_VS + n * MXU_TILE)))
        attend_some()
    vg = jnp.concatenate(vg, axis=1)
    mu = jnp.mean(vg, axis=-1, keepdims=True)
    vc = vg - mu
    var = jnp.mean(vc * vc, axis=-1, keepdims=True)
    vl_s[...] = ((vc * lax.rsqrt(var + NORM_EPS)) * lng_ref[...] + lnb_ref[...]).astype(bf16)
    for n in range(SGU_WIDTH // MXU_TILE):
        gs_s[:, n * MXU_TILE:(n + 1) * MXU_TILE] = _silu(proj_tile(OFF_ZS + n * MXU_TILE))
        attend_some()
    assert len(attn_out) == len(units)

    ga = jnp.concatenate(ga_t, axis=1)
    for u, (j, hh) in enumerate(units):
        rows = slice(j * BLOCK, (j + 1) * BLOCK)
        for c2 in range(Q_PER_KV // 2):
            c = hh * (Q_PER_KV // 2) + c2
            cols = slice(c * BLOCK, (c + 1) * BLOCK)
            mix_s[rows, cols] = (attn_out[u][c2] * ga[rows, cols]).astype(bf16)

    for j in range(n_sub):
        rows = slice(j * BLOCK, (j + 1) * BLOCK)
        for p_idx in range(N_SGU_HEADS // 2):
            cols = slice(p_idx * BLOCK, (p_idx + 1) * BLOCK)
            vp = vl_s[rows, cols]
            zero = jnp.zeros_like(vp)
            rhs = jnp.concatenate([jnp.where(low_half, vp, zero),
                                   jnp.where(low_half, zero, vp)], axis=0)
            mixed = jnp.dot(wp_s[p_idx], rhs, preferred_element_type=f32) + sb_s[:, cols]
            mix_s[rows, ATTN_WIDTH + p_idx * BLOCK:ATTN_WIDTH + (p_idx + 1) * BLOCK] = (
                (u_s[rows, cols] * mixed) * gs_s[rows, cols]).astype(bf16)

    next_starts_seq = (t_idx + 1) % tiles_per_seq == 0
    half_d = D_MODEL // 2
    y_lo = jnp.dot(mix_s[...], wout_ref[:, 0:half_d], preferred_element_type=f32)
    norm_and_project_q(xn_ref)
    project_kv(next_starts_seq)
    y_hi = jnp.dot(mix_s[...], wout_ref[:, half_d:D_MODEL], preferred_element_type=f32)

    y = xc_ref[0] + jnp.concatenate([y_lo, y_hi], axis=1) + bout_ref[...]
    ms2 = jnp.mean(y * y, axis=-1, keepdims=True)
    o_ref[0] = (y * lax.rsqrt(ms2 + NORM_EPS)) * fg_ref[...]

    score(next_starts_seq)


def _layer_call(x, sinks, norm_g, w_in, b_in, ln_g, ln_b, sgu_w, sgu_bt, w_out, b_out, final_g,
                *, seq_tile=SEQ_TILE):
    batch, seq, d_model = x.shape
    assert d_model == D_MODEL and seq % seq_tile == 0 and seq_tile % (2 * BLOCK) == 0
    assert seq_tile % ROW_CHUNK == 0
    f32, bf16 = jnp.float32, jnp.bfloat16
    tiles_per_seq = seq // seq_tile
    n_tiles = batch * tiles_per_seq
    n_units = (seq_tile // BLOCK) * N_KV_HEADS

    def full(shape):
        return pl.BlockSpec(shape, lambda t: (0,) * len(shape))

    def tile_index(t):
        return (t // tiles_per_seq, t % tiles_per_seq, 0)

    tile = (1, seq_tile, D_MODEL)
    return pl.pallas_call(
        functools.partial(_layer_kernel, seq_tile=seq_tile, tiles_per_seq=tiles_per_seq),
        grid=(n_tiles,),
        in_specs=[
            pl.BlockSpec(memory_space=pltpu.SMEM),
            pl.BlockSpec(tile, lambda t: tile_index(t)),
            pl.BlockSpec(tile, lambda t: tile_index(jnp.minimum(t + 1, n_tiles - 1))),
            full((1, D_MODEL)),
            full((D_MODEL, IN_WIDTH)),
            full((1, IN_WIDTH)),
            full((1, SGU_WIDTH)),
            full((1, SGU_WIDTH)),
            full((N_SGU_HEADS, BLOCK, BLOCK)),
            full((BLOCK, N_SGU_HEADS)),
            full((D_MODEL, D_MODEL)),
            full((1, D_MODEL)),
            full((1, D_MODEL)),
        ],
        out_specs=pl.BlockSpec(tile, lambda t: tile_index(t)),
        out_shape=jax.ShapeDtypeStruct(x.shape, x.dtype),
        scratch_shapes=[
            pltpu.VMEM((seq_tile, D_MODEL), bf16),
            pltpu.VMEM((n_units, Q_PER_KV * BLOCK, BLOCK), f32),
            pltpu.VMEM((seq_tile, ATTN_WIDTH), bf16),
            pltpu.VMEM((N_KV_HEADS, BLOCK + seq_tile, KV_WIDTH), bf16),
            pltpu.VMEM((N_KV_HEADS, BLOCK + seq_tile, KV_WIDTH), bf16),
            pltpu.VMEM((seq_tile, SGU_WIDTH), f32),
            pltpu.VMEM((seq_tile, SGU_WIDTH), bf16),
            pltpu.VMEM((seq_tile, SGU_WIDTH), f32),
            pltpu.VMEM((seq_tile, D_MODEL), bf16),
            pltpu.VMEM((N_SGU_HEADS // 2, BLOCK, 2 * BLOCK), bf16),
            pltpu.VMEM((BLOCK, SGU_WIDTH), f32),
        ],
        compiler_params=pltpu.CompilerParams(
            dimension_semantics=("arbitrary",),
            vmem_limit_bytes=V7X_VMEM_LIMIT_BYTES),
        name="hybrid_layer",
    )(sinks, x, x, norm_g, w_in, b_in, ln_g, ln_b, sgu_w, sgu_bt, w_out, b_out, final_g)


def kernel(x, norm_g, w_in, b_in, attn_sinks, sgu_ln_g, sgu_ln_b, sgu_w, sgu_b, w_out, b_out, final_norm_g):
    depth = norm_g.shape[0]
    bf16 = jnp.bfloat16
    for l in range(depth):
        last = l == depth - 1
        assert last, "the fused call applies the final norm; only depth 1 is supported"
        x = _layer_call(
            x, attn_sinks[l], norm_g[l][None, :], w_in[l].astype(bf16), b_in[l][None, :],
            sgu_ln_g[l][None, :], sgu_ln_b[l][None, :], sgu_w[l], sgu_b[l].T,
            w_out[l].astype(bf16), b_out[l][None, :], final_norm_g[None, :])
    return x
```

```python
import functools

import jax
import jax.numpy as jnp
from jax import lax
from jax.experimental import pallas as pl
from jax.experimental.pallas import tpu as pltpu

D_MODEL = 1024
HEAD_DIM = 64
ATTN_WIDTH = 512
KV_WIDTH = 128
SGU_WIDTH = 512
N_KV_HEADS = 2
Q_PER_KV = 4
N_SGU_HEADS = 8
BLOCK = 128
NORM_EPS = 1e-5
NEG_INF = -1e30
ATTN_SCALE = HEAD_DIM ** -0.5

OFF_Q = 0
OFF_K = OFF_Q + ATTN_WIDTH
OFF_V = OFF_K + KV_WIDTH
OFF_ZA = OFF_V + KV_WIDTH
OFF_U = OFF_ZA + ATTN_WIDTH
OFF_VS = OFF_U + SGU_WIDTH
OFF_ZS = OFF_VS + SGU_WIDTH
IN_WIDTH = OFF_ZS + SGU_WIDTH

SEQ_TILE = 512
MXU_TILE = 256
ROW_CHUNK = 256
V7X_VMEM_LIMIT_BYTES = 56 * 1024 * 1024

_SQRT_HALF = 0.7071067811865476


def _silu(z):
    return z * (1.0 / (1.0 + jnp.exp(-z)))


def _gelu_exact(z):
    return 0.5 * z * (1.0 + lax.erf(z * _SQRT_HALF))


def _layer_kernel(sinks_ref, x_ref, ng_ref, win_ref, bin_ref, lng_ref, lnb_ref,
                  sw_ref, sbt_ref, wout_ref, bout_ref, fg_ref, o_ref,
                  h_s, sc_s, q_s, kd_s, vd_s, u_s, vl_s, gs_s, mix_s, wp_s, sb_s, *, seq_tile):
    f32, bf16 = jnp.float32, jnp.bfloat16
    n_sub = seq_tile // BLOCK
    b_idx = pl.program_id(0)
    s_idx = pl.program_id(1)

    lane = lax.broadcasted_iota(jnp.int32, (BLOCK, BLOCK), 1)
    low_half = lane < HEAD_DIM

    @pl.when((b_idx == 0) & (s_idx == 0))
    def _():
        row = lax.broadcasted_iota(jnp.int32, (BLOCK, BLOCK), 0)
        for p in range(N_SGU_HEADS // 2):
            for half in range(2):
                w = jnp.where(row >= lane, sw_ref[2 * p + half], 0.0)
                wp_s[p, :, half * BLOCK:(half + 1) * BLOCK] = w.astype(bf16)
        sbt = sbt_ref[...]
        for hh in range(N_SGU_HEADS):
            sb_s[:, hh * HEAD_DIM:(hh + 1) * HEAD_DIM] = jnp.broadcast_to(
                sbt[:, hh:hh + 1], (BLOCK, HEAD_DIM))

    @pl.when(s_idx == 0)
    def _():
        for hh in range(N_KV_HEADS):
            kd_s[hh, 0:BLOCK, :] = jnp.zeros((BLOCK, KV_WIDTH), bf16)
            vd_s[hh, 0:BLOCK, :] = jnp.zeros((BLOCK, KV_WIDTH), bf16)

    @pl.when(s_idx > 0)
    def _():
        for hh in range(N_KV_HEADS):
            kd_s[hh, 0:BLOCK, :] = kd_s[hh, seq_tile:seq_tile + BLOCK, :]
            vd_s[hh, 0:BLOCK, :] = vd_s[hh, seq_tile:seq_tile + BLOCK, :]

    for r0 in range(0, seq_tile, ROW_CHUNK):
        x = x_ref[0, r0:r0 + ROW_CHUNK, :]
        ms = jnp.mean(x * x, axis=-1, keepdims=True)
        hc = ((x * lax.rsqrt(ms + NORM_EPS)) * ng_ref[...]).astype(bf16)
        h_s[r0:r0 + ROW_CHUNK, :] = hc
        q = jnp.dot(hc, win_ref[:, OFF_Q:OFF_Q + ATTN_WIDTH], preferred_element_type=f32)
        q_s[r0:r0 + ROW_CHUNK, :] = ((q + bin_ref[:, OFF_Q:OFF_Q + ATTN_WIDTH]) * ATTN_SCALE).astype(bf16)
    half_t = seq_tile // 2

    def proj_tile(off):
        cols = slice(off, off + MXU_TILE)
        return jnp.concatenate(
            [jnp.dot(h_s[i * half_t:(i + 1) * half_t, :], win_ref[:, cols], preferred_element_type=f32)
             for i in range(2)], axis=0) + bin_ref[:, cols]

    kv = proj_tile(OFF_K)
    half_mask = lax.broadcasted_iota(jnp.int32, (seq_tile, KV_WIDTH), 1) < HEAD_DIM
    for dup_ref, off in ((kd_s, 0), (vd_s, KV_WIDTH)):
        t = kv[:, off:off + KV_WIDTH]
        t_sw = pltpu.roll(t, HEAD_DIM, axis=1)
        dup_ref[0, BLOCK:BLOCK + seq_tile, :] = jnp.where(half_mask, t, t_sw).astype(bf16)
        dup_ref[1, BLOCK:BLOCK + seq_tile, :] = jnp.where(half_mask, t_sw, t).astype(bf16)

    ga_t = [_silu(proj_tile(OFF_ZA))]

    row = lax.broadcasted_iota(jnp.int32, (BLOCK, BLOCK), 0)
    prev_side = lane > row
    prev_side4 = jnp.concatenate([prev_side] * Q_PER_KV, axis=0)
    has_prev = s_idx > 0
    units = [(j, hh) for j in range(n_sub) for hh in range(N_KV_HEADS)]
    for u_idx, (j, hh) in enumerate(units):
        rows = slice(j * BLOCK, (j + 1) * BLOCK)
        parts = []
        for g in range(Q_PER_KV):
            c = hh * (Q_PER_KV // 2) + g // 2
            qc = q_s[rows, c * BLOCK:(c + 1) * BLOCK]
            keep = low_half if g % 2 == 0 else jnp.logical_not(low_half)
            parts.append(jnp.where(keep, qc, jnp.zeros_like(qc)))
        qst = jnp.concatenate(parts, axis=0)
        kb = kd_s[hh, j * BLOCK:(j + 2) * BLOCK, :]
        sc = lax.dot_general(qst, kb, (((1,), (1,)), ((), ())),
                             preferred_element_type=f32)
        s_prev = sc[:, 0:BLOCK]
        if j == 0:
            s_prev = jnp.where(has_prev, s_prev, NEG_INF)
        sc_s[u_idx] = jnp.where(prev_side4, s_prev, sc[:, BLOCK:2 * BLOCK])

    def attend(u):
        j, hh = units[u]
        probs, inv_denoms = [], []
        for g in range(Q_PER_KV):
            sg = sc_s[u, g * BLOCK:(g + 1) * BLOCK, :]
            sink = sinks_ref[hh * Q_PER_KV + g]
            m = jnp.maximum(jnp.max(sg, axis=-1, keepdims=True), sink)
            p = jnp.exp(sg - m)
            denom = jnp.sum(p, axis=-1, keepdims=True) + jnp.exp(sink - m)
            inv_denoms.append(1.0 / denom)
            pb = p.astype(bf16)
            zero = jnp.zeros_like(pb)
            probs.append(jnp.concatenate([jnp.where(prev_side, pb, zero),
                                          jnp.where(prev_side, zero, pb)], axis=1))
        probs = jnp.concatenate(probs, axis=0)
        vb = vd_s[hh, j * BLOCK:(j + 2) * BLOCK, :]
        o = jnp.dot(probs, vb, preferred_element_type=f32)
        o = [o[g * BLOCK:(g + 1) * BLOCK, :] * inv_denoms[g] for g in range(Q_PER_KV)]
        return [jnp.where(low_half, o[2 * c2], o[2 * c2 + 1]) for c2 in range(Q_PER_KV // 2)]

    attn_out = []
    n_slots = (IN_WIDTH - OFF_ZA) // MXU_TILE - 1
    units_after = [len(units) * (i + 1) // n_slots - len(units) * i // n_slots for i in range(n_slots)]
    slot = iter(units_after)

    def attend_some():
        for _ in range(next(slot)):
            attn_out.append(attend(len(attn_out)))

    for n in range(1, ATTN_WIDTH // MXU_TILE):
        ga_t.append(_silu(proj_tile(OFF_ZA + n * MXU_TILE)))
        attend_some()
    for n in range(SGU_WIDTH // MXU_TILE):
        u_s[:, n * MXU_TILE:(n + 1) * MXU_TILE] = _gelu_exact(proj_tile(OFF_U + n * MXU_TILE))
        attend_some()
    vg = []
    for n in range(SGU_WIDTH // MXU_TILE):
        vg.append(_gelu_exact(proj_tile(OFF_VS + n * MXU_TILE)))
        attend_some()
    vg = jnp.concatenate(vg, axis=1)
    mu = jnp.mean(vg, axis=-1, keepdims=True)
    vc = vg - mu
    var = jnp.mean(vc * vc, axis=-1, keepdims=True)
    vl_s[...] = ((vc * lax.rsqrt(var + NORM_EPS)) * lng_ref[...] + lnb_ref[...]).astype(bf16)
    for n in range(SGU_WIDTH // MXU_TILE):
        gs_s[:, n * MXU_TILE:(n + 1) * MXU_TILE] = _silu(proj_tile(OFF_ZS + n * MXU_TILE))
        attend_some()
    assert len(attn_out) == len(units)

    ga = jnp.concatenate(ga_t, axis=1)
    for u, (j, hh) in enumerate(units):
        rows = slice(j * BLOCK, (j + 1) * BLOCK)
        for c2 in range(Q_PER_KV // 2):
            cols = slice((hh * (Q_PER_KV // 2) + c2) * BLOCK, (hh * (Q_PER_KV // 2) + c2 + 1) * BLOCK)
            mix_s[rows, cols] = (attn_out[u][c2] * ga[rows, cols]).astype(bf16)

    for j in range(n_sub):
        rows = slice(j * BLOCK, (j + 1) * BLOCK)
        for p_idx in range(N_SGU_HEADS // 2):
            cols = slice(p_idx * BLOCK, (p_idx + 1) * BLOCK)
            vp = vl_s[rows, cols]
            zero = jnp.zeros_like(vp)
            rhs = jnp.concatenate([jnp.where(low_half, vp, zero),
                                   jnp.where(low_half, zero, vp)], axis=0)
            mixed = jnp.dot(wp_s[p_idx], rhs, preferred_element_type=f32) + sb_s[:, cols]
            mix_s[rows, ATTN_WIDTH + p_idx * BLOCK:ATTN_WIDTH + (p_idx + 1) * BLOCK] = (
                (u_s[rows, cols] * mixed) * gs_s[rows, cols]).astype(bf16)

    y = x_ref[0] + jnp.dot(mix_s[...], wout_ref[...], preferred_element_type=f32) + bout_ref[...]
    ms2 = jnp.mean(y * y, axis=-1, keepdims=True)
    o_ref[0] = (y * lax.rsqrt(ms2 + NORM_EPS)) * fg_ref[...]


def _layer_call(x, sinks, norm_g, w_in, b_in, ln_g, ln_b, sgu_w, sgu_bt, w_out, b_out, final_g,
                *, seq_tile=SEQ_TILE):
    batch, seq, d_model = x.shape
    assert d_model == D_MODEL and seq % seq_tile == 0 and seq_tile % (2 * BLOCK) == 0
    f32, bf16 = jnp.float32, jnp.bfloat16

    def full(shape):
        return pl.BlockSpec(shape, lambda b, s: (0,) * len(shape))

    tile_spec = pl.BlockSpec((1, seq_tile, D_MODEL), lambda b, s: (b, s, 0))
    return pl.pallas_call(
        functools.partial(_layer_kernel, seq_tile=seq_tile),
        grid=(batch, seq // seq_tile),
        in_specs=[
            pl.BlockSpec(memory_space=pltpu.SMEM),
            tile_spec,
            full((1, D_MODEL)),
            full((D_MODEL, IN_WIDTH)),
            full((1, IN_WIDTH)),
            full((1, SGU_WIDTH)),
            full((1, SGU_WIDTH)),
            full((N_SGU_HEADS, BLOCK, BLOCK)),
            full((BLOCK, N_SGU_HEADS)),
            full((D_MODEL, D_MODEL)),
            full((1, D_MODEL)),
            full((1, D_MODEL)),
        ],
        out_specs=tile_spec,
        out_shape=jax.ShapeDtypeStruct(x.shape, x.dtype),
        scratch_shapes=[
            pltpu.VMEM((seq_tile, D_MODEL), bf16),
            pltpu.VMEM(((seq_tile // BLOCK) * N_KV_HEADS, Q_PER_KV * BLOCK, BLOCK), f32),
            pltpu.VMEM((seq_tile, ATTN_WIDTH), bf16),
            pltpu.VMEM((N_KV_HEADS, BLOCK + seq_tile, KV_WIDTH), bf16),
            pltpu.VMEM((N_KV_HEADS, BLOCK + seq_tile, KV_WIDTH), bf16),
            pltpu.VMEM((seq_tile, SGU_WIDTH), f32),
            pltpu.VMEM((seq_tile, SGU_WIDTH), bf16),
            pltpu.VMEM((seq_tile, SGU_WIDTH), f32),
            pltpu.VMEM((seq_tile, D_MODEL), bf16),
            pltpu.VMEM((N_SGU_HEADS // 2, BLOCK, 2 * BLOCK), bf16),
            pltpu.VMEM((BLOCK, SGU_WIDTH), f32),
        ],
        compiler_params=pltpu.CompilerParams(
            dimension_semantics=("arbitrary", "arbitrary"),
            vmem_limit_bytes=V7X_VMEM_LIMIT_BYTES),
        name="hybrid_layer",
    )(sinks, x, norm_g, w_in, b_in, ln_g, ln_b, sgu_w, sgu_bt, w_out, b_out, final_g)


def kernel(x, norm_g, w_in, b_in, attn_sinks, sgu_ln_g, sgu_ln_b, sgu_w, sgu_b, w_out, b_out, final_norm_g):
    depth = norm_g.shape[0]
    bf16 = jnp.bfloat16
    for l in range(depth):
        last = l == depth - 1
        assert last, "the fused call applies the final norm; only depth 1 is supported"
        x = _layer_call(
            x, attn_sinks[l], norm_g[l][None, :], w_in[l].astype(bf16), b_in[l][None, :],
            sgu_ln_g[l][None, :], sgu_ln_b[l][None, :], sgu_w[l], sgu_b[l].T,
            w_out[l].astype(bf16), b_out[l][None, :], final_norm_g[None, :])
    return x
```

```python
import functools

import jax
import jax.numpy as jnp
from jax import lax
from jax.experimental import pallas as pl
from jax.experimental.pallas import tpu as pltpu

D_MODEL = 1024
HEAD_DIM = 64
ATTN_WIDTH = 512
KV_WIDTH = 128
SGU_WIDTH = 512
N_KV_HEADS = 2
Q_PER_KV = 4
N_SGU_HEADS = 8
BLOCK = 128
NORM_EPS = 1e-5
NEG_INF = -1e30
ATTN_SCALE = HEAD_DIM ** -0.5

OFF_Q = 0
OFF_K = OFF_Q + ATTN_WIDTH
OFF_V = OFF_K + KV_WIDTH
OFF_ZA = OFF_V + KV_WIDTH
OFF_U = OFF_ZA + ATTN_WIDTH
OFF_VS = OFF_U + SGU_WIDTH
OFF_ZS = OFF_VS + SGU_WIDTH
IN_WIDTH = OFF_ZS + SGU_WIDTH

SEQ_TILE = 512
MXU_TILE = 256
ROW_CHUNK = 256
V7X_VMEM_LIMIT_BYTES = 56 * 1024 * 1024

_SQRT_HALF = 0.7071067811865476


def _silu(z):
    return z * (1.0 / (1.0 + jnp.exp(-z)))


def _gelu_exact(z):
    return 0.5 * z * (1.0 + lax.erf(z * _SQRT_HALF))


def _layer_kernel(sinks_ref, x_ref, xn_ref, ng_ref, win_ref, bin_ref, lng_ref, lnb_ref,
                  sw_ref, sbt_ref, wout_ref, bout_ref, fg_ref, o_ref,
                  h_s, sc_s, q_s, kd_s, vd_s, u_s, vl_s, gs_s, mix_s, wp_s, sb_s, *, seq_tile, tiles_per_seq):
    f32, bf16 = jnp.float32, jnp.bfloat16
    n_sub = seq_tile // BLOCK
    t_idx = pl.program_id(0)
    b_idx = t_idx // tiles_per_seq
    s_idx = t_idx % tiles_per_seq

    lane = lax.broadcasted_iota(jnp.int32, (BLOCK, BLOCK), 1)
    low_half = lane < HEAD_DIM

    @pl.when((b_idx == 0) & (s_idx == 0))
    def _():
        row = lax.broadcasted_iota(jnp.int32, (BLOCK, BLOCK), 0)
        for p in range(N_SGU_HEADS // 2):
            for half in range(2):
                w = jnp.where(row >= lane, sw_ref[2 * p + half], 0.0)
                wp_s[p, :, half * BLOCK:(half + 1) * BLOCK] = w.astype(bf16)
        sbt = sbt_ref[...]
        for hh in range(N_SGU_HEADS):
            sb_s[:, hh * HEAD_DIM:(hh + 1) * HEAD_DIM] = jnp.broadcast_to(
                sbt[:, hh:hh + 1], (BLOCK, HEAD_DIM))

    @pl.when(s_idx == 0)
    def _():
        for hh in range(N_KV_HEADS):
            kd_s[hh, 0:BLOCK, :] = jnp.zeros((BLOCK, KV_WIDTH), bf16)
            vd_s[hh, 0:BLOCK, :] = jnp.zeros((BLOCK, KV_WIDTH), bf16)

    @pl.when(s_idx > 0)
    def _():
        for hh in range(N_KV_HEADS):
            kd_s[hh, 0:BLOCK, :] = kd_s[hh, seq_tile:seq_tile + BLOCK, :]
            vd_s[hh, 0:BLOCK, :] = vd_s[hh, seq_tile:seq_tile + BLOCK, :]

    for r0 in range(0, seq_tile, ROW_CHUNK):
        x = x_ref[0, r0:r0 + ROW_CHUNK, :]
        ms = jnp.mean(x * x, axis=-1, keepdims=True)
        hc = ((x * lax.rsqrt(ms + NORM_EPS)) * ng_ref[...]).astype(bf16)
        h_s[r0:r0 + ROW_CHUNK, :] = hc
        q = jnp.dot(hc, win_ref[:, OFF_Q:OFF_Q + ATTN_WIDTH], preferred_element_type=f32)
        q_s[r0:r0 + ROW_CHUNK, :] = ((q + bin_ref[:, OFF_Q:OFF_Q + ATTN_WIDTH]) * ATTN_SCALE).astype(bf16)
    half_t = seq_tile // 2

    def proj_tile(off):
        cols = slice(off, off + MXU_TILE)
        return jnp.concatenate(
            [jnp.dot(h_s[i * half_t:(i + 1) * half_t, :], win_ref[:, cols], preferred_element_type=f32)
             for i in range(2)], axis=0) + bin_ref[:, cols]

    kv = proj_tile(OFF_K)
    half_mask = lax.broadcasted_iota(jnp.int32, (seq_tile, KV_WIDTH), 1) < HEAD_DIM
    for dup_ref, off in ((kd_s, 0), (vd_s, KV_WIDTH)):
        t = kv[:, off:off + KV_WIDTH]
        t_sw = pltpu.roll(t, HEAD_DIM, axis=1)
        dup_ref[0, BLOCK:BLOCK + seq_tile, :] = jnp.where(half_mask, t, t_sw).astype(bf16)
        dup_ref[1, BLOCK:BLOCK + seq_tile, :] = jnp.where(half_mask, t_sw, t).astype(bf16)

    ga_t = [_silu(proj_tile(OFF_ZA))]

    row = lax.broadcasted_iota(jnp.int32, (BLOCK, BLOCK), 0)
    prev_side = lane > row
    prev_side4 = jnp.concatenate([prev_side] * Q_PER_KV, axis=0)
    has_prev = s_idx > 0
    units = [(j, hh) for j in range(n_sub) for hh in range(N_KV_HEADS)]
    for u_idx, (j, hh) in enumerate(units):
        rows = slice(j * BLOCK, (j + 1) * BLOCK)
        parts = []
        for g in range(Q_PER_KV):
            c = hh * (Q_PER_KV // 2) + g // 2
            qc = q_s[rows, c * BLOCK:(c + 1) * BLOCK]
            keep = low_half if g % 2 == 0 else jnp.logical_not(low_half)
            parts.append(jnp.where(keep, qc, jnp.zeros_like(qc)))
        qst = jnp.concatenate(parts, axis=0)
        kb = kd_s[hh, j * BLOCK:(j + 2) * BLOCK, :]
        sc = lax.dot_general(qst, kb, (((1,), (1,)), ((), ())),
                             preferred_element_type=f32)
        s_prev = sc[:, 0:BLOCK]
        if j == 0:
            s_prev = jnp.where(has_prev, s_prev, NEG_INF)
        sc_s[u_idx] = jnp.where(prev_side4, s_prev, sc[:, BLOCK:2 * BLOCK])

    def attend(u):
        j, hh = units[u]
        probs, inv_denoms = [], []
        for g in range(Q_PER_KV):
            sg = sc_s[u, g * BLOCK:(g + 1) * BLOCK, :]
            sink = sinks_ref[hh * Q_PER_KV + g]
            m = jnp.maximum(jnp.max(sg, axis=-1, keepdims=True), sink)
            p = jnp.exp(sg - m)
            denom = jnp.sum(p, axis=-1, keepdims=True) + jnp.exp(sink - m)
            inv_denoms.append(1.0 / denom)
            pb = p.astype(bf16)
            zero = jnp.zeros_like(pb)
            probs.append(jnp.concatenate([jnp.where(prev_side, pb, zero),
                                          jnp.where(prev_side, zero, pb)], axis=1))
        probs = jnp.concatenate(probs, axis=0)
        vb = vd_s[hh, j * BLOCK:(j + 2) * BLOCK, :]
        o = jnp.dot(probs, vb, preferred_element_type=f32)
        o = [o[g * BLOCK:(g + 1) * BLOCK, :] * inv_denoms[g] for g in range(Q_PER_KV)]
        return [jnp.where(low_half, o[2 * c2], o[2 * c2 + 1]) for c2 in range(Q_PER_KV // 2)]

    attn_out = []
    n_slots = (IN_WIDTH - OFF_ZA) // MXU_TILE - 1
    units_after = [len(units) * (i + 1) // n_slots - len(units) * i // n_slots for i in range(n_slots)]
    slot = iter(units_after)

    def attend_some():
        for _ in range(next(slot)):
            attn_out.append(attend(len(attn_out)))

    for n in range(1, ATTN_WIDTH // MXU_TILE):
        ga_t.append(_silu(proj_tile(OFF_ZA + n * MXU_TILE)))
        attend_some()
    for n in range(SGU_WIDTH // MXU_TILE):
        u_s[:, n * MXU_TILE:(n + 1) * MXU_TILE] = _gelu_exact(proj_tile(OFF_U + n * MXU_TILE))
        attend_some()
    vg = []
    for n in range(SGU_WIDTH // MXU_TILE):
        vg.append(_gelu_exact(proj_tile(OFF_VS + n * MXU_TILE)))
        attend_some()
    vg = jnp.concatenate(vg, axis=1)
    mu = jnp.mean(vg, axis=-1, keepdims=True)
    vc = vg - mu
    var = jnp.mean(vc * vc, axis=-1, keepdims=True)
    vl_s[...] = ((vc * lax.rsqrt(var + NORM_EPS)) * lng_ref[...] + lnb_ref[...]).astype(bf16)
    for n in range(SGU_WIDTH // MXU_TILE):
        gs_s[:, n * MXU_TILE:(n + 1) * MXU_TILE] = _silu(proj_tile(OFF_ZS + n * MXU_TILE))
        attend_some()
    assert len(attn_out) == len(units)

    ga = jnp.concatenate(ga_t, axis=1)
    for u, (j, hh) in enumerate(units):
        rows = slice(j * BLOCK, (j + 1) * BLOCK)
        for c2 in range(Q_PER_KV // 2):
            cols = slice((hh * (Q_PER_KV // 2) + c2) * BLOCK, (hh * (Q_PER_KV // 2) + c2 + 1) * BLOCK)
            mix_s[rows, cols] = (attn_out[u][c2] * ga[rows, cols]).astype(bf16)

    for j in range(n_sub):
        rows = slice(j * BLOCK, (j + 1) * BLOCK)
        for p_idx in range(N_SGU_HEADS // 2):
            cols = slice(p_idx * BLOCK, (p_idx + 1) * BLOCK)
            vp = vl_s[rows, cols]
            zero = jnp.zeros_like(vp)
            rhs = jnp.concatenate([jnp.where(low_half, vp, zero),
                                   jnp.where(low_half, zero, vp)], axis=0)
            mixed = jnp.dot(wp_s[p_idx], rhs, preferred_element_type=f32) + sb_s[:, cols]
            mix_s[rows, ATTN_WIDTH + p_idx * BLOCK:ATTN_WIDTH + (p_idx + 1) * BLOCK] = (
                (u_s[rows, cols] * mixed) * gs_s[rows, cols]).astype(bf16)

    y = x_ref[0] + jnp.dot(mix_s[...], wout_ref[...], preferred_element_type=f32) + bout_ref[...]
    ms2 = jnp.mean(y * y, axis=-1, keepdims=True)
    o_ref[0] = (y * lax.rsqrt(ms2 + NORM_EPS)) * fg_ref[...]


def _layer_call(x, sinks, norm_g, w_in, b_in, ln_g, ln_b, sgu_w, sgu_bt, w_out, b_out, final_g,
                *, seq_tile=SEQ_TILE):
    batch, seq, d_model = x.shape
    assert d_model == D_MODEL and seq % seq_tile == 0 and seq_tile % (2 * BLOCK) == 0
    f32, bf16 = jnp.float32, jnp.bfloat16

    tiles_per_seq = seq // seq_tile
    n_tiles = batch * tiles_per_seq

    def full(shape):
        return pl.BlockSpec(shape, lambda t: (0,) * len(shape))

    def tile_index(t):
        return (t // tiles_per_seq, t % tiles_per_seq, 0)

    tile_spec = pl.BlockSpec((1, seq_tile, D_MODEL), lambda t: tile_index(t))
    next_spec = pl.BlockSpec((1, seq_tile, D_MODEL), lambda t: tile_index(jnp.minimum(t + 1, n_tiles - 1)))
    return pl.pallas_call(
        functools.partial(_layer_kernel, seq_tile=seq_tile, tiles_per_seq=tiles_per_seq),
        grid=(n_tiles,),
        in_specs=[
            pl.BlockSpec(memory_space=pltpu.SMEM),
            tile_spec,
            next_spec,
            full((1, D_MODEL)),
            full((D_MODEL, IN_WIDTH)),
            full((1, IN_WIDTH)),
            full((1, SGU_WIDTH)),
            full((1, SGU_WIDTH)),
            full((N_SGU_HEADS, BLOCK, BLOCK)),
            full((BLOCK, N_SGU_HEADS)),
            full((D_MODEL, D_MODEL)),
            full((1, D_MODEL)),
            full((1, D_MODEL)),
        ],
        out_specs=tile_spec,
        out_shape=jax.ShapeDtypeStruct(x.shape, x.dtype),
        scratch_shapes=[
            pltpu.VMEM((seq_tile, D_MODEL), bf16),
            pltpu.VMEM(((seq_tile // BLOCK) * N_KV_HEADS, Q_PER_KV * BLOCK, BLOCK), f32),
            pltpu.VMEM((seq_tile, ATTN_WIDTH), bf16),
            pltpu.VMEM((N_KV_HEADS, BLOCK + seq_tile, KV_WIDTH), bf16),
            pltpu.VMEM((N_KV_HEADS, BLOCK + seq_tile, KV_WIDTH), bf16),
            pltpu.VMEM((seq_tile, SGU_WIDTH), f32),
            pltpu.VMEM((seq_tile, SGU_WIDTH), bf16),
            pltpu.VMEM((seq_tile, SGU_WIDTH), f32),
            pltpu.VMEM((seq_tile, D_MODEL), bf16),
            pltpu.VMEM((N_SGU_HEADS // 2, BLOCK, 2 * BLOCK), bf16),
            pltpu.VMEM((BLOCK, SGU_WIDTH), f32),
        ],
        compiler_params=pltpu.CompilerParams(
            dimension_semantics=("arbitrary",),
            vmem_limit_bytes=V7X_VMEM_LIMIT_BYTES),
        name="hybrid_layer",
    )(sinks, x, x, norm_g, w_in, b_in, ln_g, ln_b, sgu_w, sgu_bt, w_out, b_out, final_g)


def kernel(x, norm_g, w_in, b_in, attn_sinks, sgu_ln_g, sgu_ln_b, sgu_w, sgu_b, w_out, b_out, final_norm_g):
    depth = norm_g.shape[0]
    bf16 = jnp.bfloat16
    for l in range(depth):
        last = l == depth - 1
        assert last, "the fused call applies the final norm; only depth 1 is supported"
        x = _layer_call(
            x, attn_sinks[l], norm_g[l][None, :], w_in[l].astype(bf16), b_in[l][None, :],
            sgu_ln_g[l][None, :], sgu_ln_b[l][None, :], sgu_w[l], sgu_b[l].T,
            w_out[l].astype(bf16), b_out[l][None, :], final_norm_g[None, :])
    return x
```

```python
import functools

import jax
import jax.numpy as jnp
from jax import lax
from jax.experimental import pallas as pl
from jax.experimental.pallas import tpu as pltpu

D_MODEL = 1024
HEAD_DIM = 64
ATTN_WIDTH = 512
KV_WIDTH = 128
SGU_WIDTH = 512
N_KV_HEADS = 2
Q_PER_KV = 4
N_SGU_HEADS = 8
BLOCK = 128
NORM_EPS = 1e-5
NEG_INF = -1e30
ATTN_SCALE = HEAD_DIM ** -0.5

OFF_Q = 0
OFF_K = OFF_Q + ATTN_WIDTH
OFF_V = OFF_K + KV_WIDTH
OFF_ZA = OFF_V + KV_WIDTH
OFF_U = OFF_ZA + ATTN_WIDTH
OFF_VS = OFF_U + SGU_WIDTH
OFF_ZS = OFF_VS + SGU_WIDTH
IN_WIDTH = OFF_ZS + SGU_WIDTH

SEQ_TILE = 1024
MXU_TILE = 256
ROW_CHUNK = 256
W_CHUNK = 128
V7X_VMEM_LIMIT_BYTES = 56 * 1024 * 1024

_SQRT_HALF = 0.7071067811865476


def _silu(z):
    return z * (1.0 / (1.0 + jnp.exp(-z)))


def _gelu_exact(z):
    return 0.5 * z * (1.0 + lax.erf(z * _SQRT_HALF))


def _layer_kernel(sinks_ref, x_ref, ng_ref, win_hbm, bin_ref, lng_ref, lnb_ref,
                  sw_ref, sbt_ref, wout_hbm, bout_ref, fg_ref, o_ref,
                  win_ref, wout_ref, stage_in, stage_out, sem_in, sem_out,
                  q_s, kd_s, vd_s, u_s, vl_s, gs_s, mix_s, wp_s, sb_s, *, seq_tile):
    f32, bf16 = jnp.float32, jnp.bfloat16
    n_sub = seq_tile // BLOCK
    b_idx = pl.program_id(0)
    s_idx = pl.program_id(1)

    lane = lax.broadcasted_iota(jnp.int32, (BLOCK, BLOCK), 1)
    low_half = lane < HEAD_DIM

    @pl.when((b_idx == 0) & (s_idx == 0))
    def _():
        row = lax.broadcasted_iota(jnp.int32, (BLOCK, BLOCK), 0)
        for p in range(N_SGU_HEADS // 2):
            for half in range(2):
                w = jnp.where(row >= lane, sw_ref[2 * p + half], 0.0)
                wp_s[p, :, half * BLOCK:(half + 1) * BLOCK] = w.astype(bf16)
        sbt = sbt_ref[...]
        for hh in range(N_SGU_HEADS):
            sb_s[:, hh * HEAD_DIM:(hh + 1) * HEAD_DIM] = jnp.broadcast_to(
                sbt[:, hh:hh + 1], (BLOCK, HEAD_DIM))
        for hbm, stage, sem, dst in ((win_hbm, stage_in, sem_in, win_ref),
                                    (wout_hbm, stage_out, sem_out, wout_ref)):
            def chunk_copy(c, hbm=hbm, stage=stage, sem=sem):
                return pltpu.make_async_copy(hbm.at[pl.ds(c * W_CHUNK, W_CHUNK), :],
                                             stage.at[c % 2], sem.at[c % 2])
            n_chunks = D_MODEL // W_CHUNK
            chunk_copy(0).start()
            for c in range(n_chunks):
                if c + 1 < n_chunks:
                    chunk_copy(c + 1).start()
                chunk_copy(c).wait()
                dst[c * W_CHUNK:(c + 1) * W_CHUNK, :] = stage[c % 2].astype(bf16)

    @pl.when(s_idx == 0)
    def _():
        for hh in range(N_KV_HEADS):
            kd_s[hh, 0:BLOCK, :] = jnp.zeros((BLOCK, KV_WIDTH), bf16)
            vd_s[hh, 0:BLOCK, :] = jnp.zeros((BLOCK, KV_WIDTH), bf16)

    @pl.when(s_idx > 0)
    def _():
        for hh in range(N_KV_HEADS):
            kd_s[hh, 0:BLOCK, :] = kd_s[hh, seq_tile:seq_tile + BLOCK, :]
            vd_s[hh, 0:BLOCK, :] = vd_s[hh, seq_tile:seq_tile + BLOCK, :]

    h_chunks = []
    for r0 in range(0, seq_tile, ROW_CHUNK):
        x = x_ref[0, r0:r0 + ROW_CHUNK, :]
        ms = jnp.mean(x * x, axis=-1, keepdims=True)
        hc = ((x * lax.rsqrt(ms + NORM_EPS)) * ng_ref[...]).astype(bf16)
        h_chunks.append(hc)
        q = jnp.dot(hc, win_ref[:, OFF_Q:OFF_Q + ATTN_WIDTH], preferred_element_type=f32)
        q_s[r0:r0 + ROW_CHUNK, :] = ((q + bin_ref[:, OFF_Q:OFF_Q + ATTN_WIDTH]) * ATTN_SCALE).astype(bf16)
    h = jnp.concatenate(h_chunks, axis=0)
    half_t = seq_tile // 2

    def proj_tile(off):
        cols = slice(off, off + MXU_TILE)
        return jnp.concatenate(
            [jnp.dot(h[i * half_t:(i + 1) * half_t], win_ref[:, cols], preferred_element_type=f32)
             for i in range(2)], axis=0) + bin_ref[:, cols]

    kv = proj_tile(OFF_K)
    half_mask = lax.broadcasted_iota(jnp.int32, (seq_tile, KV_WIDTH), 1) < HEAD_DIM
    for dup_ref, off in ((kd_s, 0), (vd_s, KV_WIDTH)):
        t = kv[:, off:off + KV_WIDTH]
        t_sw = pltpu.roll(t, HEAD_DIM, axis=1)
        dup_ref[0, BLOCK:BLOCK + seq_tile, :] = jnp.where(half_mask, t, t_sw).astype(bf16)
        dup_ref[1, BLOCK:BLOCK + seq_tile, :] = jnp.where(half_mask, t_sw, t).astype(bf16)

    ga_t = [_silu(proj_tile(OFF_ZA))]

    row = lax.broadcasted_iota(jnp.int32, (BLOCK, BLOCK), 0)
    prev_side = lane > row
    prev_side4 = jnp.concatenate([prev_side] * Q_PER_KV, axis=0)
    has_prev = s_idx > 0
    units = [(j, hh) for j in range(n_sub) for hh in range(N_KV_HEADS)]
    scores = []
    for j, hh in units:
        rows = slice(j * BLOCK, (j + 1) * BLOCK)
        parts = []
        for g in range(Q_PER_KV):
            c = hh * (Q_PER_KV // 2) + g // 2
            qc = q_s[rows, c * BLOCK:(c + 1) * BLOCK]
            keep = low_half if g % 2 == 0 else jnp.logical_not(low_half)
            parts.append(jnp.where(keep, qc, jnp.zeros_like(qc)))
        qst = jnp.concatenate(parts, axis=0)
        kb = kd_s[hh, j * BLOCK:(j + 2) * BLOCK, :]
        sc = lax.dot_general(qst, kb, (((1,), (1,)), ((), ())),
                             preferred_element_type=f32)
        s_prev = sc[:, 0:BLOCK]
        if j == 0:
            s_prev = jnp.where(has_prev, s_prev, NEG_INF)
        scores.append(jnp.where(prev_side4, s_prev, sc[:, BLOCK:2 * BLOCK]))

    def attend(u):
        j, hh = units[u]
        sc = scores[u]
        probs, inv_denoms = [], []
        for g in range(Q_PER_KV):
            sg = sc[g * BLOCK:(g + 1) * BLOCK, :]
            sink = sinks_ref[hh * Q_PER_KV + g]
            m = jnp.maximum(jnp.max(sg, axis=-1, keepdims=True), sink)
            p = jnp.exp(sg - m)
            denom = jnp.sum(p, axis=-1, keepdims=True) + jnp.exp(sink - m)
            inv_denoms.append(1.0 / denom)
            pb = p.astype(bf16)
            zero = jnp.zeros_like(pb)
            probs.append(jnp.concatenate([jnp.where(prev_side, pb, zero),
                                          jnp.where(prev_side, zero, pb)], axis=1))
        probs = jnp.concatenate(probs, axis=0)
        vb = vd_s[hh, j * BLOCK:(j + 2) * BLOCK, :]
        o = jnp.dot(probs, vb, preferred_element_type=f32)
        o = [o[g * BLOCK:(g + 1) * BLOCK, :] * inv_denoms[g] for g in range(Q_PER_KV)]
        return [jnp.where(low_half, o[2 * c2], o[2 * c2 + 1]) for c2 in range(Q_PER_KV // 2)]

    attn_out = []
    n_slots = (IN_WIDTH - OFF_ZA) // MXU_TILE - 1
    units_after = [len(units) * (i + 1) // n_slots - len(units) * i // n_slots for i in range(n_slots)]
    slot = iter(units_after)

    def attend_some():
        for _ in range(next(slot)):
            attn_out.append(attend(len(attn_out)))

    for n in range(1, ATTN_WIDTH // MXU_TILE):
        ga_t.append(_silu(proj_tile(OFF_ZA + n * MXU_TILE)))
        attend_some()
    for n in range(SGU_WIDTH // MXU_TILE):
        u_s[:, n * MXU_TILE:(n + 1) * MXU_TILE] = _gelu_exact(proj_tile(OFF_U + n * MXU_TILE))
        attend_some()
    vg = []
    for n in range(SGU_WIDTH // MXU_TILE):
        vg.append(_gelu_exact(proj_tile(OFF_VS + n * MXU_TILE)))
        attend_some()
    vg = jnp.concatenate(vg, axis=1)
    mu = jnp.mean(vg, axis=-1, keepdims=True)
    vc = vg - mu
    var = jnp.mean(vc * vc, axis=-1, keepdims=True)
    vl_s[...] = ((vc * lax.rsqrt(var + NORM_EPS)) * lng_ref[...] + lnb_ref[...]).astype(bf16)
    for n in range(SGU_WIDTH // MXU_TILE):
        gs_s[:, n * MXU_TILE:(n + 1) * MXU_TILE] = _silu(proj_tile(OFF_ZS + n * MXU_TILE))
        attend_some()
    assert len(attn_out) == len(units)

    ga = jnp.concatenate(ga_t, axis=1)
    for u, (j, hh) in enumerate(units):
        rows = slice(j * BLOCK, (j + 1) * BLOCK)
        for c2 in range(Q_PER_KV // 2):
            cols = slice((hh * (Q_PER_KV // 2) + c2) * BLOCK, (hh * (Q_PER_KV // 2) + c2 + 1) * BLOCK)
            mix_s[rows, cols] = (attn_out[u][c2] * ga[rows, cols]).astype(bf16)

    for j in range(n_sub):
        rows = slice(j * BLOCK, (j + 1) * BLOCK)
        for p_idx in range(N_SGU_HEADS // 2):
            cols = slice(p_idx * BLOCK, (p_idx + 1) * BLOCK)
            vp = vl_s[rows, cols]
            zero = jnp.zeros_like(vp)
            rhs = jnp.concatenate([jnp.where(low_half, vp, zero),
                                   jnp.where(low_half, zero, vp)], axis=0)
            mixed = jnp.dot(wp_s[p_idx], rhs, preferred_element_type=f32) + sb_s[:, cols]
            mix_s[rows, ATTN_WIDTH + p_idx * BLOCK:ATTN_WIDTH + (p_idx + 1) * BLOCK] = (
                (u_s[rows, cols] * mixed) * gs_s[rows, cols]).astype(bf16)

    y = x_ref[0] + jnp.dot(mix_s[...], wout_ref[...], preferred_element_type=f32) + bout_ref[...]
    ms2 = jnp.mean(y * y, axis=-1, keepdims=True)
    o_ref[0] = (y * lax.rsqrt(ms2 + NORM_EPS)) * fg_ref[...]


def _layer_call(x, sinks, norm_g, w_in, b_in, ln_g, ln_b, sgu_w, sgu_bt, w_out, b_out, final_g,
                *, seq_tile=SEQ_TILE):
    batch, seq, d_model = x.shape
    assert d_model == D_MODEL and seq % seq_tile == 0 and seq_tile % (2 * BLOCK) == 0
    f32, bf16 = jnp.float32, jnp.bfloat16

    def full(shape):
        return pl.BlockSpec(shape, lambda b, s: (0,) * len(shape))

    tile_spec = pl.BlockSpec((1, seq_tile, D_MODEL), lambda b, s: (b, s, 0))
    return pl.pallas_call(
        functools.partial(_layer_kernel, seq_tile=seq_tile),
        grid=(batch, seq // seq_tile),
        in_specs=[
            pl.BlockSpec(memory_space=pltpu.SMEM),
            tile_spec,
            full((1, D_MODEL)),
            pl.BlockSpec(memory_space=pl.ANY),
            full((1, IN_WIDTH)),
            full((1, SGU_WIDTH)),
            full((1, SGU_WIDTH)),
            full((N_SGU_HEADS, BLOCK, BLOCK)),
            full((BLOCK, N_SGU_HEADS)),
            pl.BlockSpec(memory_space=pl.ANY),
            full((1, D_MODEL)),
            full((1, D_MODEL)),
        ],
        out_specs=tile_spec,
        out_shape=jax.ShapeDtypeStruct(x.shape, x.dtype),
        scratch_shapes=[
            pltpu.VMEM((D_MODEL, IN_WIDTH), bf16),
            pltpu.VMEM((D_MODEL, D_MODEL), bf16),
            pltpu.VMEM((2, W_CHUNK, IN_WIDTH), f32),
            pltpu.VMEM((2, W_CHUNK, D_MODEL), f32),
            pltpu.SemaphoreType.DMA((2,)),
            pltpu.SemaphoreType.DMA((2,)),
            pltpu.VMEM((seq_tile, ATTN_WIDTH), bf16),
            pltpu.VMEM((N_KV_HEADS, BLOCK + seq_tile, KV_WIDTH), bf16),
            pltpu.VMEM((N_KV_HEADS, BLOCK + seq_tile, KV_WIDTH), bf16),
            pltpu.VMEM((seq_tile, SGU_WIDTH), f32),
            pltpu.VMEM((seq_tile, SGU_WIDTH), bf16),
            pltpu.VMEM((seq_tile, SGU_WIDTH), f32),
            pltpu.VMEM((seq_tile, D_MODEL), bf16),
            pltpu.VMEM((N_SGU_HEADS // 2, BLOCK, 2 * BLOCK), bf16),
            pltpu.VMEM((BLOCK, SGU_WIDTH), f32),
        ],
        compiler_params=pltpu.CompilerParams(
            dimension_semantics=("arbitrary", "arbitrary"),
            vmem_limit_bytes=V7X_VMEM_LIMIT_BYTES),
        name="hybrid_layer",
    )(sinks, x, norm_g, w_in, b_in, ln_g, ln_b, sgu_w, sgu_bt, w_out, b_out, final_g)


def kernel(x, norm_g, w_in, b_in, attn_sinks, sgu_ln_g, sgu_ln_b, sgu_w, sgu_b, w_out, b_out, final_norm_g):
    depth = norm_g.shape[0]
    bf16 = jnp.bfloat16
    for l in range(depth):
        last = l == depth - 1
        assert last, "the fused call applies the final norm; only depth 1 is supported"
        x = _layer_call(
            x, attn_sinks[l], norm_g[l][None, :], w_in[l], b_in[l][None, :],
            sgu_ln_g[l][None, :], sgu_ln_b[l][None, :], sgu_w[l], sgu_b[l].T,
            w_out[l], b_out[l][None, :], final_norm_g[None, :])
    return x
```

```python
import functools

import jax
import jax.numpy as jnp
from jax import lax
from jax.experimental import pallas as pl
from jax.experimental.pallas import tpu as pltpu

D_MODEL = 1024
HEAD_DIM = 64
ATTN_WIDTH = 512
KV_WIDTH = 128
SGU_WIDTH = 512
N_KV_HEADS = 2
Q_PER_KV = 4
N_SGU_HEADS = 8
BLOCK = 128
NORM_EPS = 1e-5
NEG_INF = -1e30
ATTN_SCALE = HEAD_DIM ** -0.5

OFF_Q = 0
OFF_K = OFF_Q + ATTN_WIDTH
OFF_V = OFF_K + KV_WIDTH
OFF_ZA = OFF_V + KV_WIDTH
OFF_U = OFF_ZA + ATTN_WIDTH
OFF_VS = OFF_U + SGU_WIDTH
OFF_ZS = OFF_VS + SGU_WIDTH
IN_WIDTH = OFF_ZS + SGU_WIDTH

SEQ_TILE = 1024
MXU_TILE = 256
ROW_CHUNK = 256
W_CHUNK = 128
W_SLOTS = 4
V7X_VMEM_LIMIT_BYTES = 56 * 1024 * 1024

_SQRT_HALF = 0.7071067811865476


def _silu(z):
    return z * (1.0 / (1.0 + jnp.exp(-z)))


def _gelu_exact(z):
    return 0.5 * z * (1.0 + lax.erf(z * _SQRT_HALF))


def _layer_kernel(sinks_ref, x_ref, ng_ref, win_hbm, bin_ref, lng_ref, lnb_ref,
                  sw_ref, sbt_ref, wout_hbm, bout_ref, fg_ref, o_ref,
                  win_ref, wout_ref, stage_in, stage_out, sem_in, sem_out,
                  q_s, kd_s, vd_s, u_s, vl_s, gs_s, mix_s, wp_s, sb_s, *, seq_tile):
    f32, bf16 = jnp.float32, jnp.bfloat16
    n_sub = seq_tile // BLOCK
    b_idx = pl.program_id(0)
    s_idx = pl.program_id(1)

    lane = lax.broadcasted_iota(jnp.int32, (BLOCK, BLOCK), 1)
    low_half = lane < HEAD_DIM

    @pl.when((b_idx == 0) & (s_idx == 0))
    def _():
        row = lax.broadcasted_iota(jnp.int32, (BLOCK, BLOCK), 0)
        for p in range(N_SGU_HEADS // 2):
            for half in range(2):
                w = jnp.where(row >= lane, sw_ref[2 * p + half], 0.0)
                wp_s[p, :, half * BLOCK:(half + 1) * BLOCK] = w.astype(bf16)
        sbt = sbt_ref[...]
        for hh in range(N_SGU_HEADS):
            sb_s[:, hh * HEAD_DIM:(hh + 1) * HEAD_DIM] = jnp.broadcast_to(
                sbt[:, hh:hh + 1], (BLOCK, HEAD_DIM))
        jobs = ((win_hbm, stage_in, sem_in, win_ref), (wout_hbm, stage_out, sem_out, wout_ref))
        n_chunks = D_MODEL // W_CHUNK

        def chunk_copy(job, c):
            hbm, stage, sem, _ = job
            return pltpu.make_async_copy(hbm.at[pl.ds(c * W_CHUNK, W_CHUNK), :],
                                         stage.at[c % W_SLOTS], sem.at[c % W_SLOTS])

        for job in jobs:
            for c in range(W_SLOTS):
                chunk_copy(job, c).start()
        for job in jobs:
            for c in range(n_chunks):
                chunk_copy(job, c).wait()
                job[3][c * W_CHUNK:(c + 1) * W_CHUNK, :] = job[1][c % W_SLOTS].astype(bf16)
                if c + W_SLOTS < n_chunks:
                    chunk_copy(job, c + W_SLOTS).start()

    @pl.when(s_idx == 0)
    def _():
        for hh in range(N_KV_HEADS):
            kd_s[hh, 0:BLOCK, :] = jnp.zeros((BLOCK, KV_WIDTH), bf16)
            vd_s[hh, 0:BLOCK, :] = jnp.zeros((BLOCK, KV_WIDTH), bf16)

    @pl.when(s_idx > 0)
    def _():
        for hh in range(N_KV_HEADS):
            kd_s[hh, 0:BLOCK, :] = kd_s[hh, seq_tile:seq_tile + BLOCK, :]
            vd_s[hh, 0:BLOCK, :] = vd_s[hh, seq_tile:seq_tile + BLOCK, :]

    h_chunks = []
    for r0 in range(0, seq_tile, ROW_CHUNK):
        x = x_ref[0, r0:r0 + ROW_CHUNK, :]
        ms = jnp.mean(x * x, axis=-1, keepdims=True)
        hc = ((x * lax.rsqrt(ms + NORM_EPS)) * ng_ref[...]).astype(bf16)
        h_chunks.append(hc)
        q = jnp.dot(hc, win_ref[:, OFF_Q:OFF_Q + ATTN_WIDTH], preferred_element_type=f32)
        q_s[r0:r0 + ROW_CHUNK, :] = ((q + bin_ref[:, OFF_Q:OFF_Q + ATTN_WIDTH]) * ATTN_SCALE).astype(bf16)
    h = jnp.concatenate(h_chunks, axis=0)
    half_t = seq_tile // 2

    def proj_tile(off):
        cols = slice(off, off + MXU_TILE)
        return jnp.concatenate(
            [jnp.dot(h[i * half_t:(i + 1) * half_t], win_ref[:, cols], preferred_element_type=f32)
             for i in range(2)], axis=0) + bin_ref[:, cols]

    kv = proj_tile(OFF_K)
    half_mask = lax.broadcasted_iota(jnp.int32, (seq_tile, KV_WIDTH), 1) < HEAD_DIM
    for dup_ref, off in ((kd_s, 0), (vd_s, KV_WIDTH)):
        t = kv[:, off:off + KV_WIDTH]
        t_sw = pltpu.roll(t, HEAD_DIM, axis=1)
        dup_ref[0, BLOCK:BLOCK + seq_tile, :] = jnp.where(half_mask, t, t_sw).astype(bf16)
        dup_ref[1, BLOCK:BLOCK + seq_tile, :] = jnp.where(half_mask, t_sw, t).astype(bf16)

    ga_t = [_silu(proj_tile(OFF_ZA))]

    row = lax.broadcasted_iota(jnp.int32, (BLOCK, BLOCK), 0)
    prev_side = lane > row
    prev_side4 = jnp.concatenate([prev_side] * Q_PER_KV, axis=0)
    has_prev = s_idx > 0
    units = [(j, hh) for j in range(n_sub) for hh in range(N_KV_HEADS)]
    scores = []
    for j, hh in units:
        rows = slice(j * BLOCK, (j + 1) * BLOCK)
        parts = []
        for g in range(Q_PER_KV):
            c = hh * (Q_PER_KV // 2) + g // 2
            qc = q_s[rows, c * BLOCK:(c + 1) * BLOCK]
            keep = low_half if g % 2 == 0 else jnp.logical_not(low_half)
            parts.append(jnp.where(keep, qc, jnp.zeros_like(qc)))
        qst = jnp.concatenate(parts, axis=0)
        kb = kd_s[hh, j * BLOCK:(j + 2) * BLOCK, :]
        sc = lax.dot_general(qst, kb, (((1,), (1,)), ((), ())),
                             preferred_element_type=f32)
        s_prev = sc[:, 0:BLOCK]
        if j == 0:
            s_prev = jnp.where(has_prev, s_prev, NEG_INF)
        scores.append(jnp.where(prev_side4, s_prev, sc[:, BLOCK:2 * BLOCK]))

    def attend(u):
        j, hh = units[u]
        sc = scores[u]
        probs, inv_denoms = [], []
        for g in range(Q_PER_KV):
            sg = sc[g * BLOCK:(g + 1) * BLOCK, :]
            sink = sinks_ref[hh * Q_PER_KV + g]
            m = jnp.maximum(jnp.max(sg, axis=-1, keepdims=True), sink)
            p = jnp.exp(sg - m)
            denom = jnp.sum(p, axis=-1, keepdims=True) + jnp.exp(sink - m)
            inv_denoms.append(1.0 / denom)
            pb = p.astype(bf16)
            zero = jnp.zeros_like(pb)
            probs.append(jnp.concatenate([jnp.where(prev_side, pb, zero),
                                          jnp.where(prev_side, zero, pb)], axis=1))
        probs = jnp.concatenate(probs, axis=0)
        vb = vd_s[hh, j * BLOCK:(j + 2) * BLOCK, :]
        o = jnp.dot(probs, vb, preferred_element_type=f32)
        o = [o[g * BLOCK:(g + 1) * BLOCK, :] * inv_denoms[g] for g in range(Q_PER_KV)]
        return [jnp.where(low_half, o[2 * c2], o[2 * c2 + 1]) for c2 in range(Q_PER_KV // 2)]

    attn_out = []
    n_slots = (IN_WIDTH - OFF_ZA) // MXU_TILE - 1
    units_after = [len(units) * (i + 1) // n_slots - len(units) * i // n_slots for i in range(n_slots)]
    slot = iter(units_after)

    def attend_some():
        for _ in range(next(slot)):
            attn_out.append(attend(len(attn_out)))

    for n in range(1, ATTN_WIDTH // MXU_TILE):
        ga_t.append(_silu(proj_tile(OFF_ZA + n * MXU_TILE)))
        attend_some()
    for n in range(SGU_WIDTH // MXU_TILE):
        u_s[:, n * MXU_TILE:(n + 1) * MXU_TILE] = _gelu_exact(proj_tile(OFF_U + n * MXU_TILE))
        attend_some()
    vg = []
    for n in range(SGU_WIDTH // MXU_TILE):
        vg.append(_gelu_exact(proj_tile(OFF_VS + n * MXU_TILE)))
        attend_some()
    vg = jnp.concatenate(vg, axis=1)
    mu = jnp.mean(vg, axis=-1, keepdims=True)
    vc = vg - mu
    var = jnp.mean(vc * vc, axis=-1, keepdims=True)
    vl_s[...] = ((vc * lax.rsqrt(var + NORM_EPS)) * lng_ref[...] + lnb_ref[...]).astype(bf16)
    for n in range(SGU_WIDTH // MXU_TILE):
        gs_s[:, n * MXU_TILE:(n + 1) * MXU_TILE] = _silu(proj_tile(OFF_ZS + n * MXU_TILE))
        attend_some()
    assert len(attn_out) == len(units)

    ga = jnp.concatenate(ga_t, axis=1)
    for u, (j, hh) in enumerate(units):
        rows = slice(j * BLOCK, (j + 1) * BLOCK)
        for c2 in range(Q_PER_KV // 2):
            cols = slice((hh * (Q_PER_KV // 2) + c2) * BLOCK, (hh * (Q_PER_KV // 2) + c2 + 1) * BLOCK)
            mix_s[rows, cols] = (attn_out[u][c2] * ga[rows, cols]).astype(bf16)

    for j in range(n_sub):
        rows = slice(j * BLOCK, (j + 1) * BLOCK)
        for p_idx in range(N_SGU_HEADS // 2):
            cols = slice(p_idx * BLOCK, (p_idx + 1) * BLOCK)
            vp = vl_s[rows, cols]
            zero = jnp.zeros_like(vp)
            rhs = jnp.concatenate([jnp.where(low_half, vp, zero),
                                   jnp.where(low_half, zero, vp)], axis=0)
            mixed = jnp.dot(wp_s[p_idx], rhs, preferred_element_type=f32) + sb_s[:, cols]
            mix_s[rows, ATTN_WIDTH + p_idx * BLOCK:ATTN_WIDTH + (p_idx + 1) * BLOCK] = (
                (u_s[rows, cols] * mixed) * gs_s[rows, cols]).astype(bf16)

    y = x_ref[0] + jnp.dot(mix_s[...], wout_ref[...], preferred_element_type=f32) + bout_ref[...]
    ms2 = jnp.mean(y * y, axis=-1, keepdims=True)
    o_ref[0] = (y * lax.rsqrt(ms2 + NORM_EPS)) * fg_ref[...]


def _layer_call(x, sinks, norm_g, w_in, b_in, ln_g, ln_b, sgu_w, sgu_bt, w_out, b_out, final_g,
                *, seq_tile=SEQ_TILE):
    batch, seq, d_model = x.shape
    assert d_model == D_MODEL and seq % seq_tile == 0 and seq_tile % (2 * BLOCK) == 0
    f32, bf16 = jnp.float32, jnp.bfloat16

    def full(shape):
        return pl.BlockSpec(shape, lambda b, s: (0,) * len(shape))

    tile_spec = pl.BlockSpec((1, seq_tile, D_MODEL), lambda b, s: (b, s, 0))
    return pl.pallas_call(
        functools.partial(_layer_kernel, seq_tile=seq_tile),
        grid=(batch, seq // seq_tile),
        in_specs=[
            pl.BlockSpec(memory_space=pltpu.SMEM),
            tile_spec,
            full((1, D_MODEL)),
            pl.BlockSpec(memory_space=pl.ANY),
            full((1, IN_WIDTH)),
            full((1, SGU_WIDTH)),
            full((1, SGU_WIDTH)),
            full((N_SGU_HEADS, BLOCK, BLOCK)),
            full((BLOCK, N_SGU_HEADS)),
            pl.BlockSpec(memory_space=pl.ANY),
            full((1, D_MODEL)),
            full((1, D_MODEL)),
        ],
        out_specs=tile_spec,
        out_shape=jax.ShapeDtypeStruct(x.shape, x.dtype),
        scratch_shapes=[
            pltpu.VMEM((D_MODEL, IN_WIDTH), bf16),
            pltpu.VMEM((D_MODEL, D_MODEL), bf16),
            pltpu.VMEM((W_SLOTS, W_CHUNK, IN_WIDTH), f32),
            pltpu.VMEM((W_SLOTS, W_CHUNK, D_MODEL), f32),
            pltpu.SemaphoreType.DMA((W_SLOTS,)),
            pltpu.SemaphoreType.DMA((W_SLOTS,)),
            pltpu.VMEM((seq_tile, ATTN_WIDTH), bf16),
            pltpu.VMEM((N_KV_HEADS, BLOCK + seq_tile, KV_WIDTH), bf16),
            pltpu.VMEM((N_KV_HEADS, BLOCK + seq_tile, KV_WIDTH), bf16),
            pltpu.VMEM((seq_tile, SGU_WIDTH), f32),
            pltpu.VMEM((seq_tile, SGU_WIDTH), bf16),
            pltpu.VMEM((seq_tile, SGU_WIDTH), f32),
            pltpu.VMEM((seq_tile, D_MODEL), bf16),
            pltpu.VMEM((N_SGU_HEADS // 2, BLOCK, 2 * BLOCK), bf16),
            pltpu.VMEM((BLOCK, SGU_WIDTH), f32),
        ],
        compiler_params=pltpu.CompilerParams(
            dimension_semantics=("arbitrary", "arbitrary"),
            vmem_limit_bytes=V7X_VMEM_LIMIT_BYTES),
        name="hybrid_layer",
    )(sinks, x, norm_g, w_in, b_in, ln_g, ln_b, sgu_w, sgu_bt, w_out, b_out, final_g)


def kernel(x, norm_g, w_in, b_in, attn_sinks, sgu_ln_g, sgu_ln_b, sgu_w, sgu_b, w_out, b_out, final_norm_g):
    depth = norm_g.shape[0]
    bf16 = jnp.bfloat16
    for l in range(depth):
        last = l == depth - 1
        assert last, "the fused call applies the final norm; only depth 1 is supported"
        x = _layer_call(
            x, attn_sinks[l], norm_g[l][None, :], w_in[l], b_in[l][None, :],
            sgu_ln_g[l][None, :], sgu_ln_b[l][None, :], sgu_w[l], sgu_b[l].T,
            w_out[l], b_out[l][None, :], final_norm_g[None, :])
    return x
```

```python
import functools

import jax
import jax.numpy as jnp
from jax import lax
from jax.experimental import pallas as pl
from jax.experimental.pallas import tpu as pltpu

D_MODEL = 1024
HEAD_DIM = 64
ATTN_WIDTH = 512
KV_WIDTH = 128
SGU_WIDTH = 512
N_KV_HEADS = 2
Q_PER_KV = 4
N_SGU_HEADS = 8
BLOCK = 128
NORM_EPS = 1e-5
NEG_INF = -1e30
ATTN_SCALE = HEAD_DIM ** -0.5

OFF_Q = 0
OFF_K = OFF_Q + ATTN_WIDTH
OFF_V = OFF_K + KV_WIDTH
OFF_ZA = OFF_V + KV_WIDTH
OFF_U = OFF_ZA + ATTN_WIDTH
OFF_VS = OFF_U + SGU_WIDTH
OFF_ZS = OFF_VS + SGU_WIDTH
IN_WIDTH = OFF_ZS + SGU_WIDTH

SEQ_TILE = 1024
MXU_TILE = 256
ROW_CHUNK = 256
W_CHUNK = 128
W_SLOTS = 4
V7X_VMEM_LIMIT_BYTES = 56 * 1024 * 1024

_SQRT_HALF = 0.7071067811865476


def _silu(z):
    return z * (1.0 / (1.0 + jnp.exp(-z)))


def _gelu_exact(z):
    return 0.5 * z * (1.0 + lax.erf(z * _SQRT_HALF))


def _layer_kernel(sinks_ref, x_ref, ng_ref, win_hbm, bin_ref, lng_ref, lnb_ref,
                  sw_ref, sbt_ref, wout_hbm, bout_ref, fg_ref, o_ref,
                  win_ref, wout_ref, stage_in, stage_out, sem_in, sem_out,
                  q_s, kd_s, vd_s, u_s, vl_s, gs_s, mix_s, wp_s, sb_s, *, seq_tile):
    f32, bf16 = jnp.float32, jnp.bfloat16
    n_sub = seq_tile // BLOCK
    b_idx = pl.program_id(0)
    s_idx = pl.program_id(1)

    lane = lax.broadcasted_iota(jnp.int32, (BLOCK, BLOCK), 1)
    low_half = lane < HEAD_DIM

    @pl.when((b_idx == 0) & (s_idx == 0))
    def _():
        row = lax.broadcasted_iota(jnp.int32, (BLOCK, BLOCK), 0)
        for p in range(N_SGU_HEADS // 2):
            for half in range(2):
                w = jnp.where(row >= lane, sw_ref[2 * p + half], 0.0)
                wp_s[p, :, half * BLOCK:(half + 1) * BLOCK] = w.astype(bf16)
        sbt = sbt_ref[...]
        for hh in range(N_SGU_HEADS):
            sb_s[:, hh * HEAD_DIM:(hh + 1) * HEAD_DIM] = jnp.broadcast_to(
                sbt[:, hh:hh + 1], (BLOCK, HEAD_DIM))
        jobs = ((win_hbm, stage_in, sem_in, win_ref), (wout_hbm, stage_out, sem_out, wout_ref))
        n_chunks = D_MODEL // W_CHUNK

        def chunk_copy(job, c):
            hbm, stage, sem, _ = job
            return pltpu.make_async_copy(hbm.at[pl.ds(c * W_CHUNK, W_CHUNK), :],
                                         stage.at[c % W_SLOTS], sem.at[c % W_SLOTS])

        for job in jobs:
            for c in range(W_SLOTS):
                chunk_copy(job, c).start(priority=c % 2)
        for job in jobs:
            for c in range(n_chunks):
                chunk_copy(job, c).wait()
                job[3][c * W_CHUNK:(c + 1) * W_CHUNK, :] = job[1][c % W_SLOTS].astype(bf16)
                if c + W_SLOTS < n_chunks:
                    chunk_copy(job, c + W_SLOTS).start(priority=c % 2)

    @pl.when(s_idx == 0)
    def _():
        for hh in range(N_KV_HEADS):
            kd_s[hh, 0:BLOCK, :] = jnp.zeros((BLOCK, KV_WIDTH), bf16)
            vd_s[hh, 0:BLOCK, :] = jnp.zeros((BLOCK, KV_WIDTH), bf16)

    @pl.when(s_idx > 0)
    def _():
        for hh in range(N_KV_HEADS):
            kd_s[hh, 0:BLOCK, :] = kd_s[hh, seq_tile:seq_tile + BLOCK, :]
            vd_s[hh, 0:BLOCK, :] = vd_s[hh, seq_tile:seq_tile + BLOCK, :]

    h_chunks = []
    for r0 in range(0, seq_tile, ROW_CHUNK):
        x = x_ref[0, r0:r0 + ROW_CHUNK, :]
        ms = jnp.mean(x * x, axis=-1, keepdims=True)
        hc = ((x * lax.rsqrt(ms + NORM_EPS)) * ng_ref[...]).astype(bf16)
        h_chunks.append(hc)
        q = jnp.dot(hc, win_ref[:, OFF_Q:OFF_Q + ATTN_WIDTH], preferred_element_type=f32)
        q_s[r0:r0 + ROW_CHUNK, :] = ((q + bin_ref[:, OFF_Q:OFF_Q + ATTN_WIDTH]) * ATTN_SCALE).astype(bf16)
    h = jnp.concatenate(h_chunks, axis=0)
    half_t = seq_tile // 2

    def proj_tile(off):
        cols = slice(off, off + MXU_TILE)
        return jnp.concatenate(
            [jnp.dot(h[i * half_t:(i + 1) * half_t], win_ref[:, cols], preferred_element_type=f32)
             for i in range(2)], axis=0) + bin_ref[:, cols]

    kv = proj_tile(OFF_K)
    half_mask = lax.broadcasted_iota(jnp.int32, (seq_tile, KV_WIDTH), 1) < HEAD_DIM
    for dup_ref, off in ((kd_s, 0), (vd_s, KV_WIDTH)):
        t = kv[:, off:off + KV_WIDTH]
        t_sw = pltpu.roll(t, HEAD_DIM, axis=1)
        dup_ref[0, BLOCK:BLOCK + seq_tile, :] = jnp.where(half_mask, t, t_sw).astype(bf16)
        dup_ref[1, BLOCK:BLOCK + seq_tile, :] = jnp.where(half_mask, t_sw, t).astype(bf16)

    ga_t = [_silu(proj_tile(OFF_ZA))]

    row = lax.broadcasted_iota(jnp.int32, (BLOCK, BLOCK), 0)
    prev_side = lane > row
    prev_side4 = jnp.concatenate([prev_side] * Q_PER_KV, axis=0)
    has_prev = s_idx > 0
    units = [(j, hh) for j in range(n_sub) for hh in range(N_KV_HEADS)]
    scores = []
    for j, hh in units:
        rows = slice(j * BLOCK, (j + 1) * BLOCK)
        parts = []
        for g in range(Q_PER_KV):
            c = hh * (Q_PER_KV // 2) + g // 2
            qc = q_s[rows, c * BLOCK:(c + 1) * BLOCK]
            keep = low_half if g % 2 == 0 else jnp.logical_not(low_half)
            parts.append(jnp.where(keep, qc, jnp.zeros_like(qc)))
        qst = jnp.concatenate(parts, axis=0)
        kb = kd_s[hh, j * BLOCK:(j + 2) * BLOCK, :]
        sc = lax.dot_general(qst, kb, (((1,), (1,)), ((), ())),
                             preferred_element_type=f32)
        s_prev = sc[:, 0:BLOCK]
        if j == 0:
            s_prev = jnp.where(has_prev, s_prev, NEG_INF)
        scores.append(jnp.where(prev_side4, s_prev, sc[:, BLOCK:2 * BLOCK]))

    def attend(u):
        j, hh = units[u]
        sc = scores[u]
        probs, inv_denoms = [], []
        for g in range(Q_PER_KV):
            sg = sc[g * BLOCK:(g + 1) * BLOCK, :]
            sink = sinks_ref[hh * Q_PER_KV + g]
            m = jnp.maximum(jnp.max(sg, axis=-1, keepdims=True), sink)
            p = jnp.exp(sg - m)
            denom = jnp.sum(p, axis=-1, keepdims=True) + jnp.exp(sink - m)
            inv_denoms.append(1.0 / denom)
            pb = p.astype(bf16)
            zero = jnp.zeros_like(pb)
            probs.append(jnp.concatenate([jnp.where(prev_side, pb, zero),
                                          jnp.where(prev_side, zero, pb)], axis=1))
        probs = jnp.concatenate(probs, axis=0)
        vb = vd_s[hh, j * BLOCK:(j + 2) * BLOCK, :]
        o = jnp.dot(probs, vb, preferred_element_type=f32)
        o = [o[g * BLOCK:(g + 1) * BLOCK, :] * inv_denoms[g] for g in range(Q_PER_KV)]
        return [jnp.where(low_half, o[2 * c2], o[2 * c2 + 1]) for c2 in range(Q_PER_KV // 2)]

    attn_out = []
    n_slots = (IN_WIDTH - OFF_ZA) // MXU_TILE - 1
    units_after = [len(units) * (i + 1) // n_slots - len(units) * i // n_slots for i in range(n_slots)]
    slot = iter(units_after)

    def attend_some():
        for _ in range(next(slot)):
            attn_out.append(attend(len(attn_out)))

    for n in range(1, ATTN_WIDTH // MXU_TILE):
        ga_t.append(_silu(proj_tile(OFF_ZA + n * MXU_TILE)))
        attend_some()
    for n in range(SGU_WIDTH // MXU_TILE):
        u_s[:, n * MXU_TILE:(n + 1) * MXU_TILE] = _gelu_exact(proj_tile(OFF_U + n * MXU_TILE))
        attend_some()
    vg = []
    for n in range(SGU_WIDTH // MXU_TILE):
        vg.append(_gelu_exact(proj_tile(OFF_VS + n * MXU_TILE)))
        attend_some()
    vg = jnp.concatenate(vg, axis=1)
    mu = jnp.mean(vg, axis=-1, keepdims=True)
    vc = vg - mu
    var = jnp.mean(vc * vc, axis=-1, keepdims=True)
    vl_s[...] = ((vc * lax.rsqrt(var + NORM_EPS)) * lng_ref[...] + lnb_ref[...]).astype(bf16)
    for n in range(SGU_WIDTH // MXU_TILE):
        gs_s[:, n * MXU_TILE:(n + 1) * MXU_TILE] = _silu(proj_tile(OFF_ZS + n * MXU_TILE))
        attend_some()
    assert len(attn_out) == len(units)

    ga = jnp.concatenate(ga_t, axis=1)
    for u, (j, hh) in enumerate(units):
        rows = slice(j * BLOCK, (j + 1) * BLOCK)
        for c2 in range(Q_PER_KV // 2):
            cols = slice((hh * (Q_PER_KV // 2) + c2) * BLOCK, (hh * (Q_PER_KV // 2) + c2 + 1) * BLOCK)
            mix_s[rows, cols] = (attn_out[u][c2] * ga[rows, cols]).astype(bf16)

    for j in range(n_sub):
        rows = slice(j * BLOCK, (j + 1) * BLOCK)
        for p_idx in range(N_SGU_HEADS // 2):
            cols = slice(p_idx * BLOCK, (p_idx + 1) * BLOCK)
            vp = vl_s[rows, cols]
            zero = jnp.zeros_like(vp)
            rhs = jnp.concatenate([jnp.where(low_half, vp, zero),
                                   jnp.where(low_half, zero, vp)], axis=0)
            mixed = jnp.dot(wp_s[p_idx], rhs, preferred_element_type=f32) + sb_s[:, cols]
            mix_s[rows, ATTN_WIDTH + p_idx * BLOCK:ATTN_WIDTH + (p_idx + 1) * BLOCK] = (
                (u_s[rows, cols] * mixed) * gs_s[rows, cols]).astype(bf16)

    y = x_ref[0] + jnp.dot(mix_s[...], wout_ref[...], preferred_element_type=f32) + bout_ref[...]
    ms2 = jnp.mean(y * y, axis=-1, keepdims=True)
    o_ref[0] = (y * lax.rsqrt(ms2 + NORM_EPS)) * fg_ref[...]


def _layer_call(x, sinks, norm_g, w_in, b_in, ln_g, ln_b, sgu_w, sgu_bt, w_out, b_out, final_g,
                *, seq_tile=SEQ_TILE):
    batch, seq, d_model = x.shape
    assert d_model == D_MODEL and seq % seq_tile == 0 and seq_tile % (2 * BLOCK) == 0
    f32, bf16 = jnp.float32, jnp.bfloat16

    def full(shape):
        return pl.BlockSpec(shape, lambda b, s: (0,) * len(shape))

    tile_spec = pl.BlockSpec((1, seq_tile, D_MODEL), lambda b, s: (b, s, 0))
    return pl.pallas_call(
        functools.partial(_layer_kernel, seq_tile=seq_tile),
        grid=(batch, seq // seq_tile),
        in_specs=[
            pl.BlockSpec(memory_space=pltpu.SMEM),
            tile_spec,
            full((1, D_MODEL)),
            pl.BlockSpec(memory_space=pl.ANY),
            full((1, IN_WIDTH)),
            full((1, SGU_WIDTH)),
            full((1, SGU_WIDTH)),
            full((N_SGU_HEADS, BLOCK, BLOCK)),
            full((BLOCK, N_SGU_HEADS)),
            pl.BlockSpec(memory_space=pl.ANY),
            full((1, D_MODEL)),
            full((1, D_MODEL)),
        ],
        out_specs=tile_spec,
        out_shape=jax.ShapeDtypeStruct(x.shape, x.dtype),
        scratch_shapes=[
            pltpu.VMEM((D_MODEL, IN_WIDTH), bf16),
            pltpu.VMEM((D_MODEL, D_MODEL), bf16),
            pltpu.VMEM((W_SLOTS, W_CHUNK, IN_WIDTH), f32),
            pltpu.VMEM((W_SLOTS, W_CHUNK, D_MODEL), f32),
            pltpu.SemaphoreType.DMA((W_SLOTS,)),
            pltpu.SemaphoreType.DMA((W_SLOTS,)),
            pltpu.VMEM((seq_tile, ATTN_WIDTH), bf16),
            pltpu.VMEM((N_KV_HEADS, BLOCK + seq_tile, KV_WIDTH), bf16),
            pltpu.VMEM((N_KV_HEADS, BLOCK + seq_tile, KV_WIDTH), bf16),
            pltpu.VMEM((seq_tile, SGU_WIDTH), f32),
            pltpu.VMEM((seq_tile, SGU_WIDTH), bf16),
            pltpu.VMEM((seq_tile, SGU_WIDTH), f32),
            pltpu.VMEM((seq_tile, D_MODEL), bf16),
            pltpu.VMEM((N_SGU_HEADS // 2, BLOCK, 2 * BLOCK), bf16),
            pltpu.VMEM((BLOCK, SGU_WIDTH), f32),
        ],
        compiler_params=pltpu.CompilerParams(
            dimension_semantics=("arbitrary", "arbitrary"),
            vmem_limit_bytes=V7X_VMEM_LIMIT_BYTES),
        name="hybrid_layer",
    )(sinks, x, norm_g, w_in, b_in, ln_g, ln_b, sgu_w, sgu_bt, w_out, b_out, final_g)


def kernel(x, norm_g, w_in, b_in, attn_sinks, sgu_ln_g, sgu_ln_b, sgu_w, sgu_b, w_out, b_out, final_norm_g):
    depth = norm_g.shape[0]
    for l in range(depth):
        last = l == depth - 1
        assert last, "the fused call applies the final norm; only depth 1 is supported"
        x = _layer_call(
            x, attn_sinks[l], norm_g[l][None, :], w_in[l], b_in[l][None, :],
            sgu_ln_g[l][None, :], sgu_ln_b[l][None, :], sgu_w[l], sgu_b[l].T,
            w_out[l], b_out[l][None, :], final_norm_g[None, :])
    return x
```

```python
import functools

import jax
import jax.numpy as jnp
from jax import lax
from jax.experimental import pallas as pl
from jax.experimental.pallas import tpu as pltpu

D_MODEL = 1024
HEAD_DIM = 64
ATTN_WIDTH = 512
KV_WIDTH = 128
SGU_WIDTH = 512
N_KV_HEADS = 2
Q_PER_KV = 4
N_SGU_HEADS = 8
BLOCK = 128
NORM_EPS = 1e-5
NEG_INF = -1e30
ATTN_SCALE = HEAD_DIM ** -0.5

OFF_Q = 0
OFF_K = OFF_Q + ATTN_WIDTH
OFF_V = OFF_K + KV_WIDTH
OFF_ZA = OFF_V + KV_WIDTH
OFF_U = OFF_ZA + ATTN_WIDTH
OFF_VS = OFF_U + SGU_WIDTH
OFF_ZS = OFF_VS + SGU_WIDTH
IN_WIDTH = OFF_ZS + SGU_WIDTH

SEQ_TILE = 1024
MXU_TILE = 256
ROW_CHUNK = 256
W_CHUNK = 128
W_SLOTS = 4
V7X_VMEM_LIMIT_BYTES = 56 * 1024 * 1024

_SQRT_TWO = 1.4142135623730951


def _silu_of_half(hz):
    return hz * (1.0 + jnp.tanh(hz))


def _gelu_of_half(hz):
    return hz * (1.0 + lax.erf(hz * _SQRT_TWO))


def _layer_kernel(sinks_ref, x_ref, ng_ref, win_hbm, bin_ref, lng_ref, lnb_ref,
                  sw_ref, sbt_ref, wout_hbm, bout_ref, fg_ref, o_ref,
                  win_ref, wout_ref, bsc_s, stage_in, stage_out, sem_in, sem_out,
                  q_s, kd_s, vd_s, u_s, vl_s, gs_s, mix_s, wp_s, sb_s, *, seq_tile):
    f32, bf16 = jnp.float32, jnp.bfloat16
    n_sub = seq_tile // BLOCK
    b_idx = pl.program_id(0)
    s_idx = pl.program_id(1)

    lane = lax.broadcasted_iota(jnp.int32, (BLOCK, BLOCK), 1)
    low_half = lane < HEAD_DIM

    @pl.when((b_idx == 0) & (s_idx == 0))
    def _():
        row = lax.broadcasted_iota(jnp.int32, (BLOCK, BLOCK), 0)
        for p in range(N_SGU_HEADS // 2):
            for half in range(2):
                w = jnp.where(row >= lane, sw_ref[2 * p + half], 0.0)
                wp_s[p, :, half * BLOCK:(half + 1) * BLOCK] = w.astype(bf16)
        sbt = sbt_ref[...]
        for hh in range(N_SGU_HEADS):
            sb_s[:, hh * HEAD_DIM:(hh + 1) * HEAD_DIM] = jnp.broadcast_to(
                sbt[:, hh:hh + 1], (BLOCK, HEAD_DIM))
        col = lax.broadcasted_iota(jnp.int32, (1, IN_WIDTH), 1)
        col_scale = jnp.where(col < OFF_K, ATTN_SCALE, jnp.where(col < OFF_ZA, 1.0, 0.5)).astype(f32)
        bsc_s[...] = bin_ref[...] * col_scale
        jobs = ((win_hbm, stage_in, sem_in, win_ref, col_scale),
                (wout_hbm, stage_out, sem_out, wout_ref, None))
        n_chunks = D_MODEL // W_CHUNK

        def chunk_copy(job, c):
            hbm, stage, sem = job[:3]
            return pltpu.make_async_copy(hbm.at[pl.ds(c * W_CHUNK, W_CHUNK), :],
                                         stage.at[c % W_SLOTS], sem.at[c % W_SLOTS])

        for job in jobs:
            for c in range(W_SLOTS):
                chunk_copy(job, c).start(priority=c % 2)
        for job in jobs:
            for c in range(n_chunks):
                chunk_copy(job, c).wait()
                w = job[1][c % W_SLOTS]
                if job[4] is not None:
                    w = w * job[4]
                job[3][c * W_CHUNK:(c + 1) * W_CHUNK, :] = w.astype(bf16)
                if c + W_SLOTS < n_chunks:
                    chunk_copy(job, c + W_SLOTS).start(priority=c % 2)

    @pl.when(s_idx == 0)
    def _():
        for hh in range(N_KV_HEADS):
            kd_s[hh, 0:BLOCK, :] = jnp.zeros((BLOCK, KV_WIDTH), bf16)
            vd_s[hh, 0:BLOCK, :] = jnp.zeros((BLOCK, KV_WIDTH), bf16)

    @pl.when(s_idx > 0)
    def _():
        for hh in range(N_KV_HEADS):
            kd_s[hh, 0:BLOCK, :] = kd_s[hh, seq_tile:seq_tile + BLOCK, :]
            vd_s[hh, 0:BLOCK, :] = vd_s[hh, seq_tile:seq_tile + BLOCK, :]

    h_chunks = []
    for r0 in range(0, seq_tile, ROW_CHUNK):
        x = x_ref[0, r0:r0 + ROW_CHUNK, :]
        ms = jnp.mean(x * x, axis=-1, keepdims=True)
        hc = ((x * lax.rsqrt(ms + NORM_EPS)) * ng_ref[...]).astype(bf16)
        h_chunks.append(hc)
        q = jnp.dot(hc, win_ref[:, OFF_Q:OFF_Q + ATTN_WIDTH], preferred_element_type=f32)
        q_s[r0:r0 + ROW_CHUNK, :] = (q + bsc_s[:, OFF_Q:OFF_Q + ATTN_WIDTH]).astype(bf16)
    h = jnp.concatenate(h_chunks, axis=0)
    half_t = seq_tile // 2

    def proj_tile(off):
        cols = slice(off, off + MXU_TILE)
        return jnp.concatenate(
            [jnp.dot(h[i * half_t:(i + 1) * half_t], win_ref[:, cols], preferred_element_type=f32)
             for i in range(2)], axis=0) + bsc_s[:, cols]

    kv = proj_tile(OFF_K)
    half_mask = lax.broadcasted_iota(jnp.int32, (seq_tile, KV_WIDTH), 1) < HEAD_DIM
    for dup_ref, off in ((kd_s, 0), (vd_s, KV_WIDTH)):
        t = kv[:, off:off + KV_WIDTH]
        t_sw = pltpu.roll(t, HEAD_DIM, axis=1)
        dup_ref[0, BLOCK:BLOCK + seq_tile, :] = jnp.where(half_mask, t, t_sw).astype(bf16)
        dup_ref[1, BLOCK:BLOCK + seq_tile, :] = jnp.where(half_mask, t_sw, t).astype(bf16)

    ga_t = [_silu_of_half(proj_tile(OFF_ZA))]

    row = lax.broadcasted_iota(jnp.int32, (BLOCK, BLOCK), 0)
    prev_side = lane > row
    prev_side4 = jnp.concatenate([prev_side] * Q_PER_KV, axis=0)
    has_prev = s_idx > 0
    units = [(j, hh) for j in range(n_sub) for hh in range(N_KV_HEADS)]
    scores = []
    for j, hh in units:
        rows = slice(j * BLOCK, (j + 1) * BLOCK)
        parts = []
        for g in range(Q_PER_KV):
            c = hh * (Q_PER_KV // 2) + g // 2
            qc = q_s[rows, c * BLOCK:(c + 1) * BLOCK]
            keep = low_half if g % 2 == 0 else jnp.logical_not(low_half)
            parts.append(jnp.where(keep, qc, jnp.zeros_like(qc)))
        qst = jnp.concatenate(parts, axis=0)
        kb = kd_s[hh, j * BLOCK:(j + 2) * BLOCK, :]
        sc = lax.dot_general(qst, kb, (((1,), (1,)), ((), ())),
                             preferred_element_type=f32)
        s_prev = sc[:, 0:BLOCK]
        if j == 0:
            s_prev = jnp.where(has_prev, s_prev, NEG_INF)
        scores.append(jnp.where(prev_side4, s_prev, sc[:, BLOCK:2 * BLOCK]))

    def attend(u):
        j, hh = units[u]
        sc = scores[u]
        probs, inv_denoms = [], []
        for g in range(Q_PER_KV):
            sg = sc[g * BLOCK:(g + 1) * BLOCK, :]
            sink = sinks_ref[hh * Q_PER_KV + g]
            m = jnp.maximum(jnp.max(sg, axis=-1, keepdims=True), sink)
            p = jnp.exp(sg - m)
            denom = jnp.sum(p, axis=-1, keepdims=True) + jnp.exp(sink - m)
            inv_denoms.append(1.0 / denom)
            pb = p.astype(bf16)
            zero = jnp.zeros_like(pb)
            probs.append(jnp.concatenate([jnp.where(prev_side, pb, zero),
                                          jnp.where(prev_side, zero, pb)], axis=1))
        probs = jnp.concatenate(probs, axis=0)
        vb = vd_s[hh, j * BLOCK:(j + 2) * BLOCK, :]
        o = jnp.dot(probs, vb, preferred_element_type=f32)
        o = [o[g * BLOCK:(g + 1) * BLOCK, :] * inv_denoms[g] for g in range(Q_PER_KV)]
        return [jnp.where(low_half, o[2 * c2], o[2 * c2 + 1]) for c2 in range(Q_PER_KV // 2)]

    attn_out = []
    n_slots = (IN_WIDTH - OFF_ZA) // MXU_TILE - 1
    units_after = [len(units) * (i + 1) // n_slots - len(units) * i // n_slots for i in range(n_slots)]
    slot = iter(units_after)

    def attend_some():
        for _ in range(next(slot)):
            attn_out.append(attend(len(attn_out)))

    for n in range(1, ATTN_WIDTH // MXU_TILE):
        ga_t.append(_silu_of_half(proj_tile(OFF_ZA + n * MXU_TILE)))
        attend_some()
    for n in range(SGU_WIDTH // MXU_TILE):
        u_s[:, n * MXU_TILE:(n + 1) * MXU_TILE] = _gelu_of_half(proj_tile(OFF_U + n * MXU_TILE))
        attend_some()
    vg = []
    for n in range(SGU_WIDTH // MXU_TILE):
        vg.append(_gelu_of_half(proj_tile(OFF_VS + n * MXU_TILE)))
        attend_some()
    vg = jnp.concatenate(vg, axis=1)
    mu = jnp.mean(vg, axis=-1, keepdims=True)
    vc = vg - mu
    var = jnp.mean(vc * vc, axis=-1, keepdims=True)
    vl_s[...] = ((vc * lax.rsqrt(var + NORM_EPS)) * lng_ref[...] + lnb_ref[...]).astype(bf16)
    for n in range(SGU_WIDTH // MXU_TILE):
        gs_s[:, n * MXU_TILE:(n + 1) * MXU_TILE] = _silu_of_half(proj_tile(OFF_ZS + n * MXU_TILE))
        attend_some()
    assert len(attn_out) == len(units)

    ga = jnp.concatenate(ga_t, axis=1)
    for u, (j, hh) in enumerate(units):
        rows = slice(j * BLOCK, (j + 1) * BLOCK)
        for c2 in range(Q_PER_KV // 2):
            cols = slice((hh * (Q_PER_KV // 2) + c2) * BLOCK, (hh * (Q_PER_KV // 2) + c2 + 1) * BLOCK)
            mix_s[rows, cols] = (attn_out[u][c2] * ga[rows, cols]).astype(bf16)

    for j in range(n_sub):
        rows = slice(j * BLOCK, (j + 1) * BLOCK)
        for p_idx in range(N_SGU_HEADS // 2):
            cols = slice(p_idx * BLOCK, (p_idx + 1) * BLOCK)
            vp = vl_s[rows, cols]
            zero = jnp.zeros_like(vp)
            rhs = jnp.concatenate([jnp.where(low_half, vp, zero),
                                   jnp.where(low_half, zero, vp)], axis=0)
            mixed = jnp.dot(wp_s[p_idx], rhs, preferred_element_type=f32) + sb_s[:, cols]
            mix_s[rows, ATTN_WIDTH + p_idx * BLOCK:ATTN_WIDTH + (p_idx + 1) * BLOCK] = (
                (u_s[rows, cols] * mixed) * gs_s[rows, cols]).astype(bf16)

    y = x_ref[0] + jnp.dot(mix_s[...], wout_ref[...], preferred_element_type=f32) + bout_ref[...]
    ms2 = jnp.mean(y * y, axis=-1, keepdims=True)
    o_ref[0] = (y * lax.rsqrt(ms2 + NORM_EPS)) * fg_ref[...]


def _layer_call(x, sinks, norm_g, w_in, b_in, ln_g, ln_b, sgu_w, sgu_bt, w_out, b_out, final_g,
                *, seq_tile=SEQ_TILE):
    batch, seq, d_model = x.shape
    assert d_model == D_MODEL and seq % seq_tile == 0 and seq_tile % (2 * BLOCK) == 0
    f32, bf16 = jnp.float32, jnp.bfloat16

    def full(shape):
        return pl.BlockSpec(shape, lambda b, s: (0,) * len(shape))

    tile_spec = pl.BlockSpec((1, seq_tile, D_MODEL), lambda b, s: (b, s, 0))
    return pl.pallas_call(
        functools.partial(_layer_kernel, seq_tile=seq_tile),
        grid=(batch, seq // seq_tile),
        in_specs=[
            pl.BlockSpec(memory_space=pltpu.SMEM),
            tile_spec,
            full((1, D_MODEL)),
            pl.BlockSpec(memory_space=pl.ANY),
            full((1, IN_WIDTH)),
            full((1, SGU_WIDTH)),
            full((1, SGU_WIDTH)),
            full((N_SGU_HEADS, BLOCK, BLOCK)),
            full((BLOCK, N_SGU_HEADS)),
            pl.BlockSpec(memory_space=pl.ANY),
            full((1, D_MODEL)),
            full((1, D_MODEL)),
        ],
        out_specs=tile_spec,
        out_shape=jax.ShapeDtypeStruct(x.shape, x.dtype),
        scratch_shapes=[
            pltpu.VMEM((D_MODEL, IN_WIDTH), bf16),
            pltpu.VMEM((D_MODEL, D_MODEL), bf16),
            pltpu.VMEM((1, IN_WIDTH), f32),
            pltpu.VMEM((W_SLOTS, W_CHUNK, IN_WIDTH), f32),
            pltpu.VMEM((W_SLOTS, W_CHUNK, D_MODEL), f32),
            pltpu.SemaphoreType.DMA((W_SLOTS,)),
            pltpu.SemaphoreType.DMA((W_SLOTS,)),
            pltpu.VMEM((seq_tile, ATTN_WIDTH), bf16),
            pltpu.VMEM((N_KV_HEADS, BLOCK + seq_tile, KV_WIDTH), bf16),
            pltpu.VMEM((N_KV_HEADS, BLOCK + seq_tile, KV_WIDTH), bf16),
            pltpu.VMEM((seq_tile, SGU_WIDTH), f32),
            pltpu.VMEM((seq_tile, SGU_WIDTH), bf16),
            pltpu.VMEM((seq_tile, SGU_WIDTH), f32),
            pltpu.VMEM((seq_tile, D_MODEL), bf16),
            pltpu.VMEM((N_SGU_HEADS // 2, BLOCK, 2 * BLOCK), bf16),
            pltpu.VMEM((BLOCK, SGU_WIDTH), f32),
        ],
        compiler_params=pltpu.CompilerParams(
            dimension_semantics=("arbitrary", "arbitrary"),
            vmem_limit_bytes=V7X_VMEM_LIMIT_BYTES),
        name="hybrid_layer",
    )(sinks, x, norm_g, w_in, b_in, ln_g, ln_b, sgu_w, sgu_bt, w_out, b_out, final_g)


def kernel(x, norm_g, w_in, b_in, attn_sinks, sgu_ln_g, sgu_ln_b, sgu_w, sgu_b, w_out, b_out, final_norm_g):
    depth = norm_g.shape[0]
    for l in range(depth):
        last = l == depth - 1
        assert last, "the fused call applies the final norm; only depth 1 is supported"
        x = _layer_call(
            x, attn_sinks[l], norm_g[l][None, :], w_in[l], b_in[l][None, :],
            sgu_ln_g[l][None, :], sgu_ln_b[l][None, :], sgu_w[l], sgu_b[l].T,
            w_out[l], b_out[l][None, :], final_norm_g[None, :])
    return x
```

```python
import functools

import jax
import jax.numpy as jnp
from jax import lax
from jax.experimental import pallas as pl
from jax.experimental.pallas import tpu as pltpu

D_MODEL = 1024
HEAD_DIM = 64
ATTN_WIDTH = 512
KV_WIDTH = 128
SGU_WIDTH = 512
N_KV_HEADS = 2
Q_PER_KV = 4
N_SGU_HEADS = 8
BLOCK = 128
NORM_EPS = 1e-5
NEG_INF = -1e30
ATTN_SCALE = HEAD_DIM ** -0.5
LOG2_E = 1.4426950408889634

OFF_Q = 0
OFF_K = OFF_Q + ATTN_WIDTH
OFF_V = OFF_K + KV_WIDTH
OFF_ZA = OFF_V + KV_WIDTH
OFF_U = OFF_ZA + ATTN_WIDTH
OFF_VS = OFF_U + SGU_WIDTH
OFF_ZS = OFF_VS + SGU_WIDTH
IN_WIDTH = OFF_ZS + SGU_WIDTH

SEQ_TILE = 1024
MXU_TILE = 256
ROW_CHUNK = 256
W_CHUNK = 128
W_SLOTS = 4
V7X_VMEM_LIMIT_BYTES = 56 * 1024 * 1024

_SQRT_TWO = 1.4142135623730951


def _silu_of_half(hz):
    return hz * (1.0 + jnp.tanh(hz))


def _gelu_of_half(hz):
    return hz * (1.0 + lax.erf(hz * _SQRT_TWO))


def _layer_kernel(sinks_ref, x_ref, ng_ref, win_hbm, bin_ref, lng_ref, lnb_ref,
                  sw_ref, sbt_ref, wout_hbm, bout_ref, fg_ref, o_ref,
                  win_ref, wout_ref, bsc_s, stage_in, stage_out, sem_in, sem_out,
                  q_s, kd_s, vd_s, u_s, vl_s, gs_s, mix_s, wp_s, sb_s, *, seq_tile):
    f32, bf16 = jnp.float32, jnp.bfloat16
    n_sub = seq_tile // BLOCK
    b_idx = pl.program_id(0)
    s_idx = pl.program_id(1)

    lane = lax.broadcasted_iota(jnp.int32, (BLOCK, BLOCK), 1)
    low_half = lane < HEAD_DIM

    @pl.when((b_idx == 0) & (s_idx == 0))
    def _():
        row = lax.broadcasted_iota(jnp.int32, (BLOCK, BLOCK), 0)
        for p in range(N_SGU_HEADS // 2):
            for half in range(2):
                w = jnp.where(row >= lane, sw_ref[2 * p + half], 0.0)
                wp_s[p, :, half * BLOCK:(half + 1) * BLOCK] = w.astype(bf16)
        sbt = sbt_ref[...]
        for hh in range(N_SGU_HEADS):
            sb_s[:, hh * HEAD_DIM:(hh + 1) * HEAD_DIM] = jnp.broadcast_to(
                sbt[:, hh:hh + 1], (BLOCK, HEAD_DIM))
        col = lax.broadcasted_iota(jnp.int32, (1, IN_WIDTH), 1)
        col_scale = jnp.where(col < OFF_K, ATTN_SCALE * LOG2_E,
                              jnp.where(col < OFF_ZA, 1.0, 0.5)).astype(f32)
        bsc_s[...] = bin_ref[...] * col_scale
        jobs = ((win_hbm, stage_in, sem_in, win_ref, col_scale),
                (wout_hbm, stage_out, sem_out, wout_ref, None))
        n_chunks = D_MODEL // W_CHUNK

        def chunk_copy(job, c):
            hbm, stage, sem = job[:3]
            return pltpu.make_async_copy(hbm.at[pl.ds(c * W_CHUNK, W_CHUNK), :],
                                         stage.at[c % W_SLOTS], sem.at[c % W_SLOTS])

        for job in jobs:
            for c in range(W_SLOTS):
                chunk_copy(job, c).start(priority=c % 2)
        for job in jobs:
            for c in range(n_chunks):
                chunk_copy(job, c).wait()
                w = job[1][c % W_SLOTS]
                if job[4] is not None:
                    w = w * job[4]
                job[3][c * W_CHUNK:(c + 1) * W_CHUNK, :] = w.astype(bf16)
                if c + W_SLOTS < n_chunks:
                    chunk_copy(job, c + W_SLOTS).start(priority=c % 2)

    @pl.when(s_idx == 0)
    def _():
        for hh in range(N_KV_HEADS):
            kd_s[hh, 0:BLOCK, :] = jnp.zeros((BLOCK, KV_WIDTH), bf16)
            vd_s[hh, 0:BLOCK, :] = jnp.zeros((BLOCK, KV_WIDTH), bf16)

    @pl.when(s_idx > 0)
    def _():
        for hh in range(N_KV_HEADS):
            kd_s[hh, 0:BLOCK, :] = kd_s[hh, seq_tile:seq_tile + BLOCK, :]
            vd_s[hh, 0:BLOCK, :] = vd_s[hh, seq_tile:seq_tile + BLOCK, :]

    h_chunks = []
    for r0 in range(0, seq_tile, ROW_CHUNK):
        x = x_ref[0, r0:r0 + ROW_CHUNK, :]
        ms = jnp.mean(x * x, axis=-1, keepdims=True)
        hc = ((x * lax.rsqrt(ms + NORM_EPS)) * ng_ref[...]).astype(bf16)
        h_chunks.append(hc)
        q = jnp.dot(hc, win_ref[:, OFF_Q:OFF_Q + ATTN_WIDTH], preferred_element_type=f32)
        q_s[r0:r0 + ROW_CHUNK, :] = (q + bsc_s[:, OFF_Q:OFF_Q + ATTN_WIDTH]).astype(bf16)
    h = jnp.concatenate(h_chunks, axis=0)
    half_t = seq_tile // 2

    def proj_tile(off):
        cols = slice(off, off + MXU_TILE)
        return jnp.concatenate(
            [jnp.dot(h[i * half_t:(i + 1) * half_t], win_ref[:, cols], preferred_element_type=f32)
             for i in range(2)], axis=0) + bsc_s[:, cols]

    kv = proj_tile(OFF_K)
    half_mask = lax.broadcasted_iota(jnp.int32, (seq_tile, KV_WIDTH), 1) < HEAD_DIM
    for dup_ref, off in ((kd_s, 0), (vd_s, KV_WIDTH)):
        t = kv[:, off:off + KV_WIDTH]
        t_sw = pltpu.roll(t, HEAD_DIM, axis=1)
        dup_ref[0, BLOCK:BLOCK + seq_tile, :] = jnp.where(half_mask, t, t_sw).astype(bf16)
        dup_ref[1, BLOCK:BLOCK + seq_tile, :] = jnp.where(half_mask, t_sw, t).astype(bf16)

    ga_t = [_silu_of_half(proj_tile(OFF_ZA))]

    row = lax.broadcasted_iota(jnp.int32, (BLOCK, BLOCK), 0)
    prev_side = lane > row
    prev_side4 = jnp.concatenate([prev_side] * Q_PER_KV, axis=0)
    has_prev = s_idx > 0
    units = [(j, hh) for j in range(n_sub) for hh in range(N_KV_HEADS)]
    scores = []
    for j, hh in units:
        rows = slice(j * BLOCK, (j + 1) * BLOCK)
        parts = []
        for g in range(Q_PER_KV):
            c = hh * (Q_PER_KV // 2) + g // 2
            qc = q_s[rows, c * BLOCK:(c + 1) * BLOCK]
            keep = low_half if g % 2 == 0 else jnp.logical_not(low_half)
            parts.append(jnp.where(keep, qc, jnp.zeros_like(qc)))
        qst = jnp.concatenate(parts, axis=0)
        kb = kd_s[hh, j * BLOCK:(j + 2) * BLOCK, :]
        sc = lax.dot_general(qst, kb, (((1,), (1,)), ((), ())),
                             preferred_element_type=f32)
        s_prev = sc[:, 0:BLOCK]
        if j == 0:
            s_prev = jnp.where(has_prev, s_prev, NEG_INF)
        scores.append(jnp.where(prev_side4, s_prev, sc[:, BLOCK:2 * BLOCK]))

    def attend(u):
        j, hh = units[u]
        sc = scores[u]
        probs, inv_denoms = [], []
        for g in range(Q_PER_KV):
            sg = sc[g * BLOCK:(g + 1) * BLOCK, :]
            sink = sinks_ref[hh * Q_PER_KV + g] * LOG2_E
            m = jnp.max(sg, axis=-1, keepdims=True)
            p = jnp.exp2(sg - m)
            denom = jnp.sum(p, axis=-1, keepdims=True) + jnp.exp2(sink - m)
            inv_denoms.append(1.0 / denom)
            pb = p.astype(bf16)
            zero = jnp.zeros_like(pb)
            probs.append(jnp.concatenate([jnp.where(prev_side, pb, zero),
                                          jnp.where(prev_side, zero, pb)], axis=1))
        probs = jnp.concatenate(probs, axis=0)
        vb = vd_s[hh, j * BLOCK:(j + 2) * BLOCK, :]
        o = jnp.dot(probs, vb, preferred_element_type=f32)
        o = [o[g * BLOCK:(g + 1) * BLOCK, :] * inv_denoms[g] for g in range(Q_PER_KV)]
        return [jnp.where(low_half, o[2 * c2], o[2 * c2 + 1]) for c2 in range(Q_PER_KV // 2)]

    attn_out = []
    n_slots = (IN_WIDTH - OFF_ZA) // MXU_TILE - 1
    units_after = [len(units) * (i + 1) // n_slots - len(units) * i // n_slots for i in range(n_slots)]
    slot = iter(units_after)

    def attend_some():
        for _ in range(next(slot)):
            attn_out.append(attend(len(attn_out)))

    for n in range(1, ATTN_WIDTH // MXU_TILE):
        ga_t.append(_silu_of_half(proj_tile(OFF_ZA + n * MXU_TILE)))
        attend_some()
    for n in range(SGU_WIDTH // MXU_TILE):
        u_s[:, n * MXU_TILE:(n + 1) * MXU_TILE] = _gelu_of_half(proj_tile(OFF_U + n * MXU_TILE))
        attend_some()
    vg = []
    for n in range(SGU_WIDTH // MXU_TILE):
        vg.append(_gelu_of_half(proj_tile(OFF_VS + n * MXU_TILE)))
        attend_some()
    vg = jnp.concatenate(vg, axis=1)
    mu = jnp.mean(vg, axis=-1, keepdims=True)
    vc = vg - mu
    var = jnp.mean(vc * vc, axis=-1, keepdims=True)
    vl_s[...] = ((vc * lax.rsqrt(var + NORM_EPS)) * lng_ref[...] + lnb_ref[...]).astype(bf16)
    for n in range(SGU_WIDTH // MXU_TILE):
        gs_s[:, n * MXU_TILE:(n + 1) * MXU_TILE] = _silu_of_half(proj_tile(OFF_ZS + n * MXU_TILE))
        attend_some()
    assert len(attn_out) == len(units)

    ga = jnp.concatenate(ga_t, axis=1)
    for u, (j, hh) in enumerate(units):
        rows = slice(j * BLOCK, (j + 1) * BLOCK)
        for c2 in range(Q_PER_KV // 2):
            cols = slice((hh * (Q_PER_KV // 2) + c2) * BLOCK, (hh * (Q_PER_KV // 2) + c2 + 1) * BLOCK)
            mix_s[rows, cols] = (attn_out[u][c2] * ga[rows, cols]).astype(bf16)

    for j in range(n_sub):
        rows = slice(j * BLOCK, (j + 1) * BLOCK)
        for p_idx in range(N_SGU_HEADS // 2):
            cols = slice(p_idx * BLOCK, (p_idx + 1) * BLOCK)
            vp = vl_s[rows, cols]
            zero = jnp.zeros_like(vp)
            rhs = jnp.concatenate([jnp.where(low_half, vp, zero),
                                   jnp.where(low_half, zero, vp)], axis=0)
            mixed = jnp.dot(wp_s[p_idx], rhs, preferred_element_type=f32) + sb_s[:, cols]
            mix_s[rows, ATTN_WIDTH + p_idx * BLOCK:ATTN_WIDTH + (p_idx + 1) * BLOCK] = (
                (u_s[rows, cols] * mixed) * gs_s[rows, cols]).astype(bf16)

    y = x_ref[0] + jnp.dot(mix_s[...], wout_ref[...], preferred_element_type=f32) + bout_ref[...]
    ms2 = jnp.mean(y * y, axis=-1, keepdims=True)
    o_ref[0] = (y * lax.rsqrt(ms2 + NORM_EPS)) * fg_ref[...]


def _layer_call(x, sinks, norm_g, w_in, b_in, ln_g, ln_b, sgu_w, sgu_bt, w_out, b_out, final_g,
                *, seq_tile=SEQ_TILE):
    batch, seq, d_model = x.shape
    assert d_model == D_MODEL and seq % seq_tile == 0 and seq_tile % (2 * BLOCK) == 0
    f32, bf16 = jnp.float32, jnp.bfloat16

    def full(shape):
        return pl.BlockSpec(shape, lambda b, s: (0,) * len(shape))

    tile_spec = pl.BlockSpec((1, seq_tile, D_MODEL), lambda b, s: (b, s, 0))
    return pl.pallas_call(
        functools.partial(_layer_kernel, seq_tile=seq_tile),
        grid=(batch, seq // seq_tile),
        in_specs=[
            pl.BlockSpec(memory_space=pltpu.SMEM),
            tile_spec,
            full((1, D_MODEL)),
            pl.BlockSpec(memory_space=pl.ANY),
            full((1, IN_WIDTH)),
            full((1, SGU_WIDTH)),
            full((1, SGU_WIDTH)),
            full((N_SGU_HEADS, BLOCK, BLOCK)),
            full((BLOCK, N_SGU_HEADS)),
            pl.BlockSpec(memory_space=pl.ANY),
            full((1, D_MODEL)),
            full((1, D_MODEL)),
        ],
        out_specs=tile_spec,
        out_shape=jax.ShapeDtypeStruct(x.shape, x.dtype),
        scratch_shapes=[
            pltpu.VMEM((D_MODEL, IN_WIDTH), bf16),
            pltpu.VMEM((D_MODEL, D_MODEL), bf16),
            pltpu.VMEM((1, IN_WIDTH), f32),
            pltpu.VMEM((W_SLOTS, W_CHUNK, IN_WIDTH), f32),
            pltpu.VMEM((W_SLOTS, W_CHUNK, D_MODEL), f32),
            pltpu.SemaphoreType.DMA((W_SLOTS,)),
            pltpu.SemaphoreType.DMA((W_SLOTS,)),
            pltpu.VMEM((seq_tile, ATTN_WIDTH), bf16),
            pltpu.VMEM((N_KV_HEADS, BLOCK + seq_tile, KV_WIDTH), bf16),
            pltpu.VMEM((N_KV_HEADS, BLOCK + seq_tile, KV_WIDTH), bf16),
            pltpu.VMEM((seq_tile, SGU_WIDTH), f32),
            pltpu.VMEM((seq_tile, SGU_WIDTH), bf16),
            pltpu.VMEM((seq_tile, SGU_WIDTH), f32),
            pltpu.VMEM((seq_tile, D_MODEL), bf16),
            pltpu.VMEM((N_SGU_HEADS // 2, BLOCK, 2 * BLOCK), bf16),
            pltpu.VMEM((BLOCK, SGU_WIDTH), f32),
        ],
        compiler_params=pltpu.CompilerParams(
            dimension_semantics=("arbitrary", "arbitrary"),
            vmem_limit_bytes=V7X_VMEM_LIMIT_BYTES),
        name="hybrid_layer",
    )(sinks, x, norm_g, w_in, b_in, ln_g, ln_b, sgu_w, sgu_bt, w_out, b_out, final_g)


def kernel(x, norm_g, w_in, b_in, attn_sinks, sgu_ln_g, sgu_ln_b, sgu_w, sgu_b, w_out, b_out, final_norm_g):
    depth = norm_g.shape[0]
    for l in range(depth):
        last = l == depth - 1
        assert last, "the fused call applies the final norm; only depth 1 is supported"
        x = _layer_call(
            x, attn_sinks[l], norm_g[l][None, :], w_in[l], b_in[l][None, :],
            sgu_ln_g[l][None, :], sgu_ln_b[l][None, :], sgu_w[l], sgu_b[l].T,
            w_out[l], b_out[l][None, :], final_norm_g[None, :])
    return x
```

```python
import functools

import jax
import jax.numpy as jnp
from jax import lax
from jax.experimental import pallas as pl
from jax.experimental.pallas import tpu as pltpu

D_MODEL = 1024
HEAD_DIM = 64
ATTN_WIDTH = 512
KV_WIDTH = 128
SGU_WIDTH = 512
N_KV_HEADS = 2
Q_PER_KV = 4
N_SGU_HEADS = 8
BLOCK = 128
NORM_EPS = 1e-5
NEG_INF = -1e30
ATTN_SCALE = HEAD_DIM ** -0.5

OFF_Q = 0
OFF_K = OFF_Q + ATTN_WIDTH
OFF_V = OFF_K + KV_WIDTH
OFF_ZA = OFF_V + KV_WIDTH
OFF_U = OFF_ZA + ATTN_WIDTH
OFF_VS = OFF_U + SGU_WIDTH
OFF_ZS = OFF_VS + SGU_WIDTH
IN_WIDTH = OFF_ZS + SGU_WIDTH

SEQ_TILE = 1024
MXU_TILE = 256
ROW_CHUNK = 256
W_CHUNK = 128
W_SLOTS = 4
V7X_VMEM_LIMIT_BYTES = 56 * 1024 * 1024

_SQRT_TWO = 1.4142135623730951


def _silu_of_half(hz):
    return hz * (1.0 + jnp.tanh(hz))


def _gelu_of_half(hz):
    return hz * (1.0 + lax.erf(hz * _SQRT_TWO))


def _layer_kernel(sinks_ref, x_ref, ngc_ref, win_hbm, bin_ref, lng_ref, lnb_ref,
                  sw_ref, sbt_ref, wout_hbm, bout_ref, fg_ref, o_ref,
                  win_ref, wout_ref, bsc_s, stage_in, stage_out, sem_in, sem_out,
                  q_s, kd_s, vd_s, u_s, vl_s, gs_s, mix_s, wp_s, sb_s, *, seq_tile):
    f32, bf16 = jnp.float32, jnp.bfloat16
    n_sub = seq_tile // BLOCK
    b_idx = pl.program_id(0)
    s_idx = pl.program_id(1)

    lane = lax.broadcasted_iota(jnp.int32, (BLOCK, BLOCK), 1)
    low_half = lane < HEAD_DIM

    @pl.when((b_idx == 0) & (s_idx == 0))
    def _():
        row = lax.broadcasted_iota(jnp.int32, (BLOCK, BLOCK), 0)
        for p in range(N_SGU_HEADS // 2):
            for half in range(2):
                w = jnp.where(row >= lane, sw_ref[2 * p + half], 0.0)
                wp_s[p, :, half * BLOCK:(half + 1) * BLOCK] = w.astype(bf16)
        sbt = sbt_ref[...]
        for hh in range(N_SGU_HEADS):
            sb_s[:, hh * HEAD_DIM:(hh + 1) * HEAD_DIM] = jnp.broadcast_to(
                sbt[:, hh:hh + 1], (BLOCK, HEAD_DIM))
        col = lax.broadcasted_iota(jnp.int32, (1, IN_WIDTH), 1)
        col_scale = jnp.where(col < OFF_K, ATTN_SCALE, jnp.where(col < OFF_ZA, 1.0, 0.5)).astype(f32)
        bsc_s[...] = bin_ref[...] * col_scale
        jobs = ((win_hbm, stage_in, sem_in, win_ref, col_scale),
                (wout_hbm, stage_out, sem_out, wout_ref, None))
        n_chunks = D_MODEL // W_CHUNK

        def chunk_copy(job, c):
            hbm, stage, sem = job[:3]
            return pltpu.make_async_copy(hbm.at[pl.ds(c * W_CHUNK, W_CHUNK), :],
                                         stage.at[c % W_SLOTS], sem.at[c % W_SLOTS])

        for job in jobs:
            for c in range(W_SLOTS):
                chunk_copy(job, c).start(priority=c % 2)
        for job in jobs:
            for c in range(n_chunks):
                chunk_copy(job, c).wait()
                w = job[1][c % W_SLOTS]
                if job[4] is not None:
                    w = (w * ngc_ref[c * W_CHUNK:(c + 1) * W_CHUNK, :]) * job[4]
                job[3][c * W_CHUNK:(c + 1) * W_CHUNK, :] = w.astype(bf16)
                if c + W_SLOTS < n_chunks:
                    chunk_copy(job, c + W_SLOTS).start(priority=c % 2)

    @pl.when(s_idx == 0)
    def _():
        for hh in range(N_KV_HEADS):
            kd_s[hh, 0:BLOCK, :] = jnp.zeros((BLOCK, KV_WIDTH), bf16)
            vd_s[hh, 0:BLOCK, :] = jnp.zeros((BLOCK, KV_WIDTH), bf16)

    @pl.when(s_idx > 0)
    def _():
        for hh in range(N_KV_HEADS):
            kd_s[hh, 0:BLOCK, :] = kd_s[hh, seq_tile:seq_tile + BLOCK, :]
            vd_s[hh, 0:BLOCK, :] = vd_s[hh, seq_tile:seq_tile + BLOCK, :]

    h_chunks = []
    for r0 in range(0, seq_tile, ROW_CHUNK):
        x = x_ref[0, r0:r0 + ROW_CHUNK, :]
        ms = jnp.mean(x * x, axis=-1, keepdims=True)
        hc = (x * lax.rsqrt(ms + NORM_EPS)).astype(bf16)
        h_chunks.append(hc)
        q = jnp.dot(hc, win_ref[:, OFF_Q:OFF_Q + ATTN_WIDTH], preferred_element_type=f32)
        q_s[r0:r0 + ROW_CHUNK, :] = (q + bsc_s[:, OFF_Q:OFF_Q + ATTN_WIDTH]).astype(bf16)
    h = jnp.concatenate(h_chunks, axis=0)
    half_t = seq_tile // 2

    def proj_tile(off):
        cols = slice(off, off + MXU_TILE)
        return jnp.concatenate(
            [jnp.dot(h[i * half_t:(i + 1) * half_t], win_ref[:, cols], preferred_element_type=f32)
             for i in range(2)], axis=0) + bsc_s[:, cols]

    kv = proj_tile(OFF_K)
    half_mask = lax.broadcasted_iota(jnp.int32, (seq_tile, KV_WIDTH), 1) < HEAD_DIM
    for dup_ref, off in ((kd_s, 0), (vd_s, KV_WIDTH)):
        t = kv[:, off:off + KV_WIDTH]
        t_sw = pltpu.roll(t, HEAD_DIM, axis=1)
        dup_ref[0, BLOCK:BLOCK + seq_tile, :] = jnp.where(half_mask, t, t_sw).astype(bf16)
        dup_ref[1, BLOCK:BLOCK + seq_tile, :] = jnp.where(half_mask, t_sw, t).astype(bf16)

    ga_t = [_silu_of_half(proj_tile(OFF_ZA))]

    row = lax.broadcasted_iota(jnp.int32, (BLOCK, BLOCK), 0)
    prev_side = lane > row
    prev_side4 = jnp.concatenate([prev_side] * Q_PER_KV, axis=0)
    has_prev = s_idx > 0
    units = [(j, hh) for j in range(n_sub) for hh in range(N_KV_HEADS)]
    scores = []
    for j, hh in units:
        rows = slice(j * BLOCK, (j + 1) * BLOCK)
        parts = []
        for g in range(Q_PER_KV):
            c = hh * (Q_PER_KV // 2) + g // 2
            qc = q_s[rows, c * BLOCK:(c + 1) * BLOCK]
            keep = low_half if g % 2 == 0 else jnp.logical_not(low_half)
            parts.append(jnp.where(keep, qc, jnp.zeros_like(qc)))
        qst = jnp.concatenate(parts, axis=0)
        kb = kd_s[hh, j * BLOCK:(j + 2) * BLOCK, :]
        sc = lax.dot_general(qst, kb, (((1,), (1,)), ((), ())),
                             preferred_element_type=f32)
        s_prev = sc[:, 0:BLOCK]
        if j == 0:
            s_prev = jnp.where(has_prev, s_prev, NEG_INF)
        scores.append(jnp.where(prev_side4, s_prev, sc[:, BLOCK:2 * BLOCK]))

    def attend(u):
        j, hh = units[u]
        sc = scores[u]
        probs, inv_denoms = [], []
        for g in range(Q_PER_KV):
            sg = sc[g * BLOCK:(g + 1) * BLOCK, :]
            sink = sinks_ref[hh * Q_PER_KV + g]
            m = jnp.maximum(jnp.max(sg, axis=-1, keepdims=True), sink)
            p = jnp.exp(sg - m)
            denom = jnp.sum(p, axis=-1, keepdims=True) + jnp.exp(sink - m)
            inv_denoms.append(1.0 / denom)
            pb = p.astype(bf16)
            zero = jnp.zeros_like(pb)
            probs.append(jnp.concatenate([jnp.where(prev_side, pb, zero),
                                          jnp.where(prev_side, zero, pb)], axis=1))
        probs = jnp.concatenate(probs, axis=0)
        vb = vd_s[hh, j * BLOCK:(j + 2) * BLOCK, :]
        o = jnp.dot(probs, vb, preferred_element_type=f32)
        o = [o[g * BLOCK:(g + 1) * BLOCK, :] * inv_denoms[g] for g in range(Q_PER_KV)]
        return [jnp.where(low_half, o[2 * c2], o[2 * c2 + 1]) for c2 in range(Q_PER_KV // 2)]

    attn_out = []
    n_slots = (IN_WIDTH - OFF_ZA) // MXU_TILE - 1
    units_after = [len(units) * (i + 1) // n_slots - len(units) * i // n_slots for i in range(n_slots)]
    slot = iter(units_after)

    def attend_some():
        for _ in range(next(slot)):
            attn_out.append(attend(len(attn_out)))

    for n in range(1, ATTN_WIDTH // MXU_TILE):
        ga_t.append(_silu_of_half(proj_tile(OFF_ZA + n * MXU_TILE)))
        attend_some()
    for n in range(SGU_WIDTH // MXU_TILE):
        u_s[:, n * MXU_TILE:(n + 1) * MXU_TILE] = _gelu_of_half(proj_tile(OFF_U + n * MXU_TILE))
        attend_some()
    vg = []
    for n in range(SGU_WIDTH // MXU_TILE):
        vg.append(_gelu_of_half(proj_tile(OFF_VS + n * MXU_TILE)))
        attend_some()
    vg = jnp.concatenate(vg, axis=1)
    mu = jnp.mean(vg, axis=-1, keepdims=True)
    vc = vg - mu
    var = jnp.mean(vc * vc, axis=-1, keepdims=True)
    vl_s[...] = ((vc * lax.rsqrt(var + NORM_EPS)) * lng_ref[...] + lnb_ref[...]).astype(bf16)
    for n in range(SGU_WIDTH // MXU_TILE):
        gs_s[:, n * MXU_TILE:(n + 1) * MXU_TILE] = _silu_of_half(proj_tile(OFF_ZS + n * MXU_TILE))
        attend_some()
    assert len(attn_out) == len(units)

    ga = jnp.concatenate(ga_t, axis=1)
    for u, (j, hh) in enumerate(units):
        rows = slice(j * BLOCK, (j + 1) * BLOCK)
        for c2 in range(Q_PER_KV // 2):
            cols = slice((hh * (Q_PER_KV // 2) + c2) * BLOCK, (hh * (Q_PER_KV // 2) + c2 + 1) * BLOCK)
            mix_s[rows, cols] = (attn_out[u][c2] * ga[rows, cols]).astype(bf16)

    for j in range(n_sub):
        rows = slice(j * BLOCK, (j + 1) * BLOCK)
        for p_idx in range(N_SGU_HEADS // 2):
            cols = slice(p_idx * BLOCK, (p_idx + 1) * BLOCK)
            vp = vl_s[rows, cols]
            zero = jnp.zeros_like(vp)
            rhs = jnp.concatenate([jnp.where(low_half, vp, zero),
                                   jnp.where(low_half, zero, vp)], axis=0)
            mixed = jnp.dot(wp_s[p_idx], rhs, preferred_element_type=f32) + sb_s[:, cols]
            mix_s[rows, ATTN_WIDTH + p_idx * BLOCK:ATTN_WIDTH + (p_idx + 1) * BLOCK] = (
                (u_s[rows, cols] * mixed) * gs_s[rows, cols]).astype(bf16)

    y = x_ref[0] + jnp.dot(mix_s[...], wout_ref[...], preferred_element_type=f32) + bout_ref[...]
    ms2 = jnp.mean(y * y, axis=-1, keepdims=True)
    o_ref[0] = (y * lax.rsqrt(ms2 + NORM_EPS)) * fg_ref[...]


def _layer_call(x, sinks, norm_g, w_in, b_in, ln_g, ln_b, sgu_w, sgu_bt, w_out, b_out, final_g,
                *, seq_tile=SEQ_TILE):
    batch, seq, d_model = x.shape
    assert d_model == D_MODEL and seq % seq_tile == 0 and seq_tile % (2 * BLOCK) == 0
    f32, bf16 = jnp.float32, jnp.bfloat16

    def full(shape):
        return pl.BlockSpec(shape, lambda b, s: (0,) * len(shape))

    tile_spec = pl.BlockSpec((1, seq_tile, D_MODEL), lambda b, s: (b, s, 0))
    return pl.pallas_call(
        functools.partial(_layer_kernel, seq_tile=seq_tile),
        grid=(batch, seq // seq_tile),
        in_specs=[
            pl.BlockSpec(memory_space=pltpu.SMEM),
            tile_spec,
            full((D_MODEL, 1)),
            pl.BlockSpec(memory_space=pl.ANY),
            full((1, IN_WIDTH)),
            full((1, SGU_WIDTH)),
            full((1, SGU_WIDTH)),
            full((N_SGU_HEADS, BLOCK, BLOCK)),
            full((BLOCK, N_SGU_HEADS)),
            pl.BlockSpec(memory_space=pl.ANY),
            full((1, D_MODEL)),
            full((1, D_MODEL)),
        ],
        out_specs=tile_spec,
        out_shape=jax.ShapeDtypeStruct(x.shape, x.dtype),
        scratch_shapes=[
            pltpu.VMEM((D_MODEL, IN_WIDTH), bf16),
            pltpu.VMEM((D_MODEL, D_MODEL), bf16),
            pltpu.VMEM((1, IN_WIDTH), f32),
            pltpu.VMEM((W_SLOTS, W_CHUNK, IN_WIDTH), f32),
            pltpu.VMEM((W_SLOTS, W_CHUNK, D_MODEL), f32),
            pltpu.SemaphoreType.DMA((W_SLOTS,)),
            pltpu.SemaphoreType.DMA((W_SLOTS,)),
            pltpu.VMEM((seq_tile, ATTN_WIDTH), bf16),
            pltpu.VMEM((N_KV_HEADS, BLOCK + seq_tile, KV_WIDTH), bf16),
            pltpu.VMEM((N_KV_HEADS, BLOCK + seq_tile, KV_WIDTH), bf16),
            pltpu.VMEM((seq_tile, SGU_WIDTH), f32),
            pltpu.VMEM((seq_tile, SGU_WIDTH), bf16),
            pltpu.VMEM((seq_tile, SGU_WIDTH), f32),
            pltpu.VMEM((seq_tile, D_MODEL), bf16),
            pltpu.VMEM((N_SGU_HEADS // 2, BLOCK, 2 * BLOCK), bf16),
            pltpu.VMEM((BLOCK, SGU_WIDTH), f32),
        ],
        compiler_params=pltpu.CompilerParams(
            dimension_semantics=("arbitrary", "arbitrary"),
            vmem_limit_bytes=V7X_VMEM_LIMIT_BYTES),
        name="hybrid_layer",
    )(sinks, x, norm_g, w_in, b_in, ln_g, ln_b, sgu_w, sgu_bt, w_out, b_out, final_g)


def kernel(x, norm_g, w_in, b_in, attn_sinks, sgu_ln_g, sgu_ln_b, sgu_w, sgu_b, w_out, b_out, final_norm_g):
    depth = norm_g.shape[0]
    for l in range(depth):
        last = l == depth - 1
        assert last, "the fused call applies the final norm; only depth 1 is supported"
        x = _layer_call(
            x, attn_sinks[l], norm_g[l][:, None], w_in[l], b_in[l][None, :],
            sgu_ln_g[l][None, :], sgu_ln_b[l][None, :], sgu_w[l], sgu_b[l].T,
            w_out[l], b_out[l][None, :], final_norm_g[None, :])
    return x
```

```python
import functools

import jax
import jax.numpy as jnp
from jax import lax
from jax.experimental import pallas as pl
from jax.experimental.pallas import tpu as pltpu

D_MODEL = 1024
HEAD_DIM = 64
ATTN_WIDTH = 512
KV_WIDTH = 128
SGU_WIDTH = 512
N_KV_HEADS = 2
Q_PER_KV = 4
N_SGU_HEADS = 8
BLOCK = 128
NORM_EPS = 1e-5
NEG_INF = -1e30
ATTN_SCALE = HEAD_DIM ** -0.5

OFF_Q = 0
OFF_K = OFF_Q + ATTN_WIDTH
OFF_V = OFF_K + KV_WIDTH
OFF_ZA = OFF_V + KV_WIDTH
OFF_U = OFF_ZA + ATTN_WIDTH
OFF_VS = OFF_U + SGU_WIDTH
OFF_ZS = OFF_VS + SGU_WIDTH
IN_WIDTH = OFF_ZS + SGU_WIDTH

SEQ_TILE = 1024
MXU_TILE = 256
ROW_CHUNK = 256
W_CHUNK = 128
W_SLOTS = 4
V7X_VMEM_LIMIT_BYTES = 56 * 1024 * 1024

_SQRT_TWO = 1.4142135623730951


def _silu_of_half(hz):
    return hz * (1.0 + jnp.tanh(hz))


def _gelu_of_half(hz):
    return hz * (1.0 + lax.erf(hz * _SQRT_TWO))


def _layer_kernel(sinks_ref, x_ref, ng_ref, win_hbm, bin_ref, lng_ref, lnb_ref,
                  sw_ref, sbt_ref, wout_hbm, bout_ref, fg_ref, o_ref,
                  win_ref, wout_ref, bsc_s, stage_in, stage_out, sem_in, sem_out,
                  q_s, kd_s, vd_s, u_s, vl_s, gs_s, mix_s, wp_s, sb_s, *, seq_tile):
    f32, bf16 = jnp.float32, jnp.bfloat16
    n_sub = seq_tile // BLOCK
    b_idx = pl.program_id(0)
    s_idx = pl.program_id(1)

    lane = lax.broadcasted_iota(jnp.int32, (BLOCK, BLOCK), 1)
    low_half = lane < HEAD_DIM

    @pl.when((b_idx == 0) & (s_idx == 0))
    def _():
        row = lax.broadcasted_iota(jnp.int32, (BLOCK, BLOCK), 0)
        for p in range(N_SGU_HEADS // 2):
            for half in range(2):
                w = jnp.where(row >= lane, sw_ref[2 * p + half], 0.0)
                wp_s[p, :, half * BLOCK:(half + 1) * BLOCK] = w.astype(bf16)
        sbt = sbt_ref[...]
        for hh in range(N_SGU_HEADS):
            sb_s[:, hh * HEAD_DIM:(hh + 1) * HEAD_DIM] = jnp.broadcast_to(
                sbt[:, hh:hh + 1], (BLOCK, HEAD_DIM))
        col = lax.broadcasted_iota(jnp.int32, (1, IN_WIDTH), 1)
        col_scale = jnp.where(col < OFF_K, ATTN_SCALE, jnp.where(col < OFF_ZA, 1.0, 0.5)).astype(f32)
        bsc_s[...] = bin_ref[...] * col_scale
        jobs = ((win_hbm, stage_in, sem_in, win_ref, col_scale),
                (wout_hbm, stage_out, sem_out, wout_ref, None))
        n_chunks = D_MODEL // W_CHUNK

        def chunk_copy(job, c):
            hbm, stage, sem = job[:3]
            return pltpu.make_async_copy(hbm.at[pl.ds(c * W_CHUNK, W_CHUNK), :],
                                         stage.at[c % W_SLOTS], sem.at[c % W_SLOTS])

        for job in jobs:
            for c in range(W_SLOTS):
                chunk_copy(job, c).start(priority=c % 2)
        for job in jobs:
            for c in range(n_chunks):
                chunk_copy(job, c).wait()
                w = job[1][c % W_SLOTS]
                if job[4] is not None:
                    w = w * job[4]
                job[3][c * W_CHUNK:(c + 1) * W_CHUNK, :] = w.astype(bf16)
                if c + W_SLOTS < n_chunks:
                    chunk_copy(job, c + W_SLOTS).start(priority=c % 2)

    @pl.when(s_idx == 0)
    def _():
        for hh in range(N_KV_HEADS):
            kd_s[hh, 0:BLOCK, :] = jnp.zeros((BLOCK, KV_WIDTH), bf16)
            vd_s[hh, 0:BLOCK, :] = jnp.zeros((BLOCK, KV_WIDTH), bf16)

    @pl.when(s_idx > 0)
    def _():
        for hh in range(N_KV_HEADS):
            kd_s[hh, 0:BLOCK, :] = kd_s[hh, seq_tile:seq_tile + BLOCK, :]
            vd_s[hh, 0:BLOCK, :] = vd_s[hh, seq_tile:seq_tile + BLOCK, :]

    h_chunks = []
    for r0 in range(0, seq_tile, ROW_CHUNK):
        x = x_ref[0, r0:r0 + ROW_CHUNK, :]
        ms = jnp.mean(x * x, axis=-1, keepdims=True)
        hc = ((x * lax.rsqrt(ms + NORM_EPS)) * ng_ref[...]).astype(bf16)
        h_chunks.append(hc)
        q = jnp.dot(hc, win_ref[:, OFF_Q:OFF_Q + ATTN_WIDTH], preferred_element_type=f32)
        q_s[r0:r0 + ROW_CHUNK, :] = (q + bsc_s[:, OFF_Q:OFF_Q + ATTN_WIDTH]).astype(bf16)
    h = jnp.concatenate(h_chunks, axis=0)
    half_t = seq_tile // 2

    def proj_tile(off):
        cols = slice(off, off + MXU_TILE)
        return jnp.concatenate(
            [jnp.dot(h[i * half_t:(i + 1) * half_t], win_ref[:, cols], preferred_element_type=f32)
             for i in range(2)], axis=0) + bsc_s[:, cols]

    kv = proj_tile(OFF_K)
    half_mask = lax.broadcasted_iota(jnp.int32, (seq_tile, KV_WIDTH), 1) < HEAD_DIM
    for dup_ref, off in ((kd_s, 0), (vd_s, KV_WIDTH)):
        t = kv[:, off:off + KV_WIDTH]
        t_sw = pltpu.roll(t, HEAD_DIM, axis=1)
        dup_ref[0, BLOCK:BLOCK + seq_tile, :] = jnp.where(half_mask, t, t_sw).astype(bf16)
        dup_ref[1, BLOCK:BLOCK + seq_tile, :] = jnp.where(half_mask, t_sw, t).astype(bf16)

    ga_t = [_silu_of_half(proj_tile(OFF_ZA))]

    row = lax.broadcasted_iota(jnp.int32, (BLOCK, BLOCK), 0)
    prev_side = lane > row
    prev_side4 = jnp.concatenate([prev_side] * Q_PER_KV, axis=0)
    has_prev = s_idx > 0
    units = [(j, hh) for j in range(n_sub) for hh in range(N_KV_HEADS)]
    scores = []
    for j, hh in units:
        rows = slice(j * BLOCK, (j + 1) * BLOCK)
        parts = []
        for g in range(Q_PER_KV):
            c = hh * (Q_PER_KV // 2) + g // 2
            qc = q_s[rows, c * BLOCK:(c + 1) * BLOCK]
            keep = low_half if g % 2 == 0 else jnp.logical_not(low_half)
            parts.append(jnp.where(keep, qc, jnp.zeros_like(qc)))
        qst = jnp.concatenate(parts, axis=0)
        kb = kd_s[hh, j * BLOCK:(j + 2) * BLOCK, :]
        sc = lax.dot_general(qst, kb, (((1,), (1,)), ((), ())),
                             preferred_element_type=f32)
        s_prev = sc[:, 0:BLOCK]
        if j == 0:
            s_prev = jnp.where(has_prev, s_prev, NEG_INF)
        scores.append(jnp.where(prev_side4, s_prev, sc[:, BLOCK:2 * BLOCK]))

    first_lane = lane == 0
    ones_cols = jnp.ones((2 * BLOCK, BLOCK), bf16)
    sink_row = (lax.broadcasted_iota(jnp.int32, (2 * BLOCK, KV_WIDTH), 0) == 0)

    def attend(u):
        j, hh = units[u]
        sc = scores[u]
        probs = []
        for g in range(Q_PER_KV):
            sg = sc[g * BLOCK:(g + 1) * BLOCK, :]
            sink = sinks_ref[hh * Q_PER_KV + g]
            m = jnp.maximum(jnp.max(sg, axis=-1, keepdims=True), sink)
            pb = jnp.exp(sg - m).astype(bf16)
            sink_col = jnp.where(first_lane, jnp.exp(sink - m), 0.0).astype(bf16)
            probs.append(jnp.concatenate([jnp.where(prev_side, pb, sink_col),
                                          jnp.where(prev_side, jnp.zeros_like(pb), pb)], axis=1))
        probs = jnp.concatenate(probs, axis=0)
        vb = vd_s[hh, j * BLOCK:(j + 2) * BLOCK, :]
        vb = jnp.where(sink_row, jnp.zeros_like(vb), vb)
        ext = jnp.dot(probs, jnp.concatenate([vb, ones_cols], axis=1),
                      preferred_element_type=f32)
        o = ext[:, 0:BLOCK] * (1.0 / ext[:, BLOCK:2 * BLOCK])
        o = [o[g * BLOCK:(g + 1) * BLOCK, :] for g in range(Q_PER_KV)]
        return [jnp.where(low_half, o[2 * c2], o[2 * c2 + 1]) for c2 in range(Q_PER_KV // 2)]

    attn_out = []
    n_slots = (IN_WIDTH - OFF_ZA) // MXU_TILE - 1
    units_after = [len(units) * (i + 1) // n_slots - len(units) * i // n_slots for i in range(n_slots)]
    slot = iter(units_after)

    def attend_some():
        for _ in range(next(slot)):
            attn_out.append(attend(len(attn_out)))

    for n in range(1, ATTN_WIDTH // MXU_TILE):
        ga_t.append(_silu_of_half(proj_tile(OFF_ZA + n * MXU_TILE)))
        attend_some()
    for n in range(SGU_WIDTH // MXU_TILE):
        u_s[:, n * MXU_TILE:(n + 1) * MXU_TILE] = _gelu_of_half(proj_tile(OFF_U + n * MXU_TILE))
        attend_some()
    vg = []
    for n in range(SGU_WIDTH // MXU_TILE):
        vg.append(_gelu_of_half(proj_tile(OFF_VS + n * MXU_TILE)))
        attend_some()
    vg = jnp.concatenate(vg, axis=1)
    mu = jnp.mean(vg, axis=-1, keepdims=True)
    vc = vg - mu
    var = jnp.mean(vc * vc, axis=-1, keepdims=True)
    vl_s[...] = ((vc * lax.rsqrt(var + NORM_EPS)) * lng_ref[...] + lnb_ref[...]).astype(bf16)
    for n in range(SGU_WIDTH // MXU_TILE):
        gs_s[:, n * MXU_TILE:(n + 1) * MXU_TILE] = _silu_of_half(proj_tile(OFF_ZS + n * MXU_TILE))
        attend_some()
    assert len(attn_out) == len(units)

    ga = jnp.concatenate(ga_t, axis=1)
    for u, (j, hh) in enumerate(units):
        rows = slice(j * BLOCK, (j + 1) * BLOCK)
        for c2 in range(Q_PER_KV // 2):
            cols = slice((hh * (Q_PER_KV // 2) + c2) * BLOCK, (hh * (Q_PER_KV // 2) + c2 + 1) * BLOCK)
            mix_s[rows, cols] = (attn_out[u][c2] * ga[rows, cols]).astype(bf16)

    for j in range(n_sub):
        rows = slice(j * BLOCK, (j + 1) * BLOCK)
        for p_idx in range(N_SGU_HEADS // 2):
            cols = slice(p_idx * BLOCK, (p_idx + 1) * BLOCK)
            vp = vl_s[rows, cols]
            zero = jnp.zeros_like(vp)
            rhs = jnp.concatenate([jnp.where(low_half, vp, zero),
                                   jnp.where(low_half, zero, vp)], axis=0)
            mixed = jnp.dot(wp_s[p_idx], rhs, preferred_element_type=f32) + sb_s[:, cols]
            mix_s[rows, ATTN_WIDTH + p_idx * BLOCK:ATTN_WIDTH + (p_idx + 1) * BLOCK] = (
                (u_s[rows, cols] * mixed) * gs_s[rows, cols]).astype(bf16)

    y = x_ref[0] + jnp.dot(mix_s[...], wout_ref[...], preferred_element_type=f32) + bout_ref[...]
    ms2 = jnp.mean(y * y, axis=-1, keepdims=True)
    o_ref[0] = (y * lax.rsqrt(ms2 + NORM_EPS)) * fg_ref[...]


def _layer_call(x, sinks, norm_g, w_in, b_in, ln_g, ln_b, sgu_w, sgu_bt, w_out, b_out, final_g,
                *, seq_tile=SEQ_TILE):
    batch, seq, d_model = x.shape
    assert d_model == D_MODEL and seq % seq_tile == 0 and seq_tile % (2 * BLOCK) == 0
    f32, bf16 = jnp.float32, jnp.bfloat16

    def full(shape):
        return pl.BlockSpec(shape, lambda b, s: (0,) * len(shape))

    tile_spec = pl.BlockSpec((1, seq_tile, D_MODEL), lambda b, s: (b, s, 0))
    return pl.pallas_call(
        functools.partial(_layer_kernel, seq_tile=seq_tile),
        grid=(batch, seq // seq_tile),
        in_specs=[
            pl.BlockSpec(memory_space=pltpu.SMEM),
            tile_spec,
            full((1, D_MODEL)),
            pl.BlockSpec(memory_space=pl.ANY),
            full((1, IN_WIDTH)),
            full((1, SGU_WIDTH)),
            full((1, SGU_WIDTH)),
            full((N_SGU_HEADS, BLOCK, BLOCK)),
            full((BLOCK, N_SGU_HEADS)),
            pl.BlockSpec(memory_space=pl.ANY),
            full((1, D_MODEL)),
            full((1, D_MODEL)),
        ],
        out_specs=tile_spec,
        out_shape=jax.ShapeDtypeStruct(x.shape, x.dtype),
        scratch_shapes=[
            pltpu.VMEM((D_MODEL, IN_WIDTH), bf16),
            pltpu.VMEM((D_MODEL, D_MODEL), bf16),
            pltpu.VMEM((1, IN_WIDTH), f32),
            pltpu.VMEM((W_SLOTS, W_CHUNK, IN_WIDTH), f32),
            pltpu.VMEM((W_SLOTS, W_CHUNK, D_MODEL), f32),
            pltpu.SemaphoreType.DMA((W_SLOTS,)),
            pltpu.SemaphoreType.DMA((W_SLOTS,)),
            pltpu.VMEM((seq_tile, ATTN_WIDTH), bf16),
            pltpu.VMEM((N_KV_HEADS, BLOCK + seq_tile, KV_WIDTH), bf16),
            pltpu.VMEM((N_KV_HEADS, BLOCK + seq_tile, KV_WIDTH), bf16),
            pltpu.VMEM((seq_tile, SGU_WIDTH), f32),
            pltpu.VMEM((seq_tile, SGU_WIDTH), bf16),
            pltpu.VMEM((seq_tile, SGU_WIDTH), f32),
            pltpu.VMEM((seq_tile, D_MODEL), bf16),
            pltpu.VMEM((N_SGU_HEADS // 2, BLOCK, 2 * BLOCK), bf16),
            pltpu.VMEM((BLOCK, SGU_WIDTH), f32),
        ],
        compiler_params=pltpu.CompilerParams(
            dimension_semantics=("arbitrary", "arbitrary"),
            vmem_limit_bytes=V7X_VMEM_LIMIT_BYTES),
        name="hybrid_layer",
    )(sinks, x, norm_g, w_in, b_in, ln_g, ln_b, sgu_w, sgu_bt, w_out, b_out, final_g)


def kernel(x, norm_g, w_in, b_in, attn_sinks, sgu_ln_g, sgu_ln_b, sgu_w, sgu_b, w_out, b_out, final_norm_g):
    depth = norm_g.shape[0]
    for l in range(depth):
        last = l == depth - 1
        assert last, "the fused call applies the final norm; only depth 1 is supported"
        x = _layer_call(
            x, attn_sinks[l], norm_g[l][None, :], w_in[l], b_in[l][None, :],
            sgu_ln_g[l][None, :], sgu_ln_b[l][None, :], sgu_w[l], sgu_b[l].T,
            w_out[l], b_out[l][None, :], final_norm_g[None, :])
    return x
```

```python
import functools

import jax
import jax.numpy as jnp
from jax import lax
from jax.experimental import pallas as pl
from jax.experimental.pallas import tpu as pltpu

D_MODEL = 1024
HEAD_DIM = 64
ATTN_WIDTH = 512
KV_WIDTH = 128
SGU_WIDTH = 512
N_KV_HEADS = 2
Q_PER_KV = 4
N_SGU_HEADS = 8
BLOCK = 128
NORM_EPS = 1e-5
NEG_INF = -1e30
ATTN_SCALE = HEAD_DIM ** -0.5

OFF_Q = 0
OFF_K = OFF_Q + ATTN_WIDTH
OFF_V = OFF_K + KV_WIDTH
OFF_ZA = OFF_V + KV_WIDTH
OFF_U = OFF_ZA + ATTN_WIDTH
OFF_VS = OFF_U + SGU_WIDTH
OFF_ZS = OFF_VS + SGU_WIDTH
IN_WIDTH = OFF_ZS + SGU_WIDTH

SEQ_TILE = 1024
MXU_TILE = 256
ROW_CHUNK = 512
W_CHUNK = 128
W_SLOTS = 4
V7X_VMEM_LIMIT_BYTES = 56 * 1024 * 1024

_SQRT_TWO = 1.4142135623730951


def _silu_of_half(hz):
    return hz * (1.0 + jnp.tanh(hz))


def _gelu_of_half(hz):
    return hz * (1.0 + lax.erf(hz * _SQRT_TWO))


def _layer_kernel(sinks_ref, x_ref, ng_ref, win_hbm, bin_ref, lng_ref, lnb_ref,
                  sw_ref, sbt_ref, wout_hbm, bout_ref, fg_ref, o_ref,
                  win_ref, wout_ref, bsc_s, stage_in, stage_out, sem_in, sem_out,
                  q_s, kd_s, vd_s, u_s, vl_s, gs_s, mix_s, wp_s, sb_s, *, seq_tile):
    f32, bf16 = jnp.float32, jnp.bfloat16
    n_sub = seq_tile // BLOCK
    b_idx = pl.program_id(0)
    s_idx = pl.program_id(1)

    lane = lax.broadcasted_iota(jnp.int32, (BLOCK, BLOCK), 1)
    low_half = lane < HEAD_DIM

    @pl.when((b_idx == 0) & (s_idx == 0))
    def _():
        row = lax.broadcasted_iota(jnp.int32, (BLOCK, BLOCK), 0)
        for p in range(N_SGU_HEADS // 2):
            for half in range(2):
                w = jnp.where(row >= lane, sw_ref[2 * p + half], 0.0)
                wp_s[p, :, half * BLOCK:(half + 1) * BLOCK] = w.astype(bf16)
        sbt = sbt_ref[...]
        for hh in range(N_SGU_HEADS):
            sb_s[:, hh * HEAD_DIM:(hh + 1) * HEAD_DIM] = jnp.broadcast_to(
                sbt[:, hh:hh + 1], (BLOCK, HEAD_DIM))
        col = lax.broadcasted_iota(jnp.int32, (1, IN_WIDTH), 1)
        col_scale = jnp.where(col < OFF_K, ATTN_SCALE, jnp.where(col < OFF_ZA, 1.0, 0.5)).astype(f32)
        bsc_s[...] = bin_ref[...] * col_scale
        jobs = ((win_hbm, stage_in, sem_in, win_ref, col_scale),
                (wout_hbm, stage_out, sem_out, wout_ref, None))
        n_chunks = D_MODEL // W_CHUNK

        def chunk_copy(job, c):
            hbm, stage, sem = job[:3]
            return pltpu.make_async_copy(hbm.at[pl.ds(c * W_CHUNK, W_CHUNK), :],
                                         stage.at[c % W_SLOTS], sem.at[c % W_SLOTS])

        for job in jobs:
            for c in range(W_SLOTS):
                chunk_copy(job, c).start(priority=c % 2)
        for job in jobs:
            for c in range(n_chunks):
                chunk_copy(job, c).wait()
                w = job[1][c % W_SLOTS]
                if job[4] is not None:
                    w = w * job[4]
                job[3][c * W_CHUNK:(c + 1) * W_CHUNK, :] = w.astype(bf16)
                if c + W_SLOTS < n_chunks:
                    chunk_copy(job, c + W_SLOTS).start(priority=c % 2)

    @pl.when(s_idx == 0)
    def _():
        for hh in range(N_KV_HEADS):
            kd_s[hh, 0:BLOCK, :] = jnp.zeros((BLOCK, KV_WIDTH), bf16)
            vd_s[hh, 0:BLOCK, :] = jnp.zeros((BLOCK, KV_WIDTH), bf16)

    @pl.when(s_idx > 0)
    def _():
        for hh in range(N_KV_HEADS):
            kd_s[hh, 0:BLOCK, :] = kd_s[hh, seq_tile:seq_tile + BLOCK, :]
            vd_s[hh, 0:BLOCK, :] = vd_s[hh, seq_tile:seq_tile + BLOCK, :]

    h_chunks = []
    for r0 in range(0, seq_tile, ROW_CHUNK):
        x = x_ref[0, r0:r0 + ROW_CHUNK, :]
        ms = jnp.mean(x * x, axis=-1, keepdims=True)
        hc = ((x * lax.rsqrt(ms + NORM_EPS)) * ng_ref[...]).astype(bf16)
        h_chunks.append(hc)
        q = jnp.dot(hc, win_ref[:, OFF_Q:OFF_Q + ATTN_WIDTH], preferred_element_type=f32)
        q_s[r0:r0 + ROW_CHUNK, :] = (q + bsc_s[:, OFF_Q:OFF_Q + ATTN_WIDTH]).astype(bf16)
    h = jnp.concatenate(h_chunks, axis=0)
    half_t = seq_tile // 2

    def proj_tile(off):
        cols = slice(off, off + MXU_TILE)
        return jnp.concatenate(
            [jnp.dot(h[i * half_t:(i + 1) * half_t], win_ref[:, cols], preferred_element_type=f32)
             for i in range(2)], axis=0) + bsc_s[:, cols]

    kv = proj_tile(OFF_K)
    half_mask = lax.broadcasted_iota(jnp.int32, (seq_tile, KV_WIDTH), 1) < HEAD_DIM
    for dup_ref, off in ((kd_s, 0), (vd_s, KV_WIDTH)):
        t = kv[:, off:off + KV_WIDTH]
        t_sw = pltpu.roll(t, HEAD_DIM, axis=1)
        dup_ref[0, BLOCK:BLOCK + seq_tile, :] = jnp.where(half_mask, t, t_sw).astype(bf16)
        dup_ref[1, BLOCK:BLOCK + seq_tile, :] = jnp.where(half_mask, t_sw, t).astype(bf16)

    ga_t = [_silu_of_half(proj_tile(OFF_ZA))]

    row = lax.broadcasted_iota(jnp.int32, (BLOCK, BLOCK), 0)
    prev_side = lane > row
    prev_side4 = jnp.concatenate([prev_side] * Q_PER_KV, axis=0)
    has_prev = s_idx > 0
    units = [(j, hh) for j in range(n_sub) for hh in range(N_KV_HEADS)]
    scores = []
    for j, hh in units:
        rows = slice(j * BLOCK, (j + 1) * BLOCK)
        parts = []
        for g in range(Q_PER_KV):
            c = hh * (Q_PER_KV // 2) + g // 2
            qc = q_s[rows, c * BLOCK:(c + 1) * BLOCK]
            keep = low_half if g % 2 == 0 else jnp.logical_not(low_half)
            parts.append(jnp.where(keep, qc, jnp.zeros_like(qc)))
        qst = jnp.concatenate(parts, axis=0)
        kb = kd_s[hh, j * BLOCK:(j + 2) * BLOCK, :]
        sc = lax.dot_general(qst, kb, (((1,), (1,)), ((), ())),
                             preferred_element_type=f32)
        s_prev = sc[:, 0:BLOCK]
        if j == 0:
            s_prev = jnp.where(has_prev, s_prev, NEG_INF)
        scores.append(jnp.where(prev_side4, s_prev, sc[:, BLOCK:2 * BLOCK]))

    def attend(u):
        j, hh = units[u]
        sc = scores[u]
        probs, inv_denoms = [], []
        for g in range(Q_PER_KV):
            sg = sc[g * BLOCK:(g + 1) * BLOCK, :]
            sink = sinks_ref[hh * Q_PER_KV + g]
            m = jnp.maximum(jnp.max(sg, axis=-1, keepdims=True), sink)
            p = jnp.exp(sg - m)
            denom = jnp.sum(p, axis=-1, keepdims=True) + jnp.exp(sink - m)
            inv_denoms.append(1.0 / denom)
            pb = p.astype(bf16)
            zero = jnp.zeros_like(pb)
            probs.append(jnp.concatenate([jnp.where(prev_side, pb, zero),
                                          jnp.where(prev_side, zero, pb)], axis=1))
        probs = jnp.concatenate(probs, axis=0)
        vb = vd_s[hh, j * BLOCK:(j + 2) * BLOCK, :]
        o = jnp.dot(probs, vb, preferred_element_type=f32)
        o = [o[g * BLOCK:(g + 1) * BLOCK, :] * inv_denoms[g] for g in range(Q_PER_KV)]
        return [jnp.where(low_half, o[2 * c2], o[2 * c2 + 1]) for c2 in range(Q_PER_KV // 2)]

    attn_out = []
    n_slots = (IN_WIDTH - OFF_ZA) // MXU_TILE - 1
    units_after = [len(units) * (i + 1) // n_slots - len(units) * i // n_slots for i in range(n_slots)]
    slot = iter(units_after)

    def attend_some():
        for _ in range(next(slot)):
            attn_out.append(attend(len(attn_out)))

    for n in range(1, ATTN_WIDTH // MXU_TILE):
        ga_t.append(_silu_of_half(proj_tile(OFF_ZA + n * MXU_TILE)))
        attend_some()
    for n in range(SGU_WIDTH // MXU_TILE):
        u_s[:, n * MXU_TILE:(n + 1) * MXU_TILE] = _gelu_of_half(proj_tile(OFF_U + n * MXU_TILE))
        attend_some()
    vg = []
    for n in range(SGU_WIDTH // MXU_TILE):
        vg.append(_gelu_of_half(proj_tile(OFF_VS + n * MXU_TILE)))
        attend_some()
    vg = jnp.concatenate(vg, axis=1)
    mu = jnp.mean(vg, axis=-1, keepdims=True)
    vc = vg - mu
    var = jnp.mean(vc * vc, axis=-1, keepdims=True)
    vl_s[...] = ((vc * lax.rsqrt(var + NORM_EPS)) * lng_ref[...] + lnb_ref[...]).astype(bf16)
    for n in range(SGU_WIDTH // MXU_TILE):
        gs_s[:, n * MXU_TILE:(n + 1) * MXU_TILE] = _silu_of_half(proj_tile(OFF_ZS + n * MXU_TILE))
        attend_some()
    assert len(attn_out) == len(units)

    ga = jnp.concatenate(ga_t, axis=1)
    for u, (j, hh) in enumerate(units):
        rows = slice(j * BLOCK, (j + 1) * BLOCK)
        for c2 in range(Q_PER_KV // 2):
            cols = slice((hh * (Q_PER_KV // 2) + c2) * BLOCK, (hh * (Q_PER_KV // 2) + c2 + 1) * BLOCK)
            mix_s[rows, cols] = (attn_out[u][c2] * ga[rows, cols]).astype(bf16)

    for j in range(n_sub):
        rows = slice(j * BLOCK, (j + 1) * BLOCK)
        for p_idx in range(N_SGU_HEADS // 2):
            cols = slice(p_idx * BLOCK, (p_idx + 1) * BLOCK)
            vp = vl_s[rows, cols]
            zero = jnp.zeros_like(vp)
            rhs = jnp.concatenate([jnp.where(low_half, vp, zero),
                                   jnp.where(low_half, zero, vp)], axis=0)
            mixed = jnp.dot(wp_s[p_idx], rhs, preferred_element_type=f32) + sb_s[:, cols]
            mix_s[rows, ATTN_WIDTH + p_idx * BLOCK:ATTN_WIDTH + (p_idx + 1) * BLOCK] = (
                (u_s[rows, cols] * mixed) * gs_s[rows, cols]).astype(bf16)

    y = x_ref[0] + jnp.dot(mix_s[...], wout_ref[...], preferred_element_type=f32) + bout_ref[...]
    ms2 = jnp.mean(y * y, axis=-1, keepdims=True)
    o_ref[0] = (y * lax.rsqrt(ms2 + NORM_EPS)) * fg_ref[...]


def _layer_call(x, sinks, norm_g, w_in, b_in, ln_g, ln_b, sgu_w, sgu_bt, w_out, b_out, final_g,
                *, seq_tile=SEQ_TILE):
    batch, seq, d_model = x.shape
    assert d_model == D_MODEL and seq % seq_tile == 0 and seq_tile % (2 * BLOCK) == 0
    f32, bf16 = jnp.float32, jnp.bfloat16

    def full(shape):
        return pl.BlockSpec(shape, lambda b, s: (0,) * len(shape))

    tile_spec = pl.BlockSpec((1, seq_tile, D_MODEL), lambda b, s: (b, s, 0))
    return pl.pallas_call(
        functools.partial(_layer_kernel, seq_tile=seq_tile),
        grid=(batch, seq // seq_tile),
        in_specs=[
            pl.BlockSpec(memory_space=pltpu.SMEM),
            tile_spec,
            full((1, D_MODEL)),
            pl.BlockSpec(memory_space=pl.ANY),
            full((1, IN_WIDTH)),
            full((1, SGU_WIDTH)),
            full((1, SGU_WIDTH)),
            full((N_SGU_HEADS, BLOCK, BLOCK)),
            full((BLOCK, N_SGU_HEADS)),
            pl.BlockSpec(memory_space=pl.ANY),
            full((1, D_MODEL)),
            full((1, D_MODEL)),
        ],
        out_specs=tile_spec,
        out_shape=jax.ShapeDtypeStruct(x.shape, x.dtype),
        scratch_shapes=[
            pltpu.VMEM((D_MODEL, IN_WIDTH), bf16),
            pltpu.VMEM((D_MODEL, D_MODEL), bf16),
            pltpu.VMEM((1, IN_WIDTH), f32),
            pltpu.VMEM((W_SLOTS, W_CHUNK, IN_WIDTH), f32),
            pltpu.VMEM((W_SLOTS, W_CHUNK, D_MODEL), f32),
            pltpu.SemaphoreType.DMA((W_SLOTS,)),
            pltpu.SemaphoreType.DMA((W_SLOTS,)),
            pltpu.VMEM((seq_tile, ATTN_WIDTH), bf16),
            pltpu.VMEM((N_KV_HEADS, BLOCK + seq_tile, KV_WIDTH), bf16),
            pltpu.VMEM((N_KV_HEADS, BLOCK + seq_tile, KV_WIDTH), bf16),
            pltpu.VMEM((seq_tile, SGU_WIDTH), f32),
            pltpu.VMEM((seq_tile, SGU_WIDTH), bf16),
            pltpu.VMEM((seq_tile, SGU_WIDTH), f32),
            pltpu.VMEM((seq_tile, D_MODEL), bf16),
            pltpu.VMEM((N_SGU_HEADS // 2, BLOCK, 2 * BLOCK), bf16),
            pltpu.VMEM((BLOCK, SGU_WIDTH), f32),
        ],
        compiler_params=pltpu.CompilerParams(
            dimension_semantics=("arbitrary", "arbitrary"),
            vmem_limit_bytes=V7X_VMEM_LIMIT_BYTES),
        name="hybrid_layer",
    )(sinks, x, norm_g, w_in, b_in, ln_g, ln_b, sgu_w, sgu_bt, w_out, b_out, final_g)


def kernel(x, norm_g, w_in, b_in, attn_sinks, sgu_ln_g, sgu_ln_b, sgu_w, sgu_b, w_out, b_out, final_norm_g):
    depth = norm_g.shape[0]
    for l in range(depth):
        last = l == depth - 1
        assert last, "the fused call applies the final norm; only depth 1 is supported"
        x = _layer_call(
            x, attn_sinks[l], norm_g[l][None, :], w_in[l], b_in[l][None, :],
            sgu_ln_g[l][None, :], sgu_ln_b[l][None, :], sgu_w[l], sgu_b[l].T,
            w_out[l], b_out[l][None, :], final_norm_g[None, :])
    return x
```

```python
import collections
import functools

import jax
import jax.numpy as jnp
from jax import lax
from jax.experimental import pallas as pl
from jax.experimental.pallas import tpu as pltpu

D_MODEL = 1024
HEAD_DIM = 64
ATTN_WIDTH = 512
KV_WIDTH = 128
SGU_WIDTH = 512
N_KV_HEADS = 2
Q_PER_KV = 4
N_SGU_HEADS = 8
BLOCK = 128
NORM_EPS = 1e-5
NEG_INF = -1e30
ATTN_SCALE = HEAD_DIM ** -0.5

OFF_Q = 0
OFF_K = OFF_Q + ATTN_WIDTH
OFF_V = OFF_K + KV_WIDTH
OFF_ZA = OFF_V + KV_WIDTH
OFF_U = OFF_ZA + ATTN_WIDTH
OFF_VS = OFF_U + SGU_WIDTH
OFF_ZS = OFF_VS + SGU_WIDTH
IN_WIDTH = OFF_ZS + SGU_WIDTH

SEQ_TILE = 1024
MXU_TILE = 256
ROW_CHUNK = 256
W_CHUNK = 128
W_SLOTS = 4
V7X_VMEM_LIMIT_BYTES = 56 * 1024 * 1024

_SQRT_TWO = 1.4142135623730951

_WeightCopy = collections.namedtuple("_WeightCopy", "hbm stage sem dst col_scale")


def _silu_of_half(hz):
    return hz * (1.0 + jnp.tanh(hz))


def _gelu_of_half(hz):
    return hz * (1.0 + lax.erf(hz * _SQRT_TWO))


def _layer_kernel(sinks_ref, x_ref, ng_ref, win_hbm, bin_ref, lng_ref, lnb_ref,
                  sw_ref, sbt_ref, wout_hbm, bout_ref, fg_ref, o_ref,
                  win_ref, wout_ref, bsc_s, stage_in, stage_out, sem_in, sem_out,
                  q_s, kd_s, vd_s, u_s, vl_s, gs_s, mix_s, wp_s, sb_s, *, seq_tile):
    f32, bf16 = jnp.float32, jnp.bfloat16
    n_sub = seq_tile // BLOCK
    b_idx = pl.program_id(0)
    s_idx = pl.program_id(1)

    lane = lax.broadcasted_iota(jnp.int32, (BLOCK, BLOCK), 1)
    low_half = lane < HEAD_DIM

    @pl.when((b_idx == 0) & (s_idx == 0))
    def _():
        row = lax.broadcasted_iota(jnp.int32, (BLOCK, BLOCK), 0)
        for p in range(N_SGU_HEADS // 2):
            for half in range(2):
                w = jnp.where(row >= lane, sw_ref[2 * p + half], 0.0)
                wp_s[p, :, half * BLOCK:(half + 1) * BLOCK] = w.astype(bf16)
        sbt = sbt_ref[...]
        for hh in range(N_SGU_HEADS):
            sb_s[:, hh * HEAD_DIM:(hh + 1) * HEAD_DIM] = jnp.broadcast_to(
                sbt[:, hh:hh + 1], (BLOCK, HEAD_DIM))
        col = lax.broadcasted_iota(jnp.int32, (1, IN_WIDTH), 1)
        col_scale = jnp.where(col < OFF_K, ATTN_SCALE, jnp.where(col < OFF_ZA, 1.0, 0.5)).astype(f32)
        bsc_s[...] = bin_ref[...] * col_scale
        jobs = (_WeightCopy(win_hbm, stage_in, sem_in, win_ref, col_scale),
                _WeightCopy(wout_hbm, stage_out, sem_out, wout_ref, None))
        n_chunks = D_MODEL // W_CHUNK

        def chunk_copy(job, c):
            return pltpu.make_async_copy(job.hbm.at[pl.ds(c * W_CHUNK, W_CHUNK), :],
                                         job.stage.at[c % W_SLOTS], job.sem.at[c % W_SLOTS])

        for job in jobs:
            for c in range(W_SLOTS):
                chunk_copy(job, c).start(priority=c % 2)
        for job in jobs:
            for c in range(n_chunks):
                chunk_copy(job, c).wait()
                w = job.stage[c % W_SLOTS]
                if job.col_scale is not None:
                    w = w * job.col_scale
                job.dst[c * W_CHUNK:(c + 1) * W_CHUNK, :] = w.astype(bf16)
                if c + W_SLOTS < n_chunks:
                    chunk_copy(job, c + W_SLOTS).start(priority=c % 2)

    @pl.when(s_idx == 0)
    def _():
        for hh in range(N_KV_HEADS):
            kd_s[hh, 0:BLOCK, :] = jnp.zeros((BLOCK, KV_WIDTH), bf16)
            vd_s[hh, 0:BLOCK, :] = jnp.zeros((BLOCK, KV_WIDTH), bf16)

    @pl.when(s_idx > 0)
    def _():
        for hh in range(N_KV_HEADS):
            kd_s[hh, 0:BLOCK, :] = kd_s[hh, seq_tile:seq_tile + BLOCK, :]
            vd_s[hh, 0:BLOCK, :] = vd_s[hh, seq_tile:seq_tile + BLOCK, :]

    h_chunks = []
    for r0 in range(0, seq_tile, ROW_CHUNK):
        x = x_ref[0, r0:r0 + ROW_CHUNK, :]
        ms = jnp.mean(x * x, axis=-1, keepdims=True)
        hc = ((x * lax.rsqrt(ms + NORM_EPS)) * ng_ref[...]).astype(bf16)
        h_chunks.append(hc)
        q = jnp.dot(hc, win_ref[:, OFF_Q:OFF_Q + ATTN_WIDTH], preferred_element_type=f32)
        q_s[r0:r0 + ROW_CHUNK, :] = (q + bsc_s[:, OFF_Q:OFF_Q + ATTN_WIDTH]).astype(bf16)
    h = jnp.concatenate(h_chunks, axis=0)
    half_t = seq_tile // 2

    def proj_tile(off):
        cols = slice(off, off + MXU_TILE)
        return jnp.concatenate(
            [jnp.dot(h[i * half_t:(i + 1) * half_t], win_ref[:, cols], preferred_element_type=f32)
             for i in range(2)], axis=0) + bsc_s[:, cols]

    kv = proj_tile(OFF_K)
    half_mask = lax.broadcasted_iota(jnp.int32, (seq_tile, KV_WIDTH), 1) < HEAD_DIM
    for dup_ref, off in ((kd_s, 0), (vd_s, KV_WIDTH)):
        t = kv[:, off:off + KV_WIDTH]
        t_sw = pltpu.roll(t, HEAD_DIM, axis=1)
        dup_ref[0, BLOCK:BLOCK + seq_tile, :] = jnp.where(half_mask, t, t_sw).astype(bf16)
        dup_ref[1, BLOCK:BLOCK + seq_tile, :] = jnp.where(half_mask, t_sw, t).astype(bf16)

    ga_t = [_silu_of_half(proj_tile(OFF_ZA))]

    row = lax.broadcasted_iota(jnp.int32, (BLOCK, BLOCK), 0)
    prev_side = lane > row
    prev_side4 = jnp.concatenate([prev_side] * Q_PER_KV, axis=0)
    has_prev = s_idx > 0
    units = [(j, hh) for j in range(n_sub) for hh in range(N_KV_HEADS)]
    scores = []
    for j, hh in units:
        rows = slice(j * BLOCK, (j + 1) * BLOCK)
        parts = []
        for g in range(Q_PER_KV):
            c = hh * (Q_PER_KV // 2) + g // 2
            qc = q_s[rows, c * BLOCK:(c + 1) * BLOCK]
            keep = low_half if g % 2 == 0 else jnp.logical_not(low_half)
            parts.append(jnp.where(keep, qc, jnp.zeros_like(qc)))
        qst = jnp.concatenate(parts, axis=0)
        kb = kd_s[hh, j * BLOCK:(j + 2) * BLOCK, :]
        sc = lax.dot_general(qst, kb, (((1,), (1,)), ((), ())),
                             preferred_element_type=f32)
        s_prev = sc[:, 0:BLOCK]
        if j == 0:
            s_prev = jnp.where(has_prev, s_prev, NEG_INF)
        scores.append(jnp.where(prev_side4, s_prev, sc[:, BLOCK:2 * BLOCK]))

    def attend(u):
        j, hh = units[u]
        sc = scores[u]
        probs, inv_denoms = [], []
        for g in range(Q_PER_KV):
            sg = sc[g * BLOCK:(g + 1) * BLOCK, :]
            sink = sinks_ref[hh * Q_PER_KV + g]
            m = jnp.maximum(jnp.max(sg, axis=-1, keepdims=True), sink)
            p = jnp.exp(sg - m)
            denom = jnp.sum(p, axis=-1, keepdims=True) + jnp.exp(sink - m)
            inv_denoms.append(1.0 / denom)
            pb = p.astype(bf16)
            zero = jnp.zeros_like(pb)
            probs.append(jnp.concatenate([jnp.where(prev_side, pb, zero),
                                          jnp.where(prev_side, zero, pb)], axis=1))
        probs = jnp.concatenate(probs, axis=0)
        vb = vd_s[hh, j * BLOCK:(j + 2) * BLOCK, :]
        o = jnp.dot(probs, vb, preferred_element_type=f32)
        o = [o[g * BLOCK:(g + 1) * BLOCK, :] * inv_denoms[g] for g in range(Q_PER_KV)]
        return [jnp.where(low_half, o[2 * c2], o[2 * c2 + 1]) for c2 in range(Q_PER_KV // 2)]

    attn_out = []
    n_slots = (IN_WIDTH - OFF_ZA) // MXU_TILE - 1
    units_after = [len(units) * (i + 1) // n_slots - len(units) * i // n_slots for i in range(n_slots)]
    slot = iter(units_after)

    def attend_some():
        for _ in range(next(slot)):
            attn_out.append(attend(len(attn_out)))

    for n in range(1, ATTN_WIDTH // MXU_TILE):
        ga_t.append(_silu_of_half(proj_tile(OFF_ZA + n * MXU_TILE)))
        attend_some()
    for n in range(SGU_WIDTH // MXU_TILE):
        u_s[:, n * MXU_TILE:(n + 1) * MXU_TILE] = _gelu_of_half(proj_tile(OFF_U + n * MXU_TILE))
        attend_some()
    vg = []
    for n in range(SGU_WIDTH // MXU_TILE):
        vg.append(_gelu_of_half(proj_tile(OFF_VS + n * MXU_TILE)))
        attend_some()
    vg = jnp.concatenate(vg, axis=1)
    mu = jnp.mean(vg, axis=-1, keepdims=True)
    vc = vg - mu
    var = jnp.mean(vc * vc, axis=-1, keepdims=True)
    vl_s[...] = ((vc * lax.rsqrt(var + NORM_EPS)) * lng_ref[...] + lnb_ref[...]).astype(bf16)
    for n in range(SGU_WIDTH // MXU_TILE):
        gs_s[:, n * MXU_TILE:(n + 1) * MXU_TILE] = _silu_of_half(proj_tile(OFF_ZS + n * MXU_TILE))
        attend_some()
    assert len(attn_out) == len(units)

    ga = jnp.concatenate(ga_t, axis=1)
    for u, (j, hh) in enumerate(units):
        rows = slice(j * BLOCK, (j + 1) * BLOCK)
        for c2 in range(Q_PER_KV // 2):
            cols = slice((hh * (Q_PER_KV // 2) + c2) * BLOCK, (hh * (Q_PER_KV // 2) + c2 + 1) * BLOCK)
            mix_s[rows, cols] = (attn_out[u][c2] * ga[rows, cols]).astype(bf16)

    for j in range(n_sub):
        rows = slice(j * BLOCK, (j + 1) * BLOCK)
        for p_idx in range(N_SGU_HEADS // 2):
            cols = slice(p_idx * BLOCK, (p_idx + 1) * BLOCK)
            vp = vl_s[rows, cols]
            zero = jnp.zeros_like(vp)
            rhs = jnp.concatenate([jnp.where(low_half, vp, zero),
                                   jnp.where(low_half, zero, vp)], axis=0)
            mixed = jnp.dot(wp_s[p_idx], rhs, preferred_element_type=f32) + sb_s[:, cols]
            mix_s[rows, ATTN_WIDTH + p_idx * BLOCK:ATTN_WIDTH + (p_idx + 1) * BLOCK] = (
                (u_s[rows, cols] * mixed) * gs_s[rows, cols]).astype(bf16)

    y = x_ref[0] + jnp.dot(mix_s[...], wout_ref[...], preferred_element_type=f32) + bout_ref[...]
    ms2 = jnp.mean(y * y, axis=-1, keepdims=True)
    o_ref[0] = (y * lax.rsqrt(ms2 + NORM_EPS)) * fg_ref[...]


def _layer_call(x, sinks, norm_g, w_in, b_in, ln_g, ln_b, sgu_w, sgu_bt, w_out, b_out, final_g,
                *, seq_tile=SEQ_TILE):
    batch, seq, d_model = x.shape
    assert d_model == D_MODEL and seq % seq_tile == 0 and seq_tile % (2 * BLOCK) == 0
    f32, bf16 = jnp.float32, jnp.bfloat16

    def full(shape):
        return pl.BlockSpec(shape, lambda b, s: (0,) * len(shape))

    tile_spec = pl.BlockSpec((1, seq_tile, D_MODEL), lambda b, s: (b, s, 0))
    return pl.pallas_call(
        functools.partial(_layer_kernel, seq_tile=seq_tile),
        grid=(batch, seq // seq_tile),
        in_specs=[
            pl.BlockSpec(memory_space=pltpu.SMEM),
            tile_spec,
            full((1, D_MODEL)),
            pl.BlockSpec(memory_space=pl.ANY),
            full((1, IN_WIDTH)),
            full((1, SGU_WIDTH)),
            full((1, SGU_WIDTH)),
            full((N_SGU_HEADS, BLOCK, BLOCK)),
            full((BLOCK, N_SGU_HEADS)),
            pl.BlockSpec(memory_space=pl.ANY),
            full((1, D_MODEL)),
            full((1, D_MODEL)),
        ],
        out_specs=tile_spec,
        out_shape=jax.ShapeDtypeStruct(x.shape, x.dtype),
        scratch_shapes=[
            pltpu.VMEM((D_MODEL, IN_WIDTH), bf16),
            pltpu.VMEM((D_MODEL, D_MODEL), bf16),
            pltpu.VMEM((1, IN_WIDTH), f32),
            pltpu.VMEM((W_SLOTS, W_CHUNK, IN_WIDTH), f32),
            pltpu.VMEM((W_SLOTS, W_CHUNK, D_MODEL), f32),
            pltpu.SemaphoreType.DMA((W_SLOTS,)),
            pltpu.SemaphoreType.DMA((W_SLOTS,)),
            pltpu.VMEM((seq_tile, ATTN_WIDTH), bf16),
            pltpu.VMEM((N_KV_HEADS, BLOCK + seq_tile, KV_WIDTH), bf16),
            pltpu.VMEM((N_KV_HEADS, BLOCK + seq_tile, KV_WIDTH), bf16),
            pltpu.VMEM((seq_tile, SGU_WIDTH), f32),
            pltpu.VMEM((seq_tile, SGU_WIDTH), bf16),
            pltpu.VMEM((seq_tile, SGU_WIDTH), f32),
            pltpu.VMEM((seq_tile, D_MODEL), bf16),
            pltpu.VMEM((N_SGU_HEADS // 2, BLOCK, 2 * BLOCK), bf16),
            pltpu.VMEM((BLOCK, SGU_WIDTH), f32),
        ],
        compiler_params=pltpu.CompilerParams(
            dimension_semantics=("arbitrary", "arbitrary"),
            vmem_limit_bytes=V7X_VMEM_LIMIT_BYTES),
        name="hybrid_layer",
    )(sinks, x, norm_g, w_in, b_in, ln_g, ln_b, sgu_w, sgu_bt, w_out, b_out, final_g)


def kernel(x, norm_g, w_in, b_in, attn_sinks, sgu_ln_g, sgu_ln_b, sgu_w, sgu_b, w_out, b_out, final_norm_g):
    depth = norm_g.shape[0]
    for l in range(depth):
        last = l == depth - 1
        assert last, "the fused call applies the final norm; only depth 1 is supported"
        x = _layer_call(
            x, attn_sinks[l], norm_g[l][None, :], w_in[l], b_in[l][None, :],
            sgu_ln_g[l][None, :], sgu_ln_b[l][None, :], sgu_w[l], sgu_b[l].T,
            w_out[l], b_out[l][None, :], final_norm_g[None, :])
    return x
```

```python
import collections
import functools

import jax
import jax.numpy as jnp
from jax import lax
from jax.experimental import pallas as pl
from jax.experimental.pallas import tpu as pltpu

D_MODEL = 1024
HEAD_DIM = 64
ATTN_WIDTH = 512
KV_WIDTH = 128
SGU_WIDTH = 512
N_KV_HEADS = 2
Q_PER_KV = 4
N_SGU_HEADS = 8
BLOCK = 128
NORM_EPS = 1e-5
NEG_INF = -1e30
ATTN_SCALE = HEAD_DIM ** -0.5

OFF_Q = 0
OFF_K = OFF_Q + ATTN_WIDTH
OFF_V = OFF_K + KV_WIDTH
OFF_ZA = OFF_V + KV_WIDTH
OFF_U = OFF_ZA + ATTN_WIDTH
OFF_VS = OFF_U + SGU_WIDTH
OFF_ZS = OFF_VS + SGU_WIDTH
IN_WIDTH = OFF_ZS + SGU_WIDTH

SEQ_TILE = 1024
MXU_TILE = 256
ROW_CHUNK = 256
W_CHUNK = 128
W_SLOTS = 4
V7X_VMEM_LIMIT_BYTES = 56 * 1024 * 1024

_SQRT_TWO = 1.4142135623730951

_WeightCopy = collections.namedtuple("_WeightCopy", "hbm stage sem dst col_scale")


def _silu_of_half(hz):
    return hz * (1.0 + jnp.tanh(hz))


def _gelu_of_half(hz):
    return hz * (1.0 + lax.erf(hz * _SQRT_TWO))


def _layer_kernel(sinks_ref, x_ref, ng_ref, win_hbm, bin_ref, lng_ref, lnb_ref,
                  sw_ref, sbt_ref, wout_hbm, bout_ref, fg_ref, o_ref,
                  win_ref, wout_ref, bsc_s, stage_in, stage_out, sem_in, sem_out,
                  q_s, kd_s, vd_s, u_s, vl_s, gs_s, mix_s, wp_s, sb_s, *, seq_tile):
    f32, bf16 = jnp.float32, jnp.bfloat16
    n_sub = seq_tile // BLOCK
    b_idx = pl.program_id(0)
    s_idx = pl.program_id(1)

    lane = lax.broadcasted_iota(jnp.int32, (BLOCK, BLOCK), 1)
    low_half = lane < HEAD_DIM

    @pl.when((b_idx == 0) & (s_idx == 0))
    def _():
        row = lax.broadcasted_iota(jnp.int32, (BLOCK, BLOCK), 0)
        for p in range(N_SGU_HEADS // 2):
            for half in range(2):
                w = jnp.where(row >= lane, sw_ref[2 * p + half], 0.0)
                wp_s[p, :, half * BLOCK:(half + 1) * BLOCK] = w.astype(bf16)
        sbt = sbt_ref[...]
        for hh in range(N_SGU_HEADS):
            sb_s[:, hh * HEAD_DIM:(hh + 1) * HEAD_DIM] = jnp.broadcast_to(
                sbt[:, hh:hh + 1], (BLOCK, HEAD_DIM))
        col = lax.broadcasted_iota(jnp.int32, (1, IN_WIDTH), 1)
        col_scale = jnp.where(col < OFF_K, ATTN_SCALE, jnp.where(col < OFF_ZA, 1.0, 0.5)).astype(f32)
        bsc_s[...] = bin_ref[...] * col_scale
        jobs = (_WeightCopy(win_hbm, stage_in, sem_in, win_ref, col_scale),
                _WeightCopy(wout_hbm, stage_out, sem_out, wout_ref, None))
        n_chunks = D_MODEL // W_CHUNK

        def chunk_copy(job, c):
            return pltpu.make_async_copy(job.hbm.at[pl.ds(c * W_CHUNK, W_CHUNK), :],
                                         job.stage.at[c % W_SLOTS], job.sem.at[c % W_SLOTS])

        for job in jobs:
            for c in range(W_SLOTS):
                chunk_copy(job, c).start(priority=c % 2)
        for job in jobs:
            for c in range(n_chunks):
                chunk_copy(job, c).wait()
                w = job.stage[c % W_SLOTS]
                if job.col_scale is not None:
                    w = w * job.col_scale
                job.dst[c * W_CHUNK:(c + 1) * W_CHUNK, :] = w.astype(bf16)
                if c + W_SLOTS < n_chunks:
                    chunk_copy(job, c + W_SLOTS).start(priority=c % 2)

    @pl.when(s_idx == 0)
    def _():
        for hh in range(N_KV_HEADS):
            kd_s[hh, 0:BLOCK, :] = jnp.zeros((BLOCK, KV_WIDTH), bf16)
            vd_s[hh, 0:BLOCK, :] = jnp.zeros((BLOCK, KV_WIDTH), bf16)

    @pl.when(s_idx > 0)
    def _():
        for hh in range(N_KV_HEADS):
            kd_s[hh, 0:BLOCK, :] = kd_s[hh, seq_tile:seq_tile + BLOCK, :]
            vd_s[hh, 0:BLOCK, :] = vd_s[hh, seq_tile:seq_tile + BLOCK, :]

    h_chunks = []
    for r0 in range(0, seq_tile, ROW_CHUNK):
        x = x_ref[0, r0:r0 + ROW_CHUNK, :]
        ms = jnp.mean(x * x, axis=-1, keepdims=True)
        hc = ((x * lax.rsqrt(ms + NORM_EPS)) * ng_ref[...]).astype(bf16)
        h_chunks.append(hc)
        q = jnp.dot(hc, win_ref[:, OFF_Q:OFF_Q + ATTN_WIDTH], preferred_element_type=f32)
        q_s[r0:r0 + ROW_CHUNK, :] = (q + bsc_s[:, OFF_Q:OFF_Q + ATTN_WIDTH]).astype(bf16)
    h = jnp.concatenate(h_chunks, axis=0)
    half_t = seq_tile // 2

    def proj_tile(off):
        cols = slice(off, off + MXU_TILE)
        return jnp.concatenate(
            [jnp.dot(h[i * half_t:(i + 1) * half_t], win_ref[:, cols], preferred_element_type=f32)
             for i in range(2)], axis=0) + bsc_s[:, cols]

    kv = proj_tile(OFF_K)
    half_mask = lax.broadcasted_iota(jnp.int32, (seq_tile, KV_WIDTH), 1) < HEAD_DIM
    for dup_ref, off in ((kd_s, 0), (vd_s, KV_WIDTH)):
        t = kv[:, off:off + KV_WIDTH]
        t_sw = pltpu.roll(t, HEAD_DIM, axis=1)
        dup_ref[0, BLOCK:BLOCK + seq_tile, :] = jnp.where(half_mask, t, t_sw).astype(bf16)
        dup_ref[1, BLOCK:BLOCK + seq_tile, :] = jnp.where(half_mask, t_sw, t).astype(bf16)

    def proj_wide(off):
        cols = slice(off, off + 2 * MXU_TILE)
        return jnp.dot(h, win_ref[:, cols], preferred_element_type=f32) + bsc_s[:, cols]

    ga = _silu_of_half(proj_wide(OFF_ZA))

    row = lax.broadcasted_iota(jnp.int32, (BLOCK, BLOCK), 0)
    prev_side = lane > row
    prev_side4 = jnp.concatenate([prev_side] * Q_PER_KV, axis=0)
    has_prev = s_idx > 0
    units = [(j, hh) for j in range(n_sub) for hh in range(N_KV_HEADS)]
    scores = []
    for j, hh in units:
        rows = slice(j * BLOCK, (j + 1) * BLOCK)
        parts = []
        for g in range(Q_PER_KV):
            c = hh * (Q_PER_KV // 2) + g // 2
            qc = q_s[rows, c * BLOCK:(c + 1) * BLOCK]
            keep = low_half if g % 2 == 0 else jnp.logical_not(low_half)
            parts.append(jnp.where(keep, qc, jnp.zeros_like(qc)))
        qst = jnp.concatenate(parts, axis=0)
        kb = kd_s[hh, j * BLOCK:(j + 2) * BLOCK, :]
        sc = lax.dot_general(qst, kb, (((1,), (1,)), ((), ())),
                             preferred_element_type=f32)
        s_prev = sc[:, 0:BLOCK]
        if j == 0:
            s_prev = jnp.where(has_prev, s_prev, NEG_INF)
        scores.append(jnp.where(prev_side4, s_prev, sc[:, BLOCK:2 * BLOCK]))

    def attend(u):
        j, hh = units[u]
        sc = scores[u]
        probs, inv_denoms = [], []
        for g in range(Q_PER_KV):
            sg = sc[g * BLOCK:(g + 1) * BLOCK, :]
            sink = sinks_ref[hh * Q_PER_KV + g]
            m = jnp.maximum(jnp.max(sg, axis=-1, keepdims=True), sink)
            p = jnp.exp(sg - m)
            denom = jnp.sum(p, axis=-1, keepdims=True) + jnp.exp(sink - m)
            inv_denoms.append(1.0 / denom)
            pb = p.astype(bf16)
            zero = jnp.zeros_like(pb)
            probs.append(jnp.concatenate([jnp.where(prev_side, pb, zero),
                                          jnp.where(prev_side, zero, pb)], axis=1))
        probs = jnp.concatenate(probs, axis=0)
        vb = vd_s[hh, j * BLOCK:(j + 2) * BLOCK, :]
        o = jnp.dot(probs, vb, preferred_element_type=f32)
        o = [o[g * BLOCK:(g + 1) * BLOCK, :] * inv_denoms[g] for g in range(Q_PER_KV)]
        return [jnp.where(low_half, o[2 * c2], o[2 * c2 + 1]) for c2 in range(Q_PER_KV // 2)]

    attn_out = []
    n_slots = (IN_WIDTH - OFF_U) // (2 * MXU_TILE)
    units_after = [len(units) * (i + 1) // n_slots - len(units) * i // n_slots for i in range(n_slots)]
    slot = iter(units_after)

    def attend_some():
        for _ in range(next(slot)):
            attn_out.append(attend(len(attn_out)))

    u_s[...] = _gelu_of_half(proj_wide(OFF_U))
    attend_some()
    vg = _gelu_of_half(proj_wide(OFF_VS))
    attend_some()
    mu = jnp.mean(vg, axis=-1, keepdims=True)
    vc = vg - mu
    var = jnp.mean(vc * vc, axis=-1, keepdims=True)
    vl_s[...] = ((vc * lax.rsqrt(var + NORM_EPS)) * lng_ref[...] + lnb_ref[...]).astype(bf16)
    gs_s[...] = _silu_of_half(proj_wide(OFF_ZS))
    attend_some()
    assert len(attn_out) == len(units)

    for u, (j, hh) in enumerate(units):
        rows = slice(j * BLOCK, (j + 1) * BLOCK)
        for c2 in range(Q_PER_KV // 2):
            cols = slice((hh * (Q_PER_KV // 2) + c2) * BLOCK, (hh * (Q_PER_KV // 2) + c2 + 1) * BLOCK)
            mix_s[rows, cols] = (attn_out[u][c2] * ga[rows, cols]).astype(bf16)

    for j in range(n_sub):
        rows = slice(j * BLOCK, (j + 1) * BLOCK)
        for p_idx in range(N_SGU_HEADS // 2):
            cols = slice(p_idx * BLOCK, (p_idx + 1) * BLOCK)
            vp = vl_s[rows, cols]
            zero = jnp.zeros_like(vp)
            rhs = jnp.concatenate([jnp.where(low_half, vp, zero),
                                   jnp.where(low_half, zero, vp)], axis=0)
            mixed = jnp.dot(wp_s[p_idx], rhs, preferred_element_type=f32) + sb_s[:, cols]
            mix_s[rows, ATTN_WIDTH + p_idx * BLOCK:ATTN_WIDTH + (p_idx + 1) * BLOCK] = (
                (u_s[rows, cols] * mixed) * gs_s[rows, cols]).astype(bf16)

    y = x_ref[0] + jnp.dot(mix_s[...], wout_ref[...], preferred_element_type=f32) + bout_ref[...]
    ms2 = jnp.mean(y * y, axis=-1, keepdims=True)
    o_ref[0] = (y * lax.rsqrt(ms2 + NORM_EPS)) * fg_ref[...]


def _layer_call(x, sinks, norm_g, w_in, b_in, ln_g, ln_b, sgu_w, sgu_bt, w_out, b_out, final_g,
                *, seq_tile=SEQ_TILE):
    batch, seq, d_model = x.shape
    assert d_model == D_MODEL and seq % seq_tile == 0 and seq_tile % (2 * BLOCK) == 0
    f32, bf16 = jnp.float32, jnp.bfloat16

    def full(shape):
        return pl.BlockSpec(shape, lambda b, s: (0,) * len(shape))

    tile_spec = pl.BlockSpec((1, seq_tile, D_MODEL), lambda b, s: (b, s, 0))
    return pl.pallas_call(
        functools.partial(_layer_kernel, seq_tile=seq_tile),
        grid=(batch, seq // seq_tile),
        in_specs=[
            pl.BlockSpec(memory_space=pltpu.SMEM),
            tile_spec,
            full((1, D_MODEL)),
            pl.BlockSpec(memory_space=pl.ANY),
            full((1, IN_WIDTH)),
            full((1, SGU_WIDTH)),
            full((1, SGU_WIDTH)),
            full((N_SGU_HEADS, BLOCK, BLOCK)),
            full((BLOCK, N_SGU_HEADS)),
            pl.BlockSpec(memory_space=pl.ANY),
            full((1, D_MODEL)),
            full((1, D_MODEL)),
        ],
        out_specs=tile_spec,
        out_shape=jax.ShapeDtypeStruct(x.shape, x.dtype),
        scratch_shapes=[
            pltpu.VMEM((D_MODEL, IN_WIDTH), bf16),
            pltpu.VMEM((D_MODEL, D_MODEL), bf16),
            pltpu.VMEM((1, IN_WIDTH), f32),
            pltpu.VMEM((W_SLOTS, W_CHUNK, IN_WIDTH), f32),
            pltpu.VMEM((W_SLOTS, W_CHUNK, D_MODEL), f32),
            pltpu.SemaphoreType.DMA((W_SLOTS,)),
            pltpu.SemaphoreType.DMA((W_SLOTS,)),
            pltpu.VMEM((seq_tile, ATTN_WIDTH), bf16),
            pltpu.VMEM((N_KV_HEADS, BLOCK + seq_tile, KV_WIDTH), bf16),
            pltpu.VMEM((N_KV_HEADS, BLOCK + seq_tile, KV_WIDTH), bf16),
            pltpu.VMEM((seq_tile, SGU_WIDTH), f32),
            pltpu.VMEM((seq_tile, SGU_WIDTH), bf16),
            pltpu.VMEM((seq_tile, SGU_WIDTH), f32),
            pltpu.VMEM((seq_tile, D_MODEL), bf16),
            pltpu.VMEM((N_SGU_HEADS // 2, BLOCK, 2 * BLOCK), bf16),
            pltpu.VMEM((BLOCK, SGU_WIDTH), f32),
        ],
        compiler_params=pltpu.CompilerParams(
            dimension_semantics=("arbitrary", "arbitrary"),
            vmem_limit_bytes=V7X_VMEM_LIMIT_BYTES),
        name="hybrid_layer",
    )(sinks, x, norm_g, w_in, b_in, ln_g, ln_b, sgu_w, sgu_bt, w_out, b_out, final_g)


def kernel(x, norm_g, w_in, b_in, attn_sinks, sgu_ln_g, sgu_ln_b, sgu_w, sgu_b, w_out, b_out, final_norm_g):
    depth = norm_g.shape[0]
    for l in range(depth):
        last = l == depth - 1
        assert last, "the fused call applies the final norm; only depth 1 is supported"
        x = _layer_call(
            x, attn_sinks[l], norm_g[l][None, :], w_in[l], b_in[l][None, :],
            sgu_ln_g[l][None, :], sgu_ln_b[l][None, :], sgu_w[l], sgu_b[l].T,
            w_out[l], b_out[l][None, :], final_norm_g[None, :])
    return x
```

```python
import collections
import functools

import jax
import jax.numpy as jnp
from jax import lax
from jax.experimental import pallas as pl
from jax.experimental.pallas import tpu as pltpu

D_MODEL = 1024
HEAD_DIM = 64
ATTN_WIDTH = 512
KV_WIDTH = 128
SGU_WIDTH = 512
N_KV_HEADS = 2
Q_PER_KV = 4
N_SGU_HEADS = 8
BLOCK = 128
NORM_EPS = 1e-5
NEG_INF = -1e30
ATTN_SCALE = HEAD_DIM ** -0.5

OFF_Q = 0
OFF_K = OFF_Q + ATTN_WIDTH
OFF_V = OFF_K + KV_WIDTH
OFF_ZA = OFF_V + KV_WIDTH
OFF_U = OFF_ZA + ATTN_WIDTH
OFF_VS = OFF_U + SGU_WIDTH
OFF_ZS = OFF_VS + SGU_WIDTH
IN_WIDTH = OFF_ZS + SGU_WIDTH

SEQ_TILE = 1024
MXU_TILE = 256
ROW_CHUNK = 256
OUT_ROWS = 512
W_CHUNK = 128
W_SLOTS = 3
V7X_VMEM_LIMIT_BYTES = 56 * 1024 * 1024

_SQRT_TWO = 1.4142135623730951

_WeightCopy = collections.namedtuple("_WeightCopy", "hbm stage sem dst col_scale")


def _silu_of_half(hz):
    return hz * (1.0 + jnp.tanh(hz))


def _gelu_of_half(hz):
    return hz * (1.0 + lax.erf(hz * _SQRT_TWO))


def _layer_kernel(sinks_ref, x_ref, ng_ref, win_hbm, bin_ref, lng_ref, lnb_ref,
                  sw_ref, sbt_ref, wout_hbm, bout_ref, fg_ref, o_ref,
                  win_ref, wout_ref, bsc_s, stage_in, stage_out, sem_in, sem_out,
                  q_s, kd_s, vd_s, u_s, vl_s, gs_s, mix_s, wp_s, sb_s, *, seq_tile):
    f32, bf16 = jnp.float32, jnp.bfloat16
    n_sub = seq_tile // BLOCK
    b_idx = pl.program_id(0)
    s_idx = pl.program_id(1)

    lane = lax.broadcasted_iota(jnp.int32, (BLOCK, BLOCK), 1)
    low_half = lane < HEAD_DIM

    @pl.when((b_idx == 0) & (s_idx == 0))
    def _():
        row = lax.broadcasted_iota(jnp.int32, (BLOCK, BLOCK), 0)
        for p in range(N_SGU_HEADS // 2):
            for half in range(2):
                w = jnp.where(row >= lane, sw_ref[2 * p + half], 0.0)
                wp_s[p, :, half * BLOCK:(half + 1) * BLOCK] = w.astype(bf16)
        sbt = sbt_ref[...]
        for hh in range(N_SGU_HEADS):
            sb_s[:, hh * HEAD_DIM:(hh + 1) * HEAD_DIM] = jnp.broadcast_to(
                sbt[:, hh:hh + 1], (BLOCK, HEAD_DIM))
        col = lax.broadcasted_iota(jnp.int32, (1, IN_WIDTH), 1)
        col_scale = jnp.where(col < OFF_K, ATTN_SCALE, jnp.where(col < OFF_ZA, 1.0, 0.5)).astype(f32)
        bsc_s[...] = bin_ref[...] * col_scale
        jobs = (_WeightCopy(win_hbm, stage_in, sem_in, win_ref, col_scale),
                _WeightCopy(wout_hbm, stage_out, sem_out, wout_ref, None))
        n_chunks = D_MODEL // W_CHUNK

        def chunk_copy(job, c):
            return pltpu.make_async_copy(job.hbm.at[pl.ds(c * W_CHUNK, W_CHUNK), :],
                                         job.stage.at[c % W_SLOTS], job.sem.at[c % W_SLOTS])

        for job in jobs:
            for c in range(W_SLOTS):
                chunk_copy(job, c).start(priority=c % 2)
        for job in jobs:
            for c in range(n_chunks):
                chunk_copy(job, c).wait()
                w = job.stage[c % W_SLOTS]
                if job.col_scale is not None:
                    w = w * job.col_scale
                job.dst[c * W_CHUNK:(c + 1) * W_CHUNK, :] = w.astype(bf16)
                if c + W_SLOTS < n_chunks:
                    chunk_copy(job, c + W_SLOTS).start(priority=c % 2)

    @pl.when(s_idx == 0)
    def _():
        for hh in range(N_KV_HEADS):
            kd_s[hh, 0:BLOCK, :] = jnp.zeros((BLOCK, KV_WIDTH), bf16)
            vd_s[hh, 0:BLOCK, :] = jnp.zeros((BLOCK, KV_WIDTH), bf16)

    @pl.when(s_idx > 0)
    def _():
        for hh in range(N_KV_HEADS):
            kd_s[hh, 0:BLOCK, :] = kd_s[hh, seq_tile:seq_tile + BLOCK, :]
            vd_s[hh, 0:BLOCK, :] = vd_s[hh, seq_tile:seq_tile + BLOCK, :]

    h_chunks = []
    for r0 in range(0, seq_tile, ROW_CHUNK):
        x = x_ref[0, r0:r0 + ROW_CHUNK, :]
        ms = jnp.mean(x * x, axis=-1, keepdims=True)
        hc = ((x * lax.rsqrt(ms + NORM_EPS)) * ng_ref[...]).astype(bf16)
        h_chunks.append(hc)
        q = jnp.dot(hc, win_ref[:, OFF_Q:OFF_Q + ATTN_WIDTH], preferred_element_type=f32)
        q_s[r0:r0 + ROW_CHUNK, :] = (q + bsc_s[:, OFF_Q:OFF_Q + ATTN_WIDTH]).astype(bf16)
    h = jnp.concatenate(h_chunks, axis=0)
    half_t = seq_tile // 2

    def proj_tile(off):
        cols = slice(off, off + MXU_TILE)
        return jnp.concatenate(
            [jnp.dot(h[i * half_t:(i + 1) * half_t], win_ref[:, cols], preferred_element_type=f32)
             for i in range(2)], axis=0) + bsc_s[:, cols]

    kv = proj_tile(OFF_K)
    half_mask = lax.broadcasted_iota(jnp.int32, (seq_tile, KV_WIDTH), 1) < HEAD_DIM
    for dup_ref, off in ((kd_s, 0), (vd_s, KV_WIDTH)):
        t = kv[:, off:off + KV_WIDTH]
        t_sw = pltpu.roll(t, HEAD_DIM, axis=1)
        dup_ref[0, BLOCK:BLOCK + seq_tile, :] = jnp.where(half_mask, t, t_sw).astype(bf16)
        dup_ref[1, BLOCK:BLOCK + seq_tile, :] = jnp.where(half_mask, t_sw, t).astype(bf16)

    ga_t = [_silu_of_half(proj_tile(OFF_ZA))]

    row = lax.broadcasted_iota(jnp.int32, (BLOCK, BLOCK), 0)
    prev_side = lane > row
    prev_side4 = jnp.concatenate([prev_side] * Q_PER_KV, axis=0)
    has_prev = s_idx > 0
    units = [(j, hh) for j in range(n_sub) for hh in range(N_KV_HEADS)]
    scores = []
    for j, hh in units:
        rows = slice(j * BLOCK, (j + 1) * BLOCK)
        parts = []
        for g in range(Q_PER_KV):
            c = hh * (Q_PER_KV // 2) + g // 2
            qc = q_s[rows, c * BLOCK:(c + 1) * BLOCK]
            keep = low_half if g % 2 == 0 else jnp.logical_not(low_half)
            parts.append(jnp.where(keep, qc, jnp.zeros_like(qc)))
        qst = jnp.concatenate(parts, axis=0)
        kb = kd_s[hh, j * BLOCK:(j + 2) * BLOCK, :]
        sc = lax.dot_general(qst, kb, (((1,), (1,)), ((), ())),
                             preferred_element_type=f32)
        s_prev = sc[:, 0:BLOCK]
        if j == 0:
            s_prev = jnp.where(has_prev, s_prev, NEG_INF)
        scores.append(jnp.where(prev_side4, s_prev, sc[:, BLOCK:2 * BLOCK]))

    def attend(u):
        j, hh = units[u]
        sc = scores[u]
        probs, inv_denoms = [], []
        for g in range(Q_PER_KV):
            sg = sc[g * BLOCK:(g + 1) * BLOCK, :]
            sink = sinks_ref[hh * Q_PER_KV + g]
            m = jnp.maximum(jnp.max(sg, axis=-1, keepdims=True), sink)
            p = jnp.exp(sg - m)
            denom = jnp.sum(p, axis=-1, keepdims=True) + jnp.exp(sink - m)
            inv_denoms.append(1.0 / denom)
            pb = p.astype(bf16)
            zero = jnp.zeros_like(pb)
            probs.append(jnp.concatenate([jnp.where(prev_side, pb, zero),
                                          jnp.where(prev_side, zero, pb)], axis=1))
        probs = jnp.concatenate(probs, axis=0)
        vb = vd_s[hh, j * BLOCK:(j + 2) * BLOCK, :]
        o = jnp.dot(probs, vb, preferred_element_type=f32)
        o = [o[g * BLOCK:(g + 1) * BLOCK, :] * inv_denoms[g] for g in range(Q_PER_KV)]
        return [jnp.where(low_half, o[2 * c2], o[2 * c2 + 1]) for c2 in range(Q_PER_KV // 2)]

    attn_out = []
    n_slots = (IN_WIDTH - OFF_ZA) // MXU_TILE - 1
    units_after = [len(units) * (i + 1) // n_slots - len(units) * i // n_slots for i in range(n_slots)]
    slot = iter(units_after)

    def attend_some():
        for _ in range(next(slot)):
            attn_out.append(attend(len(attn_out)))

    for n in range(1, ATTN_WIDTH // MXU_TILE):
        ga_t.append(_silu_of_half(proj_tile(OFF_ZA + n * MXU_TILE)))
        attend_some()
    for n in range(SGU_WIDTH // MXU_TILE):
        u_s[:, n * MXU_TILE:(n + 1) * MXU_TILE] = _gelu_of_half(proj_tile(OFF_U + n * MXU_TILE))
        attend_some()
    vg = []
    for n in range(SGU_WIDTH // MXU_TILE):
        vg.append(_gelu_of_half(proj_tile(OFF_VS + n * MXU_TILE)))
        attend_some()
    vg = jnp.concatenate(vg, axis=1)
    mu = jnp.mean(vg, axis=-1, keepdims=True)
    vc = vg - mu
    var = jnp.mean(vc * vc, axis=-1, keepdims=True)
    vl_s[...] = ((vc * lax.rsqrt(var + NORM_EPS)) * lng_ref[...] + lnb_ref[...]).astype(bf16)
    for n in range(SGU_WIDTH // MXU_TILE):
        gs_s[:, n * MXU_TILE:(n + 1) * MXU_TILE] = _silu_of_half(proj_tile(OFF_ZS + n * MXU_TILE))
        attend_some()
    assert len(attn_out) == len(units)

    ga = jnp.concatenate(ga_t, axis=1)
    for u, (j, hh) in enumerate(units):
        rows = slice(j * BLOCK, (j + 1) * BLOCK)
        for c2 in range(Q_PER_KV // 2):
            cols = slice((hh * (Q_PER_KV // 2) + c2) * BLOCK, (hh * (Q_PER_KV // 2) + c2 + 1) * BLOCK)
            mix_s[rows, cols] = (attn_out[u][c2] * ga[rows, cols]).astype(bf16)

    for j in range(n_sub):
        rows = slice(j * BLOCK, (j + 1) * BLOCK)
        for p_idx in range(N_SGU_HEADS // 2):
            cols = slice(p_idx * BLOCK, (p_idx + 1) * BLOCK)
            vp = vl_s[rows, cols]
            zero = jnp.zeros_like(vp)
            rhs = jnp.concatenate([jnp.where(low_half, vp, zero),
                                   jnp.where(low_half, zero, vp)], axis=0)
            mixed = jnp.dot(wp_s[p_idx], rhs, preferred_element_type=f32) + sb_s[:, cols]
            mix_s[rows, ATTN_WIDTH + p_idx * BLOCK:ATTN_WIDTH + (p_idx + 1) * BLOCK] = (
                (u_s[rows, cols] * mixed) * gs_s[rows, cols]).astype(bf16)

    for r0 in range(0, seq_tile, OUT_ROWS):
        rows = slice(r0, r0 + OUT_ROWS)
        y = (x_ref[0, rows, :] + jnp.dot(mix_s[rows, :], wout_ref[...], preferred_element_type=f32)
             + bout_ref[...])
        ms2 = jnp.mean(y * y, axis=-1, keepdims=True)
        o_ref[0, rows, :] = (y * lax.rsqrt(ms2 + NORM_EPS)) * fg_ref[...]


def _layer_call(x, sinks, norm_g, w_in, b_in, ln_g, ln_b, sgu_w, sgu_bt, w_out, b_out, final_g,
                *, seq_tile=SEQ_TILE):
    batch, seq, d_model = x.shape
    assert d_model == D_MODEL and seq % seq_tile == 0 and seq_tile % (2 * BLOCK) == 0
    f32, bf16 = jnp.float32, jnp.bfloat16

    def full(shape):
        return pl.BlockSpec(shape, lambda b, s: (0,) * len(shape))

    tile_spec = pl.BlockSpec((1, seq_tile, D_MODEL), lambda b, s: (b, s, 0))
    return pl.pallas_call(
        functools.partial(_layer_kernel, seq_tile=seq_tile),
        grid=(batch, seq // seq_tile),
        in_specs=[
            pl.BlockSpec(memory_space=pltpu.SMEM),
            tile_spec,
            full((1, D_MODEL)),
            pl.BlockSpec(memory_space=pl.ANY),
            full((1, IN_WIDTH)),
            full((1, SGU_WIDTH)),
            full((1, SGU_WIDTH)),
            full((N_SGU_HEADS, BLOCK, BLOCK)),
            full((BLOCK, N_SGU_HEADS)),
            pl.BlockSpec(memory_space=pl.ANY),
            full((1, D_MODEL)),
            full((1, D_MODEL)),
        ],
        out_specs=tile_spec,
        out_shape=jax.ShapeDtypeStruct(x.shape, x.dtype),
        scratch_shapes=[
            pltpu.VMEM((D_MODEL, IN_WIDTH), bf16),
            pltpu.VMEM((D_MODEL, D_MODEL), bf16),
            pltpu.VMEM((1, IN_WIDTH), f32),
            pltpu.VMEM((W_SLOTS, W_CHUNK, IN_WIDTH), f32),
            pltpu.VMEM((W_SLOTS, W_CHUNK, D_MODEL), f32),
            pltpu.SemaphoreType.DMA((W_SLOTS,)),
            pltpu.SemaphoreType.DMA((W_SLOTS,)),
            pltpu.VMEM((seq_tile, ATTN_WIDTH), bf16),
            pltpu.VMEM((N_KV_HEADS, BLOCK + seq_tile, KV_WIDTH), bf16),
            pltpu.VMEM((N_KV_HEADS, BLOCK + seq_tile, KV_WIDTH), bf16),
            pltpu.VMEM((seq_tile, SGU_WIDTH), f32),
            pltpu.VMEM((seq_tile, SGU_WIDTH), bf16),
            pltpu.VMEM((seq_tile, SGU_WIDTH), f32),
            pltpu.VMEM((seq_tile, D_MODEL), bf16),
            pltpu.VMEM((N_SGU_HEADS // 2, BLOCK, 2 * BLOCK), bf16),
            pltpu.VMEM((BLOCK, SGU_WIDTH), f32),
        ],
        compiler_params=pltpu.CompilerParams(
            dimension_semantics=("arbitrary", "arbitrary"),
            vmem_limit_bytes=V7X_VMEM_LIMIT_BYTES),
        name="hybrid_layer",
    )(sinks, x, norm_g, w_in, b_in, ln_g, ln_b, sgu_w, sgu_bt, w_out, b_out, final_g)


def kernel(x, norm_g, w_in, b_in, attn_sinks, sgu_ln_g, sgu_ln_b, sgu_w, sgu_b, w_out, b_out, final_norm_g):
    depth = norm_g.shape[0]
    for l in range(depth):
        last = l == depth - 1
        assert last, "the fused call applies the final norm; only depth 1 is supported"
        x = _layer_call(
            x, attn_sinks[l], norm_g[l][None, :], w_in[l], b_in[l][None, :],
            sgu_ln_g[l][None, :], sgu_ln_b[l][None, :], sgu_w[l], sgu_b[l].T,
            w_out[l], b_out[l][None, :], final_norm_g[None, :])
    return x
```

```python
import collections
import functools

import jax
import jax.numpy as jnp
from jax import lax
from jax.experimental import pallas as pl
from jax.experimental.pallas import tpu as pltpu

D_MODEL = 1024
HEAD_DIM = 64
ATTN_WIDTH = 512
KV_WIDTH = 128
SGU_WIDTH = 512
N_KV_HEADS = 2
Q_PER_KV = 4
N_SGU_HEADS = 8
BLOCK = 128
NORM_EPS = 1e-5
NEG_INF = -1e30
ATTN_SCALE = HEAD_DIM ** -0.5

OFF_Q = 0
OFF_K = OFF_Q + ATTN_WIDTH
OFF_V = OFF_K + KV_WIDTH
OFF_ZA = OFF_V + KV_WIDTH
OFF_U = OFF_ZA + ATTN_WIDTH
OFF_VS = OFF_U + SGU_WIDTH
OFF_ZS = OFF_VS + SGU_WIDTH
IN_WIDTH = OFF_ZS + SGU_WIDTH

SEQ_TILE = 1024
MXU_TILE = 256
ROW_CHUNK = 256
TILE_ROW_PARTS = 4
W_CHUNK = 128
W_SLOTS = 4
V7X_VMEM_LIMIT_BYTES = 56 * 1024 * 1024

_SQRT_TWO = 1.4142135623730951

_WeightCopy = collections.namedtuple("_WeightCopy", "hbm stage sem dst col_scale")


def _silu_of_half(hz):
    return hz * (1.0 + jnp.tanh(hz))


def _gelu_of_half(hz):
    return hz * (1.0 + lax.erf(hz * _SQRT_TWO))


def _layer_kernel(sinks_ref, x_ref, ng_ref, win_hbm, bin_ref, lng_ref, lnb_ref,
                  sw_ref, sbt_ref, wout_hbm, bout_ref, fg_ref, o_ref,
                  win_ref, wout_ref, bsc_s, stage_in, stage_out, sem_in, sem_out,
                  q_s, kd_s, vd_s, u_s, vl_s, gs_s, mix_s, wp_s, sb_s, *, seq_tile):
    f32, bf16 = jnp.float32, jnp.bfloat16
    n_sub = seq_tile // BLOCK
    b_idx = pl.program_id(0)
    s_idx = pl.program_id(1)

    lane = lax.broadcasted_iota(jnp.int32, (BLOCK, BLOCK), 1)
    low_half = lane < HEAD_DIM

    @pl.when((b_idx == 0) & (s_idx == 0))
    def _():
        row = lax.broadcasted_iota(jnp.int32, (BLOCK, BLOCK), 0)
        for p in range(N_SGU_HEADS // 2):
            for half in range(2):
                w = jnp.where(row >= lane, sw_ref[2 * p + half], 0.0)
                wp_s[p, :, half * BLOCK:(half + 1) * BLOCK] = w.astype(bf16)
        sbt = sbt_ref[...]
        for hh in range(N_SGU_HEADS):
            sb_s[:, hh * HEAD_DIM:(hh + 1) * HEAD_DIM] = jnp.broadcast_to(
                sbt[:, hh:hh + 1], (BLOCK, HEAD_DIM))
        col = lax.broadcasted_iota(jnp.int32, (1, IN_WIDTH), 1)
        col_scale = jnp.where(col < OFF_K, ATTN_SCALE, jnp.where(col < OFF_ZA, 1.0, 0.5)).astype(f32)
        bsc_s[...] = bin_ref[...] * col_scale
        jobs = (_WeightCopy(win_hbm, stage_in, sem_in, win_ref, col_scale),
                _WeightCopy(wout_hbm, stage_out, sem_out, wout_ref, None))
        n_chunks = D_MODEL // W_CHUNK

        def chunk_copy(job, c):
            return pltpu.make_async_copy(job.hbm.at[pl.ds(c * W_CHUNK, W_CHUNK), :],
                                         job.stage.at[c % W_SLOTS], job.sem.at[c % W_SLOTS])

        for job in jobs:
            for c in range(W_SLOTS):
                chunk_copy(job, c).start(priority=c % 2)
        for job in jobs:
            for c in range(n_chunks):
                chunk_copy(job, c).wait()
                w = job.stage[c % W_SLOTS]
                if job.col_scale is not None:
                    w = w * job.col_scale
                job.dst[c * W_CHUNK:(c + 1) * W_CHUNK, :] = w.astype(bf16)
                if c + W_SLOTS < n_chunks:
                    chunk_copy(job, c + W_SLOTS).start(priority=c % 2)

    @pl.when(s_idx == 0)
    def _():
        for hh in range(N_KV_HEADS):
            kd_s[hh, 0:BLOCK, :] = jnp.zeros((BLOCK, KV_WIDTH), bf16)
            vd_s[hh, 0:BLOCK, :] = jnp.zeros((BLOCK, KV_WIDTH), bf16)

    @pl.when(s_idx > 0)
    def _():
        for hh in range(N_KV_HEADS):
            kd_s[hh, 0:BLOCK, :] = kd_s[hh, seq_tile:seq_tile + BLOCK, :]
            vd_s[hh, 0:BLOCK, :] = vd_s[hh, seq_tile:seq_tile + BLOCK, :]

    h_chunks = []
    for r0 in range(0, seq_tile, ROW_CHUNK):
        x = x_ref[0, r0:r0 + ROW_CHUNK, :]
        ms = jnp.mean(x * x, axis=-1, keepdims=True)
        hc = ((x * lax.rsqrt(ms + NORM_EPS)) * ng_ref[...]).astype(bf16)
        h_chunks.append(hc)
        q = jnp.dot(hc, win_ref[:, OFF_Q:OFF_Q + ATTN_WIDTH], preferred_element_type=f32)
        q_s[r0:r0 + ROW_CHUNK, :] = (q + bsc_s[:, OFF_Q:OFF_Q + ATTN_WIDTH]).astype(bf16)
    h = jnp.concatenate(h_chunks, axis=0)
    half_t = seq_tile // 2

    def proj_tile(off):
        cols = slice(off, off + MXU_TILE)
        part_t = seq_tile // TILE_ROW_PARTS
        return jnp.concatenate(
            [jnp.dot(h[i * part_t:(i + 1) * part_t], win_ref[:, cols], preferred_element_type=f32)
             for i in range(TILE_ROW_PARTS)], axis=0) + bsc_s[:, cols]

    kv = proj_tile(OFF_K)
    half_mask = lax.broadcasted_iota(jnp.int32, (seq_tile, KV_WIDTH), 1) < HEAD_DIM
    for dup_ref, off in ((kd_s, 0), (vd_s, KV_WIDTH)):
        t = kv[:, off:off + KV_WIDTH]
        t_sw = pltpu.roll(t, HEAD_DIM, axis=1)
        dup_ref[0, BLOCK:BLOCK + seq_tile, :] = jnp.where(half_mask, t, t_sw).astype(bf16)
        dup_ref[1, BLOCK:BLOCK + seq_tile, :] = jnp.where(half_mask, t_sw, t).astype(bf16)

    ga_t = [_silu_of_half(proj_tile(OFF_ZA))]

    row = lax.broadcasted_iota(jnp.int32, (BLOCK, BLOCK), 0)
    prev_side = lane > row
    prev_side4 = jnp.concatenate([prev_side] * Q_PER_KV, axis=0)
    has_prev = s_idx > 0
    units = [(j, hh) for j in range(n_sub) for hh in range(N_KV_HEADS)]
    scores = []
    for j, hh in units:
        rows = slice(j * BLOCK, (j + 1) * BLOCK)
        parts = []
        for g in range(Q_PER_KV):
            c = hh * (Q_PER_KV // 2) + g // 2
            qc = q_s[rows, c * BLOCK:(c + 1) * BLOCK]
            keep = low_half if g % 2 == 0 else jnp.logical_not(low_half)
            parts.append(jnp.where(keep, qc, jnp.zeros_like(qc)))
        qst = jnp.concatenate(parts, axis=0)
        kb = kd_s[hh, j * BLOCK:(j + 2) * BLOCK, :]
        sc = lax.dot_general(qst, kb, (((1,), (1,)), ((), ())),
                             preferred_element_type=f32)
        s_prev = sc[:, 0:BLOCK]
        if j == 0:
            s_prev = jnp.where(has_prev, s_prev, NEG_INF)
        scores.append(jnp.where(prev_side4, s_prev, sc[:, BLOCK:2 * BLOCK]))

    def attend(u):
        j, hh = units[u]
        sc = scores[u]
        probs, inv_denoms = [], []
        for g in range(Q_PER_KV):
            sg = sc[g * BLOCK:(g + 1) * BLOCK, :]
            sink = sinks_ref[hh * Q_PER_KV + g]
            m = jnp.maximum(jnp.max(sg, axis=-1, keepdims=True), sink)
            p = jnp.exp(sg - m)
            denom = jnp.sum(p, axis=-1, keepdims=True) + jnp.exp(sink - m)
            inv_denoms.append(1.0 / denom)
            pb = p.astype(bf16)
            zero = jnp.zeros_like(pb)
            probs.append(jnp.concatenate([jnp.where(prev_side, pb, zero),
                                          jnp.where(prev_side, zero, pb)], axis=1))
        probs = jnp.concatenate(probs, axis=0)
        vb = vd_s[hh, j * BLOCK:(j + 2) * BLOCK, :]
        o = jnp.dot(probs, vb, preferred_element_type=f32)
        o = [o[g * BLOCK:(g + 1) * BLOCK, :] * inv_denoms[g] for g in range(Q_PER_KV)]
        return [jnp.where(low_half, o[2 * c2], o[2 * c2 + 1]) for c2 in range(Q_PER_KV // 2)]

    attn_out = []
    n_slots = (IN_WIDTH - OFF_ZA) // MXU_TILE - 1
    units_after = [len(units) * (i + 1) // n_slots - len(units) * i // n_slots for i in range(n_slots)]
    slot = iter(units_after)

    def attend_some():
        for _ in range(next(slot)):
            attn_out.append(attend(len(attn_out)))

    for n in range(1, ATTN_WIDTH // MXU_TILE):
        ga_t.append(_silu_of_half(proj_tile(OFF_ZA + n * MXU_TILE)))
        attend_some()
    for n in range(SGU_WIDTH // MXU_TILE):
        u_s[:, n * MXU_TILE:(n + 1) * MXU_TILE] = _gelu_of_half(proj_tile(OFF_U + n * MXU_TILE))
        attend_some()
    vg = []
    for n in range(SGU_WIDTH // MXU_TILE):
        vg.append(_gelu_of_half(proj_tile(OFF_VS + n * MXU_TILE)))
        attend_some()
    vg = jnp.concatenate(vg, axis=1)
    mu = jnp.mean(vg, axis=-1, keepdims=True)
    vc = vg - mu
    var = jnp.mean(vc * vc, axis=-1, keepdims=True)
    vl_s[...] = ((vc * lax.rsqrt(var + NORM_EPS)) * lng_ref[...] + lnb_ref[...]).astype(bf16)
    for n in range(SGU_WIDTH // MXU_TILE):
        gs_s[:, n * MXU_TILE:(n + 1) * MXU_TILE] = _silu_of_half(proj_tile(OFF_ZS + n * MXU_TILE))
        attend_some()
    assert len(attn_out) == len(units)

    ga = jnp.concatenate(ga_t, axis=1)
    for u, (j, hh) in enumerate(units):
        rows = slice(j * BLOCK, (j + 1) * BLOCK)
        for c2 in range(Q_PER_KV // 2):
            cols = slice((hh * (Q_PER_KV // 2) + c2) * BLOCK, (hh * (Q_PER_KV // 2) + c2 + 1) * BLOCK)
            mix_s[rows, cols] = (attn_out[u][c2] * ga[rows, cols]).astype(bf16)

    for j in range(n_sub):
        rows = slice(j * BLOCK, (j + 1) * BLOCK)
        for p_idx in range(N_SGU_HEADS // 2):
            cols = slice(p_idx * BLOCK, (p_idx + 1) * BLOCK)
            vp = vl_s[rows, cols]
            zero = jnp.zeros_like(vp)
            rhs = jnp.concatenate([jnp.where(low_half, vp, zero),
                                   jnp.where(low_half, zero, vp)], axis=0)
            mixed = jnp.dot(wp_s[p_idx], rhs, preferred_element_type=f32) + sb_s[:, cols]
            mix_s[rows, ATTN_WIDTH + p_idx * BLOCK:ATTN_WIDTH + (p_idx + 1) * BLOCK] = (
                (u_s[rows, cols] * mixed) * gs_s[rows, cols]).astype(bf16)

    y = x_ref[0] + jnp.dot(mix_s[...], wout_ref[...], preferred_element_type=f32) + bout_ref[...]
    ms2 = jnp.mean(y * y, axis=-1, keepdims=True)
    o_ref[0] = (y * lax.rsqrt(ms2 + NORM_EPS)) * fg_ref[...]


def _layer_call(x, sinks, norm_g, w_in, b_in, ln_g, ln_b, sgu_w, sgu_bt, w_out, b_out, final_g,
                *, seq_tile=SEQ_TILE):
    batch, seq, d_model = x.shape
    assert d_model == D_MODEL and seq % seq_tile == 0 and seq_tile % (2 * BLOCK) == 0
    f32, bf16 = jnp.float32, jnp.bfloat16

    def full(shape):
        return pl.BlockSpec(shape, lambda b, s: (0,) * len(shape))

    tile_spec = pl.BlockSpec((1, seq_tile, D_MODEL), lambda b, s: (b, s, 0))
    return pl.pallas_call(
        functools.partial(_layer_kernel, seq_tile=seq_tile),
        grid=(batch, seq // seq_tile),
        in_specs=[
            pl.BlockSpec(memory_space=pltpu.SMEM),
            tile_spec,
            full((1, D_MODEL)),
            pl.BlockSpec(memory_space=pl.ANY),
            full((1, IN_WIDTH)),
            full((1, SGU_WIDTH)),
            full((1, SGU_WIDTH)),
            full((N_SGU_HEADS, BLOCK, BLOCK)),
            full((BLOCK, N_SGU_HEADS)),
            pl.BlockSpec(memory_space=pl.ANY),
            full((1, D_MODEL)),
            full((1, D_MODEL)),
        ],
        out_specs=tile_spec,
        out_shape=jax.ShapeDtypeStruct(x.shape, x.dtype),
        scratch_shapes=[
            pltpu.VMEM((D_MODEL, IN_WIDTH), bf16),
            pltpu.VMEM((D_MODEL, D_MODEL), bf16),
            pltpu.VMEM((1, IN_WIDTH), f32),
            pltpu.VMEM((W_SLOTS, W_CHUNK, IN_WIDTH), f32),
            pltpu.VMEM((W_SLOTS, W_CHUNK, D_MODEL), f32),
            pltpu.SemaphoreType.DMA((W_SLOTS,)),
            pltpu.SemaphoreType.DMA((W_SLOTS,)),
            pltpu.VMEM((seq_tile, ATTN_WIDTH), bf16),
            pltpu.VMEM((N_KV_HEADS, BLOCK + seq_tile, KV_WIDTH), bf16),
            pltpu.VMEM((N_KV_HEADS, BLOCK + seq_tile, KV_WIDTH), bf16),
            pltpu.VMEM((seq_tile, SGU_WIDTH), f32),
            pltpu.VMEM((seq_tile, SGU_WIDTH), bf16),
            pltpu.VMEM((seq_tile, SGU_WIDTH), f32),
            pltpu.VMEM((seq_tile, D_MODEL), bf16),
            pltpu.VMEM((N_SGU_HEADS // 2, BLOCK, 2 * BLOCK), bf16),
            pltpu.VMEM((BLOCK, SGU_WIDTH), f32),
        ],
        compiler_params=pltpu.CompilerParams(
            dimension_semantics=("arbitrary", "arbitrary"),
            vmem_limit_bytes=V7X_VMEM_LIMIT_BYTES),
        name="hybrid_layer",
    )(sinks, x, norm_g, w_in, b_in, ln_g, ln_b, sgu_w, sgu_bt, w_out, b_out, final_g)


def kernel(x, norm_g, w_in, b_in, attn_sinks, sgu_ln_g, sgu_ln_b, sgu_w, sgu_b, w_out, b_out, final_norm_g):
    depth = norm_g.shape[0]
    for l in range(depth):
        last = l == depth - 1
        assert last, "the fused call applies the final norm; only depth 1 is supported"
        x = _layer_call(
            x, attn_sinks[l], norm_g[l][None, :], w_in[l], b_in[l][None, :],
            sgu_ln_g[l][None, :], sgu_ln_b[l][None, :], sgu_w[l], sgu_b[l].T,
            w_out[l], b_out[l][None, :], final_norm_g[None, :])
    return x
```

```python
import collections
import functools

import jax
import jax.numpy as jnp
from jax import lax
from jax.experimental import pallas as pl
from jax.experimental.pallas import tpu as pltpu

D_MODEL = 1024
HEAD_DIM = 64
ATTN_WIDTH = 512
KV_WIDTH = 128
SGU_WIDTH = 512
N_KV_HEADS = 2
Q_PER_KV = 4
N_SGU_HEADS = 8
BLOCK = 128
NORM_EPS = 1e-5
NEG_INF = -1e30
ATTN_SCALE = HEAD_DIM ** -0.5

OFF_Q = 0
OFF_K = OFF_Q + ATTN_WIDTH
OFF_V = OFF_K + KV_WIDTH
OFF_ZA = OFF_V + KV_WIDTH
OFF_U = OFF_ZA + ATTN_WIDTH
OFF_VS = OFF_U + SGU_WIDTH
OFF_ZS = OFF_VS + SGU_WIDTH
IN_WIDTH = OFF_ZS + SGU_WIDTH

SEQ_TILE = 1024
MXU_TILE = 256
ROW_CHUNK = 256
W_CHUNK = 128
W_SLOTS = 4
V7X_VMEM_LIMIT_BYTES = 56 * 1024 * 1024

_SQRT_TWO = 1.4142135623730951

_WeightCopy = collections.namedtuple("_WeightCopy", "hbm stage sem dst col_scale")


def _silu_of_half(hz):
    return hz * (1.0 + jnp.tanh(hz))


def _gelu_of_half(hz):
    return hz * (1.0 + lax.erf(hz * _SQRT_TWO))


def _layer_kernel(sinks_ref, x_ref, ng_ref, win_hbm, bin_ref, lng_ref, lnb_ref,
                  sw_ref, sbt_ref, wout_hbm, bout_ref, fg_ref, o_ref,
                  win_ref, wout_ref, bsc_s, stage_in, stage_out, sem_in, sem_out,
                  q_s, kd_s, vd_s, u_s, vl_s, gs_s, mix_s, wp_s, sb_s, *, seq_tile):
    f32, bf16 = jnp.float32, jnp.bfloat16
    n_sub = seq_tile // BLOCK
    b_idx = pl.program_id(0)
    s_idx = pl.program_id(1)

    lane = lax.broadcasted_iota(jnp.int32, (BLOCK, BLOCK), 1)
    low_half = lane < HEAD_DIM

    @pl.when((b_idx == 0) & (s_idx == 0))
    def _():
        row = lax.broadcasted_iota(jnp.int32, (BLOCK, BLOCK), 0)
        for p in range(N_SGU_HEADS // 2):
            for half in range(2):
                w = jnp.where(row >= lane, sw_ref[2 * p + half], 0.0)
                wp_s[p, :, half * BLOCK:(half + 1) * BLOCK] = w.astype(bf16)
        sbt = sbt_ref[...]
        for hh in range(N_SGU_HEADS):
            sb_s[:, hh * HEAD_DIM:(hh + 1) * HEAD_DIM] = jnp.broadcast_to(
                sbt[:, hh:hh + 1], (BLOCK, HEAD_DIM))
        col = lax.broadcasted_iota(jnp.int32, (1, IN_WIDTH), 1)
        col_scale = jnp.where(col < OFF_K, ATTN_SCALE, jnp.where(col < OFF_ZA, 1.0, 0.5)).astype(f32)
        bsc_s[...] = bin_ref[...] * col_scale
        jobs = (_WeightCopy(win_hbm, stage_in, sem_in, win_ref, col_scale),
                _WeightCopy(wout_hbm, stage_out, sem_out, wout_ref, None))
        n_chunks = D_MODEL // W_CHUNK

        def chunk_copy(job, c):
            return pltpu.make_async_copy(job.hbm.at[pl.ds(c * W_CHUNK, W_CHUNK), :],
                                         job.stage.at[c % W_SLOTS], job.sem.at[c % W_SLOTS])

        for job in jobs:
            for c in range(W_SLOTS):
                chunk_copy(job, c).start(priority=c % 2)
        for job in jobs:
            for c in range(n_chunks):
                chunk_copy(job, c).wait()
                w = job.stage[c % W_SLOTS]
                if job.col_scale is not None:
                    w = w * job.col_scale
                job.dst[c * W_CHUNK:(c + 1) * W_CHUNK, :] = w.astype(bf16)
                if c + W_SLOTS < n_chunks:
                    chunk_copy(job, c + W_SLOTS).start(priority=c % 2)

    @pl.when(s_idx == 0)
    def _():
        for hh in range(N_KV_HEADS):
            kd_s[hh, 0:BLOCK, :] = jnp.zeros((BLOCK, KV_WIDTH), bf16)
            vd_s[hh, 0:BLOCK, :] = jnp.zeros((BLOCK, KV_WIDTH), bf16)

    @pl.when(s_idx > 0)
    def _():
        for hh in range(N_KV_HEADS):
            kd_s[hh, 0:BLOCK, :] = kd_s[hh, seq_tile:seq_tile + BLOCK, :]
            vd_s[hh, 0:BLOCK, :] = vd_s[hh, seq_tile:seq_tile + BLOCK, :]

    h_chunks = []
    for r0 in range(0, seq_tile, ROW_CHUNK):
        x = x_ref[0, r0:r0 + ROW_CHUNK, :]
        ms = jnp.mean(x * x, axis=-1, keepdims=True)
        hc = ((x * lax.rsqrt(ms + NORM_EPS)) * ng_ref[...]).astype(bf16)
        h_chunks.append(hc)
        q = jnp.dot(hc, win_ref[:, OFF_Q:OFF_Q + ATTN_WIDTH], preferred_element_type=f32)
        q_s[r0:r0 + ROW_CHUNK, :] = (q + bsc_s[:, OFF_Q:OFF_Q + ATTN_WIDTH]).astype(bf16)
    h = jnp.concatenate(h_chunks, axis=0)
    half_t = seq_tile // 2

    def proj_tile(off):
        cols = slice(off, off + MXU_TILE)
        return jnp.concatenate(
            [jnp.dot(h[i * half_t:(i + 1) * half_t], win_ref[:, cols], preferred_element_type=f32)
             for i in range(2)], axis=0) + bsc_s[:, cols]

    kv = proj_tile(OFF_K)
    half_mask = lax.broadcasted_iota(jnp.int32, (seq_tile, KV_WIDTH), 1) < HEAD_DIM
    for dup_ref, off in ((kd_s, 0), (vd_s, KV_WIDTH)):
        t = kv[:, off:off + KV_WIDTH]
        t_sw = pltpu.roll(t, HEAD_DIM, axis=1)
        dup_ref[0, BLOCK:BLOCK + seq_tile, :] = jnp.where(half_mask, t, t_sw).astype(bf16)
        dup_ref[1, BLOCK:BLOCK + seq_tile, :] = jnp.where(half_mask, t_sw, t).astype(bf16)

    ga_t = [_silu_of_half(proj_tile(OFF_ZA))]

    row = lax.broadcasted_iota(jnp.int32, (BLOCK, BLOCK), 0)
    prev_side = lane > row
    prev_side4 = jnp.concatenate([prev_side] * Q_PER_KV, axis=0)
    has_prev = s_idx > 0
    units = [(j, hh) for j in range(n_sub) for hh in range(N_KV_HEADS)]
    scores = []
    for j, hh in units:
        rows = slice(j * BLOCK, (j + 1) * BLOCK)
        parts = []
        for g in range(Q_PER_KV):
            c = hh * (Q_PER_KV // 2) + g // 2
            qc = q_s[rows, c * BLOCK:(c + 1) * BLOCK]
            keep = low_half if g % 2 == 0 else jnp.logical_not(low_half)
            parts.append(jnp.where(keep, qc, jnp.zeros_like(qc)))
        qst = jnp.concatenate(parts, axis=0)
        kb = kd_s[hh, j * BLOCK:(j + 2) * BLOCK, :]
        sc = lax.dot_general(qst, kb, (((1,), (1,)), ((), ())),
                             preferred_element_type=f32)
        s_prev = sc[:, 0:BLOCK]
        if j == 0:
            s_prev = jnp.where(has_prev, s_prev, NEG_INF)
        scores.append(jnp.where(prev_side4, s_prev, sc[:, BLOCK:2 * BLOCK]))

    def attend(u):
        j, hh = units[u]
        sc = scores[u]
        probs, inv_denoms = [], []
        for g in range(Q_PER_KV):
            sg = sc[g * BLOCK:(g + 1) * BLOCK, :]
            sink = sinks_ref[hh * Q_PER_KV + g]
            m = jnp.maximum(jnp.max(sg, axis=-1, keepdims=True), sink)
            p = jnp.exp(sg - m)
            denom = jnp.sum(p, axis=-1, keepdims=True) + jnp.exp(sink - m)
            inv_denoms.append(1.0 / denom)
            pb = p.astype(bf16)
            zero = jnp.zeros_like(pb)
            probs.append(jnp.concatenate([jnp.where(prev_side, pb, zero),
                                          jnp.where(prev_side, zero, pb)], axis=1))
        probs = jnp.concatenate(probs, axis=0)
        vb = vd_s[hh, j * BLOCK:(j + 2) * BLOCK, :]
        o = jnp.dot(probs, vb, preferred_element_type=f32)
        o = [o[g * BLOCK:(g + 1) * BLOCK, :] * inv_denoms[g] for g in range(Q_PER_KV)]
        return [jnp.where(low_half, o[2 * c2], o[2 * c2 + 1]) for c2 in range(Q_PER_KV // 2)]

    attn_out = []
    n_slots = (IN_WIDTH - OFF_ZA) // MXU_TILE - 1
    units_after = [3, 3, 3, 2, 2, 2, 1]
    assert sum(units_after) == len(units) and len(units_after) == n_slots
    slot = iter(units_after)

    def attend_some():
        for _ in range(next(slot)):
            attn_out.append(attend(len(attn_out)))

    for n in range(1, ATTN_WIDTH // MXU_TILE):
        ga_t.append(_silu_of_half(proj_tile(OFF_ZA + n * MXU_TILE)))
        attend_some()
    for n in range(SGU_WIDTH // MXU_TILE):
        u_s[:, n * MXU_TILE:(n + 1) * MXU_TILE] = _gelu_of_half(proj_tile(OFF_U + n * MXU_TILE))
        attend_some()
    vg = []
    for n in range(SGU_WIDTH // MXU_TILE):
        vg.append(_gelu_of_half(proj_tile(OFF_VS + n * MXU_TILE)))
        attend_some()
    vg = jnp.concatenate(vg, axis=1)
    mu = jnp.mean(vg, axis=-1, keepdims=True)
    vc = vg - mu
    var = jnp.mean(vc * vc, axis=-1, keepdims=True)
    vl_s[...] = ((vc * lax.rsqrt(var + NORM_EPS)) * lng_ref[...] + lnb_ref[...]).astype(bf16)
    for n in range(SGU_WIDTH // MXU_TILE):
        gs_s[:, n * MXU_TILE:(n + 1) * MXU_TILE] = _silu_of_half(proj_tile(OFF_ZS + n * MXU_TILE))
        attend_some()
    assert len(attn_out) == len(units)

    ga = jnp.concatenate(ga_t, axis=1)
    for u, (j, hh) in enumerate(units):
        rows = slice(j * BLOCK, (j + 1) * BLOCK)
        for c2 in range(Q_PER_KV // 2):
            cols = slice((hh * (Q_PER_KV // 2) + c2) * BLOCK, (hh * (Q_PER_KV // 2) + c2 + 1) * BLOCK)
            mix_s[rows, cols] = (attn_out[u][c2] * ga[rows, cols]).astype(bf16)

    for j in range(n_sub):
        rows = slice(j * BLOCK, (j + 1) * BLOCK)
        for p_idx in range(N_SGU_HEADS // 2):
            cols = slice(p_idx * BLOCK, (p_idx + 1) * BLOCK)
            vp = vl_s[rows, cols]
            zero = jnp.zeros_like(vp)
            rhs = jnp.concatenate([jnp.where(low_half, vp, zero),
                                   jnp.where(low_half, zero, vp)], axis=0)
            mixed = jnp.dot(wp_s[p_idx], rhs, preferred_element_type=f32) + sb_s[:, cols]
            mix_s[rows, ATTN_WIDTH + p_idx * BLOCK:ATTN_WIDTH + (p_idx + 1) * BLOCK] = (
                (u_s[rows, cols] * mixed) * gs_s[rows, cols]).astype(bf16)

    y = x_ref[0] + jnp.dot(mix_s[...], wout_ref[...], preferred_element_type=f32) + bout_ref[...]
    ms2 = jnp.mean(y * y, axis=-1, keepdims=True)
    o_ref[0] = (y * lax.rsqrt(ms2 + NORM_EPS)) * fg_ref[...]


def _layer_call(x, sinks, norm_g, w_in, b_in, ln_g, ln_b, sgu_w, sgu_bt, w_out, b_out, final_g,
                *, seq_tile=SEQ_TILE):
    batch, seq, d_model = x.shape
    assert d_model == D_MODEL and seq % seq_tile == 0 and seq_tile % (2 * BLOCK) == 0
    f32, bf16 = jnp.float32, jnp.bfloat16

    def full(shape):
        return pl.BlockSpec(shape, lambda b, s: (0,) * len(shape))

    tile_spec = pl.BlockSpec((1, seq_tile, D_MODEL), lambda b, s: (b, s, 0))
    return pl.pallas_call(
        functools.partial(_layer_kernel, seq_tile=seq_tile),
        grid=(batch, seq // seq_tile),
        in_specs=[
            pl.BlockSpec(memory_space=pltpu.SMEM),
            tile_spec,
            full((1, D_MODEL)),
            pl.BlockSpec(memory_space=pl.ANY),
            full((1, IN_WIDTH)),
            full((1, SGU_WIDTH)),
            full((1, SGU_WIDTH)),
            full((N_SGU_HEADS, BLOCK, BLOCK)),
            full((BLOCK, N_SGU_HEADS)),
            pl.BlockSpec(memory_space=pl.ANY),
            full((1, D_MODEL)),
            full((1, D_MODEL)),
        ],
        out_specs=tile_spec,
        out_shape=jax.ShapeDtypeStruct(x.shape, x.dtype),
        scratch_shapes=[
            pltpu.VMEM((D_MODEL, IN_WIDTH), bf16),
            pltpu.VMEM((D_MODEL, D_MODEL), bf16),
            pltpu.VMEM((1, IN_WIDTH), f32),
            pltpu.VMEM((W_SLOTS, W_CHUNK, IN_WIDTH), f32),
            pltpu.VMEM((W_SLOTS, W_CHUNK, D_MODEL), f32),
            pltpu.SemaphoreType.DMA((W_SLOTS,)),
            pltpu.SemaphoreType.DMA((W_SLOTS,)),
            pltpu.VMEM((seq_tile, ATTN_WIDTH), bf16),
            pltpu.VMEM((N_KV_HEADS, BLOCK + seq_tile, KV_WIDTH), bf16),
            pltpu.VMEM((N_KV_HEADS, BLOCK + seq_tile, KV_WIDTH), bf16),
            pltpu.VMEM((seq_tile, SGU_WIDTH), f32),
            pltpu.VMEM((seq_tile, SGU_WIDTH), bf16),
            pltpu.VMEM((seq_tile, SGU_WIDTH), f32),
            pltpu.VMEM((seq_tile, D_MODEL), bf16),
            pltpu.VMEM((N_SGU_HEADS // 2, BLOCK, 2 * BLOCK), bf16),
            pltpu.VMEM((BLOCK, SGU_WIDTH), f32),
        ],
        compiler_params=pltpu.CompilerParams(
            dimension_semantics=("arbitrary", "arbitrary"),
            vmem_limit_bytes=V7X_VMEM_LIMIT_BYTES),
        name="hybrid_layer",
    )(sinks, x, norm_g, w_in, b_in, ln_g, ln_b, sgu_w, sgu_bt, w_out, b_out, final_g)


def kernel(x, norm_g, w_in, b_in, attn_sinks, sgu_ln_g, sgu_ln_b, sgu_w, sgu_b, w_out, b_out, final_norm_g):
    depth = norm_g.shape[0]
    for l in range(depth):
        last = l == depth - 1
        assert last, "the fused call applies the final norm; only depth 1 is supported"
        x = _layer_call(
            x, attn_sinks[l], norm_g[l][None, :], w_in[l], b_in[l][None, :],
            sgu_ln_g[l][None, :], sgu_ln_b[l][None, :], sgu_w[l], sgu_b[l].T,
            w_out[l], b_out[l][None, :], final_norm_g[None, :])
    return x
```

```python
import collections
import functools

import jax
import jax.numpy as jnp
from jax import lax
from jax.experimental import pallas as pl
from jax.experimental.pallas import tpu as pltpu

D_MODEL = 1024
HEAD_DIM = 64
ATTN_WIDTH = 512
KV_WIDTH = 128
SGU_WIDTH = 512
N_KV_HEADS = 2
Q_PER_KV = 4
N_SGU_HEADS = 8
BLOCK = 128
NORM_EPS = 1e-5
NEG_INF = -1e30
ATTN_SCALE = HEAD_DIM ** -0.5

OFF_Q = 0
OFF_K = OFF_Q + ATTN_WIDTH
OFF_V = OFF_K + KV_WIDTH
OFF_ZA = OFF_V + KV_WIDTH
OFF_U = OFF_ZA + ATTN_WIDTH
OFF_VS = OFF_U + SGU_WIDTH
OFF_ZS = OFF_VS + SGU_WIDTH
IN_WIDTH = OFF_ZS + SGU_WIDTH

SEQ_TILE = 1024
MXU_TILE = 256
ROW_CHUNK = 256
W_CHUNK = 128
W_SLOTS = 4
V7X_VMEM_LIMIT_BYTES = 56 * 1024 * 1024

_SQRT_TWO = 1.4142135623730951

_WeightCopy = collections.namedtuple("_WeightCopy", "hbm stage sem dst col_scale")


def _silu_of_half(hz):
    return hz * (1.0 + jnp.tanh(hz))


def _gelu_of_half(hz):
    return hz * (1.0 + lax.erf(hz * _SQRT_TWO))


def _layer_kernel(sinks_ref, x_ref, ng_ref, win_hbm, bin_ref, lng_ref, lnb_ref,
                  sw_ref, sbt_ref, wout_hbm, bout_ref, fg_ref, o_ref,
                  win_ref, wout_ref, bsc_s, stage_in, stage_out, sem_in, sem_out,
                  q_s, kd_s, vd_s, u_s, vl_s, gs_s, mix_s, wp_s, sb_s, *, seq_tile):
    f32, bf16 = jnp.float32, jnp.bfloat16
    n_sub = seq_tile // BLOCK
    b_idx = pl.program_id(0)
    s_idx = pl.program_id(1)

    lane = lax.broadcasted_iota(jnp.int32, (BLOCK, BLOCK), 1)
    low_half = lane < HEAD_DIM

    @pl.when((b_idx == 0) & (s_idx == 0))
    def _():
        row = lax.broadcasted_iota(jnp.int32, (BLOCK, BLOCK), 0)
        for p in range(N_SGU_HEADS // 2):
            for half in range(2):
                w = jnp.where(row >= lane, sw_ref[2 * p + half], 0.0)
                wp_s[p, :, half * BLOCK:(half + 1) * BLOCK] = w.astype(bf16)
        sbt = sbt_ref[...]
        for hh in range(N_SGU_HEADS):
            sb_s[:, hh * HEAD_DIM:(hh + 1) * HEAD_DIM] = jnp.broadcast_to(
                sbt[:, hh:hh + 1], (BLOCK, HEAD_DIM))
        col = lax.broadcasted_iota(jnp.int32, (1, IN_WIDTH), 1)
        col_scale = jnp.where(col < OFF_K, ATTN_SCALE, jnp.where(col < OFF_ZA, 1.0, 0.5)).astype(f32)
        bsc_s[...] = bin_ref[...] * col_scale
        jobs = (_WeightCopy(win_hbm, stage_in, sem_in, win_ref, col_scale),
                _WeightCopy(wout_hbm, stage_out, sem_out, wout_ref, None))
        n_chunks = D_MODEL // W_CHUNK

        def chunk_copy(job, c):
            return pltpu.make_async_copy(job.hbm.at[pl.ds(c * W_CHUNK, W_CHUNK), :],
                                         job.stage.at[c % W_SLOTS], job.sem.at[c % W_SLOTS])

        for job in jobs:
            for c in range(W_SLOTS):
                chunk_copy(job, c).start(priority=c % 2)
        for job in jobs:
            for c in range(n_chunks):
                chunk_copy(job, c).wait()
                w = job.stage[c % W_SLOTS]
                if job.col_scale is not None:
                    w = w * job.col_scale
                job.dst[c * W_CHUNK:(c + 1) * W_CHUNK, :] = w.astype(bf16)
                if c + W_SLOTS < n_chunks:
                    chunk_copy(job, c + W_SLOTS).start(priority=c % 2)

    @pl.when(s_idx == 0)
    def _():
        for hh in range(N_KV_HEADS):
            kd_s[hh, 0:BLOCK, :] = jnp.zeros((BLOCK, KV_WIDTH), bf16)
            vd_s[hh, 0:BLOCK, :] = jnp.zeros((BLOCK, KV_WIDTH), bf16)

    @pl.when(s_idx > 0)
    def _():
        for hh in range(N_KV_HEADS):
            kd_s[hh, 0:BLOCK, :] = kd_s[hh, seq_tile:seq_tile + BLOCK, :]
            vd_s[hh, 0:BLOCK, :] = vd_s[hh, seq_tile:seq_tile + BLOCK, :]

    h_chunks = []
    for r0 in range(0, seq_tile, ROW_CHUNK):
        x = x_ref[0, r0:r0 + ROW_CHUNK, :]
        ms = jnp.mean(x * x, axis=-1, keepdims=True)
        hc = ((x * lax.rsqrt(ms + NORM_EPS)) * ng_ref[...]).astype(bf16)
        h_chunks.append(hc)
        q = jnp.dot(hc, win_ref[:, OFF_Q:OFF_Q + ATTN_WIDTH], preferred_element_type=f32)
        q_s[r0:r0 + ROW_CHUNK, :] = (q + bsc_s[:, OFF_Q:OFF_Q + ATTN_WIDTH]).astype(bf16)
    h = jnp.concatenate(h_chunks, axis=0)
    half_t = seq_tile // 2

    def proj_tile(off):
        cols = slice(off, off + MXU_TILE)
        return jnp.concatenate(
            [jnp.dot(h[i * half_t:(i + 1) * half_t], win_ref[:, cols], preferred_element_type=f32)
             for i in range(2)], axis=0) + bsc_s[:, cols]

    kv = proj_tile(OFF_K)
    half_mask = lax.broadcasted_iota(jnp.int32, (seq_tile, KV_WIDTH), 1) < HEAD_DIM
    for dup_ref, off in ((kd_s, 0), (vd_s, KV_WIDTH)):
        t = kv[:, off:off + KV_WIDTH]
        t_sw = pltpu.roll(t, HEAD_DIM, axis=1)
        dup_ref[0, BLOCK:BLOCK + seq_tile, :] = jnp.where(half_mask, t, t_sw).astype(bf16)
        dup_ref[1, BLOCK:BLOCK + seq_tile, :] = jnp.where(half_mask, t_sw, t).astype(bf16)

    ga_t = [_silu_of_half(proj_tile(OFF_ZA))]

    row = lax.broadcasted_iota(jnp.int32, (BLOCK, BLOCK), 0)
    prev_side = lane > row
    prev_side4 = jnp.concatenate([prev_side] * Q_PER_KV, axis=0)
    has_prev = s_idx > 0
    units = [(j, hh) for j in range(n_sub) for hh in range(N_KV_HEADS)]
    scores = []
    for j, hh in units:
        rows = slice(j * BLOCK, (j + 1) * BLOCK)
        parts = []
        for g in range(Q_PER_KV):
            c = hh * (Q_PER_KV // 2) + g // 2
            qc = q_s[rows, c * BLOCK:(c + 1) * BLOCK]
            keep = low_half if g % 2 == 0 else jnp.logical_not(low_half)
            parts.append(jnp.where(keep, qc, jnp.zeros_like(qc)))
        qst = jnp.concatenate(parts, axis=0)
        kb = kd_s[hh, j * BLOCK:(j + 2) * BLOCK, :]
        sc = lax.dot_general(qst, kb, (((1,), (1,)), ((), ())),
                             preferred_element_type=f32)
        s_prev = sc[:, 0:BLOCK]
        if j == 0:
            s_prev = jnp.where(has_prev, s_prev, NEG_INF)
        scores.append(jnp.where(prev_side4, s_prev, sc[:, BLOCK:2 * BLOCK]))

    def attend(u):
        j, hh = units[u]
        sc = scores[u]
        probs, inv_denoms = [], []
        for g in range(Q_PER_KV):
            sg = sc[g * BLOCK:(g + 1) * BLOCK, :]
            sink = sinks_ref[hh * Q_PER_KV + g]
            m = jnp.maximum(jnp.max(sg, axis=-1, keepdims=True), sink)
            p = jnp.exp(sg - m)
            denom = jnp.sum(p, axis=-1, keepdims=True) + jnp.exp(sink - m)
            inv_denoms.append(1.0 / denom)
            pb = p.astype(bf16)
            zero = jnp.zeros_like(pb)
            probs.append(jnp.concatenate([jnp.where(prev_side, pb, zero),
                                          jnp.where(prev_side, zero, pb)], axis=1))
        probs = jnp.concatenate(probs, axis=0)
        vb = vd_s[hh, j * BLOCK:(j + 2) * BLOCK, :]
        o = jnp.dot(probs, vb, preferred_element_type=f32)
        o = [o[g * BLOCK:(g + 1) * BLOCK, :] * inv_denoms[g] for g in range(Q_PER_KV)]
        return [jnp.where(low_half, o[2 * c2], o[2 * c2 + 1]) for c2 in range(Q_PER_KV // 2)]

    attn_out = []
    n_slots = (IN_WIDTH - OFF_ZA) // MXU_TILE - 1
    units_after = [1, 2, 2, 2, 3, 3, 3]
    assert sum(units_after) == len(units) and len(units_after) == n_slots
    slot = iter(units_after)

    def attend_some():
        for _ in range(next(slot)):
            attn_out.append(attend(len(attn_out)))

    for n in range(1, ATTN_WIDTH // MXU_TILE):
        ga_t.append(_silu_of_half(proj_tile(OFF_ZA + n * MXU_TILE)))
        attend_some()
    for n in range(SGU_WIDTH // MXU_TILE):
        u_s[:, n * MXU_TILE:(n + 1) * MXU_TILE] = _gelu_of_half(proj_tile(OFF_U + n * MXU_TILE))
        attend_some()
    vg = []
    for n in range(SGU_WIDTH // MXU_TILE):
        vg.append(_gelu_of_half(proj_tile(OFF_VS + n * MXU_TILE)))
        attend_some()
    vg = jnp.concatenate(vg, axis=1)
    mu = jnp.mean(vg, axis=-1, keepdims=True)
    vc = vg - mu
    var = jnp.mean(vc * vc, axis=-1, keepdims=True)
    vl_s[...] = ((vc * lax.rsqrt(var + NORM_EPS)) * lng_ref[...] + lnb_ref[...]).astype(bf16)
    for n in range(SGU_WIDTH // MXU_TILE):
        gs_s[:, n * MXU_TILE:(n + 1) * MXU_TILE] = _silu_of_half(proj_tile(OFF_ZS + n * MXU_TILE))
        attend_some()
    assert len(attn_out) == len(units)

    ga = jnp.concatenate(ga_t, axis=1)
    for u, (j, hh) in enumerate(units):
        rows = slice(j * BLOCK, (j + 1) * BLOCK)
        for c2 in range(Q_PER_KV // 2):
            cols = slice((hh * (Q_PER_KV // 2) + c2) * BLOCK, (hh * (Q_PER_KV // 2) + c2 + 1) * BLOCK)
            mix_s[rows, cols] = (attn_out[u][c2] * ga[rows, cols]).astype(bf16)

    for j in range(n_sub):
        rows = slice(j * BLOCK, (j + 1) * BLOCK)
        for p_idx in range(N_SGU_HEADS // 2):
            cols = slice(p_idx * BLOCK, (p_idx + 1) * BLOCK)
            vp = vl_s[rows, cols]
            zero = jnp.zeros_like(vp)
            rhs = jnp.concatenate([jnp.where(low_half, vp, zero),
                                   jnp.where(low_half, zero, vp)], axis=0)
            mixed = jnp.dot(wp_s[p_idx], rhs, preferred_element_type=f32) + sb_s[:, cols]
            mix_s[rows, ATTN_WIDTH + p_idx * BLOCK:ATTN_WIDTH + (p_idx + 1) * BLOCK] = (
                (u_s[rows, cols] * mixed) * gs_s[rows, cols]).astype(bf16)

    y = x_ref[0] + jnp.dot(mix_s[...], wout_ref[...], preferred_element_type=f32) + bout_ref[...]
    ms2 = jnp.mean(y * y, axis=-1, keepdims=True)
    o_ref[0] = (y * lax.rsqrt(ms2 + NORM_EPS)) * fg_ref[...]


def _layer_call(x, sinks, norm_g, w_in, b_in, ln_g, ln_b, sgu_w, sgu_bt, w_out, b_out, final_g,
                *, seq_tile=SEQ_TILE):
    batch, seq, d_model = x.shape
    assert d_model == D_MODEL and seq % seq_tile == 0 and seq_tile % (2 * BLOCK) == 0
    f32, bf16 = jnp.float32, jnp.bfloat16

    def full(shape):
        return pl.BlockSpec(shape, lambda b, s: (0,) * len(shape))

    tile_spec = pl.BlockSpec((1, seq_tile, D_MODEL), lambda b, s: (b, s, 0))
    return pl.pallas_call(
        functools.partial(_layer_kernel, seq_tile=seq_tile),
        grid=(batch, seq // seq_tile),
        in_specs=[
            pl.BlockSpec(memory_space=pltpu.SMEM),
            tile_spec,
            full((1, D_MODEL)),
            pl.BlockSpec(memory_space=pl.ANY),
            full((1, IN_WIDTH)),
            full((1, SGU_WIDTH)),
            full((1, SGU_WIDTH)),
            full((N_SGU_HEADS, BLOCK, BLOCK)),
            full((BLOCK, N_SGU_HEADS)),
            pl.BlockSpec(memory_space=pl.ANY),
            full((1, D_MODEL)),
            full((1, D_MODEL)),
        ],
        out_specs=tile_spec,
        out_shape=jax.ShapeDtypeStruct(x.shape, x.dtype),
        scratch_shapes=[
            pltpu.VMEM((D_MODEL, IN_WIDTH), bf16),
            pltpu.VMEM((D_MODEL, D_MODEL), bf16),
            pltpu.VMEM((1, IN_WIDTH), f32),
            pltpu.VMEM((W_SLOTS, W_CHUNK, IN_WIDTH), f32),
            pltpu.VMEM((W_SLOTS, W_CHUNK, D_MODEL), f32),
            pltpu.SemaphoreType.DMA((W_SLOTS,)),
            pltpu.SemaphoreType.DMA((W_SLOTS,)),
            pltpu.VMEM((seq_tile, ATTN_WIDTH), bf16),
            pltpu.VMEM((N_KV_HEADS, BLOCK + seq_tile, KV_WIDTH), bf16),
            pltpu.VMEM((N_KV_HEADS, BLOCK + seq_tile, KV_WIDTH), bf16),
            pltpu.VMEM((seq_tile, SGU_WIDTH), f32),
            pltpu.VMEM((seq_tile, SGU_WIDTH), bf16),
            pltpu.VMEM((seq_tile, SGU_WIDTH), f32),
            pltpu.VMEM((seq_tile, D_MODEL), bf16),
            pltpu.VMEM((N_SGU_HEADS // 2, BLOCK, 2 * BLOCK), bf16),
            pltpu.VMEM((BLOCK, SGU_WIDTH), f32),
        ],
        compiler_params=pltpu.CompilerParams(
            dimension_semantics=("arbitrary", "arbitrary"),
            vmem_limit_bytes=V7X_VMEM_LIMIT_BYTES),
        name="hybrid_layer",
    )(sinks, x, norm_g, w_in, b_in, ln_g, ln_b, sgu_w, sgu_bt, w_out, b_out, final_g)


def kernel(x, norm_g, w_in, b_in, attn_sinks, sgu_ln_g, sgu_ln_b, sgu_w, sgu_b, w_out, b_out, final_norm_g):
    depth = norm_g.shape[0]
    for l in range(depth):
        last = l == depth - 1
        assert last, "the fused call applies the final norm; only depth 1 is supported"
        x = _layer_call(
            x, attn_sinks[l], norm_g[l][None, :], w_in[l], b_in[l][None, :],
            sgu_ln_g[l][None, :], sgu_ln_b[l][None, :], sgu_w[l], sgu_b[l].T,
            w_out[l], b_out[l][None, :], final_norm_g[None, :])
    return x
```

```python
import collections
import functools

import jax
import jax.numpy as jnp
from jax import lax
from jax.experimental import pallas as pl
from jax.experimental.pallas import tpu as pltpu

D_MODEL = 1024
HEAD_DIM = 64
ATTN_WIDTH = 512
KV_WIDTH = 128
SGU_WIDTH = 512
N_KV_HEADS = 2
Q_PER_KV = 4
N_SGU_HEADS = 8
BLOCK = 128
NORM_EPS = 1e-5
NEG_INF = -1e30
ATTN_SCALE = HEAD_DIM ** -0.5

OFF_Q = 0
OFF_K = OFF_Q + ATTN_WIDTH
OFF_V = OFF_K + KV_WIDTH
OFF_ZA = OFF_V + KV_WIDTH
OFF_U = OFF_ZA + ATTN_WIDTH
OFF_VS = OFF_U + SGU_WIDTH
OFF_ZS = OFF_VS + SGU_WIDTH
IN_WIDTH = OFF_ZS + SGU_WIDTH

SEQ_TILE = 1024
MXU_TILE = 256
ROW_CHUNK = 256
SCORE_AHEAD = 8
W_CHUNK = 128
W_SLOTS = 4
V7X_VMEM_LIMIT_BYTES = 56 * 1024 * 1024

_SQRT_TWO = 1.4142135623730951

_WeightCopy = collections.namedtuple("_WeightCopy", "hbm stage sem dst col_scale")


def _silu_of_half(hz):
    return hz * (1.0 + jnp.tanh(hz))


def _gelu_of_half(hz):
    return hz * (1.0 + lax.erf(hz * _SQRT_TWO))


def _layer_kernel(sinks_ref, x_ref, ng_ref, win_hbm, bin_ref, lng_ref, lnb_ref,
                  sw_ref, sbt_ref, wout_hbm, bout_ref, fg_ref, o_ref,
                  win_ref, wout_ref, bsc_s, stage_in, stage_out, sem_in, sem_out,
                  q_s, kd_s, vd_s, u_s, vl_s, gs_s, mix_s, wp_s, sb_s, *, seq_tile):
    f32, bf16 = jnp.float32, jnp.bfloat16
    n_sub = seq_tile // BLOCK
    b_idx = pl.program_id(0)
    s_idx = pl.program_id(1)

    lane = lax.broadcasted_iota(jnp.int32, (BLOCK, BLOCK), 1)
    low_half = lane < HEAD_DIM

    @pl.when((b_idx == 0) & (s_idx == 0))
    def _():
        row = lax.broadcasted_iota(jnp.int32, (BLOCK, BLOCK), 0)
        for p in range(N_SGU_HEADS // 2):
            for half in range(2):
                w = jnp.where(row >= lane, sw_ref[2 * p + half], 0.0)
                wp_s[p, :, half * BLOCK:(half + 1) * BLOCK] = w.astype(bf16)
        sbt = sbt_ref[...]
        for hh in range(N_SGU_HEADS):
            sb_s[:, hh * HEAD_DIM:(hh + 1) * HEAD_DIM] = jnp.broadcast_to(
                sbt[:, hh:hh + 1], (BLOCK, HEAD_DIM))
        col = lax.broadcasted_iota(jnp.int32, (1, IN_WIDTH), 1)
        col_scale = jnp.where(col < OFF_K, ATTN_SCALE, jnp.where(col < OFF_ZA, 1.0, 0.5)).astype(f32)
        bsc_s[...] = bin_ref[...] * col_scale
        jobs = (_WeightCopy(win_hbm, stage_in, sem_in, win_ref, col_scale),
                _WeightCopy(wout_hbm, stage_out, sem_out, wout_ref, None))
        n_chunks = D_MODEL // W_CHUNK

        def chunk_copy(job, c):
            return pltpu.make_async_copy(job.hbm.at[pl.ds(c * W_CHUNK, W_CHUNK), :],
                                         job.stage.at[c % W_SLOTS], job.sem.at[c % W_SLOTS])

        for job in jobs:
            for c in range(W_SLOTS):
                chunk_copy(job, c).start(priority=c % 2)
        for job in jobs:
            for c in range(n_chunks):
                chunk_copy(job, c).wait()
                w = job.stage[c % W_SLOTS]
                if job.col_scale is not None:
                    w = w * job.col_scale
                job.dst[c * W_CHUNK:(c + 1) * W_CHUNK, :] = w.astype(bf16)
                if c + W_SLOTS < n_chunks:
                    chunk_copy(job, c + W_SLOTS).start(priority=c % 2)

    @pl.when(s_idx == 0)
    def _():
        for hh in range(N_KV_HEADS):
            kd_s[hh, 0:BLOCK, :] = jnp.zeros((BLOCK, KV_WIDTH), bf16)
            vd_s[hh, 0:BLOCK, :] = jnp.zeros((BLOCK, KV_WIDTH), bf16)

    @pl.when(s_idx > 0)
    def _():
        for hh in range(N_KV_HEADS):
            kd_s[hh, 0:BLOCK, :] = kd_s[hh, seq_tile:seq_tile + BLOCK, :]
            vd_s[hh, 0:BLOCK, :] = vd_s[hh, seq_tile:seq_tile + BLOCK, :]

    h_chunks = []
    for r0 in range(0, seq_tile, ROW_CHUNK):
        x = x_ref[0, r0:r0 + ROW_CHUNK, :]
        ms = jnp.mean(x * x, axis=-1, keepdims=True)
        hc = ((x * lax.rsqrt(ms + NORM_EPS)) * ng_ref[...]).astype(bf16)
        h_chunks.append(hc)
        q = jnp.dot(hc, win_ref[:, OFF_Q:OFF_Q + ATTN_WIDTH], preferred_element_type=f32)
        q_s[r0:r0 + ROW_CHUNK, :] = (q + bsc_s[:, OFF_Q:OFF_Q + ATTN_WIDTH]).astype(bf16)
    h = jnp.concatenate(h_chunks, axis=0)
    half_t = seq_tile // 2

    def proj_tile(off):
        cols = slice(off, off + MXU_TILE)
        return jnp.concatenate(
            [jnp.dot(h[i * half_t:(i + 1) * half_t], win_ref[:, cols], preferred_element_type=f32)
             for i in range(2)], axis=0) + bsc_s[:, cols]

    kv = proj_tile(OFF_K)
    half_mask = lax.broadcasted_iota(jnp.int32, (seq_tile, KV_WIDTH), 1) < HEAD_DIM
    for dup_ref, off in ((kd_s, 0), (vd_s, KV_WIDTH)):
        t = kv[:, off:off + KV_WIDTH]
        t_sw = pltpu.roll(t, HEAD_DIM, axis=1)
        dup_ref[0, BLOCK:BLOCK + seq_tile, :] = jnp.where(half_mask, t, t_sw).astype(bf16)
        dup_ref[1, BLOCK:BLOCK + seq_tile, :] = jnp.where(half_mask, t_sw, t).astype(bf16)

    ga_t = [_silu_of_half(proj_tile(OFF_ZA))]

    row = lax.broadcasted_iota(jnp.int32, (BLOCK, BLOCK), 0)
    prev_side = lane > row
    prev_side4 = jnp.concatenate([prev_side] * Q_PER_KV, axis=0)
    has_prev = s_idx > 0
    units = [(j, hh) for j in range(n_sub) for hh in range(N_KV_HEADS)]
    scores = {}

    def score(u):
        j, hh = units[u]
        rows = slice(j * BLOCK, (j + 1) * BLOCK)
        parts = []
        for g in range(Q_PER_KV):
            c = hh * (Q_PER_KV // 2) + g // 2
            qc = q_s[rows, c * BLOCK:(c + 1) * BLOCK]
            keep = low_half if g % 2 == 0 else jnp.logical_not(low_half)
            parts.append(jnp.where(keep, qc, jnp.zeros_like(qc)))
        qst = jnp.concatenate(parts, axis=0)
        kb = kd_s[hh, j * BLOCK:(j + 2) * BLOCK, :]
        sc = lax.dot_general(qst, kb, (((1,), (1,)), ((), ())),
                             preferred_element_type=f32)
        s_prev = sc[:, 0:BLOCK]
        if j == 0:
            s_prev = jnp.where(has_prev, s_prev, NEG_INF)
        scores[u] = jnp.where(prev_side4, s_prev, sc[:, BLOCK:2 * BLOCK])

    for u in range(min(SCORE_AHEAD, len(units))):
        score(u)

    def attend(u):
        j, hh = units[u]
        sc = scores[u]
        probs, inv_denoms = [], []
        for g in range(Q_PER_KV):
            sg = sc[g * BLOCK:(g + 1) * BLOCK, :]
            sink = sinks_ref[hh * Q_PER_KV + g]
            m = jnp.maximum(jnp.max(sg, axis=-1, keepdims=True), sink)
            p = jnp.exp(sg - m)
            denom = jnp.sum(p, axis=-1, keepdims=True) + jnp.exp(sink - m)
            inv_denoms.append(1.0 / denom)
            pb = p.astype(bf16)
            zero = jnp.zeros_like(pb)
            probs.append(jnp.concatenate([jnp.where(prev_side, pb, zero),
                                          jnp.where(prev_side, zero, pb)], axis=1))
        probs = jnp.concatenate(probs, axis=0)
        vb = vd_s[hh, j * BLOCK:(j + 2) * BLOCK, :]
        o = jnp.dot(probs, vb, preferred_element_type=f32)
        o = [o[g * BLOCK:(g + 1) * BLOCK, :] * inv_denoms[g] for g in range(Q_PER_KV)]
        return [jnp.where(low_half, o[2 * c2], o[2 * c2 + 1]) for c2 in range(Q_PER_KV // 2)]

    attn_out = []
    n_slots = (IN_WIDTH - OFF_ZA) // MXU_TILE - 1
    units_after = [len(units) * (i + 1) // n_slots - len(units) * i // n_slots for i in range(n_slots)]
    slot = iter(units_after)

    def attend_some():
        for _ in range(next(slot)):
            if len(attn_out) + SCORE_AHEAD < len(units):
                score(len(attn_out) + SCORE_AHEAD)
            attn_out.append(attend(len(attn_out)))

    for n in range(1, ATTN_WIDTH // MXU_TILE):
        ga_t.append(_silu_of_half(proj_tile(OFF_ZA + n * MXU_TILE)))
        attend_some()
    for n in range(SGU_WIDTH // MXU_TILE):
        u_s[:, n * MXU_TILE:(n + 1) * MXU_TILE] = _gelu_of_half(proj_tile(OFF_U + n * MXU_TILE))
        attend_some()
    vg = []
    for n in range(SGU_WIDTH // MXU_TILE):
        vg.append(_gelu_of_half(proj_tile(OFF_VS + n * MXU_TILE)))
        attend_some()
    vg = jnp.concatenate(vg, axis=1)
    mu = jnp.mean(vg, axis=-1, keepdims=True)
    vc = vg - mu
    var = jnp.mean(vc * vc, axis=-1, keepdims=True)
    vl_s[...] = ((vc * lax.rsqrt(var + NORM_EPS)) * lng_ref[...] + lnb_ref[...]).astype(bf16)
    for n in range(SGU_WIDTH // MXU_TILE):
        gs_s[:, n * MXU_TILE:(n + 1) * MXU_TILE] = _silu_of_half(proj_tile(OFF_ZS + n * MXU_TILE))
        attend_some()
    assert len(attn_out) == len(units)

    ga = jnp.concatenate(ga_t, axis=1)
    for u, (j, hh) in enumerate(units):
        rows = slice(j * BLOCK, (j + 1) * BLOCK)
        for c2 in range(Q_PER_KV // 2):
            cols = slice((hh * (Q_PER_KV // 2) + c2) * BLOCK, (hh * (Q_PER_KV // 2) + c2 + 1) * BLOCK)
            mix_s[rows, cols] = (attn_out[u][c2] * ga[rows, cols]).astype(bf16)

    for j in range(n_sub):
        rows = slice(j * BLOCK, (j + 1) * BLOCK)
        for p_idx in range(N_SGU_HEADS // 2):
            cols = slice(p_idx * BLOCK, (p_idx + 1) * BLOCK)
            vp = vl_s[rows, cols]
            zero = jnp.zeros_like(vp)
            rhs = jnp.concatenate([jnp.where(low_half, vp, zero),
                                   jnp.where(low_half, zero, vp)], axis=0)
            mixed = jnp.dot(wp_s[p_idx], rhs, preferred_element_type=f32) + sb_s[:, cols]
            mix_s[rows, ATTN_WIDTH + p_idx * BLOCK:ATTN_WIDTH + (p_idx + 1) * BLOCK] = (
                (u_s[rows, cols] * mixed) * gs_s[rows, cols]).astype(bf16)

    y = x_ref[0] + jnp.dot(mix_s[...], wout_ref[...], preferred_element_type=f32) + bout_ref[...]
    ms2 = jnp.mean(y * y, axis=-1, keepdims=True)
    o_ref[0] = (y * lax.rsqrt(ms2 + NORM_EPS)) * fg_ref[...]


def _layer_call(x, sinks, norm_g, w_in, b_in, ln_g, ln_b, sgu_w, sgu_bt, w_out, b_out, final_g,
                *, seq_tile=SEQ_TILE):
    batch, seq, d_model = x.shape
    assert d_model == D_MODEL and seq % seq_tile == 0 and seq_tile % (2 * BLOCK) == 0
    f32, bf16 = jnp.float32, jnp.bfloat16

    def full(shape):
        return pl.BlockSpec(shape, lambda b, s: (0,) * len(shape))

    tile_spec = pl.BlockSpec((1, seq_tile, D_MODEL), lambda b, s: (b, s, 0))
    return pl.pallas_call(
        functools.partial(_layer_kernel, seq_tile=seq_tile),
        grid=(batch, seq // seq_tile),
        in_specs=[
            pl.BlockSpec(memory_space=pltpu.SMEM),
            tile_spec,
            full((1, D_MODEL)),
            pl.BlockSpec(memory_space=pl.ANY),
            full((1, IN_WIDTH)),
            full((1, SGU_WIDTH)),
            full((1, SGU_WIDTH)),
            full((N_SGU_HEADS, BLOCK, BLOCK)),
            full((BLOCK, N_SGU_HEADS)),
            pl.BlockSpec(memory_space=pl.ANY),
            full((1, D_MODEL)),
            full((1, D_MODEL)),
        ],
        out_specs=tile_spec,
        out_shape=jax.ShapeDtypeStruct(x.shape, x.dtype),
        scratch_shapes=[
            pltpu.VMEM((D_MODEL, IN_WIDTH), bf16),
            pltpu.VMEM((D_MODEL, D_MODEL), bf16),
            pltpu.VMEM((1, IN_WIDTH), f32),
            pltpu.VMEM((W_SLOTS, W_CHUNK, IN_WIDTH), f32),
            pltpu.VMEM((W_SLOTS, W_CHUNK, D_MODEL), f32),
            pltpu.SemaphoreType.DMA((W_SLOTS,)),
            pltpu.SemaphoreType.DMA((W_SLOTS,)),
            pltpu.VMEM((seq_tile, ATTN_WIDTH), bf16),
            pltpu.VMEM((N_KV_HEADS, BLOCK + seq_tile, KV_WIDTH), bf16),
            pltpu.VMEM((N_KV_HEADS, BLOCK + seq_tile, KV_WIDTH), bf16),
            pltpu.VMEM((seq_tile, SGU_WIDTH), f32),
            pltpu.VMEM((seq_tile, SGU_WIDTH), bf16),
            pltpu.VMEM((seq_tile, SGU_WIDTH), f32),
            pltpu.VMEM((seq_tile, D_MODEL), bf16),
            pltpu.VMEM((N_SGU_HEADS // 2, BLOCK, 2 * BLOCK), bf16),
            pltpu.VMEM((BLOCK, SGU_WIDTH), f32),
        ],
        compiler_params=pltpu.CompilerParams(
            dimension_semantics=("arbitrary", "arbitrary"),
            vmem_limit_bytes=V7X_VMEM_LIMIT_BYTES),
        name="hybrid_layer",
    )(sinks, x, norm_g, w_in, b_in, ln_g, ln_b, sgu_w, sgu_bt, w_out, b_out, final_g)


def kernel(x, norm_g, w_in, b_in, attn_sinks, sgu_ln_g, sgu_ln_b, sgu_w, sgu_b, w_out, b_out, final_norm_g):
    depth = norm_g.shape[0]
    for l in range(depth):
        last = l == depth - 1
        assert last, "the fused call applies the final norm; only depth 1 is supported"
        x = _layer_call(
            x, attn_sinks[l], norm_g[l][None, :], w_in[l], b_in[l][None, :],
            sgu_ln_g[l][None, :], sgu_ln_b[l][None, :], sgu_w[l], sgu_b[l].T,
            w_out[l], b_out[l][None, :], final_norm_g[None, :])
    return x
```

```python
import collections
import functools

import jax
import jax.numpy as jnp
from jax import lax
from jax.experimental import pallas as pl
from jax.experimental.pallas import tpu as pltpu

D_MODEL = 1024
HEAD_DIM = 64
ATTN_WIDTH = 512
KV_WIDTH = 128
SGU_WIDTH = 512
N_KV_HEADS = 2
Q_PER_KV = 4
N_SGU_HEADS = 8
BLOCK = 128
NORM_EPS = 1e-5
NEG_INF = -1e30
ATTN_SCALE = HEAD_DIM ** -0.5

OFF_Q = 0
OFF_K = OFF_Q + ATTN_WIDTH
OFF_V = OFF_K + KV_WIDTH
OFF_ZA = OFF_V + KV_WIDTH
OFF_U = OFF_ZA + ATTN_WIDTH
OFF_VS = OFF_U + SGU_WIDTH
OFF_ZS = OFF_VS + SGU_WIDTH
IN_WIDTH = OFF_ZS + SGU_WIDTH

SEQ_TILE = 1024
MXU_TILE = 256
ROW_CHUNK = 256
W_CHUNK = 128
W_SLOTS = 4
V7X_VMEM_LIMIT_BYTES = 56 * 1024 * 1024

_SQRT_TWO = 1.4142135623730951

_WeightCopy = collections.namedtuple("_WeightCopy", "hbm stage sem dst col_scale")


def _silu_of_half(hz):
    return hz * (1.0 + jnp.tanh(hz))


def _gelu_of_half(hz):
    return hz * (1.0 + lax.erf(hz * _SQRT_TWO))


def _layer_kernel(sinks_ref, x_ref, ng_ref, win_hbm, bin_ref, lng_ref, lnb_ref,
                  sw_ref, sbt_ref, wout_hbm, bout_ref, fg_ref, o_ref,
                  win_ref, wout_ref, bsc_s, stage_in, stage_out, sem_in, sem_out,
                  q_s, kd_s, vd_s, u_s, vl_s, gs_s, mix_s, wp_s, sb_s, *, seq_tile):
    f32, bf16 = jnp.float32, jnp.bfloat16
    n_sub = seq_tile // BLOCK
    b_idx = pl.program_id(0)
    s_idx = pl.program_id(1)

    lane = lax.broadcasted_iota(jnp.int32, (BLOCK, BLOCK), 1)
    low_half = lane < HEAD_DIM

    @pl.when((b_idx == 0) & (s_idx == 0))
    def _():
        row = lax.broadcasted_iota(jnp.int32, (BLOCK, BLOCK), 0)
        for p in range(N_SGU_HEADS // 2):
            for half in range(2):
                w = jnp.where(row >= lane, sw_ref[2 * p + half], 0.0)
                wp_s[p, :, half * BLOCK:(half + 1) * BLOCK] = w.astype(bf16)
        sbt = sbt_ref[...].T
        for hh in range(N_SGU_HEADS):
            sb_s[:, hh * HEAD_DIM:(hh + 1) * HEAD_DIM] = jnp.broadcast_to(
                sbt[:, hh:hh + 1], (BLOCK, HEAD_DIM))
        col = lax.broadcasted_iota(jnp.int32, (1, IN_WIDTH), 1)
        col_scale = jnp.where(col < OFF_K, ATTN_SCALE, jnp.where(col < OFF_ZA, 1.0, 0.5)).astype(f32)
        bsc_s[...] = bin_ref[...] * col_scale
        jobs = (_WeightCopy(win_hbm, stage_in, sem_in, win_ref, col_scale),
                _WeightCopy(wout_hbm, stage_out, sem_out, wout_ref, None))
        n_chunks = D_MODEL // W_CHUNK

        def chunk_copy(job, c):
            return pltpu.make_async_copy(job.hbm.at[pl.ds(c * W_CHUNK, W_CHUNK), :],
                                         job.stage.at[c % W_SLOTS], job.sem.at[c % W_SLOTS])

        for job in jobs:
            for c in range(W_SLOTS):
                chunk_copy(job, c).start(priority=c % 2)
        for job in jobs:
            for c in range(n_chunks):
                chunk_copy(job, c).wait()
                w = job.stage[c % W_SLOTS]
                if job.col_scale is not None:
                    w = w * job.col_scale
                job.dst[c * W_CHUNK:(c + 1) * W_CHUNK, :] = w.astype(bf16)
                if c + W_SLOTS < n_chunks:
                    chunk_copy(job, c + W_SLOTS).start(priority=c % 2)

    @pl.when(s_idx == 0)
    def _():
        for hh in range(N_KV_HEADS):
            kd_s[hh, 0:BLOCK, :] = jnp.zeros((BLOCK, KV_WIDTH), bf16)
            vd_s[hh, 0:BLOCK, :] = jnp.zeros((BLOCK, KV_WIDTH), bf16)

    @pl.when(s_idx > 0)
    def _():
        for hh in range(N_KV_HEADS):
            kd_s[hh, 0:BLOCK, :] = kd_s[hh, seq_tile:seq_tile + BLOCK, :]
            vd_s[hh, 0:BLOCK, :] = vd_s[hh, seq_tile:seq_tile + BLOCK, :]

    h_chunks = []
    for r0 in range(0, seq_tile, ROW_CHUNK):
        x = x_ref[0, r0:r0 + ROW_CHUNK, :]
        ms = jnp.mean(x * x, axis=-1, keepdims=True)
        hc = ((x * lax.rsqrt(ms + NORM_EPS)) * ng_ref[...]).astype(bf16)
        h_chunks.append(hc)
        q = jnp.dot(hc, win_ref[:, OFF_Q:OFF_Q + ATTN_WIDTH], preferred_element_type=f32)
        q_s[r0:r0 + ROW_CHUNK, :] = (q + bsc_s[:, OFF_Q:OFF_Q + ATTN_WIDTH]).astype(bf16)
    h = jnp.concatenate(h_chunks, axis=0)
    half_t = seq_tile // 2

    def proj_tile(off):
        cols = slice(off, off + MXU_TILE)
        return jnp.concatenate(
            [jnp.dot(h[i * half_t:(i + 1) * half_t], win_ref[:, cols], preferred_element_type=f32)
             for i in range(2)], axis=0) + bsc_s[:, cols]

    kv = proj_tile(OFF_K)
    half_mask = lax.broadcasted_iota(jnp.int32, (seq_tile, KV_WIDTH), 1) < HEAD_DIM
    for dup_ref, off in ((kd_s, 0), (vd_s, KV_WIDTH)):
        t = kv[:, off:off + KV_WIDTH]
        t_sw = pltpu.roll(t, HEAD_DIM, axis=1)
        dup_ref[0, BLOCK:BLOCK + seq_tile, :] = jnp.where(half_mask, t, t_sw).astype(bf16)
        dup_ref[1, BLOCK:BLOCK + seq_tile, :] = jnp.where(half_mask, t_sw, t).astype(bf16)

    ga_t = [_silu_of_half(proj_tile(OFF_ZA))]

    row = lax.broadcasted_iota(jnp.int32, (BLOCK, BLOCK), 0)
    prev_side = lane > row
    prev_side4 = jnp.concatenate([prev_side] * Q_PER_KV, axis=0)
    has_prev = s_idx > 0
    units = [(j, hh) for j in range(n_sub) for hh in range(N_KV_HEADS)]
    scores = []
    for j, hh in units:
        rows = slice(j * BLOCK, (j + 1) * BLOCK)
        parts = []
        for g in range(Q_PER_KV):
            c = hh * (Q_PER_KV // 2) + g // 2
            qc = q_s[rows, c * BLOCK:(c + 1) * BLOCK]
            keep = low_half if g % 2 == 0 else jnp.logical_not(low_half)
            parts.append(jnp.where(keep, qc, jnp.zeros_like(qc)))
        qst = jnp.concatenate(parts, axis=0)
        kb = kd_s[hh, j * BLOCK:(j + 2) * BLOCK, :]
        sc = lax.dot_general(qst, kb, (((1,), (1,)), ((), ())),
                             preferred_element_type=f32)
        s_prev = sc[:, 0:BLOCK]
        if j == 0:
            s_prev = jnp.where(has_prev, s_prev, NEG_INF)
        scores.append(jnp.where(prev_side4, s_prev, sc[:, BLOCK:2 * BLOCK]))

    def attend(u):
        j, hh = units[u]
        sc = scores[u]
        probs, inv_denoms = [], []
        for g in range(Q_PER_KV):
            sg = sc[g * BLOCK:(g + 1) * BLOCK, :]
            sink = sinks_ref[hh * Q_PER_KV + g]
            m = jnp.maximum(jnp.max(sg, axis=-1, keepdims=True), sink)
            p = jnp.exp(sg - m)
            denom = jnp.sum(p, axis=-1, keepdims=True) + jnp.exp(sink - m)
            inv_denoms.append(1.0 / denom)
            pb = p.astype(bf16)
            zero = jnp.zeros_like(pb)
            probs.append(jnp.concatenate([jnp.where(prev_side, pb, zero),
                                          jnp.where(prev_side, zero, pb)], axis=1))
        probs = jnp.concatenate(probs, axis=0)
        vb = vd_s[hh, j * BLOCK:(j + 2) * BLOCK, :]
        o = jnp.dot(probs, vb, preferred_element_type=f32)
        o = [o[g * BLOCK:(g + 1) * BLOCK, :] * inv_denoms[g] for g in range(Q_PER_KV)]
        return [jnp.where(low_half, o[2 * c2], o[2 * c2 + 1]) for c2 in range(Q_PER_KV // 2)]

    attn_out = []
    n_slots = (IN_WIDTH - OFF_ZA) // MXU_TILE - 1
    units_after = [len(units) * (i + 1) // n_slots - len(units) * i // n_slots for i in range(n_slots)]
    slot = iter(units_after)

    def attend_some():
        for _ in range(next(slot)):
            attn_out.append(attend(len(attn_out)))

    for n in range(1, ATTN_WIDTH // MXU_TILE):
        ga_t.append(_silu_of_half(proj_tile(OFF_ZA + n * MXU_TILE)))
        attend_some()
    for n in range(SGU_WIDTH // MXU_TILE):
        u_s[:, n * MXU_TILE:(n + 1) * MXU_TILE] = _gelu_of_half(proj_tile(OFF_U + n * MXU_TILE))
        attend_some()
    vg = []
    for n in range(SGU_WIDTH // MXU_TILE):
        vg.append(_gelu_of_half(proj_tile(OFF_VS + n * MXU_TILE)))
        attend_some()
    vg = jnp.concatenate(vg, axis=1)
    mu = jnp.mean(vg, axis=-1, keepdims=True)
    vc = vg - mu
    var = jnp.mean(vc * vc, axis=-1, keepdims=True)
    vl_s[...] = ((vc * lax.rsqrt(var + NORM_EPS)) * lng_ref[...] + lnb_ref[...]).astype(bf16)
    for n in range(SGU_WIDTH // MXU_TILE):
        gs_s[:, n * MXU_TILE:(n + 1) * MXU_TILE] = _silu_of_half(proj_tile(OFF_ZS + n * MXU_TILE))
        attend_some()
    assert len(attn_out) == len(units)

    ga = jnp.concatenate(ga_t, axis=1)
    for u, (j, hh) in enumerate(units):
        rows = slice(j * BLOCK, (j + 1) * BLOCK)
        for c2 in range(Q_PER_KV // 2):
            cols = slice((hh * (Q_PER_KV // 2) + c2) * BLOCK, (hh * (Q_PER_KV // 2) + c2 + 1) * BLOCK)
            mix_s[rows, cols] = (attn_out[u][c2] * ga[rows, cols]).astype(bf16)

    for j in range(n_sub):
        rows = slice(j * BLOCK, (j + 1) * BLOCK)
        for p_idx in range(N_SGU_HEADS // 2):
            cols = slice(p_idx * BLOCK, (p_idx + 1) * BLOCK)
            vp = vl_s[rows, cols]
            zero = jnp.zeros_like(vp)
            rhs = jnp.concatenate([jnp.where(low_half, vp, zero),
                                   jnp.where(low_half, zero, vp)], axis=0)
            mixed = jnp.dot(wp_s[p_idx], rhs, preferred_element_type=f32) + sb_s[:, cols]
            mix_s[rows, ATTN_WIDTH + p_idx * BLOCK:ATTN_WIDTH + (p_idx + 1) * BLOCK] = (
                (u_s[rows, cols] * mixed) * gs_s[rows, cols]).astype(bf16)

    y = x_ref[0] + jnp.dot(mix_s[...], wout_ref[...], preferred_element_type=f32) + bout_ref[...]
    ms2 = jnp.mean(y * y, axis=-1, keepdims=True)
    o_ref[0] = (y * lax.rsqrt(ms2 + NORM_EPS)) * fg_ref[...]


def _layer_call(x, sinks, norm_g, w_in, b_in, ln_g, ln_b, sgu_w, sgu_bt, w_out, b_out, final_g,
                *, seq_tile=SEQ_TILE):
    batch, seq, d_model = x.shape
    assert d_model == D_MODEL and seq % seq_tile == 0 and seq_tile % (2 * BLOCK) == 0
    f32, bf16 = jnp.float32, jnp.bfloat16

    def full(shape):
        return pl.BlockSpec(shape, lambda b, s: (0,) * len(shape))

    tile_spec = pl.BlockSpec((1, seq_tile, D_MODEL), lambda b, s: (b, s, 0))
    return pl.pallas_call(
        functools.partial(_layer_kernel, seq_tile=seq_tile),
        grid=(batch, seq // seq_tile),
        in_specs=[
            pl.BlockSpec(memory_space=pltpu.SMEM),
            tile_spec,
            full((1, D_MODEL)),
            pl.BlockSpec(memory_space=pl.ANY),
            full((1, IN_WIDTH)),
            full((1, SGU_WIDTH)),
            full((1, SGU_WIDTH)),
            full((N_SGU_HEADS, BLOCK, BLOCK)),
            full((N_SGU_HEADS, BLOCK)),
            pl.BlockSpec(memory_space=pl.ANY),
            full((1, D_MODEL)),
            full((1, D_MODEL)),
        ],
        out_specs=tile_spec,
        out_shape=jax.ShapeDtypeStruct(x.shape, x.dtype),
        scratch_shapes=[
            pltpu.VMEM((D_MODEL, IN_WIDTH), bf16),
            pltpu.VMEM((D_MODEL, D_MODEL), bf16),
            pltpu.VMEM((1, IN_WIDTH), f32),
            pltpu.VMEM((W_SLOTS, W_CHUNK, IN_WIDTH), f32),
            pltpu.VMEM((W_SLOTS, W_CHUNK, D_MODEL), f32),
            pltpu.SemaphoreType.DMA((W_SLOTS,)),
            pltpu.SemaphoreType.DMA((W_SLOTS,)),
            pltpu.VMEM((seq_tile, ATTN_WIDTH), bf16),
            pltpu.VMEM((N_KV_HEADS, BLOCK + seq_tile, KV_WIDTH), bf16),
            pltpu.VMEM((N_KV_HEADS, BLOCK + seq_tile, KV_WIDTH), bf16),
            pltpu.VMEM((seq_tile, SGU_WIDTH), f32),
            pltpu.VMEM((seq_tile, SGU_WIDTH), bf16),
            pltpu.VMEM((seq_tile, SGU_WIDTH), f32),
            pltpu.VMEM((seq_tile, D_MODEL), bf16),
            pltpu.VMEM((N_SGU_HEADS // 2, BLOCK, 2 * BLOCK), bf16),
            pltpu.VMEM((BLOCK, SGU_WIDTH), f32),
        ],
        compiler_params=pltpu.CompilerParams(
            dimension_semantics=("arbitrary", "arbitrary"),
            vmem_limit_bytes=V7X_VMEM_LIMIT_BYTES),
        name="hybrid_layer",
    )(sinks, x, norm_g, w_in, b_in, ln_g, ln_b, sgu_w, sgu_bt, w_out, b_out, final_g)


def kernel(x, norm_g, w_in, b_in, attn_sinks, sgu_ln_g, sgu_ln_b, sgu_w, sgu_b, w_out, b_out, final_norm_g):
    depth = norm_g.shape[0]
    for l in range(depth):
        last = l == depth - 1
        assert last, "the fused call applies the final norm; only depth 1 is supported"
        x = _layer_call(
            x, attn_sinks[l], norm_g[l][None, :], w_in[l], b_in[l][None, :],
            sgu_ln_g[l][None, :], sgu_ln_b[l][None, :], sgu_w[l], sgu_b[l],
            w_out[l], b_out[l][None, :], final_norm_g[None, :])
    return x
```

```python
import collections
import functools

import jax
import jax.numpy as jnp
from jax import lax
from jax.experimental import pallas as pl
from jax.experimental.pallas import tpu as pltpu

D_MODEL = 1024
HEAD_DIM = 64
ATTN_WIDTH = 512
KV_WIDTH = 128
SGU_WIDTH = 512
N_KV_HEADS = 2
Q_PER_KV = 4
N_SGU_HEADS = 8
BLOCK = 128
NORM_EPS = 1e-5
NEG_INF = -1e30
ATTN_SCALE = HEAD_DIM ** -0.5

OFF_Q = 0
OFF_K = OFF_Q + ATTN_WIDTH
OFF_V = OFF_K + KV_WIDTH
OFF_ZA = OFF_V + KV_WIDTH
OFF_U = OFF_ZA + ATTN_WIDTH
OFF_VS = OFF_U + SGU_WIDTH
OFF_ZS = OFF_VS + SGU_WIDTH
IN_WIDTH = OFF_ZS + SGU_WIDTH

SEQ_TILE = 1024
MXU_TILE = 256
ROW_CHUNK = 256
W_CHUNK = 128
W_SLOTS = 4
V7X_VMEM_LIMIT_BYTES = 56 * 1024 * 1024

_SQRT_TWO = 1.4142135623730951

_WeightCopy = collections.namedtuple("_WeightCopy", "hbm stage sem dst col_scale")


def _silu_of_half(hz):
    return hz * (1.0 + jnp.tanh(hz))


def _gelu_of_half(hz):
    return hz * (1.0 + lax.erf(hz * _SQRT_TWO))


def _layer_kernel(sinks_ref, x_ref, ng_ref, win_hbm, bin_ref, lng_ref, lnb_ref,
                  sw_ref, sb_ref, wout_hbm, bout_ref, fg_ref, o_ref,
                  win_ref, wout_ref, bsc_s, stage_in, stage_out, sem_in, sem_out,
                  q_s, kd_s, vd_s, u_s, vl_s, gs_s, mix_s, wp_s, sb_s, *, seq_tile):
    f32, bf16 = jnp.float32, jnp.bfloat16
    n_sub = seq_tile // BLOCK
    b_idx = pl.program_id(0)
    s_idx = pl.program_id(1)

    lane = lax.broadcasted_iota(jnp.int32, (BLOCK, BLOCK), 1)
    low_half = lane < HEAD_DIM

    @pl.when((b_idx == 0) & (s_idx == 0))
    def _():
        row = lax.broadcasted_iota(jnp.int32, (BLOCK, BLOCK), 0)
        for p in range(N_SGU_HEADS // 2):
            for half in range(2):
                w = jnp.where(row >= lane, sw_ref[2 * p + half], 0.0)
                wp_s[p, :, half * BLOCK:(half + 1) * BLOCK] = w.astype(bf16)
        sbt = sb_ref[...].T
        for hh in range(N_SGU_HEADS):
            sb_s[:, hh * HEAD_DIM:(hh + 1) * HEAD_DIM] = jnp.broadcast_to(
                sbt[:, hh:hh + 1], (BLOCK, HEAD_DIM))
        col = lax.broadcasted_iota(jnp.int32, (1, IN_WIDTH), 1)
        col_scale = jnp.where(col < OFF_K, ATTN_SCALE, jnp.where(col < OFF_ZA, 1.0, 0.5)).astype(f32)
        bsc_s[...] = bin_ref[...] * col_scale
        jobs = (_WeightCopy(win_hbm, stage_in, sem_in, win_ref, col_scale),
                _WeightCopy(wout_hbm, stage_out, sem_out, wout_ref, None))
        n_chunks = D_MODEL // W_CHUNK

        def chunk_copy(job, c):
            return pltpu.make_async_copy(job.hbm.at[pl.ds(c * W_CHUNK, W_CHUNK), :],
                                         job.stage.at[c % W_SLOTS], job.sem.at[c % W_SLOTS])

        for job in jobs:
            for c in range(W_SLOTS):
                chunk_copy(job, c).start(priority=c % 2)
        for job in jobs:
            for c in range(n_chunks):
                chunk_copy(job, c).wait()
                w = job.stage[c % W_SLOTS]
                if job.col_scale is not None:
                    w = w * job.col_scale
                job.dst[c * W_CHUNK:(c + 1) * W_CHUNK, :] = w.astype(bf16)
                if c + W_SLOTS < n_chunks:
                    chunk_copy(job, c + W_SLOTS).start(priority=c % 2)

    @pl.when(s_idx == 0)
    def _():
        for hh in range(N_KV_HEADS):
            kd_s[hh, 0:BLOCK, :] = jnp.zeros((BLOCK, KV_WIDTH), bf16)
            vd_s[hh, 0:BLOCK, :] = jnp.zeros((BLOCK, KV_WIDTH), bf16)

    @pl.when(s_idx > 0)
    def _():
        for hh in range(N_KV_HEADS):
            kd_s[hh, 0:BLOCK, :] = kd_s[hh, seq_tile:seq_tile + BLOCK, :]
            vd_s[hh, 0:BLOCK, :] = vd_s[hh, seq_tile:seq_tile + BLOCK, :]

    h_chunks = []
    for r0 in range(0, seq_tile, ROW_CHUNK):
        x = x_ref[0, r0:r0 + ROW_CHUNK, :]
        ms = jnp.mean(x * x, axis=-1, keepdims=True)
        hc = ((x * lax.rsqrt(ms + NORM_EPS)) * ng_ref[...]).astype(bf16)
        h_chunks.append(hc)
        q = jnp.dot(hc, win_ref[:, OFF_Q:OFF_Q + ATTN_WIDTH], preferred_element_type=f32)
        q_s[r0:r0 + ROW_CHUNK, :] = (q + bsc_s[:, OFF_Q:OFF_Q + ATTN_WIDTH]).astype(bf16)
    h = jnp.concatenate(h_chunks, axis=0)
    half_t = seq_tile // 2

    def proj_tile(off):
        cols = slice(off, off + MXU_TILE)
        return jnp.concatenate(
            [jnp.dot(h[i * half_t:(i + 1) * half_t], win_ref[:, cols], preferred_element_type=f32)
             for i in range(2)], axis=0) + bsc_s[:, cols]

    kv = proj_tile(OFF_K)
    half_mask = lax.broadcasted_iota(jnp.int32, (seq_tile, KV_WIDTH), 1) < HEAD_DIM
    for dup_ref, off in ((kd_s, 0), (vd_s, KV_WIDTH)):
        t = kv[:, off:off + KV_WIDTH]
        t_sw = pltpu.roll(t, HEAD_DIM, axis=1)
        dup_ref[0, BLOCK:BLOCK + seq_tile, :] = jnp.where(half_mask, t, t_sw).astype(bf16)
        dup_ref[1, BLOCK:BLOCK + seq_tile, :] = jnp.where(half_mask, t_sw, t).astype(bf16)

    ga_t = [_silu_of_half(proj_tile(OFF_ZA))]

    row = lax.broadcasted_iota(jnp.int32, (BLOCK, BLOCK), 0)
    prev_side = lane > row
    prev_side4 = jnp.concatenate([prev_side] * Q_PER_KV, axis=0)
    has_prev = s_idx > 0
    units = [(j, hh) for j in range(n_sub) for hh in range(N_KV_HEADS)]
    scores = []
    for j, hh in units:
        rows = slice(j * BLOCK, (j + 1) * BLOCK)
        parts = []
        for g in range(Q_PER_KV):
            c = hh * (Q_PER_KV // 2) + g // 2
            qc = q_s[rows, c * BLOCK:(c + 1) * BLOCK]
            keep = low_half if g % 2 == 0 else jnp.logical_not(low_half)
            parts.append(jnp.where(keep, qc, jnp.zeros_like(qc)))
        qst = jnp.concatenate(parts, axis=0)
        kb = kd_s[hh, j * BLOCK:(j + 2) * BLOCK, :]
        sc = lax.dot_general(qst, kb, (((1,), (1,)), ((), ())),
                             preferred_element_type=f32)
        s_prev = sc[:, 0:BLOCK]
        if j == 0:
            s_prev = jnp.where(has_prev, s_prev, NEG_INF)
        scores.append(jnp.where(prev_side4, s_prev, sc[:, BLOCK:2 * BLOCK]))

    def attend(u):
        j, hh = units[u]
        sc = scores[u]
        probs, inv_denoms = [], []
        for g in range(Q_PER_KV):
            sg = sc[g * BLOCK:(g + 1) * BLOCK, :]
            sink = sinks_ref[hh * Q_PER_KV + g]
            m = jnp.maximum(jnp.max(sg, axis=-1, keepdims=True), sink)
            p = jnp.exp(sg - m)
            denom = jnp.sum(p, axis=-1, keepdims=True) + jnp.exp(sink - m)
            inv_denoms.append(1.0 / denom)
            pb = p.astype(bf16)
            zero = jnp.zeros_like(pb)
            probs.append(jnp.concatenate([jnp.where(prev_side, pb, zero),
                                          jnp.where(prev_side, zero, pb)], axis=1))
        probs = jnp.concatenate(probs, axis=0)
        vb = vd_s[hh, j * BLOCK:(j + 2) * BLOCK, :]
        o = jnp.dot(probs, vb, preferred_element_type=f32)
        o = [o[g * BLOCK:(g + 1) * BLOCK, :] * inv_denoms[g] for g in range(Q_PER_KV)]
        return [jnp.where(low_half, o[2 * c2], o[2 * c2 + 1]) for c2 in range(Q_PER_KV // 2)]

    attn_out = []
    n_slots = (IN_WIDTH - OFF_ZA) // MXU_TILE - 1
    units_after = [len(units) * (i + 1) // n_slots - len(units) * i // n_slots for i in range(n_slots)]
    slot = iter(units_after)

    def attend_some():
        for _ in range(next(slot)):
            attn_out.append(attend(len(attn_out)))

    for n in range(1, ATTN_WIDTH // MXU_TILE):
        ga_t.append(_silu_of_half(proj_tile(OFF_ZA + n * MXU_TILE)))
        attend_some()
    for n in range(SGU_WIDTH // MXU_TILE):
        u_s[:, n * MXU_TILE:(n + 1) * MXU_TILE] = _gelu_of_half(proj_tile(OFF_U + n * MXU_TILE))
        attend_some()
    vg = []
    for n in range(SGU_WIDTH // MXU_TILE):
        vg.append(_gelu_of_half(proj_tile(OFF_VS + n * MXU_TILE)))
        attend_some()
    vg = jnp.concatenate(vg, axis=1)
    mu = jnp.mean(vg, axis=-1, keepdims=True)
    vc = vg - mu
    var = jnp.mean(vc * vc, axis=-1, keepdims=True)
    vl_s[...] = ((vc * lax.rsqrt(var + NORM_EPS)) * lng_ref[...] + lnb_ref[...]).astype(bf16)
    for n in range(SGU_WIDTH // MXU_TILE):
        gs_s[:, n * MXU_TILE:(n + 1) * MXU_TILE] = _silu_of_half(proj_tile(OFF_ZS + n * MXU_TILE))
        attend_some()
    assert len(attn_out) == len(units)

    ga = jnp.concatenate(ga_t, axis=1)
    for u, (j, hh) in enumerate(units):
        rows = slice(j * BLOCK, (j + 1) * BLOCK)
        for c2 in range(Q_PER_KV // 2):
            cols = slice((hh * (Q_PER_KV // 2) + c2) * BLOCK, (hh * (Q_PER_KV // 2) + c2 + 1) * BLOCK)
            mix_s[rows, cols] = (attn_out[u][c2] * ga[rows, cols]).astype(bf16)

    for j in range(n_sub):
        rows = slice(j * BLOCK, (j + 1) * BLOCK)
        for p_idx in range(N_SGU_HEADS // 2):
            cols = slice(p_idx * BLOCK, (p_idx + 1) * BLOCK)
            vp = vl_s[rows, cols]
            zero = jnp.zeros_like(vp)
            rhs = jnp.concatenate([jnp.where(low_half, vp, zero),
                                   jnp.where(low_half, zero, vp)], axis=0)
            mixed = jnp.dot(wp_s[p_idx], rhs, preferred_element_type=f32) + sb_s[:, cols]
            mix_s[rows, ATTN_WIDTH + p_idx * BLOCK:ATTN_WIDTH + (p_idx + 1) * BLOCK] = (
                (u_s[rows, cols] * mixed) * gs_s[rows, cols]).astype(bf16)

    y = x_ref[0] + jnp.dot(mix_s[...], wout_ref[...], preferred_element_type=f32) + bout_ref[...]
    ms2 = jnp.mean(y * y, axis=-1, keepdims=True)
    o_ref[0] = (y * lax.rsqrt(ms2 + NORM_EPS)) * fg_ref[...]


def _layer_call(x, sinks, norm_g, w_in, b_in, ln_g, ln_b, sgu_w, sgu_b, w_out, b_out, final_g,
                *, seq_tile=SEQ_TILE):
    batch, seq, d_model = x.shape
    assert d_model == D_MODEL and seq % seq_tile == 0 and seq_tile % (2 * BLOCK) == 0
    f32, bf16 = jnp.float32, jnp.bfloat16

    def full(shape):
        return pl.BlockSpec(shape, lambda b, s: (0,) * len(shape))

    tile_spec = pl.BlockSpec((1, seq_tile, D_MODEL), lambda b, s: (b, s, 0))
    return pl.pallas_call(
        functools.partial(_layer_kernel, seq_tile=seq_tile),
        grid=(batch, seq // seq_tile),
        in_specs=[
            pl.BlockSpec(memory_space=pltpu.SMEM),
            tile_spec,
            full((1, D_MODEL)),
            pl.BlockSpec(memory_space=pl.ANY),
            full((1, IN_WIDTH)),
            full((1, SGU_WIDTH)),
            full((1, SGU_WIDTH)),
            full((N_SGU_HEADS, BLOCK, BLOCK)),
            full((N_SGU_HEADS, BLOCK)),
            pl.BlockSpec(memory_space=pl.ANY),
            full((1, D_MODEL)),
            full((1, D_MODEL)),
        ],
        out_specs=tile_spec,
        out_shape=jax.ShapeDtypeStruct(x.shape, x.dtype),
        scratch_shapes=[
            pltpu.VMEM((D_MODEL, IN_WIDTH), bf16),
            pltpu.VMEM((D_MODEL, D_MODEL), bf16),
            pltpu.VMEM((1, IN_WIDTH), f32),
            pltpu.VMEM((W_SLOTS, W_CHUNK, IN_WIDTH), f32),
            pltpu.VMEM((W_SLOTS, W_CHUNK, D_MODEL), f32),
            pltpu.SemaphoreType.DMA((W_SLOTS,)),
            pltpu.SemaphoreType.DMA((W_SLOTS,)),
            pltpu.VMEM((seq_tile, ATTN_WIDTH), bf16),
            pltpu.VMEM((N_KV_HEADS, BLOCK + seq_tile, KV_WIDTH), bf16),
            pltpu.VMEM((N_KV_HEADS, BLOCK + seq_tile, KV_WIDTH), bf16),
            pltpu.VMEM((seq_tile, SGU_WIDTH), f32),
            pltpu.VMEM((seq_tile, SGU_WIDTH), bf16),
            pltpu.VMEM((seq_tile, SGU_WIDTH), f32),
            pltpu.VMEM((seq_tile, D_MODEL), bf16),
            pltpu.VMEM((N_SGU_HEADS // 2, BLOCK, 2 * BLOCK), bf16),
            pltpu.VMEM((BLOCK, SGU_WIDTH), f32),
        ],
        compiler_params=pltpu.CompilerParams(
            dimension_semantics=("arbitrary", "arbitrary"),
            vmem_limit_bytes=V7X_VMEM_LIMIT_BYTES),
        name="hybrid_layer",
    )(sinks, x, norm_g, w_in, b_in, ln_g, ln_b, sgu_w, sgu_b, w_out, b_out, final_g)


def kernel(x, norm_g, w_in, b_in, attn_sinks, sgu_ln_g, sgu_ln_b, sgu_w, sgu_b, w_out, b_out, final_norm_g):
    depth = norm_g.shape[0]
    for l in range(depth):
        last = l == depth - 1
        assert last, "the fused call applies the final norm; only depth 1 is supported"
        x = _layer_call(
            x, attn_sinks[l], norm_g[l][None, :], w_in[l], b_in[l][None, :],
            sgu_ln_g[l][None, :], sgu_ln_b[l][None, :], sgu_w[l], sgu_b[l],
            w_out[l], b_out[l][None, :], final_norm_g[None, :])
    return x
```

```python
import collections
import functools

import jax
import jax.numpy as jnp
from jax import lax
from jax.experimental import pallas as pl
from jax.experimental.pallas import tpu as pltpu

D_MODEL = 1024
HEAD_DIM = 64
ATTN_WIDTH = 512
KV_WIDTH = 128
SGU_WIDTH = 512
N_KV_HEADS = 2
Q_PER_KV = 4
N_SGU_HEADS = 8
BLOCK = 128
NORM_EPS = 1e-5
NEG_INF = -1e30
ATTN_SCALE = HEAD_DIM ** -0.5

OFF_Q = 0
OFF_K = OFF_Q + ATTN_WIDTH
OFF_V = OFF_K + KV_WIDTH
OFF_ZA = OFF_V + KV_WIDTH
OFF_U = OFF_ZA + ATTN_WIDTH
OFF_VS = OFF_U + SGU_WIDTH
OFF_ZS = OFF_VS + SGU_WIDTH
IN_WIDTH = OFF_ZS + SGU_WIDTH

SEQ_TILE = 1024
MXU_TILE = 256
ROW_CHUNK = 256
BIAS_LANES = 4096
W_CHUNK = 128
W_SLOTS = 4
V7X_VMEM_LIMIT_BYTES = 56 * 1024 * 1024

_SQRT_TWO = 1.4142135623730951

_WeightCopy = collections.namedtuple("_WeightCopy", "hbm stage sem dst col_scale")


def _silu_of_half(hz):
    return hz * (1.0 + jnp.tanh(hz))


def _gelu_of_half(hz):
    return hz * (1.0 + lax.erf(hz * _SQRT_TWO))


def _layer_kernel(sinks_ref, x_ref, ng_ref, win_hbm, bin_ref, lng_ref, lnb_ref,
                  sw_ref, sb_ref, wout_hbm, bout_ref, fg_ref, o_ref,
                  win_ref, wout_ref, bsc_s, stage_in, stage_out, sem_in, sem_out,
                  q_s, kd_s, vd_s, u_s, vl_s, gs_s, mix_s, wp_s, sb_s, *, seq_tile):
    f32, bf16 = jnp.float32, jnp.bfloat16
    n_sub = seq_tile // BLOCK
    b_idx = pl.program_id(0)
    s_idx = pl.program_id(1)

    lane = lax.broadcasted_iota(jnp.int32, (BLOCK, BLOCK), 1)
    low_half = lane < HEAD_DIM

    @pl.when((b_idx == 0) & (s_idx == 0))
    def _():
        row = lax.broadcasted_iota(jnp.int32, (BLOCK, BLOCK), 0)
        for p in range(N_SGU_HEADS // 2):
            for half in range(2):
                w = jnp.where(row >= lane, sw_ref[2 * p + half], 0.0)
                wp_s[p, :, half * BLOCK:(half + 1) * BLOCK] = w.astype(bf16)
        sbt = sb_ref[...].T
        for hh in range(N_SGU_HEADS):
            sb_s[:, hh * HEAD_DIM:(hh + 1) * HEAD_DIM] = jnp.broadcast_to(
                sbt[:, hh:hh + 1], (BLOCK, HEAD_DIM))
        col = lax.broadcasted_iota(jnp.int32, (1, IN_WIDTH), 1)
        col_scale = jnp.where(col < OFF_K, ATTN_SCALE, jnp.where(col < OFF_ZA, 1.0, 0.5)).astype(f32)
        bsc_s[:, 0:IN_WIDTH] = bin_ref[...] * col_scale
        jobs = (_WeightCopy(win_hbm, stage_in, sem_in, win_ref, col_scale),
                _WeightCopy(wout_hbm, stage_out, sem_out, wout_ref, None))
        n_chunks = D_MODEL // W_CHUNK

        def chunk_copy(job, c):
            return pltpu.make_async_copy(job.hbm.at[pl.ds(c * W_CHUNK, W_CHUNK), :],
                                         job.stage.at[c % W_SLOTS], job.sem.at[c % W_SLOTS])

        for job in jobs:
            for c in range(W_SLOTS):
                chunk_copy(job, c).start(priority=c % 2)
        for job in jobs:
            for c in range(n_chunks):
                chunk_copy(job, c).wait()
                w = job.stage[c % W_SLOTS]
                if job.col_scale is not None:
                    w = w * job.col_scale
                job.dst[c * W_CHUNK:(c + 1) * W_CHUNK, :] = w.astype(bf16)
                if c + W_SLOTS < n_chunks:
                    chunk_copy(job, c + W_SLOTS).start(priority=c % 2)

    @pl.when(s_idx == 0)
    def _():
        for hh in range(N_KV_HEADS):
            kd_s[hh, 0:BLOCK, :] = jnp.zeros((BLOCK, KV_WIDTH), bf16)
            vd_s[hh, 0:BLOCK, :] = jnp.zeros((BLOCK, KV_WIDTH), bf16)

    @pl.when(s_idx > 0)
    def _():
        for hh in range(N_KV_HEADS):
            kd_s[hh, 0:BLOCK, :] = kd_s[hh, seq_tile:seq_tile + BLOCK, :]
            vd_s[hh, 0:BLOCK, :] = vd_s[hh, seq_tile:seq_tile + BLOCK, :]

    h_chunks = []
    for r0 in range(0, seq_tile, ROW_CHUNK):
        x = x_ref[0, r0:r0 + ROW_CHUNK, :]
        ms = jnp.mean(x * x, axis=-1, keepdims=True)
        hc = ((x * lax.rsqrt(ms + NORM_EPS)) * ng_ref[...]).astype(bf16)
        h_chunks.append(hc)
        q = jnp.dot(hc, win_ref[:, OFF_Q:OFF_Q + ATTN_WIDTH], preferred_element_type=f32)
        q_s[r0:r0 + ROW_CHUNK, :] = (q + bsc_s[:, OFF_Q:OFF_Q + ATTN_WIDTH]).astype(bf16)
    h = jnp.concatenate(h_chunks, axis=0)
    half_t = seq_tile // 2

    def proj_tile(off):
        cols = slice(off, off + MXU_TILE)
        return jnp.concatenate(
            [jnp.dot(h[i * half_t:(i + 1) * half_t], win_ref[:, cols], preferred_element_type=f32)
             for i in range(2)], axis=0) + bsc_s[:, cols]

    kv = proj_tile(OFF_K)
    half_mask = lax.broadcasted_iota(jnp.int32, (seq_tile, KV_WIDTH), 1) < HEAD_DIM
    for dup_ref, off in ((kd_s, 0), (vd_s, KV_WIDTH)):
        t = kv[:, off:off + KV_WIDTH]
        t_sw = pltpu.roll(t, HEAD_DIM, axis=1)
        dup_ref[0, BLOCK:BLOCK + seq_tile, :] = jnp.where(half_mask, t, t_sw).astype(bf16)
        dup_ref[1, BLOCK:BLOCK + seq_tile, :] = jnp.where(half_mask, t_sw, t).astype(bf16)

    ga_t = [_silu_of_half(proj_tile(OFF_ZA))]

    row = lax.broadcasted_iota(jnp.int32, (BLOCK, BLOCK), 0)
    prev_side = lane > row
    prev_side4 = jnp.concatenate([prev_side] * Q_PER_KV, axis=0)
    has_prev = s_idx > 0
    units = [(j, hh) for j in range(n_sub) for hh in range(N_KV_HEADS)]
    scores = []
    for j, hh in units:
        rows = slice(j * BLOCK, (j + 1) * BLOCK)
        parts = []
        for g in range(Q_PER_KV):
            c = hh * (Q_PER_KV // 2) + g // 2
            qc = q_s[rows, c * BLOCK:(c + 1) * BLOCK]
            keep = low_half if g % 2 == 0 else jnp.logical_not(low_half)
            parts.append(jnp.where(keep, qc, jnp.zeros_like(qc)))
        qst = jnp.concatenate(parts, axis=0)
        kb = kd_s[hh, j * BLOCK:(j + 2) * BLOCK, :]
        sc = lax.dot_general(qst, kb, (((1,), (1,)), ((), ())),
                             preferred_element_type=f32)
        s_prev = sc[:, 0:BLOCK]
        if j == 0:
            s_prev = jnp.where(has_prev, s_prev, NEG_INF)
        scores.append(jnp.where(prev_side4, s_prev, sc[:, BLOCK:2 * BLOCK]))

    def attend(u):
        j, hh = units[u]
        sc = scores[u]
        probs, inv_denoms = [], []
        for g in range(Q_PER_KV):
            sg = sc[g * BLOCK:(g + 1) * BLOCK, :]
            sink = sinks_ref[hh * Q_PER_KV + g]
            m = jnp.maximum(jnp.max(sg, axis=-1, keepdims=True), sink)
            p = jnp.exp(sg - m)
            denom = jnp.sum(p, axis=-1, keepdims=True) + jnp.exp(sink - m)
            inv_denoms.append(1.0 / denom)
            pb = p.astype(bf16)
            zero = jnp.zeros_like(pb)
            probs.append(jnp.concatenate([jnp.where(prev_side, pb, zero),
                                          jnp.where(prev_side, zero, pb)], axis=1))
        probs = jnp.concatenate(probs, axis=0)
        vb = vd_s[hh, j * BLOCK:(j + 2) * BLOCK, :]
        o = jnp.dot(probs, vb, preferred_element_type=f32)
        o = [o[g * BLOCK:(g + 1) * BLOCK, :] * inv_denoms[g] for g in range(Q_PER_KV)]
        return [jnp.where(low_half, o[2 * c2], o[2 * c2 + 1]) for c2 in range(Q_PER_KV // 2)]

    attn_out = []
    n_slots = (IN_WIDTH - OFF_ZA) // MXU_TILE - 1
    units_after = [len(units) * (i + 1) // n_slots - len(units) * i // n_slots for i in range(n_slots)]
    slot = iter(units_after)

    def attend_some():
        for _ in range(next(slot)):
            attn_out.append(attend(len(attn_out)))

    for n in range(1, ATTN_WIDTH // MXU_TILE):
        ga_t.append(_silu_of_half(proj_tile(OFF_ZA + n * MXU_TILE)))
        attend_some()
    for n in range(SGU_WIDTH // MXU_TILE):
        u_s[:, n * MXU_TILE:(n + 1) * MXU_TILE] = _gelu_of_half(proj_tile(OFF_U + n * MXU_TILE))
        attend_some()
    vg = []
    for n in range(SGU_WIDTH // MXU_TILE):
        vg.append(_gelu_of_half(proj_tile(OFF_VS + n * MXU_TILE)))
        attend_some()
    vg = jnp.concatenate(vg, axis=1)
    mu = jnp.mean(vg, axis=-1, keepdims=True)
    vc = vg - mu
    var = jnp.mean(vc * vc, axis=-1, keepdims=True)
    vl_s[...] = ((vc * lax.rsqrt(var + NORM_EPS)) * lng_ref[...] + lnb_ref[...]).astype(bf16)
    for n in range(SGU_WIDTH // MXU_TILE):
        gs_s[:, n * MXU_TILE:(n + 1) * MXU_TILE] = _silu_of_half(proj_tile(OFF_ZS + n * MXU_TILE))
        attend_some()
    assert len(attn_out) == len(units)

    ga = jnp.concatenate(ga_t, axis=1)
    for u, (j, hh) in enumerate(units):
        rows = slice(j * BLOCK, (j + 1) * BLOCK)
        for c2 in range(Q_PER_KV // 2):
            cols = slice((hh * (Q_PER_KV // 2) + c2) * BLOCK, (hh * (Q_PER_KV // 2) + c2 + 1) * BLOCK)
            mix_s[rows, cols] = (attn_out[u][c2] * ga[rows, cols]).astype(bf16)

    for j in range(n_sub):
        rows = slice(j * BLOCK, (j + 1) * BLOCK)
        for p_idx in range(N_SGU_HEADS // 2):
            cols = slice(p_idx * BLOCK, (p_idx + 1) * BLOCK)
            vp = vl_s[rows, cols]
            zero = jnp.zeros_like(vp)
            rhs = jnp.concatenate([jnp.where(low_half, vp, zero),
                                   jnp.where(low_half, zero, vp)], axis=0)
            mixed = jnp.dot(wp_s[p_idx], rhs, preferred_element_type=f32) + sb_s[:, cols]
            mix_s[rows, ATTN_WIDTH + p_idx * BLOCK:ATTN_WIDTH + (p_idx + 1) * BLOCK] = (
                (u_s[rows, cols] * mixed) * gs_s[rows, cols]).astype(bf16)

    y = x_ref[0] + jnp.dot(mix_s[...], wout_ref[...], preferred_element_type=f32) + bout_ref[...]
    ms2 = jnp.mean(y * y, axis=-1, keepdims=True)
    o_ref[0] = (y * lax.rsqrt(ms2 + NORM_EPS)) * fg_ref[...]


def _layer_call(x, sinks, norm_g, w_in, b_in, ln_g, ln_b, sgu_w, sgu_b, w_out, b_out, final_g,
                *, seq_tile=SEQ_TILE):
    batch, seq, d_model = x.shape
    assert d_model == D_MODEL and seq % seq_tile == 0 and seq_tile % (2 * BLOCK) == 0
    f32, bf16 = jnp.float32, jnp.bfloat16

    def full(shape):
        return pl.BlockSpec(shape, lambda b, s: (0,) * len(shape))

    tile_spec = pl.BlockSpec((1, seq_tile, D_MODEL), lambda b, s: (b, s, 0))
    return pl.pallas_call(
        functools.partial(_layer_kernel, seq_tile=seq_tile),
        grid=(batch, seq // seq_tile),
        in_specs=[
            pl.BlockSpec(memory_space=pltpu.SMEM),
            tile_spec,
            full((1, D_MODEL)),
            pl.BlockSpec(memory_space=pl.ANY),
            full((1, IN_WIDTH)),
            full((1, SGU_WIDTH)),
            full((1, SGU_WIDTH)),
            full((N_SGU_HEADS, BLOCK, BLOCK)),
            full((N_SGU_HEADS, BLOCK)),
            pl.BlockSpec(memory_space=pl.ANY),
            full((1, D_MODEL)),
            full((1, D_MODEL)),
        ],
        out_specs=tile_spec,
        out_shape=jax.ShapeDtypeStruct(x.shape, x.dtype),
        scratch_shapes=[
            pltpu.VMEM((D_MODEL, IN_WIDTH), bf16),
            pltpu.VMEM((D_MODEL, D_MODEL), bf16),
            pltpu.VMEM((1, BIAS_LANES), f32),
            pltpu.VMEM((W_SLOTS, W_CHUNK, IN_WIDTH), f32),
            pltpu.VMEM((W_SLOTS, W_CHUNK, D_MODEL), f32),
            pltpu.SemaphoreType.DMA((W_SLOTS,)),
            pltpu.SemaphoreType.DMA((W_SLOTS,)),
            pltpu.VMEM((seq_tile, ATTN_WIDTH), bf16),
            pltpu.VMEM((N_KV_HEADS, BLOCK + seq_tile, KV_WIDTH), bf16),
            pltpu.VMEM((N_KV_HEADS, BLOCK + seq_tile, KV_WIDTH), bf16),
            pltpu.VMEM((seq_tile, SGU_WIDTH), f32),
            pltpu.VMEM((seq_tile, SGU_WIDTH), bf16),
            pltpu.VMEM((seq_tile, SGU_WIDTH), f32),
            pltpu.VMEM((seq_tile, D_MODEL), bf16),
            pltpu.VMEM((N_SGU_HEADS // 2, BLOCK, 2 * BLOCK), bf16),
            pltpu.VMEM((BLOCK, SGU_WIDTH), f32),
        ],
        compiler_params=pltpu.CompilerParams(
            dimension_semantics=("arbitrary", "arbitrary"),
            vmem_limit_bytes=V7X_VMEM_LIMIT_BYTES),
        name="hybrid_layer",
    )(sinks, x, norm_g, w_in, b_in, ln_g, ln_b, sgu_w, sgu_b, w_out, b_out, final_g)


def kernel(x, norm_g, w_in, b_in, attn_sinks, sgu_ln_g, sgu_ln_b, sgu_w, sgu_b, w_out, b_out, final_norm_g):
    depth = norm_g.shape[0]
    for l in range(depth):
        last = l == depth - 1
        assert last, "the fused call applies the final norm; only depth 1 is supported"
        x = _layer_call(
            x, attn_sinks[l], norm_g[l][None, :], w_in[l], b_in[l][None, :],
            sgu_ln_g[l][None, :], sgu_ln_b[l][None, :], sgu_w[l], sgu_b[l],
            w_out[l], b_out[l][None, :], final_norm_g[None, :])
    return x
```

```python
import collections
import functools

import jax
import jax.numpy as jnp
from jax import lax
from jax.experimental import pallas as pl
from jax.experimental.pallas import tpu as pltpu

D_MODEL = 1024
HEAD_DIM = 64
ATTN_WIDTH = 512
KV_WIDTH = 128
SGU_WIDTH = 512
N_KV_HEADS = 2
Q_PER_KV = 4
N_SGU_HEADS = 8
BLOCK = 128
NORM_EPS = 1e-5
NEG_INF = -1e30
ATTN_SCALE = HEAD_DIM ** -0.5

OFF_Q = 0
OFF_K = OFF_Q + ATTN_WIDTH
OFF_V = OFF_K + KV_WIDTH
OFF_ZA = OFF_V + KV_WIDTH
OFF_U = OFF_ZA + ATTN_WIDTH
OFF_VS = OFF_U + SGU_WIDTH
OFF_ZS = OFF_VS + SGU_WIDTH
IN_WIDTH = OFF_ZS + SGU_WIDTH

SEQ_TILE = 1024
MXU_TILE = 256
ROW_CHUNK = 256
W_CHUNK = 64
W_SLOTS = 8
V7X_VMEM_LIMIT_BYTES = 56 * 1024 * 1024

_SQRT_TWO = 1.4142135623730951

_WeightCopy = collections.namedtuple("_WeightCopy", "hbm stage sem dst col_scale")


def _silu_of_half(hz):
    return hz * (1.0 + jnp.tanh(hz))


def _gelu_of_half(hz):
    return hz * (1.0 + lax.erf(hz * _SQRT_TWO))


def _layer_kernel(sinks_ref, x_ref, ng_ref, win_hbm, bin_ref, lng_ref, lnb_ref,
                  sw_ref, sb_ref, wout_hbm, bout_ref, fg_ref, o_ref,
                  win_ref, wout_ref, bsc_s, stage_in, stage_out, sem_in, sem_out,
                  q_s, kd_s, vd_s, u_s, vl_s, gs_s, mix_s, wp_s, sb_s, *, seq_tile):
    f32, bf16 = jnp.float32, jnp.bfloat16
    n_sub = seq_tile // BLOCK
    b_idx = pl.program_id(0)
    s_idx = pl.program_id(1)

    lane = lax.broadcasted_iota(jnp.int32, (BLOCK, BLOCK), 1)
    low_half = lane < HEAD_DIM

    @pl.when((b_idx == 0) & (s_idx == 0))
    def _():
        row = lax.broadcasted_iota(jnp.int32, (BLOCK, BLOCK), 0)
        for p in range(N_SGU_HEADS // 2):
            for half in range(2):
                w = jnp.where(row >= lane, sw_ref[2 * p + half], 0.0)
                wp_s[p, :, half * BLOCK:(half + 1) * BLOCK] = w.astype(bf16)
        sbt = sb_ref[...].T
        for hh in range(N_SGU_HEADS):
            sb_s[:, hh * HEAD_DIM:(hh + 1) * HEAD_DIM] = jnp.broadcast_to(
                sbt[:, hh:hh + 1], (BLOCK, HEAD_DIM))
        col = lax.broadcasted_iota(jnp.int32, (1, IN_WIDTH), 1)
        col_scale = jnp.where(col < OFF_K, ATTN_SCALE, jnp.where(col < OFF_ZA, 1.0, 0.5)).astype(f32)
        bsc_s[...] = bin_ref[...] * col_scale
        jobs = (_WeightCopy(win_hbm, stage_in, sem_in, win_ref, col_scale),
                _WeightCopy(wout_hbm, stage_out, sem_out, wout_ref, None))
        n_chunks = D_MODEL // W_CHUNK

        def chunk_copy(job, c):
            return pltpu.make_async_copy(job.hbm.at[pl.ds(c * W_CHUNK, W_CHUNK), :],
                                         job.stage.at[c % W_SLOTS], job.sem.at[c % W_SLOTS])

        for job in jobs:
            for c in range(W_SLOTS):
                chunk_copy(job, c).start(priority=c % 2)
        for job in jobs:
            for c in range(n_chunks):
                chunk_copy(job, c).wait()
                w = job.stage[c % W_SLOTS]
                if job.col_scale is not None:
                    w = w * job.col_scale
                job.dst[c * W_CHUNK:(c + 1) * W_CHUNK, :] = w.astype(bf16)
                if c + W_SLOTS < n_chunks:
                    chunk_copy(job, c + W_SLOTS).start(priority=c % 2)

    @pl.when(s_idx == 0)
    def _():
        for hh in range(N_KV_HEADS):
            kd_s[hh, 0:BLOCK, :] = jnp.zeros((BLOCK, KV_WIDTH), bf16)
            vd_s[hh, 0:BLOCK, :] = jnp.zeros((BLOCK, KV_WIDTH), bf16)

    @pl.when(s_idx > 0)
    def _():
        for hh in range(N_KV_HEADS):
            kd_s[hh, 0:BLOCK, :] = kd_s[hh, seq_tile:seq_tile + BLOCK, :]
            vd_s[hh, 0:BLOCK, :] = vd_s[hh, seq_tile:seq_tile + BLOCK, :]

    h_chunks = []
    for r0 in range(0, seq_tile, ROW_CHUNK):
        x = x_ref[0, r0:r0 + ROW_CHUNK, :]
        ms = jnp.mean(x * x, axis=-1, keepdims=True)
        hc = ((x * lax.rsqrt(ms + NORM_EPS)) * ng_ref[...]).astype(bf16)
        h_chunks.append(hc)
        q = jnp.dot(hc, win_ref[:, OFF_Q:OFF_Q + ATTN_WIDTH], preferred_element_type=f32)
        q_s[r0:r0 + ROW_CHUNK, :] = (q + bsc_s[:, OFF_Q:OFF_Q + ATTN_WIDTH]).astype(bf16)
    h = jnp.concatenate(h_chunks, axis=0)
    half_t = seq_tile // 2

    def proj_tile(off):
        cols = slice(off, off + MXU_TILE)
        return jnp.concatenate(
            [jnp.dot(h[i * half_t:(i + 1) * half_t], win_ref[:, cols], preferred_element_type=f32)
             for i in range(2)], axis=0) + bsc_s[:, cols]

    kv = proj_tile(OFF_K)
    half_mask = lax.broadcasted_iota(jnp.int32, (seq_tile, KV_WIDTH), 1) < HEAD_DIM
    for dup_ref, off in ((kd_s, 0), (vd_s, KV_WIDTH)):
        t = kv[:, off:off + KV_WIDTH]
        t_sw = pltpu.roll(t, HEAD_DIM, axis=1)
        dup_ref[0, BLOCK:BLOCK + seq_tile, :] = jnp.where(half_mask, t, t_sw).astype(bf16)
        dup_ref[1, BLOCK:BLOCK + seq_tile, :] = jnp.where(half_mask, t_sw, t).astype(bf16)

    ga_t = [_silu_of_half(proj_tile(OFF_ZA))]

    row = lax.broadcasted_iota(jnp.int32, (BLOCK, BLOCK), 0)
    prev_side = lane > row
    prev_side4 = jnp.concatenate([prev_side] * Q_PER_KV, axis=0)
    has_prev = s_idx > 0
    units = [(j, hh) for j in range(n_sub) for hh in range(N_KV_HEADS)]
    scores = []
    for j, hh in units:
        rows = slice(j * BLOCK, (j + 1) * BLOCK)
        parts = []
        for g in range(Q_PER_KV):
            c = hh * (Q_PER_KV // 2) + g // 2
            qc = q_s[rows, c * BLOCK:(c + 1) * BLOCK]
            keep = low_half if g % 2 == 0 else jnp.logical_not(low_half)
            parts.append(jnp.where(keep, qc, jnp.zeros_like(qc)))
        qst = jnp.concatenate(parts, axis=0)
        kb = kd_s[hh, j * BLOCK:(j + 2) * BLOCK, :]
        sc = lax.dot_general(qst, kb, (((1,), (1,)), ((), ())),
                             preferred_element_type=f32)
        s_prev = sc[:, 0:BLOCK]
        if j == 0:
            s_prev = jnp.where(has_prev, s_prev, NEG_INF)
        scores.append(jnp.where(prev_side4, s_prev, sc[:, BLOCK:2 * BLOCK]))

    def attend(u):
        j, hh = units[u]
        sc = scores[u]
        probs, inv_denoms = [], []
        for g in range(Q_PER_KV):
            sg = sc[g * BLOCK:(g + 1) * BLOCK, :]
            sink = sinks_ref[hh * Q_PER_KV + g]
            m = jnp.maximum(jnp.max(sg, axis=-1, keepdims=True), sink)
            p = jnp.exp(sg - m)
            denom = jnp.sum(p, axis=-1, keepdims=True) + jnp.exp(sink - m)
            inv_denoms.append(1.0 / denom)
            pb = p.astype(bf16)
            zero = jnp.zeros_like(pb)
            probs.append(jnp.concatenate([jnp.where(prev_side, pb, zero),
                                          jnp.where(prev_side, zero, pb)], axis=1))
        probs = jnp.concatenate(probs, axis=0)
        vb = vd_s[hh, j * BLOCK:(j + 2) * BLOCK, :]
        o = jnp.dot(probs, vb, preferred_element_type=f32)
        o = [o[g * BLOCK:(g + 1) * BLOCK, :] * inv_denoms[g] for g in range(Q_PER_KV)]
        return [jnp.where(low_half, o[2 * c2], o[2 * c2 + 1]) for c2 in range(Q_PER_KV // 2)]

    attn_out = []
    n_slots = (IN_WIDTH - OFF_ZA) // MXU_TILE - 1
    units_after = [len(units) * (i + 1) // n_slots - len(units) * i // n_slots for i in range(n_slots)]
    slot = iter(units_after)

    def attend_some():
        for _ in range(next(slot)):
            attn_out.append(attend(len(attn_out)))

    for n in range(1, ATTN_WIDTH // MXU_TILE):
        ga_t.append(_silu_of_half(proj_tile(OFF_ZA + n * MXU_TILE)))
        attend_some()
    for n in range(SGU_WIDTH // MXU_TILE):
        u_s[:, n * MXU_TILE:(n + 1) * MXU_TILE] = _gelu_of_half(proj_tile(OFF_U + n * MXU_TILE))
        attend_some()
    vg = []
    for n in range(SGU_WIDTH // MXU_TILE):
        vg.append(_gelu_of_half(proj_tile(OFF_VS + n * MXU_TILE)))
        attend_some()
    vg = jnp.concatenate(vg, axis=1)
    mu = jnp.mean(vg, axis=-1, keepdims=True)
    vc = vg - mu
    var = jnp.mean(vc * vc, axis=-1, keepdims=True)
    vl_s[...] = ((vc * lax.rsqrt(var + NORM_EPS)) * lng_ref[...] + lnb_ref[...]).astype(bf16)
    for n in range(SGU_WIDTH // MXU_TILE):
        gs_s[:, n * MXU_TILE:(n + 1) * MXU_TILE] = _silu_of_half(proj_tile(OFF_ZS + n * MXU_TILE))
        attend_some()
    assert len(attn_out) == len(units)

    ga = jnp.concatenate(ga_t, axis=1)
    for u, (j, hh) in enumerate(units):
        rows = slice(j * BLOCK, (j + 1) * BLOCK)
        for c2 in range(Q_PER_KV // 2):
            cols = slice((hh * (Q_PER_KV // 2) + c2) * BLOCK, (hh * (Q_PER_KV // 2) + c2 + 1) * BLOCK)
            mix_s[rows, cols] = (attn_out[u][c2] * ga[rows, cols]).astype(bf16)

    for j in range(n_sub):
        rows = slice(j * BLOCK, (j + 1) * BLOCK)
        for p_idx in range(N_SGU_HEADS // 2):
            cols = slice(p_idx * BLOCK, (p_idx + 1) * BLOCK)
            vp = vl_s[rows, cols]
            zero = jnp.zeros_like(vp)
            rhs = jnp.concatenate([jnp.where(low_half, vp, zero),
                                   jnp.where(low_half, zero, vp)], axis=0)
            mixed = jnp.dot(wp_s[p_idx], rhs, preferred_element_type=f32) + sb_s[:, cols]
            mix_s[rows, ATTN_WIDTH + p_idx * BLOCK:ATTN_WIDTH + (p_idx + 1) * BLOCK] = (
                (u_s[rows, cols] * mixed) * gs_s[rows, cols]).astype(bf16)

    y = x_ref[0] + jnp.dot(mix_s[...], wout_ref[...], preferred_element_type=f32) + bout_ref[...]
    ms2 = jnp.mean(y * y, axis=-1, keepdims=True)
    o_ref[0] = (y * lax.rsqrt(ms2 + NORM_EPS)) * fg_ref[...]


def _layer_call(x, sinks, norm_g, w_in, b_in, ln_g, ln_b, sgu_w, sgu_b, w_out, b_out, final_g,
                *, seq_tile=SEQ_TILE):
    batch, seq, d_model = x.shape
    assert d_model == D_MODEL and seq % seq_tile == 0 and seq_tile % (2 * BLOCK) == 0
    f32, bf16 = jnp.float32, jnp.bfloat16

    def full(shape):
        return pl.BlockSpec(shape, lambda b, s: (0,) * len(shape))

    tile_spec = pl.BlockSpec((1, seq_tile, D_MODEL), lambda b, s: (b, s, 0))
    return pl.pallas_call(
        functools.partial(_layer_kernel, seq_tile=seq_tile),
        grid=(batch, seq // seq_tile),
        in_specs=[
            pl.BlockSpec(memory_space=pltpu.SMEM),
            tile_spec,
            full((1, D_MODEL)),
            pl.BlockSpec(memory_space=pl.ANY),
            full((1, IN_WIDTH)),
            full((1, SGU_WIDTH)),
            full((1, SGU_WIDTH)),
            full((N_SGU_HEADS, BLOCK, BLOCK)),
            full((N_SGU_HEADS, BLOCK)),
            pl.BlockSpec(memory_space=pl.ANY),
            full((1, D_MODEL)),
            full((1, D_MODEL)),
        ],
        out_specs=tile_spec,
        out_shape=jax.ShapeDtypeStruct(x.shape, x.dtype),
        scratch_shapes=[
            pltpu.VMEM((D_MODEL, IN_WIDTH), bf16),
            pltpu.VMEM((D_MODEL, D_MODEL), bf16),
            pltpu.VMEM((1, IN_WIDTH), f32),
            pltpu.VMEM((W_SLOTS, W_CHUNK, IN_WIDTH), f32),
            pltpu.VMEM((W_SLOTS, W_CHUNK, D_MODEL), f32),
            pltpu.SemaphoreType.DMA((W_SLOTS,)),
            pltpu.SemaphoreType.DMA((W_SLOTS,)),
            pltpu.VMEM((seq_tile, ATTN_WIDTH), bf16),
            pltpu.VMEM((N_KV_HEADS, BLOCK + seq_tile, KV_WIDTH), bf16),
            pltpu.VMEM((N_KV_HEADS, BLOCK + seq_tile, KV_WIDTH), bf16),
            pltpu.VMEM((seq_tile, SGU_WIDTH), f32),
            pltpu.VMEM((seq_tile, SGU_WIDTH), bf16),
            pltpu.VMEM((seq_tile, SGU_WIDTH), f32),
            pltpu.VMEM((seq_tile, D_MODEL), bf16),
            pltpu.VMEM((N_SGU_HEADS // 2, BLOCK, 2 * BLOCK), bf16),
            pltpu.VMEM((BLOCK, SGU_WIDTH), f32),
        ],
        compiler_params=pltpu.CompilerParams(
            dimension_semantics=("arbitrary", "arbitrary"),
            vmem_limit_bytes=V7X_VMEM_LIMIT_BYTES),
        name="hybrid_layer",
    )(sinks, x, norm_g, w_in, b_in, ln_g, ln_b, sgu_w, sgu_b, w_out, b_out, final_g)


def kernel(x, norm_g, w_in, b_in, attn_sinks, sgu_ln_g, sgu_ln_b, sgu_w, sgu_b, w_out, b_out, final_norm_g):
    depth = norm_g.shape[0]
    for l in range(depth):
        last = l == depth - 1
        assert last, "the fused call applies the final norm; only depth 1 is supported"
        x = _layer_call(
            x, attn_sinks[l], norm_g[l][None, :], w_in[l], b_in[l][None, :],
            sgu_ln_g[l][None, :], sgu_ln_b[l][None, :], sgu_w[l], sgu_b[l],
            w_out[l], b_out[l][None, :], final_norm_g[None, :])
    return x
```

```python
import collections
import functools

import jax
import jax.numpy as jnp
from jax import lax
from jax.experimental import pallas as pl
from jax.experimental.pallas import tpu as pltpu

D_MODEL = 1024
HEAD_DIM = 64
ATTN_WIDTH = 512
KV_WIDTH = 128
SGU_WIDTH = 512
N_KV_HEADS = 2
Q_PER_KV = 4
N_SGU_HEADS = 8
BLOCK = 128
NORM_EPS = 1e-5
NEG_INF = -1e30
ATTN_SCALE = HEAD_DIM ** -0.5

OFF_Q = 0
OFF_K = OFF_Q + ATTN_WIDTH
OFF_V = OFF_K + KV_WIDTH
OFF_ZA = OFF_V + KV_WIDTH
OFF_U = OFF_ZA + ATTN_WIDTH
OFF_VS = OFF_U + SGU_WIDTH
OFF_ZS = OFF_VS + SGU_WIDTH
IN_WIDTH = OFF_ZS + SGU_WIDTH

SEQ_TILE = 1024
MXU_TILE = 256
ROW_CHUNK = 256
W_CHUNK = 32
W_SLOTS = 16
V7X_VMEM_LIMIT_BYTES = 56 * 1024 * 1024

_SQRT_TWO = 1.4142135623730951

_WeightCopy = collections.namedtuple("_WeightCopy", "hbm stage sem dst col_scale")


def _silu_of_half(hz):
    return hz * (1.0 + jnp.tanh(hz))


def _gelu_of_half(hz):
    return hz * (1.0 + lax.erf(hz * _SQRT_TWO))


def _layer_kernel(sinks_ref, x_ref, ng_ref, win_hbm, bin_ref, lng_ref, lnb_ref,
                  sw_ref, sb_ref, wout_hbm, bout_ref, fg_ref, o_ref,
                  win_ref, wout_ref, bsc_s, stage_in, stage_out, sem_in, sem_out,
                  q_s, kd_s, vd_s, u_s, vl_s, gs_s, mix_s, wp_s, sb_s, *, seq_tile):
    f32, bf16 = jnp.float32, jnp.bfloat16
    n_sub = seq_tile // BLOCK
    b_idx = pl.program_id(0)
    s_idx = pl.program_id(1)

    lane = lax.broadcasted_iota(jnp.int32, (BLOCK, BLOCK), 1)
    low_half = lane < HEAD_DIM

    @pl.when((b_idx == 0) & (s_idx == 0))
    def _():
        row = lax.broadcasted_iota(jnp.int32, (BLOCK, BLOCK), 0)
        for p in range(N_SGU_HEADS // 2):
            for half in range(2):
                w = jnp.where(row >= lane, sw_ref[2 * p + half], 0.0)
                wp_s[p, :, half * BLOCK:(half + 1) * BLOCK] = w.astype(bf16)
        sbt = sb_ref[...].T
        for hh in range(N_SGU_HEADS):
            sb_s[:, hh * HEAD_DIM:(hh + 1) * HEAD_DIM] = jnp.broadcast_to(
                sbt[:, hh:hh + 1], (BLOCK, HEAD_DIM))
        col = lax.broadcasted_iota(jnp.int32, (1, IN_WIDTH), 1)
        col_scale = jnp.where(col < OFF_K, ATTN_SCALE, jnp.where(col < OFF_ZA, 1.0, 0.5)).astype(f32)
        bsc_s[...] = bin_ref[...] * col_scale
        jobs = (_WeightCopy(win_hbm, stage_in, sem_in, win_ref, col_scale),
                _WeightCopy(wout_hbm, stage_out, sem_out, wout_ref, None))
        n_chunks = D_MODEL // W_CHUNK

        def chunk_copy(job, c):
            return pltpu.make_async_copy(job.hbm.at[pl.ds(c * W_CHUNK, W_CHUNK), :],
                                         job.stage.at[c % W_SLOTS], job.sem.at[c % W_SLOTS])

        for job in jobs:
            for c in range(W_SLOTS):
                chunk_copy(job, c).start(priority=c % 2)
        for job in jobs:
            for c in range(n_chunks):
                chunk_copy(job, c).wait()
                w = job.stage[c % W_SLOTS]
                if job.col_scale is not None:
                    w = w * job.col_scale
                job.dst[c * W_CHUNK:(c + 1) * W_CHUNK, :] = w.astype(bf16)
                if c + W_SLOTS < n_chunks:
                    chunk_copy(job, c + W_SLOTS).start(priority=c % 2)

    @pl.when(s_idx == 0)
    def _():
        for hh in range(N_KV_HEADS):
            kd_s[hh, 0:BLOCK, :] = jnp.zeros((BLOCK, KV_WIDTH), bf16)
            vd_s[hh, 0:BLOCK, :] = jnp.zeros((BLOCK, KV_WIDTH), bf16)

    @pl.when(s_idx > 0)
    def _():
        for hh in range(N_KV_HEADS):
            kd_s[hh, 0:BLOCK, :] = kd_s[hh, seq_tile:seq_tile + BLOCK, :]
            vd_s[hh, 0:BLOCK, :] = vd_s[hh, seq_tile:seq_tile + BLOCK, :]

    h_chunks = []
    for r0 in range(0, seq_tile, ROW_CHUNK):
        x = x_ref[0, r0:r0 + ROW_CHUNK, :]
        ms = jnp.mean(x * x, axis=-1, keepdims=True)
        hc = ((x * lax.rsqrt(ms + NORM_EPS)) * ng_ref[...]).astype(bf16)
        h_chunks.append(hc)
        q = jnp.dot(hc, win_ref[:, OFF_Q:OFF_Q + ATTN_WIDTH], preferred_element_type=f32)
        q_s[r0:r0 + ROW_CHUNK, :] = (q + bsc_s[:, OFF_Q:OFF_Q + ATTN_WIDTH]).astype(bf16)
    h = jnp.concatenate(h_chunks, axis=0)
    half_t = seq_tile // 2

    def proj_tile(off):
        cols = slice(off, off + MXU_TILE)
        return jnp.concatenate(
            [jnp.dot(h[i * half_t:(i + 1) * half_t], win_ref[:, cols], preferred_element_type=f32)
             for i in range(2)], axis=0) + bsc_s[:, cols]

    kv = proj_tile(OFF_K)
    half_mask = lax.broadcasted_iota(jnp.int32, (seq_tile, KV_WIDTH), 1) < HEAD_DIM
    for dup_ref, off in ((kd_s, 0), (vd_s, KV_WIDTH)):
        t = kv[:, off:off + KV_WIDTH]
        t_sw = pltpu.roll(t, HEAD_DIM, axis=1)
        dup_ref[0, BLOCK:BLOCK + seq_tile, :] = jnp.where(half_mask, t, t_sw).astype(bf16)
        dup_ref[1, BLOCK:BLOCK + seq_tile, :] = jnp.where(half_mask, t_sw, t).astype(bf16)

    ga_t = [_silu_of_half(proj_tile(OFF_ZA))]

    row = lax.broadcasted_iota(jnp.int32, (BLOCK, BLOCK), 0)
    prev_side = lane > row
    prev_side4 = jnp.concatenate([prev_side] * Q_PER_KV, axis=0)
    has_prev = s_idx > 0
    units = [(j, hh) for j in range(n_sub) for hh in range(N_KV_HEADS)]
    scores = []
    for j, hh in units:
        rows = slice(j * BLOCK, (j + 1) * BLOCK)
        parts = []
        for g in range(Q_PER_KV):
            c = hh * (Q_PER_KV // 2) + g // 2
            qc = q_s[rows, c * BLOCK:(c + 1) * BLOCK]
            keep = low_half if g % 2 == 0 else jnp.logical_not(low_half)
            parts.append(jnp.where(keep, qc, jnp.zeros_like(qc)))
        qst = jnp.concatenate(parts, axis=0)
        kb = kd_s[hh, j * BLOCK:(j + 2) * BLOCK, :]
        sc = lax.dot_general(qst, kb, (((1,), (1,)), ((), ())),
                             preferred_element_type=f32)
        s_prev = sc[:, 0:BLOCK]
        if j == 0:
            s_prev = jnp.where(has_prev, s_prev, NEG_INF)
        scores.append(jnp.where(prev_side4, s_prev, sc[:, BLOCK:2 * BLOCK]))

    def attend(u):
        j, hh = units[u]
        sc = scores[u]
        probs, inv_denoms = [], []
        for g in range(Q_PER_KV):
            sg = sc[g * BLOCK:(g + 1) * BLOCK, :]
            sink = sinks_ref[hh * Q_PER_KV + g]
            m = jnp.maximum(jnp.max(sg, axis=-1, keepdims=True), sink)
            p = jnp.exp(sg - m)
            denom = jnp.sum(p, axis=-1, keepdims=True) + jnp.exp(sink - m)
            inv_denoms.append(1.0 / denom)
            pb = p.astype(bf16)
            zero = jnp.zeros_like(pb)
            probs.append(jnp.concatenate([jnp.where(prev_side, pb, zero),
                                          jnp.where(prev_side, zero, pb)], axis=1))
        probs = jnp.concatenate(probs, axis=0)
        vb = vd_s[hh, j * BLOCK:(j + 2) * BLOCK, :]
        o = jnp.dot(probs, vb, preferred_element_type=f32)
        o = [o[g * BLOCK:(g + 1) * BLOCK, :] * inv_denoms[g] for g in range(Q_PER_KV)]
        return [jnp.where(low_half, o[2 * c2], o[2 * c2 + 1]) for c2 in range(Q_PER_KV // 2)]

    attn_out = []
    n_slots = (IN_WIDTH - OFF_ZA) // MXU_TILE - 1
    units_after = [len(units) * (i + 1) // n_slots - len(units) * i // n_slots for i in range(n_slots)]
    slot = iter(units_after)

    def attend_some():
        for _ in range(next(slot)):
            attn_out.append(attend(len(attn_out)))

    for n in range(1, ATTN_WIDTH // MXU_TILE):
        ga_t.append(_silu_of_half(proj_tile(OFF_ZA + n * MXU_TILE)))
        attend_some()
    for n in range(SGU_WIDTH // MXU_TILE):
        u_s[:, n * MXU_TILE:(n + 1) * MXU_TILE] = _gelu_of_half(proj_tile(OFF_U + n * MXU_TILE))
        attend_some()
    vg = []
    for n in range(SGU_WIDTH // MXU_TILE):
        vg.append(_gelu_of_half(proj_tile(OFF_VS + n * MXU_TILE)))
        attend_some()
    vg = jnp.concatenate(vg, axis=1)
    mu = jnp.mean(vg, axis=-1, keepdims=True)
    vc = vg - mu
    var = jnp.mean(vc * vc, axis=-1, keepdims=True)
    vl_s[...] = ((vc * lax.rsqrt(var + NORM_EPS)) * lng_ref[...] + lnb_ref[...]).astype(bf16)
    for n in range(SGU_WIDTH // MXU_TILE):
        gs_s[:, n * MXU_TILE:(n + 1) * MXU_TILE] = _silu_of_half(proj_tile(OFF_ZS + n * MXU_TILE))
        attend_some()
    assert len(attn_out) == len(units)

    ga = jnp.concatenate(ga_t, axis=1)
    for u, (j, hh) in enumerate(units):
        rows = slice(j * BLOCK, (j + 1) * BLOCK)
        for c2 in range(Q_PER_KV // 2):
            cols = slice((hh * (Q_PER_KV // 2) + c2) * BLOCK, (hh * (Q_PER_KV // 2) + c2 + 1) * BLOCK)
            mix_s[rows, cols] = (attn_out[u][c2] * ga[rows, cols]).astype(bf16)

    for j in range(n_sub):
        rows = slice(j * BLOCK, (j + 1) * BLOCK)
        for p_idx in range(N_SGU_HEADS // 2):
            cols = slice(p_idx * BLOCK, (p_idx + 1) * BLOCK)
            vp = vl_s[rows, cols]
            zero = jnp.zeros_like(vp)
            rhs = jnp.concatenate([jnp.where(low_half, vp, zero),
                                   jnp.where(low_half, zero, vp)], axis=0)
            mixed = jnp.dot(wp_s[p_idx], rhs, preferred_element_type=f32) + sb_s[:, cols]
            mix_s[rows, ATTN_WIDTH + p_idx * BLOCK:ATTN_WIDTH + (p_idx + 1) * BLOCK] = (
                (u_s[rows, cols] * mixed) * gs_s[rows, cols]).astype(bf16)

    y = x_ref[0] + jnp.dot(mix_s[...], wout_ref[...], preferred_element_type=f32) + bout_ref[...]
    ms2 = jnp.mean(y * y, axis=-1, keepdims=True)
    o_ref[0] = (y * lax.rsqrt(ms2 + NORM_EPS)) * fg_ref[...]


def _layer_call(x, sinks, norm_g, w_in, b_in, ln_g, ln_b, sgu_w, sgu_b, w_out, b_out, final_g,
                *, seq_tile=SEQ_TILE):
    batch, seq, d_model = x.shape
    assert d_model == D_MODEL and seq % seq_tile == 0 and seq_tile % (2 * BLOCK) == 0
    f32, bf16 = jnp.float32, jnp.bfloat16

    def full(shape):
        return pl.BlockSpec(shape, lambda b, s: (0,) * len(shape))

    tile_spec = pl.BlockSpec((1, seq_tile, D_MODEL), lambda b, s: (b, s, 0))
    return pl.pallas_call(
        functools.partial(_layer_kernel, seq_tile=seq_tile),
        grid=(batch, seq // seq_tile),
        in_specs=[
            pl.BlockSpec(memory_space=pltpu.SMEM),
            tile_spec,
            full((1, D_MODEL)),
            pl.BlockSpec(memory_space=pl.ANY),
            full((1, IN_WIDTH)),
            full((1, SGU_WIDTH)),
            full((1, SGU_WIDTH)),
            full((N_SGU_HEADS, BLOCK, BLOCK)),
            full((N_SGU_HEADS, BLOCK)),
            pl.BlockSpec(memory_space=pl.ANY),
            full((1, D_MODEL)),
            full((1, D_MODEL)),
        ],
        out_specs=tile_spec,
        out_shape=jax.ShapeDtypeStruct(x.shape, x.dtype),
        scratch_shapes=[
            pltpu.VMEM((D_MODEL, IN_WIDTH), bf16),
            pltpu.VMEM((D_MODEL, D_MODEL), bf16),
            pltpu.VMEM((1, IN_WIDTH), f32),
            pltpu.VMEM((W_SLOTS, W_CHUNK, IN_WIDTH), f32),
            pltpu.VMEM((W_SLOTS, W_CHUNK, D_MODEL), f32),
            pltpu.SemaphoreType.DMA((W_SLOTS,)),
            pltpu.SemaphoreType.DMA((W_SLOTS,)),
            pltpu.VMEM((seq_tile, ATTN_WIDTH), bf16),
            pltpu.VMEM((N_KV_HEADS, BLOCK + seq_tile, KV_WIDTH), bf16),
            pltpu.VMEM((N_KV_HEADS, BLOCK + seq_tile, KV_WIDTH), bf16),
            pltpu.VMEM((seq_tile, SGU_WIDTH), f32),
            pltpu.VMEM((seq_tile, SGU_WIDTH), bf16),
            pltpu.VMEM((seq_tile, SGU_WIDTH), f32),
            pltpu.VMEM((seq_tile, D_MODEL), bf16),
            pltpu.VMEM((N_SGU_HEADS // 2, BLOCK, 2 * BLOCK), bf16),
            pltpu.VMEM((BLOCK, SGU_WIDTH), f32),
        ],
        compiler_params=pltpu.CompilerParams(
            dimension_semantics=("arbitrary", "arbitrary"),
            vmem_limit_bytes=V7X_VMEM_LIMIT_BYTES),
        name="hybrid_layer",
    )(sinks, x, norm_g, w_in, b_in, ln_g, ln_b, sgu_w, sgu_b, w_out, b_out, final_g)


def kernel(x, norm_g, w_in, b_in, attn_sinks, sgu_ln_g, sgu_ln_b, sgu_w, sgu_b, w_out, b_out, final_norm_g):
    depth = norm_g.shape[0]
    for l in range(depth):
        last = l == depth - 1
        assert last, "the fused call applies the final norm; only depth 1 is supported"
        x = _layer_call(
            x, attn_sinks[l], norm_g[l][None, :], w_in[l], b_in[l][None, :],
            sgu_ln_g[l][None, :], sgu_ln_b[l][None, :], sgu_w[l], sgu_b[l],
            w_out[l], b_out[l][None, :], final_norm_g[None, :])
    return x
```

```python
import collections
import functools

import jax
import jax.numpy as jnp
from jax import lax
from jax.experimental import pallas as pl
from jax.experimental.pallas import tpu as pltpu

D_MODEL = 1024
HEAD_DIM = 64
ATTN_WIDTH = 512
KV_WIDTH = 128
SGU_WIDTH = 512
N_KV_HEADS = 2
Q_PER_KV = 4
N_SGU_HEADS = 8
BLOCK = 128
NORM_EPS = 1e-5
NEG_INF = -1e30
ATTN_SCALE = HEAD_DIM ** -0.5

OFF_Q = 0
OFF_K = OFF_Q + ATTN_WIDTH
OFF_V = OFF_K + KV_WIDTH
OFF_ZA = OFF_V + KV_WIDTH
OFF_U = OFF_ZA + ATTN_WIDTH
OFF_VS = OFF_U + SGU_WIDTH
OFF_ZS = OFF_VS + SGU_WIDTH
IN_WIDTH = OFF_ZS + SGU_WIDTH

SEQ_TILE = 1024
MXU_TILE = 256
ROW_CHUNK = 256
W_CHUNK = 64
W_SLOTS = 8
V7X_VMEM_LIMIT_BYTES = 56 * 1024 * 1024

_SQRT_TWO = 1.4142135623730951

_WeightCopy = collections.namedtuple("_WeightCopy", "hbm stage sem dst col_scale")


def _silu_of_half(hz):
    return hz * (1.0 + jnp.tanh(hz))


def _gelu_of_half(hz):
    return hz * (1.0 + lax.erf(hz * _SQRT_TWO))


def _layer_kernel(sinks_ref, x_ref, ng_ref, win_hbm, bin_ref, lng_ref, lnb_ref,
                  sw_ref, sb_ref, wout_hbm, bout_ref, fg_ref, o_ref,
                  win_ref, wout_ref, bsc_s, stage_in, stage_out, sem_in, sem_out,
                  q_s, kd_s, vd_s, u_s, vl_s, gs_s, mix_s, wp_s, sb_s, *, seq_tile):
    f32, bf16 = jnp.float32, jnp.bfloat16
    n_sub = seq_tile // BLOCK
    b_idx = pl.program_id(0)
    s_idx = pl.program_id(1)

    lane = lax.broadcasted_iota(jnp.int32, (BLOCK, BLOCK), 1)
    low_half = lane < HEAD_DIM

    @pl.when((b_idx == 0) & (s_idx == 0))
    def _():
        row = lax.broadcasted_iota(jnp.int32, (BLOCK, BLOCK), 0)
        for p in range(N_SGU_HEADS // 2):
            for half in range(2):
                w = jnp.where(row >= lane, sw_ref[2 * p + half], 0.0)
                wp_s[p, :, half * BLOCK:(half + 1) * BLOCK] = w.astype(bf16)
        sbt = sb_ref[...].T
        for hh in range(N_SGU_HEADS):
            sb_s[:, hh * HEAD_DIM:(hh + 1) * HEAD_DIM] = jnp.broadcast_to(
                sbt[:, hh:hh + 1], (BLOCK, HEAD_DIM))
        col = lax.broadcasted_iota(jnp.int32, (1, IN_WIDTH), 1)
        col_scale = jnp.where(col < OFF_K, ATTN_SCALE, jnp.where(col < OFF_ZA, 1.0, 0.5)).astype(f32)
        bsc_s[...] = bin_ref[...] * col_scale
        jobs = (_WeightCopy(win_hbm, stage_in, sem_in, win_ref, col_scale),
                _WeightCopy(wout_hbm, stage_out, sem_out, wout_ref, None))
        n_chunks = D_MODEL // W_CHUNK

        def chunk_copy(job, c):
            return pltpu.make_async_copy(job.hbm.at[pl.ds(c * W_CHUNK, W_CHUNK), :],
                                         job.stage.at[c % W_SLOTS], job.sem.at[c % W_SLOTS])

        for job in jobs:
            for c in range(W_SLOTS):
                chunk_copy(job, c).start(priority=c % 2)
        for job in jobs:
            for c in range(n_chunks):
                chunk_copy(job, c).wait()
                w = job.stage[c % W_SLOTS]
                if job.col_scale is not None:
                    w = w * job.col_scale
                job.dst[c * W_CHUNK:(c + 1) * W_CHUNK, :] = w.astype(bf16)
                if c + W_SLOTS < n_chunks:
                    chunk_copy(job, c + W_SLOTS).start(priority=c % 2)

    @pl.when(s_idx == 0)
    def _():
        for hh in range(N_KV_HEADS):
            kd_s[hh, :, 0:BLOCK] = jnp.zeros((KV_WIDTH, BLOCK), bf16)
            vd_s[hh, 0:BLOCK, :] = jnp.zeros((BLOCK, KV_WIDTH), bf16)

    @pl.when(s_idx > 0)
    def _():
        for hh in range(N_KV_HEADS):
            kd_s[hh, :, 0:BLOCK] = kd_s[hh, :, seq_tile:seq_tile + BLOCK]
            vd_s[hh, 0:BLOCK, :] = vd_s[hh, seq_tile:seq_tile + BLOCK, :]

    h_chunks = []
    for r0 in range(0, seq_tile, ROW_CHUNK):
        x = x_ref[0, r0:r0 + ROW_CHUNK, :]
        ms = jnp.mean(x * x, axis=-1, keepdims=True)
        hc = ((x * lax.rsqrt(ms + NORM_EPS)) * ng_ref[...]).astype(bf16)
        h_chunks.append(hc)
        q = jnp.dot(hc, win_ref[:, OFF_Q:OFF_Q + ATTN_WIDTH], preferred_element_type=f32)
        q_s[r0:r0 + ROW_CHUNK, :] = (q + bsc_s[:, OFF_Q:OFF_Q + ATTN_WIDTH]).astype(bf16)
    h = jnp.concatenate(h_chunks, axis=0)
    half_t = seq_tile // 2

    def proj_tile(off):
        cols = slice(off, off + MXU_TILE)
        return jnp.concatenate(
            [jnp.dot(h[i * half_t:(i + 1) * half_t], win_ref[:, cols], preferred_element_type=f32)
             for i in range(2)], axis=0) + bsc_s[:, cols]

    kv = proj_tile(OFF_K)
    half_mask = lax.broadcasted_iota(jnp.int32, (seq_tile, KV_WIDTH), 1) < HEAD_DIM
    for off in (0, KV_WIDTH):
        t = kv[:, off:off + KV_WIDTH]
        t_sw = pltpu.roll(t, HEAD_DIM, axis=1)
        dups = (jnp.where(half_mask, t, t_sw), jnp.where(half_mask, t_sw, t))
        for hh in range(N_KV_HEADS):
            if off == 0:
                kd_s[hh, :, BLOCK:BLOCK + seq_tile] = dups[hh].T.astype(bf16)
            else:
                vd_s[hh, BLOCK:BLOCK + seq_tile, :] = dups[hh].astype(bf16)

    ga_t = [_silu_of_half(proj_tile(OFF_ZA))]

    row = lax.broadcasted_iota(jnp.int32, (BLOCK, BLOCK), 0)
    prev_side = lane > row
    prev_side4 = jnp.concatenate([prev_side] * Q_PER_KV, axis=0)
    has_prev = s_idx > 0
    units = [(j, hh) for j in range(n_sub) for hh in range(N_KV_HEADS)]
    scores = []
    for j, hh in units:
        rows = slice(j * BLOCK, (j + 1) * BLOCK)
        parts = []
        for g in range(Q_PER_KV):
            c = hh * (Q_PER_KV // 2) + g // 2
            qc = q_s[rows, c * BLOCK:(c + 1) * BLOCK]
            keep = low_half if g % 2 == 0 else jnp.logical_not(low_half)
            parts.append(jnp.where(keep, qc, jnp.zeros_like(qc)))
        qst = jnp.concatenate(parts, axis=0)
        kbt = kd_s[hh, :, j * BLOCK:(j + 2) * BLOCK]
        sc = jnp.dot(qst, kbt, preferred_element_type=f32)
        s_prev = sc[:, 0:BLOCK]
        if j == 0:
            s_prev = jnp.where(has_prev, s_prev, NEG_INF)
        scores.append(jnp.where(prev_side4, s_prev, sc[:, BLOCK:2 * BLOCK]))

    def attend(u):
        j, hh = units[u]
        sc = scores[u]
        probs, inv_denoms = [], []
        for g in range(Q_PER_KV):
            sg = sc[g * BLOCK:(g + 1) * BLOCK, :]
            sink = sinks_ref[hh * Q_PER_KV + g]
            m = jnp.maximum(jnp.max(sg, axis=-1, keepdims=True), sink)
            p = jnp.exp(sg - m)
            denom = jnp.sum(p, axis=-1, keepdims=True) + jnp.exp(sink - m)
            inv_denoms.append(1.0 / denom)
            pb = p.astype(bf16)
            zero = jnp.zeros_like(pb)
            probs.append(jnp.concatenate([jnp.where(prev_side, pb, zero),
                                          jnp.where(prev_side, zero, pb)], axis=1))
        probs = jnp.concatenate(probs, axis=0)
        vb = vd_s[hh, j * BLOCK:(j + 2) * BLOCK, :]
        o = jnp.dot(probs, vb, preferred_element_type=f32)
        o = [o[g * BLOCK:(g + 1) * BLOCK, :] * inv_denoms[g] for g in range(Q_PER_KV)]
        return [jnp.where(low_half, o[2 * c2], o[2 * c2 + 1]) for c2 in range(Q_PER_KV // 2)]

    attn_out = []
    n_slots = (IN_WIDTH - OFF_ZA) // MXU_TILE - 1
    units_after = [len(units) * (i + 1) // n_slots - len(units) * i // n_slots for i in range(n_slots)]
    slot = iter(units_after)

    def attend_some():
        for _ in range(next(slot)):
            attn_out.append(attend(len(attn_out)))

    for n in range(1, ATTN_WIDTH // MXU_TILE):
        ga_t.append(_silu_of_half(proj_tile(OFF_ZA + n * MXU_TILE)))
        attend_some()
    for n in range(SGU_WIDTH // MXU_TILE):
        u_s[:, n * MXU_TILE:(n + 1) * MXU_TILE] = _gelu_of_half(proj_tile(OFF_U + n * MXU_TILE))
        attend_some()
    vg = []
    for n in range(SGU_WIDTH // MXU_TILE):
        vg.append(_gelu_of_half(proj_tile(OFF_VS + n * MXU_TILE)))
        attend_some()
    vg = jnp.concatenate(vg, axis=1)
    mu = jnp.mean(vg, axis=-1, keepdims=True)
    vc = vg - mu
    var = jnp.mean(vc * vc, axis=-1, keepdims=True)
    vl_s[...] = ((vc * lax.rsqrt(var + NORM_EPS)) * lng_ref[...] + lnb_ref[...]).astype(bf16)
    for n in range(SGU_WIDTH // MXU_TILE):
        gs_s[:, n * MXU_TILE:(n + 1) * MXU_TILE] = _silu_of_half(proj_tile(OFF_ZS + n * MXU_TILE))
        attend_some()
    assert len(attn_out) == len(units)

    ga = jnp.concatenate(ga_t, axis=1)
    for u, (j, hh) in enumerate(units):
        rows = slice(j * BLOCK, (j + 1) * BLOCK)
        for c2 in range(Q_PER_KV // 2):
            cols = slice((hh * (Q_PER_KV // 2) + c2) * BLOCK, (hh * (Q_PER_KV // 2) + c2 + 1) * BLOCK)
            mix_s[rows, cols] = (attn_out[u][c2] * ga[rows, cols]).astype(bf16)

    for j in range(n_sub):
        rows = slice(j * BLOCK, (j + 1) * BLOCK)
        for p_idx in range(N_SGU_HEADS // 2):
            cols = slice(p_idx * BLOCK, (p_idx + 1) * BLOCK)
            vp = vl_s[rows, cols]
            zero = jnp.zeros_like(vp)
            rhs = jnp.concatenate([jnp.where(low_half, vp, zero),
                                   jnp.where(low_half, zero, vp)], axis=0)
            mixed = jnp.dot(wp_s[p_idx], rhs, preferred_element_type=f32) + sb_s[:, cols]
            mix_s[rows, ATTN_WIDTH + p_idx * BLOCK:ATTN_WIDTH + (p_idx + 1) * BLOCK] = (
                (u_s[rows, cols] * mixed) * gs_s[rows, cols]).astype(bf16)

    y = x_ref[0] + jnp.dot(mix_s[...], wout_ref[...], preferred_element_type=f32) + bout_ref[...]
    ms2 = jnp.mean(y * y, axis=-1, keepdims=True)
    o_ref[0] = (y * lax.rsqrt(ms2 + NORM_EPS)) * fg_ref[...]


def _layer_call(x, sinks, norm_g, w_in, b_in, ln_g, ln_b, sgu_w, sgu_b, w_out, b_out, final_g,
                *, seq_tile=SEQ_TILE):
    batch, seq, d_model = x.shape
    assert d_model == D_MODEL and seq % seq_tile == 0 and seq_tile % (2 * BLOCK) == 0
    f32, bf16 = jnp.float32, jnp.bfloat16

    def full(shape):
        return pl.BlockSpec(shape, lambda b, s: (0,) * len(shape))

    tile_spec = pl.BlockSpec((1, seq_tile, D_MODEL), lambda b, s: (b, s, 0))
    return pl.pallas_call(
        functools.partial(_layer_kernel, seq_tile=seq_tile),
        grid=(batch, seq // seq_tile),
        in_specs=[
            pl.BlockSpec(memory_space=pltpu.SMEM),
            tile_spec,
            full((1, D_MODEL)),
            pl.BlockSpec(memory_space=pl.ANY),
            full((1, IN_WIDTH)),
            full((1, SGU_WIDTH)),
            full((1, SGU_WIDTH)),
            full((N_SGU_HEADS, BLOCK, BLOCK)),
            full((N_SGU_HEADS, BLOCK)),
            pl.BlockSpec(memory_space=pl.ANY),
            full((1, D_MODEL)),
            full((1, D_MODEL)),
        ],
        out_specs=tile_spec,
        out_shape=jax.ShapeDtypeStruct(x.shape, x.dtype),
        scratch_shapes=[
            pltpu.VMEM((D_MODEL, IN_WIDTH), bf16),
            pltpu.VMEM((D_MODEL, D_MODEL), bf16),
            pltpu.VMEM((1, IN_WIDTH), f32),
            pltpu.VMEM((W_SLOTS, W_CHUNK, IN_WIDTH), f32),
            pltpu.VMEM((W_SLOTS, W_CHUNK, D_MODEL), f32),
            pltpu.SemaphoreType.DMA((W_SLOTS,)),
            pltpu.SemaphoreType.DMA((W_SLOTS,)),
            pltpu.VMEM((seq_tile, ATTN_WIDTH), bf16),
            pltpu.VMEM((N_KV_HEADS, KV_WIDTH, BLOCK + seq_tile), bf16),
            pltpu.VMEM((N_KV_HEADS, BLOCK + seq_tile, KV_WIDTH), bf16),
            pltpu.VMEM((seq_tile, SGU_WIDTH), f32),
            pltpu.VMEM((seq_tile, SGU_WIDTH), bf16),
            pltpu.VMEM((seq_tile, SGU_WIDTH), f32),
            pltpu.VMEM((seq_tile, D_MODEL), bf16),
            pltpu.VMEM((N_SGU_HEADS // 2, BLOCK, 2 * BLOCK), bf16),
            pltpu.VMEM((BLOCK, SGU_WIDTH), f32),
        ],
        compiler_params=pltpu.CompilerParams(
            dimension_semantics=("arbitrary", "arbitrary"),
            vmem_limit_bytes=V7X_VMEM_LIMIT_BYTES),
        name="hybrid_layer",
    )(sinks, x, norm_g, w_in, b_in, ln_g, ln_b, sgu_w, sgu_b, w_out, b_out, final_g)


def kernel(x, norm_g, w_in, b_in, attn_sinks, sgu_ln_g, sgu_ln_b, sgu_w, sgu_b, w_out, b_out, final_norm_g):
    depth = norm_g.shape[0]
    for l in range(depth):
        last = l == depth - 1
        assert last, "the fused call applies the final norm; only depth 1 is supported"
        x = _layer_call(
            x, attn_sinks[l], norm_g[l][None, :], w_in[l], b_in[l][None, :],
            sgu_ln_g[l][None, :], sgu_ln_b[l][None, :], sgu_w[l], sgu_b[l],
            w_out[l], b_out[l][None, :], final_norm_g[None, :])
    return x
```

```python
import collections
import functools

import jax
import jax.numpy as jnp
from jax import lax
from jax.experimental import pallas as pl
from jax.experimental.pallas import tpu as pltpu

D_MODEL = 1024
HEAD_DIM = 64
ATTN_WIDTH = 512
KV_WIDTH = 128
SGU_WIDTH = 512
N_KV_HEADS = 2
Q_PER_KV = 4
N_SGU_HEADS = 8
BLOCK = 128
NORM_EPS = 1e-5
NEG_INF = -1e30
ATTN_SCALE = HEAD_DIM ** -0.5

OFF_Q = 0
OFF_K = OFF_Q + ATTN_WIDTH
OFF_V = OFF_K + KV_WIDTH
OFF_ZA = OFF_V + KV_WIDTH
OFF_U = OFF_ZA + ATTN_WIDTH
OFF_VS = OFF_U + SGU_WIDTH
OFF_ZS = OFF_VS + SGU_WIDTH
IN_WIDTH = OFF_ZS + SGU_WIDTH

SEQ_TILE = 1024
MXU_TILE = 256
ROW_CHUNK = 256
W_CHUNK = 64
W_SLOTS = 8
V7X_VMEM_LIMIT_BYTES = 56 * 1024 * 1024

_SQRT_TWO = 1.4142135623730951

_WeightCopy = collections.namedtuple("_WeightCopy", "hbm stage sem dst col_scale")


def _silu_of_half(hz):
    return hz * (1.0 + jnp.tanh(hz))


def _gelu_of_half(hz):
    return hz * (1.0 + lax.erf(hz * _SQRT_TWO))


def _layer_kernel(sinks_ref, x_ref, ng_ref, win_hbm, bin_ref, lng_ref, lnb_ref,
                  sw_ref, sb_ref, wout_hbm, bout_ref, fg_ref, o_ref,
                  win_ref, wout_ref, bsc_s, stage_in, stage_out, sem_in, sem_out,
                  q_s, kd_s, vd_s, u_s, vl_s, gs_s, mix_s, wp_s, sb_s, *, seq_tile):
    f32, bf16 = jnp.float32, jnp.bfloat16
    n_sub = seq_tile // BLOCK
    b_idx = pl.program_id(0)
    s_idx = pl.program_id(1)

    lane = lax.broadcasted_iota(jnp.int32, (BLOCK, BLOCK), 1)
    low_half = lane < HEAD_DIM

    @pl.when((b_idx == 0) & (s_idx == 0))
    def _():
        row = lax.broadcasted_iota(jnp.int32, (BLOCK, BLOCK), 0)
        for p in range(N_SGU_HEADS // 2):
            for half in range(2):
                w = jnp.where(row >= lane, sw_ref[2 * p + half], 0.0)
                wp_s[p, :, half * BLOCK:(half + 1) * BLOCK] = w.astype(bf16)
        sbt = sb_ref[...].T
        for hh in range(N_SGU_HEADS):
            sb_s[:, hh * HEAD_DIM:(hh + 1) * HEAD_DIM] = jnp.broadcast_to(
                sbt[:, hh:hh + 1], (BLOCK, HEAD_DIM))
        col = lax.broadcasted_iota(jnp.int32, (1, IN_WIDTH), 1)
        col_scale = jnp.where(col < OFF_K, ATTN_SCALE, jnp.where(col < OFF_ZA, 1.0, 0.5)).astype(f32)
        bsc_s[...] = bin_ref[...] * col_scale
        jobs = (_WeightCopy(win_hbm, stage_in, sem_in, win_ref, col_scale),
                _WeightCopy(wout_hbm, stage_out, sem_out, wout_ref, None))
        n_chunks = D_MODEL // W_CHUNK

        def chunk_copy(job, c):
            return pltpu.make_async_copy(job.hbm.at[pl.ds(c * W_CHUNK, W_CHUNK), :],
                                         job.stage.at[c % W_SLOTS], job.sem.at[c % W_SLOTS])

        for job in jobs:
            for c in range(W_SLOTS):
                chunk_copy(job, c).start(priority=c % 2)
        for c in range(n_chunks):
            for job in jobs:
                chunk_copy(job, c).wait()
                w = job.stage[c % W_SLOTS]
                if job.col_scale is not None:
                    w = w * job.col_scale
                job.dst[c * W_CHUNK:(c + 1) * W_CHUNK, :] = w.astype(bf16)
                if c + W_SLOTS < n_chunks:
                    chunk_copy(job, c + W_SLOTS).start(priority=c % 2)

    @pl.when(s_idx == 0)
    def _():
        for hh in range(N_KV_HEADS):
            kd_s[hh, 0:BLOCK, :] = jnp.zeros((BLOCK, KV_WIDTH), bf16)
            vd_s[hh, 0:BLOCK, :] = jnp.zeros((BLOCK, KV_WIDTH), bf16)

    @pl.when(s_idx > 0)
    def _():
        for hh in range(N_KV_HEADS):
            kd_s[hh, 0:BLOCK, :] = kd_s[hh, seq_tile:seq_tile + BLOCK, :]
            vd_s[hh, 0:BLOCK, :] = vd_s[hh, seq_tile:seq_tile + BLOCK, :]

    h_chunks = []
    for r0 in range(0, seq_tile, ROW_CHUNK):
        x = x_ref[0, r0:r0 + ROW_CHUNK, :]
        ms = jnp.mean(x * x, axis=-1, keepdims=True)
        hc = ((x * lax.rsqrt(ms + NORM_EPS)) * ng_ref[...]).astype(bf16)
        h_chunks.append(hc)
        q = jnp.dot(hc, win_ref[:, OFF_Q:OFF_Q + ATTN_WIDTH], preferred_element_type=f32)
        q_s[r0:r0 + ROW_CHUNK, :] = (q + bsc_s[:, OFF_Q:OFF_Q + ATTN_WIDTH]).astype(bf16)
    h = jnp.concatenate(h_chunks, axis=0)
    half_t = seq_tile // 2

    def proj_tile(off):
        cols = slice(off, off + MXU_TILE)
        return jnp.concatenate(
            [jnp.dot(h[i * half_t:(i + 1) * half_t], win_ref[:, cols], preferred_element_type=f32)
             for i in range(2)], axis=0) + bsc_s[:, cols]

    kv = proj_tile(OFF_K)
    half_mask = lax.broadcasted_iota(jnp.int32, (seq_tile, KV_WIDTH), 1) < HEAD_DIM
    for dup_ref, off in ((kd_s, 0), (vd_s, KV_WIDTH)):
        t = kv[:, off:off + KV_WIDTH]
        t_sw = pltpu.roll(t, HEAD_DIM, axis=1)
        dup_ref[0, BLOCK:BLOCK + seq_tile, :] = jnp.where(half_mask, t, t_sw).astype(bf16)
        dup_ref[1, BLOCK:BLOCK + seq_tile, :] = jnp.where(half_mask, t_sw, t).astype(bf16)

    ga_t = [_silu_of_half(proj_tile(OFF_ZA))]

    row = lax.broadcasted_iota(jnp.int32, (BLOCK, BLOCK), 0)
    prev_side = lane > row
    prev_side4 = jnp.concatenate([prev_side] * Q_PER_KV, axis=0)
    has_prev = s_idx > 0
    units = [(j, hh) for j in range(n_sub) for hh in range(N_KV_HEADS)]
    scores = []
    for j, hh in units:
        rows = slice(j * BLOCK, (j + 1) * BLOCK)
        parts = []
        for g in range(Q_PER_KV):
            c = hh * (Q_PER_KV // 2) + g // 2
            qc = q_s[rows, c * BLOCK:(c + 1) * BLOCK]
            keep = low_half if g % 2 == 0 else jnp.logical_not(low_half)
            parts.append(jnp.where(keep, qc, jnp.zeros_like(qc)))
        qst = jnp.concatenate(parts, axis=0)
        kb = kd_s[hh, j * BLOCK:(j + 2) * BLOCK, :]
        sc = lax.dot_general(qst, kb, (((1,), (1,)), ((), ())),
                             preferred_element_type=f32)
        s_prev = sc[:, 0:BLOCK]
        if j == 0:
            s_prev = jnp.where(has_prev, s_prev, NEG_INF)
        scores.append(jnp.where(prev_side4, s_prev, sc[:, BLOCK:2 * BLOCK]))

    def attend(u):
        j, hh = units[u]
        sc = scores[u]
        probs, inv_denoms = [], []
        for g in range(Q_PER_KV):
            sg = sc[g * BLOCK:(g + 1) * BLOCK, :]
            sink = sinks_ref[hh * Q_PER_KV + g]
            m = jnp.maximum(jnp.max(sg, axis=-1, keepdims=True), sink)
            p = jnp.exp(sg - m)
            denom = jnp.sum(p, axis=-1, keepdims=True) + jnp.exp(sink - m)
            inv_denoms.append(1.0 / denom)
            pb = p.astype(bf16)
            zero = jnp.zeros_like(pb)
            probs.append(jnp.concatenate([jnp.where(prev_side, pb, zero),
                                          jnp.where(prev_side, zero, pb)], axis=1))
        probs = jnp.concatenate(probs, axis=0)
        vb = vd_s[hh, j * BLOCK:(j + 2) * BLOCK, :]
        o = jnp.dot(probs, vb, preferred_element_type=f32)
        o = [o[g * BLOCK:(g + 1) * BLOCK, :] * inv_denoms[g] for g in range(Q_PER_KV)]
        return [jnp.where(low_half, o[2 * c2], o[2 * c2 + 1]) for c2 in range(Q_PER_KV // 2)]

    attn_out = []
    n_slots = (IN_WIDTH - OFF_ZA) // MXU_TILE - 1
    units_after = [len(units) * (i + 1) // n_slots - len(units) * i // n_slots for i in range(n_slots)]
    slot = iter(units_after)

    def attend_some():
        for _ in range(next(slot)):
            attn_out.append(attend(len(attn_out)))

    for n in range(1, ATTN_WIDTH // MXU_TILE):
        ga_t.append(_silu_of_half(proj_tile(OFF_ZA + n * MXU_TILE)))
        attend_some()
    for n in range(SGU_WIDTH // MXU_TILE):
        u_s[:, n * MXU_TILE:(n + 1) * MXU_TILE] = _gelu_of_half(proj_tile(OFF_U + n * MXU_TILE))
        attend_some()
    vg = []
    for n in range(SGU_WIDTH // MXU_TILE):
        vg.append(_gelu_of_half(proj_tile(OFF_VS + n * MXU_TILE)))
        attend_some()
    vg = jnp.concatenate(vg, axis=1)
    mu = jnp.mean(vg, axis=-1, keepdims=True)
    vc = vg - mu
    var = jnp.mean(vc * vc, axis=-1, keepdims=True)
    vl_s[...] = ((vc * lax.rsqrt(var + NORM_EPS)) * lng_ref[...] + lnb_ref[...]).astype(bf16)
    for n in range(SGU_WIDTH // MXU_TILE):
        gs_s[:, n * MXU_TILE:(n + 1) * MXU_TILE] = _silu_of_half(proj_tile(OFF_ZS + n * MXU_TILE))
        attend_some()
    assert len(attn_out) == len(units)

    ga = jnp.concatenate(ga_t, axis=1)
    for u, (j, hh) in enumerate(units):
        rows = slice(j * BLOCK, (j + 1) * BLOCK)
        for c2 in range(Q_PER_KV // 2):
            cols = slice((hh * (Q_PER_KV // 2) + c2) * BLOCK, (hh * (Q_PER_KV // 2) + c2 + 1) * BLOCK)
            mix_s[rows, cols] = (attn_out[u][c2] * ga[rows, cols]).astype(bf16)

    for j in range(n_sub):
        rows = slice(j * BLOCK, (j + 1) * BLOCK)
        for p_idx in range(N_SGU_HEADS // 2):
            cols = slice(p_idx * BLOCK, (p_idx + 1) * BLOCK)
            vp = vl_s[rows, cols]
            zero = jnp.zeros_like(vp)
            rhs = jnp.concatenate([jnp.where(low_half, vp, zero),
                                   jnp.where(low_half, zero, vp)], axis=0)
            mixed = jnp.dot(wp_s[p_idx], rhs, preferred_element_type=f32) + sb_s[:, cols]
            mix_s[rows, ATTN_WIDTH + p_idx * BLOCK:ATTN_WIDTH + (p_idx + 1) * BLOCK] = (
                (u_s[rows, cols] * mixed) * gs_s[rows, cols]).astype(bf16)

    y = x_ref[0] + jnp.dot(mix_s[...], wout_ref[...], preferred_element_type=f32) + bout_ref[...]
    ms2 = jnp.mean(y * y, axis=-1, keepdims=True)
    o_ref[0] = (y * lax.rsqrt(ms2 + NORM_EPS)) * fg_ref[...]


def _layer_call(x, sinks, norm_g, w_in, b_in, ln_g, ln_b, sgu_w, sgu_b, w_out, b_out, final_g,
                *, seq_tile=SEQ_TILE):
    batch, seq, d_model = x.shape
    assert d_model == D_MODEL and seq % seq_tile == 0 and seq_tile % (2 * BLOCK) == 0
    f32, bf16 = jnp.float32, jnp.bfloat16

    def full(shape):
        return pl.BlockSpec(shape, lambda b, s: (0,) * len(shape))

    tile_spec = pl.BlockSpec((1, seq_tile, D_MODEL), lambda b, s: (b, s, 0))
    return pl.pallas_call(
        functools.partial(_layer_kernel, seq_tile=seq_tile),
        grid=(batch, seq // seq_tile),
        in_specs=[
            pl.BlockSpec(memory_space=pltpu.SMEM),
            tile_spec,
            full((1, D_MODEL)),
            pl.BlockSpec(memory_space=pl.ANY),
            full((1, IN_WIDTH)),
            full((1, SGU_WIDTH)),
            full((1, SGU_WIDTH)),
            full((N_SGU_HEADS, BLOCK, BLOCK)),
            full((N_SGU_HEADS, BLOCK)),
            pl.BlockSpec(memory_space=pl.ANY),
            full((1, D_MODEL)),
            full((1, D_MODEL)),
        ],
        out_specs=tile_spec,
        out_shape=jax.ShapeDtypeStruct(x.shape, x.dtype),
        scratch_shapes=[
            pltpu.VMEM((D_MODEL, IN_WIDTH), bf16),
            pltpu.VMEM((D_MODEL, D_MODEL), bf16),
            pltpu.VMEM((1, IN_WIDTH), f32),
            pltpu.VMEM((W_SLOTS, W_CHUNK, IN_WIDTH), f32),
            pltpu.VMEM((W_SLOTS, W_CHUNK, D_MODEL), f32),
            pltpu.SemaphoreType.DMA((W_SLOTS,)),
            pltpu.SemaphoreType.DMA((W_SLOTS,)),
            pltpu.VMEM((seq_tile, ATTN_WIDTH), bf16),
            pltpu.VMEM((N_KV_HEADS, BLOCK + seq_tile, KV_WIDTH), bf16),
            pltpu.VMEM((N_KV_HEADS, BLOCK + seq_tile, KV_WIDTH), bf16),
            pltpu.VMEM((seq_tile, SGU_WIDTH), f32),
            pltpu.VMEM((seq_tile, SGU_WIDTH), bf16),
            pltpu.VMEM((seq_tile, SGU_WIDTH), f32),
            pltpu.VMEM((seq_tile, D_MODEL), bf16),
            pltpu.VMEM((N_SGU_HEADS // 2, BLOCK, 2 * BLOCK), bf16),
            pltpu.VMEM((BLOCK, SGU_WIDTH), f32),
        ],
        compiler_params=pltpu.CompilerParams(
            dimension_semantics=("arbitrary", "arbitrary"),
            vmem_limit_bytes=V7X_VMEM_LIMIT_BYTES),
        name="hybrid_layer",
    )(sinks, x, norm_g, w_in, b_in, ln_g, ln_b, sgu_w, sgu_b, w_out, b_out, final_g)


def kernel(x, norm_g, w_in, b_in, attn_sinks, sgu_ln_g, sgu_ln_b, sgu_w, sgu_b, w_out, b_out, final_norm_g):
    depth = norm_g.shape[0]
    for l in range(depth):
        last = l == depth - 1
        assert last, "the fused call applies the final norm; only depth 1 is supported"
        x = _layer_call(
            x, attn_sinks[l], norm_g[l][None, :], w_in[l], b_in[l][None, :],
            sgu_ln_g[l][None, :], sgu_ln_b[l][None, :], sgu_w[l], sgu_b[l],
            w_out[l], b_out[l][None, :], final_norm_g[None, :])
    return x
```

```python
import collections
import functools

import jax
import jax.numpy as jnp
from jax import lax
from jax.experimental import pallas as pl
from jax.experimental.pallas import tpu as pltpu

D_MODEL = 1024
HEAD_DIM = 64
ATTN_WIDTH = 512
KV_WIDTH = 128
SGU_WIDTH = 512
N_KV_HEADS = 2
Q_PER_KV = 4
N_SGU_HEADS = 8
BLOCK = 128
NORM_EPS = 1e-5
NEG_INF = -1e30
ATTN_SCALE = HEAD_DIM ** -0.5

OFF_Q = 0
OFF_K = OFF_Q + ATTN_WIDTH
OFF_V = OFF_K + KV_WIDTH
OFF_ZA = OFF_V + KV_WIDTH
OFF_U = OFF_ZA + ATTN_WIDTH
OFF_VS = OFF_U + SGU_WIDTH
OFF_ZS = OFF_VS + SGU_WIDTH
IN_WIDTH = OFF_ZS + SGU_WIDTH

SEQ_TILE = 1024
MXU_TILE = 256
ROW_CHUNK = 128
W_CHUNK = 64
W_SLOTS = 8
V7X_VMEM_LIMIT_BYTES = 56 * 1024 * 1024

_SQRT_TWO = 1.4142135623730951

_WeightCopy = collections.namedtuple("_WeightCopy", "hbm stage sem dst col_scale")


def _silu_of_half(hz):
    return hz * (1.0 + jnp.tanh(hz))


def _gelu_of_half(hz):
    return hz * (1.0 + lax.erf(hz * _SQRT_TWO))


def _layer_kernel(sinks_ref, x_ref, ng_ref, win_hbm, bin_ref, lng_ref, lnb_ref,
                  sw_ref, sb_ref, wout_hbm, bout_ref, fg_ref, o_ref,
                  win_ref, wout_ref, bsc_s, stage_in, stage_out, sem_in, sem_out,
                  q_s, kd_s, vd_s, u_s, vl_s, gs_s, mix_s, wp_s, sb_s, *, seq_tile):
    f32, bf16 = jnp.float32, jnp.bfloat16
    n_sub = seq_tile // BLOCK
    b_idx = pl.program_id(0)
    s_idx = pl.program_id(1)

    lane = lax.broadcasted_iota(jnp.int32, (BLOCK, BLOCK), 1)
    low_half = lane < HEAD_DIM

    @pl.when((b_idx == 0) & (s_idx == 0))
    def _():
        row = lax.broadcasted_iota(jnp.int32, (BLOCK, BLOCK), 0)
        for p in range(N_SGU_HEADS // 2):
            for half in range(2):
                w = jnp.where(row >= lane, sw_ref[2 * p + half], 0.0)
                wp_s[p, :, half * BLOCK:(half + 1) * BLOCK] = w.astype(bf16)
        sbt = sb_ref[...].T
        for hh in range(N_SGU_HEADS):
            sb_s[:, hh * HEAD_DIM:(hh + 1) * HEAD_DIM] = jnp.broadcast_to(
                sbt[:, hh:hh + 1], (BLOCK, HEAD_DIM))
        col = lax.broadcasted_iota(jnp.int32, (1, IN_WIDTH), 1)
        col_scale = jnp.where(col < OFF_K, ATTN_SCALE, jnp.where(col < OFF_ZA, 1.0, 0.5)).astype(f32)
        bsc_s[...] = bin_ref[...] * col_scale
        jobs = (_WeightCopy(win_hbm, stage_in, sem_in, win_ref, col_scale),
                _WeightCopy(wout_hbm, stage_out, sem_out, wout_ref, None))
        n_chunks = D_MODEL // W_CHUNK

        def chunk_copy(job, c):
            return pltpu.make_async_copy(job.hbm.at[pl.ds(c * W_CHUNK, W_CHUNK), :],
                                         job.stage.at[c % W_SLOTS], job.sem.at[c % W_SLOTS])

        for job in jobs:
            for c in range(W_SLOTS):
                chunk_copy(job, c).start(priority=c % 2)
        for job in jobs:
            for c in range(n_chunks):
                chunk_copy(job, c).wait()
                w = job.stage[c % W_SLOTS]
                if job.col_scale is not None:
                    w = w * job.col_scale
                job.dst[c * W_CHUNK:(c + 1) * W_CHUNK, :] = w.astype(bf16)
                if c + W_SLOTS < n_chunks:
                    chunk_copy(job, c + W_SLOTS).start(priority=c % 2)

    @pl.when(s_idx == 0)
    def _():
        for hh in range(N_KV_HEADS):
            kd_s[hh, 0:BLOCK, :] = jnp.zeros((BLOCK, KV_WIDTH), bf16)
            vd_s[hh, 0:BLOCK, :] = jnp.zeros((BLOCK, KV_WIDTH), bf16)

    @pl.when(s_idx > 0)
    def _():
        for hh in range(N_KV_HEADS):
            kd_s[hh, 0:BLOCK, :] = kd_s[hh, seq_tile:seq_tile + BLOCK, :]
            vd_s[hh, 0:BLOCK, :] = vd_s[hh, seq_tile:seq_tile + BLOCK, :]

    h_chunks = []
    for r0 in range(0, seq_tile, ROW_CHUNK):
        x = x_ref[0, r0:r0 + ROW_CHUNK, :]
        ms = jnp.mean(x * x, axis=-1, keepdims=True)
        hc = ((x * lax.rsqrt(ms + NORM_EPS)) * ng_ref[...]).astype(bf16)
        h_chunks.append(hc)
        q = jnp.dot(hc, win_ref[:, OFF_Q:OFF_Q + ATTN_WIDTH], preferred_element_type=f32)
        q_s[r0:r0 + ROW_CHUNK, :] = (q + bsc_s[:, OFF_Q:OFF_Q + ATTN_WIDTH]).astype(bf16)
    h = jnp.concatenate(h_chunks, axis=0)
    half_t = seq_tile // 2

    def proj_tile(off):
        cols = slice(off, off + MXU_TILE)
        return jnp.concatenate(
            [jnp.dot(h[i * half_t:(i + 1) * half_t], win_ref[:, cols], preferred_element_type=f32)
             for i in range(2)], axis=0) + bsc_s[:, cols]

    kv = proj_tile(OFF_K)
    half_mask = lax.broadcasted_iota(jnp.int32, (seq_tile, KV_WIDTH), 1) < HEAD_DIM
    for dup_ref, off in ((kd_s, 0), (vd_s, KV_WIDTH)):
        t = kv[:, off:off + KV_WIDTH]
        t_sw = pltpu.roll(t, HEAD_DIM, axis=1)
        dup_ref[0, BLOCK:BLOCK + seq_tile, :] = jnp.where(half_mask, t, t_sw).astype(bf16)
        dup_ref[1, BLOCK:BLOCK + seq_tile, :] = jnp.where(half_mask, t_sw, t).astype(bf16)

    ga_t = [_silu_of_half(proj_tile(OFF_ZA))]

    row = lax.broadcasted_iota(jnp.int32, (BLOCK, BLOCK), 0)
    prev_side = lane > row
    prev_side4 = jnp.concatenate([prev_side] * Q_PER_KV, axis=0)
    has_prev = s_idx > 0
    units = [(j, hh) for j in range(n_sub) for hh in range(N_KV_HEADS)]
    scores = []
    for j, hh in units:
        rows = slice(j * BLOCK, (j + 1) * BLOCK)
        parts = []
        for g in range(Q_PER_KV):
            c = hh * (Q_PER_KV // 2) + g // 2
            qc = q_s[rows, c * BLOCK:(c + 1) * BLOCK]
            keep = low_half if g % 2 == 0 else jnp.logical_not(low_half)
            parts.append(jnp.where(keep, qc, jnp.zeros_like(qc)))
        qst = jnp.concatenate(parts, axis=0)
        kb = kd_s[hh, j * BLOCK:(j + 2) * BLOCK, :]
        sc = lax.dot_general(qst, kb, (((1,), (1,)), ((), ())),
                             preferred_element_type=f32)
        s_prev = sc[:, 0:BLOCK]
        if j == 0:
            s_prev = jnp.where(has_prev, s_prev, NEG_INF)
        scores.append(jnp.where(prev_side4, s_prev, sc[:, BLOCK:2 * BLOCK]))

    def attend(u):
        j, hh = units[u]
        sc = scores[u]
        probs, inv_denoms = [], []
        for g in range(Q_PER_KV):
            sg = sc[g * BLOCK:(g + 1) * BLOCK, :]
            sink = sinks_ref[hh * Q_PER_KV + g]
            m = jnp.maximum(jnp.max(sg, axis=-1, keepdims=True), sink)
            p = jnp.exp(sg - m)
            denom = jnp.sum(p, axis=-1, keepdims=True) + jnp.exp(sink - m)
            inv_denoms.append(1.0 / denom)
            pb = p.astype(bf16)
            zero = jnp.zeros_like(pb)
            probs.append(jnp.concatenate([jnp.where(prev_side, pb, zero),
                                          jnp.where(prev_side, zero, pb)], axis=1))
        probs = jnp.concatenate(probs, axis=0)
        vb = vd_s[hh, j * BLOCK:(j + 2) * BLOCK, :]
        o = jnp.dot(probs, vb, preferred_element_type=f32)
        o = [o[g * BLOCK:(g + 1) * BLOCK, :] * inv_denoms[g] for g in range(Q_PER_KV)]
        return [jnp.where(low_half, o[2 * c2], o[2 * c2 + 1]) for c2 in range(Q_PER_KV // 2)]

    attn_out = []
    n_slots = (IN_WIDTH - OFF_ZA) // MXU_TILE - 1
    units_after = [len(units) * (i + 1) // n_slots - len(units) * i // n_slots for i in range(n_slots)]
    slot = iter(units_after)

    def attend_some():
        for _ in range(next(slot)):
            attn_out.append(attend(len(attn_out)))

    for n in range(1, ATTN_WIDTH // MXU_TILE):
        ga_t.append(_silu_of_half(proj_tile(OFF_ZA + n * MXU_TILE)))
        attend_some()
    for n in range(SGU_WIDTH // MXU_TILE):
        u_s[:, n * MXU_TILE:(n + 1) * MXU_TILE] = _gelu_of_half(proj_tile(OFF_U + n * MXU_TILE))
        attend_some()
    vg = []
    for n in range(SGU_WIDTH // MXU_TILE):
        vg.append(_gelu_of_half(proj_tile(OFF_VS + n * MXU_TILE)))
        attend_some()
    vg = jnp.concatenate(vg, axis=1)
    mu = jnp.mean(vg, axis=-1, keepdims=True)
    vc = vg - mu
    var = jnp.mean(vc * vc, axis=-1, keepdims=True)
    vl_s[...] = ((vc * lax.rsqrt(var + NORM_EPS)) * lng_ref[...] + lnb_ref[...]).astype(bf16)
    for n in range(SGU_WIDTH // MXU_TILE):
        gs_s[:, n * MXU_TILE:(n + 1) * MXU_TILE] = _silu_of_half(proj_tile(OFF_ZS + n * MXU_TILE))
        attend_some()
    assert len(attn_out) == len(units)

    ga = jnp.concatenate(ga_t, axis=1)
    for u, (j, hh) in enumerate(units):
        rows = slice(j * BLOCK, (j + 1) * BLOCK)
        for c2 in range(Q_PER_KV // 2):
            cols = slice((hh * (Q_PER_KV // 2) + c2) * BLOCK, (hh * (Q_PER_KV // 2) + c2 + 1) * BLOCK)
            mix_s[rows, cols] = (attn_out[u][c2] * ga[rows, cols]).astype(bf16)

    for j in range(n_sub):
        rows = slice(j * BLOCK, (j + 1) * BLOCK)
        for p_idx in range(N_SGU_HEADS // 2):
            cols = slice(p_idx * BLOCK, (p_idx + 1) * BLOCK)
            vp = vl_s[rows, cols]
            zero = jnp.zeros_like(vp)
            rhs = jnp.concatenate([jnp.where(low_half, vp, zero),
                                   jnp.where(low_half, zero, vp)], axis=0)
            mixed = jnp.dot(wp_s[p_idx], rhs, preferred_element_type=f32) + sb_s[:, cols]
            mix_s[rows, ATTN_WIDTH + p_idx * BLOCK:ATTN_WIDTH + (p_idx + 1) * BLOCK] = (
                (u_s[rows, cols] * mixed) * gs_s[rows, cols]).astype(bf16)

    y = x_ref[0] + jnp.dot(mix_s[...], wout_ref[...], preferred_element_type=f32) + bout_ref[...]
    ms2 = jnp.mean(y * y, axis=-1, keepdims=True)
    o_ref[0] = (y * lax.rsqrt(ms2 + NORM_EPS)) * fg_ref[...]


def _layer_call(x, sinks, norm_g, w_in, b_in, ln_g, ln_b, sgu_w, sgu_b, w_out, b_out, final_g,
                *, seq_tile=SEQ_TILE):
    batch, seq, d_model = x.shape
    assert d_model == D_MODEL and seq % seq_tile == 0 and seq_tile % (2 * BLOCK) == 0
    f32, bf16 = jnp.float32, jnp.bfloat16

    def full(shape):
        return pl.BlockSpec(shape, lambda b, s: (0,) * len(shape))

    tile_spec = pl.BlockSpec((1, seq_tile, D_MODEL), lambda b, s: (b, s, 0))
    return pl.pallas_call(
        functools.partial(_layer_kernel, seq_tile=seq_tile),
        grid=(batch, seq // seq_tile),
        in_specs=[
            pl.BlockSpec(memory_space=pltpu.SMEM),
            tile_spec,
            full((1, D_MODEL)),
            pl.BlockSpec(memory_space=pl.ANY),
            full((1, IN_WIDTH)),
            full((1, SGU_WIDTH)),
            full((1, SGU_WIDTH)),
            full((N_SGU_HEADS, BLOCK, BLOCK)),
            full((N_SGU_HEADS, BLOCK)),
            pl.BlockSpec(memory_space=pl.ANY),
            full((1, D_MODEL)),
            full((1, D_MODEL)),
        ],
        out_specs=tile_spec,
        out_shape=jax.ShapeDtypeStruct(x.shape, x.dtype),
        scratch_shapes=[
            pltpu.VMEM((D_MODEL, IN_WIDTH), bf16),
            pltpu.VMEM((D_MODEL, D_MODEL), bf16),
            pltpu.VMEM((1, IN_WIDTH), f32),
            pltpu.VMEM((W_SLOTS, W_CHUNK, IN_WIDTH), f32),
            pltpu.VMEM((W_SLOTS, W_CHUNK, D_MODEL), f32),
            pltpu.SemaphoreType.DMA((W_SLOTS,)),
            pltpu.SemaphoreType.DMA((W_SLOTS,)),
            pltpu.VMEM((seq_tile, ATTN_WIDTH), bf16),
            pltpu.VMEM((N_KV_HEADS, BLOCK + seq_tile, KV_WIDTH), bf16),
            pltpu.VMEM((N_KV_HEADS, BLOCK + seq_tile, KV_WIDTH), bf16),
            pltpu.VMEM((seq_tile, SGU_WIDTH), f32),
            pltpu.VMEM((seq_tile, SGU_WIDTH), bf16),
            pltpu.VMEM((seq_tile, SGU_WIDTH), f32),
            pltpu.VMEM((seq_tile, D_MODEL), bf16),
            pltpu.VMEM((N_SGU_HEADS // 2, BLOCK, 2 * BLOCK), bf16),
            pltpu.VMEM((BLOCK, SGU_WIDTH), f32),
        ],
        compiler_params=pltpu.CompilerParams(
            dimension_semantics=("arbitrary", "arbitrary"),
            vmem_limit_bytes=V7X_VMEM_LIMIT_BYTES),
        name="hybrid_layer",
    )(sinks, x, norm_g, w_in, b_in, ln_g, ln_b, sgu_w, sgu_b, w_out, b_out, final_g)


def kernel(x, norm_g, w_in, b_in, attn_sinks, sgu_ln_g, sgu_ln_b, sgu_w, sgu_b, w_out, b_out, final_norm_g):
    depth = norm_g.shape[0]
    for l in range(depth):
        last = l == depth - 1
        assert last, "the fused call applies the final norm; only depth 1 is supported"
        x = _layer_call(
            x, attn_sinks[l], norm_g[l][None, :], w_in[l], b_in[l][None, :],
            sgu_ln_g[l][None, :], sgu_ln_b[l][None, :], sgu_w[l], sgu_b[l],
            w_out[l], b_out[l][None, :], final_norm_g[None, :])
    return x
```

```python
import collections
import functools

import jax
import jax.numpy as jnp
from jax import lax
from jax.experimental import pallas as pl
from jax.experimental.pallas import tpu as pltpu

D_MODEL = 1024
HEAD_DIM = 64
ATTN_WIDTH = 512
KV_WIDTH = 128
SGU_WIDTH = 512
N_KV_HEADS = 2
Q_PER_KV = 4
N_SGU_HEADS = 8
BLOCK = 128
NORM_EPS = 1e-5
NEG_INF = -1e30
ATTN_SCALE = HEAD_DIM ** -0.5

OFF_Q = 0
OFF_K = OFF_Q + ATTN_WIDTH
OFF_V = OFF_K + KV_WIDTH
OFF_ZA = OFF_V + KV_WIDTH
OFF_U = OFF_ZA + ATTN_WIDTH
OFF_VS = OFF_U + SGU_WIDTH
OFF_ZS = OFF_VS + SGU_WIDTH
IN_WIDTH = OFF_ZS + SGU_WIDTH

SEQ_TILE = 1024
MXU_TILE = 256
ROW_CHUNK = 256
W_CHUNK = 64
W_SLOTS = 8
V7X_VMEM_LIMIT_BYTES = 56 * 1024 * 1024

_SQRT_TWO = 1.4142135623730951

_WeightCopy = collections.namedtuple("_WeightCopy", "hbm stage sem dst col_scale")


def _silu_of_half(hz):
    return hz * (1.0 + jnp.tanh(hz))


def _gelu_of_half(hz):
    return hz * (1.0 + lax.erf(hz * _SQRT_TWO))


def _layer_kernel(sinks_ref, x_ref, ng_ref, win_hbm, bin_ref, lng_ref, lnb_ref,
                  sw_ref, sb_ref, wout_hbm, bout_ref, fg_ref, o_ref,
                  win_ref, wout_ref, bsc_s, stage_in, stage_out, sem_in, sem_out,
                  q_s, kd_s, vd_s, u_s, vl_s, gs_s, mix_s, wp_s, sb_s, *, seq_tile):
    f32, bf16 = jnp.float32, jnp.bfloat16
    n_sub = seq_tile // BLOCK
    b_idx = pl.program_id(0)
    s_idx = pl.program_id(1)

    lane = lax.broadcasted_iota(jnp.int32, (BLOCK, BLOCK), 1)
    low_half = lane < HEAD_DIM

    @pl.when((b_idx == 0) & (s_idx == 0))
    def _():
        row = lax.broadcasted_iota(jnp.int32, (BLOCK, BLOCK), 0)
        for p in range(N_SGU_HEADS // 2):
            for half in range(2):
                w = jnp.where(row >= lane, sw_ref[2 * p + half], 0.0)
                wp_s[p, :, half * BLOCK:(half + 1) * BLOCK] = w.astype(bf16)
        sbt = sb_ref[...].T
        for hh in range(N_SGU_HEADS):
            sb_s[:, hh * HEAD_DIM:(hh + 1) * HEAD_DIM] = jnp.broadcast_to(
                sbt[:, hh:hh + 1], (BLOCK, HEAD_DIM))
        col = lax.broadcasted_iota(jnp.int32, (1, IN_WIDTH), 1)
        col_scale = jnp.where(col < OFF_K, ATTN_SCALE, jnp.where(col < OFF_ZA, 1.0, 0.5)).astype(f32)
        bsc_s[...] = bin_ref[...] * col_scale
        for dup_ref in (kd_s, vd_s):
            for hh in range(N_KV_HEADS):
                dup_ref[hh, seq_tile:seq_tile + BLOCK, :] = jnp.zeros((BLOCK, KV_WIDTH), bf16)
        jobs = (_WeightCopy(win_hbm, stage_in, sem_in, win_ref, col_scale),
                _WeightCopy(wout_hbm, stage_out, sem_out, wout_ref, None))
        n_chunks = D_MODEL // W_CHUNK

        def chunk_copy(job, c):
            return pltpu.make_async_copy(job.hbm.at[pl.ds(c * W_CHUNK, W_CHUNK), :],
                                         job.stage.at[c % W_SLOTS], job.sem.at[c % W_SLOTS])

        for job in jobs:
            for c in range(W_SLOTS):
                chunk_copy(job, c).start(priority=c % 2)
        for job in jobs:
            for c in range(n_chunks):
                chunk_copy(job, c).wait()
                w = job.stage[c % W_SLOTS]
                if job.col_scale is not None:
                    w = w * job.col_scale
                job.dst[c * W_CHUNK:(c + 1) * W_CHUNK, :] = w.astype(bf16)
                if c + W_SLOTS < n_chunks:
                    chunk_copy(job, c + W_SLOTS).start(priority=c % 2)

    for dup_ref in (kd_s, vd_s):
        for hh in range(N_KV_HEADS):
            tail = dup_ref[hh, seq_tile:seq_tile + BLOCK, :]
            dup_ref[hh, 0:BLOCK, :] = jnp.where(s_idx > 0, tail, jnp.zeros_like(tail))

    h_chunks = []
    for r0 in range(0, seq_tile, ROW_CHUNK):
        x = x_ref[0, r0:r0 + ROW_CHUNK, :]
        ms = jnp.mean(x * x, axis=-1, keepdims=True)
        hc = ((x * lax.rsqrt(ms + NORM_EPS)) * ng_ref[...]).astype(bf16)
        h_chunks.append(hc)
        q = jnp.dot(hc, win_ref[:, OFF_Q:OFF_Q + ATTN_WIDTH], preferred_element_type=f32)
        q_s[r0:r0 + ROW_CHUNK, :] = (q + bsc_s[:, OFF_Q:OFF_Q + ATTN_WIDTH]).astype(bf16)
    h = jnp.concatenate(h_chunks, axis=0)
    half_t = seq_tile // 2

    def proj_tile(off):
        cols = slice(off, off + MXU_TILE)
        return jnp.concatenate(
            [jnp.dot(h[i * half_t:(i + 1) * half_t], win_ref[:, cols], preferred_element_type=f32)
             for i in range(2)], axis=0) + bsc_s[:, cols]

    kv = proj_tile(OFF_K)
    half_mask = lax.broadcasted_iota(jnp.int32, (seq_tile, KV_WIDTH), 1) < HEAD_DIM
    for dup_ref, off in ((kd_s, 0), (vd_s, KV_WIDTH)):
        t = kv[:, off:off + KV_WIDTH]
        t_sw = pltpu.roll(t, HEAD_DIM, axis=1)
        dup_ref[0, BLOCK:BLOCK + seq_tile, :] = jnp.where(half_mask, t, t_sw).astype(bf16)
        dup_ref[1, BLOCK:BLOCK + seq_tile, :] = jnp.where(half_mask, t_sw, t).astype(bf16)

    ga_t = [_silu_of_half(proj_tile(OFF_ZA))]

    row = lax.broadcasted_iota(jnp.int32, (BLOCK, BLOCK), 0)
    prev_side = lane > row
    prev_side4 = jnp.concatenate([prev_side] * Q_PER_KV, axis=0)
    has_prev = s_idx > 0
    units = [(j, hh) for j in range(n_sub) for hh in range(N_KV_HEADS)]
    scores = []
    for j, hh in units:
        rows = slice(j * BLOCK, (j + 1) * BLOCK)
        parts = []
        for g in range(Q_PER_KV):
            c = hh * (Q_PER_KV // 2) + g // 2
            qc = q_s[rows, c * BLOCK:(c + 1) * BLOCK]
            keep = low_half if g % 2 == 0 else jnp.logical_not(low_half)
            parts.append(jnp.where(keep, qc, jnp.zeros_like(qc)))
        qst = jnp.concatenate(parts, axis=0)
        kb = kd_s[hh, j * BLOCK:(j + 2) * BLOCK, :]
        sc = lax.dot_general(qst, kb, (((1,), (1,)), ((), ())),
                             preferred_element_type=f32)
        s_prev = sc[:, 0:BLOCK]
        if j == 0:
            s_prev = jnp.where(has_prev, s_prev, NEG_INF)
        scores.append(jnp.where(prev_side4, s_prev, sc[:, BLOCK:2 * BLOCK]))

    def attend(u):
        j, hh = units[u]
        sc = scores[u]
        probs, inv_denoms = [], []
        for g in range(Q_PER_KV):
            sg = sc[g * BLOCK:(g + 1) * BLOCK, :]
            sink = sinks_ref[hh * Q_PER_KV + g]
            m = jnp.maximum(jnp.max(sg, axis=-1, keepdims=True), sink)
            p = jnp.exp(sg - m)
            denom = jnp.sum(p, axis=-1, keepdims=True) + jnp.exp(sink - m)
            inv_denoms.append(1.0 / denom)
            pb = p.astype(bf16)
            zero = jnp.zeros_like(pb)
            probs.append(jnp.concatenate([jnp.where(prev_side, pb, zero),
                                          jnp.where(prev_side, zero, pb)], axis=1))
        probs = jnp.concatenate(probs, axis=0)
        vb = vd_s[hh, j * BLOCK:(j + 2) * BLOCK, :]
        o = jnp.dot(probs, vb, preferred_element_type=f32)
        o = [o[g * BLOCK:(g + 1) * BLOCK, :] * inv_denoms[g] for g in range(Q_PER_KV)]
        return [jnp.where(low_half, o[2 * c2], o[2 * c2 + 1]) for c2 in range(Q_PER_KV // 2)]

    attn_out = []
    n_slots = (IN_WIDTH - OFF_ZA) // MXU_TILE - 1
    units_after = [len(units) * (i + 1) // n_slots - len(units) * i // n_slots for i in range(n_slots)]
    slot = iter(units_after)

    def attend_some():
        for _ in range(next(slot)):
            attn_out.append(attend(len(attn_out)))

    for n in range(1, ATTN_WIDTH // MXU_TILE):
        ga_t.append(_silu_of_half(proj_tile(OFF_ZA + n * MXU_TILE)))
        attend_some()
    for n in range(SGU_WIDTH // MXU_TILE):
        u_s[:, n * MXU_TILE:(n + 1) * MXU_TILE] = _gelu_of_half(proj_tile(OFF_U + n * MXU_TILE))
        attend_some()
    vg = []
    for n in range(SGU_WIDTH // MXU_TILE):
        vg.append(_gelu_of_half(proj_tile(OFF_VS + n * MXU_TILE)))
        attend_some()
    vg = jnp.concatenate(vg, axis=1)
    mu = jnp.mean(vg, axis=-1, keepdims=True)
    vc = vg - mu
    var = jnp.mean(vc * vc, axis=-1, keepdims=True)
    vl_s[...] = ((vc * lax.rsqrt(var + NORM_EPS)) * lng_ref[...] + lnb_ref[...]).astype(bf16)
    for n in range(SGU_WIDTH // MXU_TILE):
        gs_s[:, n * MXU_TILE:(n + 1) * MXU_TILE] = _silu_of_half(proj_tile(OFF_ZS + n * MXU_TILE))
        attend_some()
    assert len(attn_out) == len(units)

    ga = jnp.concatenate(ga_t, axis=1)
    for u, (j, hh) in enumerate(units):
        rows = slice(j * BLOCK, (j + 1) * BLOCK)
        for c2 in range(Q_PER_KV // 2):
            cols = slice((hh * (Q_PER_KV // 2) + c2) * BLOCK, (hh * (Q_PER_KV // 2) + c2 + 1) * BLOCK)
            mix_s[rows, cols] = (attn_out[u][c2] * ga[rows, cols]).astype(bf16)

    for j in range(n_sub):
        rows = slice(j * BLOCK, (j + 1) * BLOCK)
        for p_idx in range(N_SGU_HEADS // 2):
            cols = slice(p_idx * BLOCK, (p_idx + 1) * BLOCK)
            vp = vl_s[rows, cols]
            zero = jnp.zeros_like(vp)
            rhs = jnp.concatenate([jnp.where(low_half, vp, zero),
                                   jnp.where(low_half, zero, vp)], axis=0)
            mixed = jnp.dot(wp_s[p_idx], rhs, preferred_element_type=f32) + sb_s[:, cols]
            mix_s[rows, ATTN_WIDTH + p_idx * BLOCK:ATTN_WIDTH + (p_idx + 1) * BLOCK] = (
                (u_s[rows, cols] * mixed) * gs_s[rows, cols]).astype(bf16)

    y = x_ref[0] + jnp.dot(mix_s[...], wout_ref[...], preferred_element_type=f32) + bout_ref[...]
    ms2 = jnp.mean(y * y, axis=-1, keepdims=True)
    o_ref[0] = (y * lax.rsqrt(ms2 + NORM_EPS)) * fg_ref[...]


def _layer_call(x, sinks, norm_g, w_in, b_in, ln_g, ln_b, sgu_w, sgu_b, w_out, b_out, final_g,
                *, seq_tile=SEQ_TILE):
    batch, seq, d_model = x.shape
    assert d_model == D_MODEL and seq % seq_tile == 0 and seq_tile % (2 * BLOCK) == 0
    f32, bf16 = jnp.float32, jnp.bfloat16

    def full(shape):
        return pl.BlockSpec(shape, lambda b, s: (0,) * len(shape))

    tile_spec = pl.BlockSpec((1, seq_tile, D_MODEL), lambda b, s: (b, s, 0))
    return pl.pallas_call(
        functools.partial(_layer_kernel, seq_tile=seq_tile),
        grid=(batch, seq // seq_tile),
        in_specs=[
            pl.BlockSpec(memory_space=pltpu.SMEM),
            tile_spec,
            full((1, D_MODEL)),
            pl.BlockSpec(memory_space=pl.ANY),
            full((1, IN_WIDTH)),
            full((1, SGU_WIDTH)),
            full((1, SGU_WIDTH)),
            full((N_SGU_HEADS, BLOCK, BLOCK)),
            full((N_SGU_HEADS, BLOCK)),
            pl.BlockSpec(memory_space=pl.ANY),
            full((1, D_MODEL)),
            full((1, D_MODEL)),
        ],
        out_specs=tile_spec,
        out_shape=jax.ShapeDtypeStruct(x.shape, x.dtype),
        scratch_shapes=[
            pltpu.VMEM((D_MODEL, IN_WIDTH), bf16),
            pltpu.VMEM((D_MODEL, D_MODEL), bf16),
            pltpu.VMEM((1, IN_WIDTH), f32),
            pltpu.VMEM((W_SLOTS, W_CHUNK, IN_WIDTH), f32),
            pltpu.VMEM((W_SLOTS, W_CHUNK, D_MODEL), f32),
            pltpu.SemaphoreType.DMA((W_SLOTS,)),
            pltpu.SemaphoreType.DMA((W_SLOTS,)),
            pltpu.VMEM((seq_tile, ATTN_WIDTH), bf16),
            pltpu.VMEM((N_KV_HEADS, BLOCK + seq_tile, KV_WIDTH), bf16),
            pltpu.VMEM((N_KV_HEADS, BLOCK + seq_tile, KV_WIDTH), bf16),
            pltpu.VMEM((seq_tile, SGU_WIDTH), f32),
            pltpu.VMEM((seq_tile, SGU_WIDTH), bf16),
            pltpu.VMEM((seq_tile, SGU_WIDTH), f32),
            pltpu.VMEM((seq_tile, D_MODEL), bf16),
            pltpu.VMEM((N_SGU_HEADS // 2, BLOCK, 2 * BLOCK), bf16),
            pltpu.VMEM((BLOCK, SGU_WIDTH), f32),
        ],
        compiler_params=pltpu.CompilerParams(
            dimension_semantics=("arbitrary", "arbitrary"),
            vmem_limit_bytes=V7X_VMEM_LIMIT_BYTES),
        name="hybrid_layer",
    )(sinks, x, norm_g, w_in, b_in, ln_g, ln_b, sgu_w, sgu_b, w_out, b_out, final_g)


def kernel(x, norm_g, w_in, b_in, attn_sinks, sgu_ln_g, sgu_ln_b, sgu_w, sgu_b, w_out, b_out, final_norm_g):
    depth = norm_g.shape[0]
    for l in range(depth):
        last = l == depth - 1
        assert last, "the fused call applies the final norm; only depth 1 is supported"
        x = _layer_call(
            x, attn_sinks[l], norm_g[l][None, :], w_in[l], b_in[l][None, :],
            sgu_ln_g[l][None, :], sgu_ln_b[l][None, :], sgu_w[l], sgu_b[l],
            w_out[l], b_out[l][None, :], final_norm_g[None, :])
    return x
```

```python
import collections
import functools

import jax
import jax.numpy as jnp
from jax import lax
from jax.experimental import pallas as pl
from jax.experimental.pallas import tpu as pltpu

D_MODEL = 1024
HEAD_DIM = 64
ATTN_WIDTH = 512
KV_WIDTH = 128
SGU_WIDTH = 512
N_KV_HEADS = 2
Q_PER_KV = 4
N_SGU_HEADS = 8
BLOCK = 128
NORM_EPS = 1e-5
NEG_INF = -1e30
ATTN_SCALE = HEAD_DIM ** -0.5

OFF_Q = 0
OFF_K = OFF_Q + ATTN_WIDTH
OFF_V = OFF_K + KV_WIDTH
OFF_ZA = OFF_V + KV_WIDTH
OFF_U = OFF_ZA + ATTN_WIDTH
OFF_VS = OFF_U + SGU_WIDTH
OFF_ZS = OFF_VS + SGU_WIDTH
IN_WIDTH = OFF_ZS + SGU_WIDTH

SEQ_TILE = 1024
MXU_TILE = 256
ROW_CHUNK = 256
W_CHUNK = 64
W_SLOTS = 8
V7X_VMEM_LIMIT_BYTES = 56 * 1024 * 1024

_SQRT_TWO = 1.4142135623730951

_WeightCopy = collections.namedtuple("_WeightCopy", "hbm stage sem dst col_scale")


def _silu_of_half(hz):
    return hz * (1.0 + jnp.tanh(hz))


def _gelu_of_half(hz):
    return hz * (1.0 + lax.erf(hz * _SQRT_TWO))


def _layer_kernel(sinks_ref, x_ref, ng_ref, win_hbm, bin_ref, lng_ref, lnb_ref,
                  sw_ref, sb_ref, wout_hbm, bout_ref, fg_ref, o_ref,
                  win_ref, wout_ref, bsc_s, stage_in, stage_out, sem_in, sem_out,
                  q_s, kd_s, vd_s, u_s, vl_s, gs_s, mix_s, wp_s, sb_s, *, seq_tile):
    f32, bf16 = jnp.float32, jnp.bfloat16
    n_sub = seq_tile // BLOCK
    b_idx = pl.program_id(0)
    s_idx = pl.program_id(1)

    lane = lax.broadcasted_iota(jnp.int32, (BLOCK, BLOCK), 1)
    low_half = lane < HEAD_DIM

    @pl.when((b_idx == 0) & (s_idx == 0))
    def _():
        row = lax.broadcasted_iota(jnp.int32, (BLOCK, BLOCK), 0)
        for p in range(N_SGU_HEADS // 2):
            for half in range(2):
                w = jnp.where(row >= lane, sw_ref[2 * p + half], 0.0)
                wp_s[p, :, half * BLOCK:(half + 1) * BLOCK] = w.astype(bf16)
        sbt = sb_ref[...].T
        for hh in range(N_SGU_HEADS):
            sb_s[:, hh * HEAD_DIM:(hh + 1) * HEAD_DIM] = jnp.broadcast_to(
                sbt[:, hh:hh + 1], (BLOCK, HEAD_DIM))
        col = lax.broadcasted_iota(jnp.int32, (1, IN_WIDTH), 1)
        col_scale = jnp.where(col < OFF_K, ATTN_SCALE, jnp.where(col < OFF_ZA, 1.0, 0.5)).astype(f32)
        bsc_s[...] = bin_ref[...] * col_scale
        jobs = (_WeightCopy(win_hbm, stage_in, sem_in, win_ref, col_scale),
                _WeightCopy(wout_hbm, stage_out, sem_out, wout_ref, None))
        n_chunks = D_MODEL // W_CHUNK

        def chunk_copy(job, c):
            return pltpu.make_async_copy(job.hbm.at[pl.ds(c * W_CHUNK, W_CHUNK), :],
                                         job.stage.at[c % W_SLOTS], job.sem.at[c % W_SLOTS])

        for job in jobs:
            for c in range(W_SLOTS):
                chunk_copy(job, c).start(priority=c % 2)
        for job in jobs:
            for c in range(n_chunks):
                chunk_copy(job, c).wait()
                w = job.stage[c % W_SLOTS]
                if job.col_scale is not None:
                    w = w * job.col_scale
                job.dst[c * W_CHUNK:(c + 1) * W_CHUNK, :] = w.astype(bf16)
                if c + W_SLOTS < n_chunks:
                    chunk_copy(job, c + W_SLOTS).start(priority=c % 2)

    @pl.when(s_idx == 0)
    def _():
        for hh in range(N_KV_HEADS):
            kd_s[hh, 0:BLOCK, :] = jnp.zeros((BLOCK, KV_WIDTH), bf16)
            vd_s[hh, 0:BLOCK, :] = jnp.zeros((BLOCK, KV_WIDTH), bf16)

    @pl.when(s_idx > 0)
    def _():
        for hh in range(N_KV_HEADS):
            kd_s[hh, 0:BLOCK, :] = kd_s[hh, seq_tile:seq_tile + BLOCK, :]
            vd_s[hh, 0:BLOCK, :] = vd_s[hh, seq_tile:seq_tile + BLOCK, :]

    h_chunks = []
    for r0 in range(0, seq_tile, ROW_CHUNK):
        x = x_ref[0, r0:r0 + ROW_CHUNK, :]
        ms = jnp.mean(x * x, axis=-1, keepdims=True)
        hc = ((x * lax.rsqrt(ms + NORM_EPS)) * ng_ref[...]).astype(bf16)
        h_chunks.append(hc)
        q = jnp.dot(hc, win_ref[:, OFF_Q:OFF_Q + ATTN_WIDTH], preferred_element_type=f32)
        q_s[r0:r0 + ROW_CHUNK, :] = (q + bsc_s[:, OFF_Q:OFF_Q + ATTN_WIDTH]).astype(bf16)
    h = jnp.concatenate(h_chunks, axis=0)
    half_t = seq_tile // 2

    def proj_tile(off):
        cols = slice(off, off + MXU_TILE)
        return jnp.concatenate(
            [jnp.dot(h[i * half_t:(i + 1) * half_t], win_ref[:, cols], preferred_element_type=f32)
             for i in range(2)], axis=0) + bsc_s[:, cols]

    kv = proj_tile(OFF_K)
    half_mask = lax.broadcasted_iota(jnp.int32, (seq_tile, KV_WIDTH), 1) < HEAD_DIM
    for dup_ref, off in ((kd_s, 0), (vd_s, KV_WIDTH)):
        t = kv[:, off:off + KV_WIDTH]
        t_sw = pltpu.roll(t, HEAD_DIM, axis=1)
        dup_ref[0, BLOCK:BLOCK + seq_tile, :] = jnp.where(half_mask, t, t_sw).astype(bf16)
        dup_ref[1, BLOCK:BLOCK + seq_tile, :] = jnp.where(half_mask, t_sw, t).astype(bf16)

    ga_t = [_silu_of_half(proj_tile(OFF_ZA))]

    row = lax.broadcasted_iota(jnp.int32, (BLOCK, BLOCK), 0)
    prev_side = lane > row
    prev_side4 = jnp.concatenate([prev_side] * Q_PER_KV, axis=0)
    has_prev = s_idx > 0
    units = [(j, hh) for j in range(n_sub) for hh in range(N_KV_HEADS)]
    scores = []
    for j, hh in units:
        rows = slice(j * BLOCK, (j + 1) * BLOCK)
        parts = []
        for g in range(Q_PER_KV):
            c = hh * (Q_PER_KV // 2) + g // 2
            qc = q_s[rows, c * BLOCK:(c + 1) * BLOCK]
            keep = low_half if g % 2 == 0 else jnp.logical_not(low_half)
            parts.append(jnp.where(keep, qc, jnp.zeros_like(qc)))
        qst = jnp.concatenate(parts, axis=0)
        kb = kd_s[hh, j * BLOCK:(j + 2) * BLOCK, :]
        sc = lax.dot_general(qst, kb, (((1,), (1,)), ((), ())),
                             preferred_element_type=f32)
        s_prev = sc[:, 0:BLOCK]
        if j == 0:
            s_prev = jnp.where(has_prev, s_prev, NEG_INF)
        scores.append(jnp.where(prev_side4, s_prev, sc[:, BLOCK:2 * BLOCK]))

    def attend(u):
        j, hh = units[u]
        sc = scores[u]
        probs, inv_denoms = [], []
        for g in range(Q_PER_KV):
            sg = sc[g * BLOCK:(g + 1) * BLOCK, :]
            sink = sinks_ref[hh * Q_PER_KV + g]
            m = jnp.maximum(jnp.max(sg, axis=-1, keepdims=True), sink)
            p = jnp.exp(sg - m)
            denom = jnp.sum(p, axis=-1, keepdims=True) + jnp.exp(sink - m)
            inv_denoms.append(1.0 / denom)
            pb = p.astype(bf16)
            zero = jnp.zeros_like(pb)
            probs.append(jnp.concatenate([jnp.where(prev_side, pb, zero),
                                          jnp.where(prev_side, zero, pb)], axis=1))
        probs = jnp.concatenate(probs, axis=0)
        vb = vd_s[hh, j * BLOCK:(j + 2) * BLOCK, :]
        o = jnp.dot(probs, vb, preferred_element_type=f32)
        o = [o[g * BLOCK:(g + 1) * BLOCK, :] * inv_denoms[g] for g in range(Q_PER_KV)]
        return [jnp.where(low_half, o[2 * c2], o[2 * c2 + 1]) for c2 in range(Q_PER_KV // 2)]

    attn_out = []
    n_slots = (IN_WIDTH - OFF_ZA) // MXU_TILE - 1
    bounds = [round(len(units) * i / n_slots) for i in range(n_slots + 1)]
    units_after = [bounds[i + 1] - bounds[i] for i in range(n_slots)]
    slot = iter(units_after)

    def attend_some():
        for _ in range(next(slot)):
            attn_out.append(attend(len(attn_out)))

    for n in range(1, ATTN_WIDTH // MXU_TILE):
        ga_t.append(_silu_of_half(proj_tile(OFF_ZA + n * MXU_TILE)))
        attend_some()
    for n in range(SGU_WIDTH // MXU_TILE):
        u_s[:, n * MXU_TILE:(n + 1) * MXU_TILE] = _gelu_of_half(proj_tile(OFF_U + n * MXU_TILE))
        attend_some()
    vg = []
    for n in range(SGU_WIDTH // MXU_TILE):
        vg.append(_gelu_of_half(proj_tile(OFF_VS + n * MXU_TILE)))
        attend_some()
    vg = jnp.concatenate(vg, axis=1)
    mu = jnp.mean(vg, axis=-1, keepdims=True)
    vc = vg - mu
    var = jnp.mean(vc * vc, axis=-1, keepdims=True)
    vl_s[...] = ((vc * lax.rsqrt(var + NORM_EPS)) * lng_ref[...] + lnb_ref[...]).astype(bf16)
    for n in range(SGU_WIDTH // MXU_TILE):
        gs_s[:, n * MXU_TILE:(n + 1) * MXU_TILE] = _silu_of_half(proj_tile(OFF_ZS + n * MXU_TILE))
        attend_some()
    assert len(attn_out) == len(units)

    ga = jnp.concatenate(ga_t, axis=1)
    for u, (j, hh) in enumerate(units):
        rows = slice(j * BLOCK, (j + 1) * BLOCK)
        for c2 in range(Q_PER_KV // 2):
            cols = slice((hh * (Q_PER_KV // 2) + c2) * BLOCK, (hh * (Q_PER_KV // 2) + c2 + 1) * BLOCK)
            mix_s[rows, cols] = (attn_out[u][c2] * ga[rows, cols]).astype(bf16)

    for j in range(n_sub):
        rows = slice(j * BLOCK, (j + 1) * BLOCK)
        for p_idx in range(N_SGU_HEADS // 2):
            cols = slice(p_idx * BLOCK, (p_idx + 1) * BLOCK)
            vp = vl_s[rows, cols]
            zero = jnp.zeros_like(vp)
            rhs = jnp.concatenate([jnp.where(low_half, vp, zero),
                                   jnp.where(low_half, zero, vp)], axis=0)
            mixed = jnp.dot(wp_s[p_idx], rhs, preferred_element_type=f32) + sb_s[:, cols]
            mix_s[rows, ATTN_WIDTH + p_idx * BLOCK:ATTN_WIDTH + (p_idx + 1) * BLOCK] = (
                (u_s[rows, cols] * mixed) * gs_s[rows, cols]).astype(bf16)

    y = x_ref[0] + jnp.dot(mix_s[...], wout_ref[...], preferred_element_type=f32) + bout_ref[...]
    ms2 = jnp.mean(y * y, axis=-1, keepdims=True)
    o_ref[0] = (y * lax.rsqrt(ms2 + NORM_EPS)) * fg_ref[...]


def _layer_call(x, sinks, norm_g, w_in, b_in, ln_g, ln_b, sgu_w, sgu_b, w_out, b_out, final_g,
                *, seq_tile=SEQ_TILE):
    batch, seq, d_model = x.shape
    assert d_model == D_MODEL and seq % seq_tile == 0 and seq_tile % (2 * BLOCK) == 0
    f32, bf16 = jnp.float32, jnp.bfloat16

    def full(shape):
        return pl.BlockSpec(shape, lambda b, s: (0,) * len(shape))

    tile_spec = pl.BlockSpec((1, seq_tile, D_MODEL), lambda b, s: (b, s, 0))
    return pl.pallas_call(
        functools.partial(_layer_kernel, seq_tile=seq_tile),
        grid=(batch, seq // seq_tile),
        in_specs=[
            pl.BlockSpec(memory_space=pltpu.SMEM),
            tile_spec,
            full((1, D_MODEL)),
            pl.BlockSpec(memory_space=pl.ANY),
            full((1, IN_WIDTH)),
            full((1, SGU_WIDTH)),
            full((1, SGU_WIDTH)),
            full((N_SGU_HEADS, BLOCK, BLOCK)),
            full((N_SGU_HEADS, BLOCK)),
            pl.BlockSpec(memory_space=pl.ANY),
            full((1, D_MODEL)),
            full((1, D_MODEL)),
        ],
        out_specs=tile_spec,
        out_shape=jax.ShapeDtypeStruct(x.shape, x.dtype),
        scratch_shapes=[
            pltpu.VMEM((D_MODEL, IN_WIDTH), bf16),
            pltpu.VMEM((D_MODEL, D_MODEL), bf16),
            pltpu.VMEM((1, IN_WIDTH), f32),
            pltpu.VMEM((W_SLOTS, W_CHUNK, IN_WIDTH), f32),
            pltpu.VMEM((W_SLOTS, W_CHUNK, D_MODEL), f32),
            pltpu.SemaphoreType.DMA((W_SLOTS,)),
            pltpu.SemaphoreType.DMA((W_SLOTS,)),
            pltpu.VMEM((seq_tile, ATTN_WIDTH), bf16),
            pltpu.VMEM((N_KV_HEADS, BLOCK + seq_tile, KV_WIDTH), bf16),
            pltpu.VMEM((N_KV_HEADS, BLOCK + seq_tile, KV_WIDTH), bf16),
            pltpu.VMEM((seq_tile, SGU_WIDTH), f32),
            pltpu.VMEM((seq_tile, SGU_WIDTH), bf16),
            pltpu.VMEM((seq_tile, SGU_WIDTH), f32),
            pltpu.VMEM((seq_tile, D_MODEL), bf16),
            pltpu.VMEM((N_SGU_HEADS // 2, BLOCK, 2 * BLOCK), bf16),
            pltpu.VMEM((BLOCK, SGU_WIDTH), f32),
        ],
        compiler_params=pltpu.CompilerParams(
            dimension_semantics=("arbitrary", "arbitrary"),
            vmem_limit_bytes=V7X_VMEM_LIMIT_BYTES),
        name="hybrid_layer",
    )(sinks, x, norm_g, w_in, b_in, ln_g, ln_b, sgu_w, sgu_b, w_out, b_out, final_g)


def kernel(x, norm_g, w_in, b_in, attn_sinks, sgu_ln_g, sgu_ln_b, sgu_w, sgu_b, w_out, b_out, final_norm_g):
    depth = norm_g.shape[0]
    for l in range(depth):
        last = l == depth - 1
        assert last, "the fused call applies the final norm; only depth 1 is supported"
        x = _layer_call(
            x, attn_sinks[l], norm_g[l][None, :], w_in[l], b_in[l][None, :],
            sgu_ln_g[l][None, :], sgu_ln_b[l][None, :], sgu_w[l], sgu_b[l],
            w_out[l], b_out[l][None, :], final_norm_g[None, :])
    return x
```

```python
import collections
import functools

import jax
import jax.numpy as jnp
from jax import lax
from jax.experimental import pallas as pl
from jax.experimental.pallas import tpu as pltpu

D_MODEL = 1024
HEAD_DIM = 64
ATTN_WIDTH = 512
KV_WIDTH = 128
SGU_WIDTH = 512
N_KV_HEADS = 2
Q_PER_KV = 4
N_SGU_HEADS = 8
BLOCK = 128
NORM_EPS = 1e-5
NEG_INF = -1e30
ATTN_SCALE = HEAD_DIM ** -0.5

OFF_Q = 0
OFF_K = OFF_Q + ATTN_WIDTH
OFF_V = OFF_K + KV_WIDTH
OFF_ZA = OFF_V + KV_WIDTH
OFF_U = OFF_ZA + ATTN_WIDTH
OFF_VS = OFF_U + SGU_WIDTH
OFF_ZS = OFF_VS + SGU_WIDTH
IN_WIDTH = OFF_ZS + SGU_WIDTH

SEQ_TILE = 1024
MXU_TILE = 256
ROW_CHUNK = 256
W_CHUNK = 64
W_SLOTS = 8
V7X_VMEM_LIMIT_BYTES = 56 * 1024 * 1024

_SQRT_TWO = 1.4142135623730951

_WeightCopy = collections.namedtuple("_WeightCopy", "hbm stage sem dst col_scale")


def _silu_of_half(hz):
    return hz * (1.0 + jnp.tanh(hz))


def _gelu_of_half(hz):
    return hz * (1.0 + lax.erf(hz * _SQRT_TWO))


def _layer_kernel(sinks_ref, x_ref, ng_ref, win_hbm, bin_ref, lng_ref, lnb_ref,
                  sw_ref, sb_ref, wout_hbm, bout_ref, fg_ref, o_ref,
                  win_ref, wout_ref, bsc_s, stage_in, stage_out, sem_in, sem_out,
                  q_s, kd_s, vd_s, u_s, vl_s, gs_s, mix_s, wp_s, sb_s, *, seq_tile):
    f32, bf16 = jnp.float32, jnp.bfloat16
    n_sub = seq_tile // BLOCK
    b_idx = pl.program_id(0)
    s_idx = pl.program_id(1)

    lane = lax.broadcasted_iota(jnp.int32, (BLOCK, BLOCK), 1)
    low_half = lane < HEAD_DIM

    @pl.when((b_idx == 0) & (s_idx == 0))
    def _():
        row = lax.broadcasted_iota(jnp.int32, (BLOCK, BLOCK), 0)
        for p in range(N_SGU_HEADS // 2):
            for half in range(2):
                w = jnp.where(row >= lane, sw_ref[2 * p + half], 0.0)
                wp_s[p, :, half * BLOCK:(half + 1) * BLOCK] = w.astype(bf16)
        sbt = sb_ref[...].T
        for hh in range(N_SGU_HEADS):
            sb_s[:, hh * HEAD_DIM:(hh + 1) * HEAD_DIM] = jnp.broadcast_to(
                sbt[:, hh:hh + 1], (BLOCK, HEAD_DIM))
        col = lax.broadcasted_iota(jnp.int32, (1, IN_WIDTH), 1)
        col_scale = jnp.where(col < OFF_K, ATTN_SCALE, jnp.where(col < OFF_ZA, 1.0, 0.5)).astype(f32)
        bsc_s[...] = bin_ref[...] * col_scale
        jobs = (_WeightCopy(win_hbm, stage_in, sem_in, win_ref, col_scale),
                _WeightCopy(wout_hbm, stage_out, sem_out, wout_ref, None))
        n_chunks = D_MODEL // W_CHUNK

        def chunk_copy(job, c):
            return pltpu.make_async_copy(job.hbm.at[pl.ds(c * W_CHUNK, W_CHUNK), :],
                                         job.stage.at[c % W_SLOTS], job.sem.at[c % W_SLOTS])

        for job in jobs:
            for c in range(W_SLOTS):
                chunk_copy(job, c).start(priority=c % 2)
        for job in jobs:
            for c in range(n_chunks):
                chunk_copy(job, c).wait()
                w = job.stage[c % W_SLOTS]
                if job.col_scale is not None:
                    w = w * job.col_scale
                job.dst[c * W_CHUNK:(c + 1) * W_CHUNK, :] = w.astype(bf16)
                if c + W_SLOTS < n_chunks:
                    chunk_copy(job, c + W_SLOTS).start(priority=c % 2)

    @pl.when(s_idx == 0)
    def _():
        for hh in range(N_KV_HEADS):
            kd_s[hh, 0:BLOCK, :] = jnp.zeros((BLOCK, KV_WIDTH), bf16)
            vd_s[hh, 0:BLOCK, :] = jnp.zeros((BLOCK, KV_WIDTH), bf16)

    @pl.when(s_idx > 0)
    def _():
        for hh in range(N_KV_HEADS):
            kd_s[hh, 0:BLOCK, :] = kd_s[hh, seq_tile:seq_tile + BLOCK, :]
            vd_s[hh, 0:BLOCK, :] = vd_s[hh, seq_tile:seq_tile + BLOCK, :]

    h_chunks = []
    for r0 in range(0, seq_tile, ROW_CHUNK):
        x = x_ref[0, r0:r0 + ROW_CHUNK, :]
        ms = jnp.mean(x * x, axis=-1, keepdims=True)
        hc = ((x * lax.rsqrt(ms + NORM_EPS)) * ng_ref[...]).astype(bf16)
        h_chunks.append(hc)
        q = jnp.dot(hc, win_ref[:, OFF_Q:OFF_Q + ATTN_WIDTH], preferred_element_type=f32)
        q_s[r0:r0 + ROW_CHUNK, :] = (q + bsc_s[:, OFF_Q:OFF_Q + ATTN_WIDTH]).astype(bf16)
    h = jnp.concatenate(h_chunks, axis=0)
    half_t = seq_tile // 2

    def proj_tile(off):
        cols = slice(off, off + MXU_TILE)
        return jnp.concatenate(
            [jnp.dot(h[i * half_t:(i + 1) * half_t], win_ref[:, cols], preferred_element_type=f32)
             for i in range(2)], axis=0) + bsc_s[:, cols]

    kv = proj_tile(OFF_K)
    half_mask = lax.broadcasted_iota(jnp.int32, (seq_tile, KV_WIDTH), 1) < HEAD_DIM
    for dup_ref, off in ((kd_s, 0), (vd_s, KV_WIDTH)):
        t = kv[:, off:off + KV_WIDTH]
        t_sw = pltpu.roll(t, HEAD_DIM, axis=1)
        dup_ref[0, BLOCK:BLOCK + seq_tile, :] = jnp.where(half_mask, t, t_sw).astype(bf16)
        dup_ref[1, BLOCK:BLOCK + seq_tile, :] = jnp.where(half_mask, t_sw, t).astype(bf16)

    ga_t = [_silu_of_half(proj_tile(OFF_ZA))]

    row = lax.broadcasted_iota(jnp.int32, (BLOCK, BLOCK), 0)
    prev_side = lane > row
    prev_side4 = jnp.concatenate([prev_side] * Q_PER_KV, axis=0)
    has_prev = s_idx > 0
    units = [(j, hh) for j in range(n_sub) for hh in range(N_KV_HEADS)]
    scores = []
    for j, hh in units:
        rows = slice(j * BLOCK, (j + 1) * BLOCK)
        parts = []
        for g in range(Q_PER_KV):
            c = hh * (Q_PER_KV // 2) + g // 2
            qc = q_s[rows, c * BLOCK:(c + 1) * BLOCK]
            keep = low_half if g % 2 == 0 else jnp.logical_not(low_half)
            parts.append(jnp.where(keep, qc, jnp.zeros_like(qc)))
        qst = jnp.concatenate(parts, axis=0)
        kb = kd_s[hh, j * BLOCK:(j + 2) * BLOCK, :]
        sc = lax.dot_general(qst, kb, (((1,), (1,)), ((), ())),
                             preferred_element_type=f32)
        s_prev = sc[:, 0:BLOCK]
        if j == 0:
            s_prev = jnp.where(has_prev, s_prev, NEG_INF)
        scores.append(jnp.where(prev_side4, s_prev, sc[:, BLOCK:2 * BLOCK]))

    def attend(u):
        j, hh = units[u]
        sc = scores[u]
        probs, inv_denoms = [], []
        for g in range(Q_PER_KV):
            sg = sc[g * BLOCK:(g + 1) * BLOCK, :]
            sink = sinks_ref[hh * Q_PER_KV + g]
            m = jnp.maximum(jnp.max(sg, axis=-1, keepdims=True), sink)
            p = jnp.exp(sg - m)
            denom = jnp.sum(p, axis=-1, keepdims=True) + jnp.exp(sink - m)
            inv_denoms.append(1.0 / denom)
            pb = p.astype(bf16)
            zero = jnp.zeros_like(pb)
            probs.append(jnp.concatenate([jnp.where(prev_side, pb, zero),
                                          jnp.where(prev_side, zero, pb)], axis=1))
        probs = jnp.concatenate(probs, axis=0)
        vb = vd_s[hh, j * BLOCK:(j + 2) * BLOCK, :]
        o = jnp.dot(probs, vb, preferred_element_type=f32)
        o = [o[g * BLOCK:(g + 1) * BLOCK, :] * inv_denoms[g] for g in range(Q_PER_KV)]
        return [jnp.where(low_half, o[2 * c2], o[2 * c2 + 1]) for c2 in range(Q_PER_KV // 2)]

    attn_out = []
    n_slots = (IN_WIDTH - OFF_ZA) // MXU_TILE - 1
    units_after = [2, 2, 2, 2, 3, 2, 3]
    assert sum(units_after) == len(units) and len(units_after) == n_slots
    slot = iter(units_after)

    def attend_some():
        for _ in range(next(slot)):
            attn_out.append(attend(len(attn_out)))

    for n in range(1, ATTN_WIDTH // MXU_TILE):
        ga_t.append(_silu_of_half(proj_tile(OFF_ZA + n * MXU_TILE)))
        attend_some()
    for n in range(SGU_WIDTH // MXU_TILE):
        u_s[:, n * MXU_TILE:(n + 1) * MXU_TILE] = _gelu_of_half(proj_tile(OFF_U + n * MXU_TILE))
        attend_some()
    vg = []
    for n in range(SGU_WIDTH // MXU_TILE):
        vg.append(_gelu_of_half(proj_tile(OFF_VS + n * MXU_TILE)))
        attend_some()
    vg = jnp.concatenate(vg, axis=1)
    mu = jnp.mean(vg, axis=-1, keepdims=True)
    vc = vg - mu
    var = jnp.mean(vc * vc, axis=-1, keepdims=True)
    vl_s[...] = ((vc * lax.rsqrt(var + NORM_EPS)) * lng_ref[...] + lnb_ref[...]).astype(bf16)
    for n in range(SGU_WIDTH // MXU_TILE):
        gs_s[:, n * MXU_TILE:(n + 1) * MXU_TILE] = _silu_of_half(proj_tile(OFF_ZS + n * MXU_TILE))
        attend_some()
    assert len(attn_out) == len(units)

    ga = jnp.concatenate(ga_t, axis=1)
    for u, (j, hh) in enumerate(units):
        rows = slice(j * BLOCK, (j + 1) * BLOCK)
        for c2 in range(Q_PER_KV // 2):
            cols = slice((hh * (Q_PER_KV // 2) + c2) * BLOCK, (hh * (Q_PER_KV // 2) + c2 + 1) * BLOCK)
            mix_s[rows, cols] = (attn_out[u][c2] * ga[rows, cols]).astype(bf16)

    for j in range(n_sub):
        rows = slice(j * BLOCK, (j + 1) * BLOCK)
        for p_idx in range(N_SGU_HEADS // 2):
            cols = slice(p_idx * BLOCK, (p_idx + 1) * BLOCK)
            vp = vl_s[rows, cols]
            zero = jnp.zeros_like(vp)
            rhs = jnp.concatenate([jnp.where(low_half, vp, zero),
                                   jnp.where(low_half, zero, vp)], axis=0)
            mixed = jnp.dot(wp_s[p_idx], rhs, preferred_element_type=f32) + sb_s[:, cols]
            mix_s[rows, ATTN_WIDTH + p_idx * BLOCK:ATTN_WIDTH + (p_idx + 1) * BLOCK] = (
                (u_s[rows, cols] * mixed) * gs_s[rows, cols]).astype(bf16)

    y = x_ref[0] + jnp.dot(mix_s[...], wout_ref[...], preferred_element_type=f32) + bout_ref[...]
    ms2 = jnp.mean(y * y, axis=-1, keepdims=True)
    o_ref[0] = (y * lax.rsqrt(ms2 + NORM_EPS)) * fg_ref[...]


def _layer_call(x, sinks, norm_g, w_in, b_in, ln_g, ln_b, sgu_w, sgu_b, w_out, b_out, final_g,
                *, seq_tile=SEQ_TILE):
    batch, seq, d_model = x.shape
    assert d_model == D_MODEL and seq % seq_tile == 0 and seq_tile % (2 * BLOCK) == 0
    f32, bf16 = jnp.float32, jnp.bfloat16

    def full(shape):
        return pl.BlockSpec(shape, lambda b, s: (0,) * len(shape))

    tile_spec = pl.BlockSpec((1, seq_tile, D_MODEL), lambda b, s: (b, s, 0))
    return pl.pallas_call(
        functools.partial(_layer_kernel, seq_tile=seq_tile),
        grid=(batch, seq // seq_tile),
        in_specs=[
            pl.BlockSpec(memory_space=pltpu.SMEM),
            tile_spec,
            full((1, D_MODEL)),
            pl.BlockSpec(memory_space=pl.ANY),
            full((1, IN_WIDTH)),
            full((1, SGU_WIDTH)),
            full((1, SGU_WIDTH)),
            full((N_SGU_HEADS, BLOCK, BLOCK)),
            full((N_SGU_HEADS, BLOCK)),
            pl.BlockSpec(memory_space=pl.ANY),
            full((1, D_MODEL)),
            full((1, D_MODEL)),
        ],
        out_specs=tile_spec,
        out_shape=jax.ShapeDtypeStruct(x.shape, x.dtype),
        scratch_shapes=[
            pltpu.VMEM((D_MODEL, IN_WIDTH), bf16),
            pltpu.VMEM((D_MODEL, D_MODEL), bf16),
            pltpu.VMEM((1, IN_WIDTH), f32),
            pltpu.VMEM((W_SLOTS, W_CHUNK, IN_WIDTH), f32),
            pltpu.VMEM((W_SLOTS, W_CHUNK, D_MODEL), f32),
            pltpu.SemaphoreType.DMA((W_SLOTS,)),
            pltpu.SemaphoreType.DMA((W_SLOTS,)),
            pltpu.VMEM((seq_tile, ATTN_WIDTH), bf16),
            pltpu.VMEM((N_KV_HEADS, BLOCK + seq_tile, KV_WIDTH), bf16),
            pltpu.VMEM((N_KV_HEADS, BLOCK + seq_tile, KV_WIDTH), bf16),
            pltpu.VMEM((seq_tile, SGU_WIDTH), f32),
            pltpu.VMEM((seq_tile, SGU_WIDTH), bf16),
            pltpu.VMEM((seq_tile, SGU_WIDTH), f32),
            pltpu.VMEM((seq_tile, D_MODEL), bf16),
            pltpu.VMEM((N_SGU_HEADS // 2, BLOCK, 2 * BLOCK), bf16),
            pltpu.VMEM((BLOCK, SGU_WIDTH), f32),
        ],
        compiler_params=pltpu.CompilerParams(
            dimension_semantics=("arbitrary", "arbitrary"),
            vmem_limit_bytes=V7X_VMEM_LIMIT_BYTES),
        name="hybrid_layer",
    )(sinks, x, norm_g, w_in, b_in, ln_g, ln_b, sgu_w, sgu_b, w_out, b_out, final_g)


def kernel(x, norm_g, w_in, b_in, attn_sinks, sgu_ln_g, sgu_ln_b, sgu_w, sgu_b, w_out, b_out, final_norm_g):
    depth = norm_g.shape[0]
    for l in range(depth):
        last = l == depth - 1
        assert last, "the fused call applies the final norm; only depth 1 is supported"
        x = _layer_call(
            x, attn_sinks[l], norm_g[l][None, :], w_in[l], b_in[l][None, :],
            sgu_ln_g[l][None, :], sgu_ln_b[l][None, :], sgu_w[l], sgu_b[l],
            w_out[l], b_out[l][None, :], final_norm_g[None, :])
    return x
```

```python
import collections
import functools

import jax
import jax.numpy as jnp
from jax import lax
from jax.experimental import pallas as pl
from jax.experimental.pallas import tpu as pltpu

D_MODEL = 1024
HEAD_DIM = 64
ATTN_WIDTH = 512
KV_WIDTH = 128
SGU_WIDTH = 512
N_KV_HEADS = 2
Q_PER_KV = 4
N_SGU_HEADS = 8
BLOCK = 128
NORM_EPS = 1e-5
NEG_INF = -1e30
ATTN_SCALE = HEAD_DIM ** -0.5

OFF_Q = 0
OFF_K = OFF_Q + ATTN_WIDTH
OFF_V = OFF_K + KV_WIDTH
OFF_ZA = OFF_V + KV_WIDTH
OFF_U = OFF_ZA + ATTN_WIDTH
OFF_VS = OFF_U + SGU_WIDTH
OFF_ZS = OFF_VS + SGU_WIDTH
IN_WIDTH = OFF_ZS + SGU_WIDTH

SEQ_TILE = 1024
MXU_TILE = 256
ROW_CHUNK = 256
W_CHUNK = 64
W_SLOTS = 8
V7X_VMEM_LIMIT_BYTES = 56 * 1024 * 1024

_SQRT_TWO = 1.4142135623730951

_WeightCopy = collections.namedtuple("_WeightCopy", "hbm stage sem dst col_scale")


def _silu_of_half(hz):
    return hz * (1.0 + jnp.tanh(hz))


def _gelu_of_half(hz):
    return hz * (1.0 + lax.erf(hz * _SQRT_TWO))


def _layer_kernel(sinks_ref, x_ref, ng_ref, win_hbm, bin_ref, lng_ref, lnb_ref,
                  sw_ref, sb_ref, wout_hbm, bout_ref, fg_ref, o_ref,
                  win_ref, wout_ref, bsc_s, stage_in, stage_out, sem_in, sem_out,
                  q_s, kd_s, vd_s, u_s, vl_s, gs_s, mix_s, wp_s, sb_s, *, seq_tile):
    f32, bf16 = jnp.float32, jnp.bfloat16
    n_sub = seq_tile // BLOCK
    b_idx = pl.program_id(0)
    s_idx = pl.program_id(1)

    lane = lax.broadcasted_iota(jnp.int32, (BLOCK, BLOCK), 1)
    low_half = lane < HEAD_DIM

    @pl.when((b_idx == 0) & (s_idx == 0))
    def _():
        row = lax.broadcasted_iota(jnp.int32, (BLOCK, BLOCK), 0)
        for p in range(N_SGU_HEADS // 2):
            for half in range(2):
                w = jnp.where(row >= lane, sw_ref[2 * p + half], 0.0)
                wp_s[p, :, half * BLOCK:(half + 1) * BLOCK] = w.astype(bf16)
        sbt = sb_ref[...].T
        for hh in range(N_SGU_HEADS):
            sb_s[:, hh * HEAD_DIM:(hh + 1) * HEAD_DIM] = jnp.broadcast_to(
                sbt[:, hh:hh + 1], (BLOCK, HEAD_DIM))
        col = lax.broadcasted_iota(jnp.int32, (1, IN_WIDTH), 1)
        col_scale = jnp.where(col < OFF_K, ATTN_SCALE, jnp.where(col < OFF_ZA, 1.0, 0.5)).astype(f32)
        bsc_s[...] = bin_ref[...] * col_scale
        jobs = (_WeightCopy(win_hbm, stage_in, sem_in, win_ref, col_scale),
                _WeightCopy(wout_hbm, stage_out, sem_out, wout_ref, None))
        n_chunks = D_MODEL // W_CHUNK

        def chunk_copy(job, c):
            return pltpu.make_async_copy(job.hbm.at[pl.ds(c * W_CHUNK, W_CHUNK), :],
                                         job.stage.at[c % W_SLOTS], job.sem.at[c % W_SLOTS])

        for job in jobs:
            for c in range(W_SLOTS):
                chunk_copy(job, c).start(priority=c % 2)
        for job in jobs:
            for c in range(n_chunks):
                chunk_copy(job, c).wait()
                w = job.stage[c % W_SLOTS]
                if job.col_scale is not None:
                    w = w * job.col_scale
                job.dst[c * W_CHUNK:(c + 1) * W_CHUNK, :] = w.astype(bf16)
                if c + W_SLOTS < n_chunks:
                    chunk_copy(job, c + W_SLOTS).start(priority=c % 2)

    @pl.when(s_idx == 0)
    def _():
        for hh in range(N_KV_HEADS):
            kd_s[hh, 0:BLOCK, :] = jnp.zeros((BLOCK, KV_WIDTH), bf16)
            vd_s[hh, 0:BLOCK, :] = jnp.zeros((BLOCK, KV_WIDTH), bf16)

    @pl.when(s_idx > 0)
    def _():
        for hh in range(N_KV_HEADS):
            kd_s[hh, 0:BLOCK, :] = kd_s[hh, seq_tile:seq_tile + BLOCK, :]
            vd_s[hh, 0:BLOCK, :] = vd_s[hh, seq_tile:seq_tile + BLOCK, :]

    h_chunks = []
    for r0 in range(0, seq_tile, ROW_CHUNK):
        x = x_ref[0, r0:r0 + ROW_CHUNK, :]
        ms = jnp.mean(x * x, axis=-1, keepdims=True)
        hc = ((x * lax.rsqrt(ms + NORM_EPS)) * ng_ref[...]).astype(bf16)
        h_chunks.append(hc)
        q = jnp.dot(hc, win_ref[:, OFF_Q:OFF_Q + ATTN_WIDTH], preferred_element_type=f32)
        q_s[r0:r0 + ROW_CHUNK, :] = (q + bsc_s[:, OFF_Q:OFF_Q + ATTN_WIDTH]).astype(bf16)
    h = jnp.concatenate(h_chunks, axis=0)
    half_t = seq_tile // 2

    def proj_tile(off):
        cols = slice(off, off + MXU_TILE)
        return jnp.concatenate(
            [jnp.dot(h[i * half_t:(i + 1) * half_t], win_ref[:, cols], preferred_element_type=f32)
             for i in range(2)], axis=0) + bsc_s[:, cols]

    kv = proj_tile(OFF_K)
    half_mask = lax.broadcasted_iota(jnp.int32, (seq_tile, KV_WIDTH), 1) < HEAD_DIM
    for dup_ref, off in ((kd_s, 0), (vd_s, KV_WIDTH)):
        t = kv[:, off:off + KV_WIDTH]
        t_sw = pltpu.roll(t, HEAD_DIM, axis=1)
        dup_ref[0, BLOCK:BLOCK + seq_tile, :] = jnp.where(half_mask, t, t_sw).astype(bf16)
        dup_ref[1, BLOCK:BLOCK + seq_tile, :] = jnp.where(half_mask, t_sw, t).astype(bf16)

    ga_t = [_silu_of_half(proj_tile(OFF_ZA))]

    row = lax.broadcasted_iota(jnp.int32, (BLOCK, BLOCK), 0)
    prev_side = lane > row
    prev_side4 = jnp.concatenate([prev_side] * Q_PER_KV, axis=0)
    has_prev = s_idx > 0
    units = [(j, hh) for j in range(n_sub) for hh in range(N_KV_HEADS)]
    scores = []
    for j, hh in units:
        rows = slice(j * BLOCK, (j + 1) * BLOCK)
        parts = []
        for g in range(Q_PER_KV):
            c = hh * (Q_PER_KV // 2) + g // 2
            qc = q_s[rows, c * BLOCK:(c + 1) * BLOCK]
            keep = low_half if g % 2 == 0 else jnp.logical_not(low_half)
            parts.append(jnp.where(keep, qc, jnp.zeros_like(qc)))
        qst = jnp.concatenate(parts, axis=0)
        kb = kd_s[hh, j * BLOCK:(j + 2) * BLOCK, :]
        sc = lax.dot_general(qst, kb, (((1,), (1,)), ((), ())),
                             preferred_element_type=f32)
        s_prev = sc[:, 0:BLOCK]
        if j == 0:
            s_prev = jnp.where(has_prev, s_prev, NEG_INF)
        scores.append(jnp.where(prev_side4, s_prev, sc[:, BLOCK:2 * BLOCK]))

    def attend(u):
        j, hh = units[u]
        sc = scores[u]
        probs, inv_denoms = [], []
        for g in range(Q_PER_KV):
            sg = sc[g * BLOCK:(g + 1) * BLOCK, :]
            sink = sinks_ref[hh * Q_PER_KV + g]
            m = jnp.maximum(jnp.max(sg, axis=-1, keepdims=True), sink)
            p = jnp.exp(sg - m)
            denom = jnp.sum(p, axis=-1, keepdims=True) + jnp.exp(sink - m)
            inv_denoms.append(1.0 / denom)
            pb = p.astype(bf16)
            zero = jnp.zeros_like(pb)
            probs.append(jnp.concatenate([jnp.where(prev_side, pb, zero),
                                          jnp.where(prev_side, zero, pb)], axis=1))
        probs = jnp.concatenate(probs, axis=0)
        vb = vd_s[hh, j * BLOCK:(j + 2) * BLOCK, :]
        o = jnp.dot(probs, vb, preferred_element_type=f32)
        o = [o[g * BLOCK:(g + 1) * BLOCK, :] * inv_denoms[g] for g in range(Q_PER_KV)]
        return [jnp.where(low_half, o[2 * c2], o[2 * c2 + 1]) for c2 in range(Q_PER_KV // 2)]

    attn_out = []
    n_slots = (IN_WIDTH - OFF_ZA) // MXU_TILE - 1
    units_after = [2, 2, 2, 3, 2, 3, 2]
    assert sum(units_after) == len(units) and len(units_after) == n_slots
    slot = iter(units_after)

    def attend_some():
        for _ in range(next(slot)):
            attn_out.append(attend(len(attn_out)))

    for n in range(1, ATTN_WIDTH // MXU_TILE):
        ga_t.append(_silu_of_half(proj_tile(OFF_ZA + n * MXU_TILE)))
        attend_some()
    for n in range(SGU_WIDTH // MXU_TILE):
        u_s[:, n * MXU_TILE:(n + 1) * MXU_TILE] = _gelu_of_half(proj_tile(OFF_U + n * MXU_TILE))
        attend_some()
    vg = []
    for n in range(SGU_WIDTH // MXU_TILE):
        vg.append(_gelu_of_half(proj_tile(OFF_VS + n * MXU_TILE)))
        attend_some()
    vg = jnp.concatenate(vg, axis=1)
    mu = jnp.mean(vg, axis=-1, keepdims=True)
    vc = vg - mu
    var = jnp.mean(vc * vc, axis=-1, keepdims=True)
    vl_s[...] = ((vc * lax.rsqrt(var + NORM_EPS)) * lng_ref[...] + lnb_ref[...]).astype(bf16)
    for n in range(SGU_WIDTH // MXU_TILE):
        gs_s[:, n * MXU_TILE:(n + 1) * MXU_TILE] = _silu_of_half(proj_tile(OFF_ZS + n * MXU_TILE))
        attend_some()
    assert len(attn_out) == len(units)

    ga = jnp.concatenate(ga_t, axis=1)
    for u, (j, hh) in enumerate(units):
        rows = slice(j * BLOCK, (j + 1) * BLOCK)
        for c2 in range(Q_PER_KV // 2):
            cols = slice((hh * (Q_PER_KV // 2) + c2) * BLOCK, (hh * (Q_PER_KV // 2) + c2 + 1) * BLOCK)
            mix_s[rows, cols] = (attn_out[u][c2] * ga[rows, cols]).astype(bf16)

    for j in range(n_sub):
        rows = slice(j * BLOCK, (j + 1) * BLOCK)
        for p_idx in range(N_SGU_HEADS // 2):
            cols = slice(p_idx * BLOCK, (p_idx + 1) * BLOCK)
            vp = vl_s[rows, cols]
            zero = jnp.zeros_like(vp)
            rhs = jnp.concatenate([jnp.where(low_half, vp, zero),
                                   jnp.where(low_half, zero, vp)], axis=0)
            mixed = jnp.dot(wp_s[p_idx], rhs, preferred_element_type=f32) + sb_s[:, cols]
            mix_s[rows, ATTN_WIDTH + p_idx * BLOCK:ATTN_WIDTH + (p_idx + 1) * BLOCK] = (
                (u_s[rows, cols] * mixed) * gs_s[rows, cols]).astype(bf16)

    y = x_ref[0] + jnp.dot(mix_s[...], wout_ref[...], preferred_element_type=f32) + bout_ref[...]
    ms2 = jnp.mean(y * y, axis=-1, keepdims=True)
    o_ref[0] = (y * lax.rsqrt(ms2 + NORM_EPS)) * fg_ref[...]


def _layer_call(x, sinks, norm_g, w_in, b_in, ln_g, ln_b, sgu_w, sgu_b, w_out, b_out, final_g,
                *, seq_tile=SEQ_TILE):
    batch, seq, d_model = x.shape
    assert d_model == D_MODEL and seq % seq_tile == 0 and seq_tile % (2 * BLOCK) == 0
    f32, bf16 = jnp.float32, jnp.bfloat16

    def full(shape):
        return pl.BlockSpec(shape, lambda b, s: (0,) * len(shape))

    tile_spec = pl.BlockSpec((1, seq_tile, D_MODEL), lambda b, s: (b, s, 0))
    return pl.pallas_call(
        functools.partial(_layer_kernel, seq_tile=seq_tile),
        grid=(batch, seq // seq_tile),
        in_specs=[
            pl.BlockSpec(memory_space=pltpu.SMEM),
            tile_spec,
            full((1, D_MODEL)),
            pl.BlockSpec(memory_space=pl.ANY),
            full((1, IN_WIDTH)),
            full((1, SGU_WIDTH)),
            full((1, SGU_WIDTH)),
            full((N_SGU_HEADS, BLOCK, BLOCK)),
            full((N_SGU_HEADS, BLOCK)),
            pl.BlockSpec(memory_space=pl.ANY),
            full((1, D_MODEL)),
            full((1, D_MODEL)),
        ],
        out_specs=tile_spec,
        out_shape=jax.ShapeDtypeStruct(x.shape, x.dtype),
        scratch_shapes=[
            pltpu.VMEM((D_MODEL, IN_WIDTH), bf16),
            pltpu.VMEM((D_MODEL, D_MODEL), bf16),
            pltpu.VMEM((1, IN_WIDTH), f32),
            pltpu.VMEM((W_SLOTS, W_CHUNK, IN_WIDTH), f32),
            pltpu.VMEM((W_SLOTS, W_CHUNK, D_MODEL), f32),
            pltpu.SemaphoreType.DMA((W_SLOTS,)),
            pltpu.SemaphoreType.DMA((W_SLOTS,)),
            pltpu.VMEM((seq_tile, ATTN_WIDTH), bf16),
            pltpu.VMEM((N_KV_HEADS, BLOCK + seq_tile, KV_WIDTH), bf16),
            pltpu.VMEM((N_KV_HEADS, BLOCK + seq_tile, KV_WIDTH), bf16),
            pltpu.VMEM((seq_tile, SGU_WIDTH), f32),
            pltpu.VMEM((seq_tile, SGU_WIDTH), bf16),
            pltpu.VMEM((seq_tile, SGU_WIDTH), f32),
            pltpu.VMEM((seq_tile, D_MODEL), bf16),
            pltpu.VMEM((N_SGU_HEADS // 2, BLOCK, 2 * BLOCK), bf16),
            pltpu.VMEM((BLOCK, SGU_WIDTH), f32),
        ],
        compiler_params=pltpu.CompilerParams(
            dimension_semantics=("arbitrary", "arbitrary"),
            vmem_limit_bytes=V7X_VMEM_LIMIT_BYTES),
        name="hybrid_layer",
    )(sinks, x, norm_g, w_in, b_in, ln_g, ln_b, sgu_w, sgu_b, w_out, b_out, final_g)


def kernel(x, norm_g, w_in, b_in, attn_sinks, sgu_ln_g, sgu_ln_b, sgu_w, sgu_b, w_out, b_out, final_norm_g):
    depth = norm_g.shape[0]
    for l in range(depth):
        last = l == depth - 1
        assert last, "the fused call applies the final norm; only depth 1 is supported"
        x = _layer_call(
            x, attn_sinks[l], norm_g[l][None, :], w_in[l], b_in[l][None, :],
            sgu_ln_g[l][None, :], sgu_ln_b[l][None, :], sgu_w[l], sgu_b[l],
            w_out[l], b_out[l][None, :], final_norm_g[None, :])
    return x
```

```python
import collections
import functools

import jax
import jax.numpy as jnp
from jax import lax
from jax.experimental import pallas as pl
from jax.experimental.pallas import tpu as pltpu

D_MODEL = 1024
HEAD_DIM = 64
ATTN_WIDTH = 512
KV_WIDTH = 128
SGU_WIDTH = 512
N_KV_HEADS = 2
Q_PER_KV = 4
N_SGU_HEADS = 8
BLOCK = 128
NORM_EPS = 1e-5
NEG_INF = -1e30
ATTN_SCALE = HEAD_DIM ** -0.5

OFF_Q = 0
OFF_K = OFF_Q + ATTN_WIDTH
OFF_V = OFF_K + KV_WIDTH
OFF_ZA = OFF_V + KV_WIDTH
OFF_U = OFF_ZA + ATTN_WIDTH
OFF_VS = OFF_U + SGU_WIDTH
OFF_ZS = OFF_VS + SGU_WIDTH
IN_WIDTH = OFF_ZS + SGU_WIDTH

SEQ_TILE = 1024
MXU_TILE = 256
ROW_CHUNK = 256
W_CHUNK = 64
W_SLOTS = 8
V7X_VMEM_LIMIT_BYTES = 56 * 1024 * 1024

_SQRT_TWO = 1.4142135623730951

_WeightCopy = collections.namedtuple("_WeightCopy", "hbm stage sem dst col_scale")


def _silu_of_half(hz):
    return hz * (1.0 + jnp.tanh(hz))


def _gelu_of_half(hz):
    return hz * (1.0 + lax.erf(hz * _SQRT_TWO))


def _layer_kernel(sinks_ref, x_ref, ng_ref, win_hbm, bin_ref, lng_ref, lnb_ref,
                  sw_ref, sb_ref, wout_hbm, bout_ref, fg_ref, o_ref,
                  win_ref, wout_ref, bsc_s, stage_in, stage_out, sem_in, sem_out,
                  q_s, kd_s, vd_s, u_s, vl_s, gs_s, mix_s, wp_s, sb_s, *, seq_tile):
    f32, bf16 = jnp.float32, jnp.bfloat16
    n_sub = seq_tile // BLOCK
    b_idx = pl.program_id(0)
    s_idx = pl.program_id(1)

    lane = lax.broadcasted_iota(jnp.int32, (BLOCK, BLOCK), 1)
    low_half = lane < HEAD_DIM

    @pl.when((b_idx == 0) & (s_idx == 0))
    def _():
        row = lax.broadcasted_iota(jnp.int32, (BLOCK, BLOCK), 0)
        for p in range(N_SGU_HEADS // 2):
            for half in range(2):
                w = jnp.where(row >= lane, sw_ref[2 * p + half], 0.0)
                wp_s[p, :, half * BLOCK:(half + 1) * BLOCK] = w.astype(bf16)
        sbt = sb_ref[...].T
        for hh in range(N_SGU_HEADS):
            sb_s[:, hh * HEAD_DIM:(hh + 1) * HEAD_DIM] = jnp.broadcast_to(
                sbt[:, hh:hh + 1], (BLOCK, HEAD_DIM))
        col = lax.broadcasted_iota(jnp.int32, (1, IN_WIDTH), 1)
        col_scale = jnp.where(col < OFF_K, ATTN_SCALE, jnp.where(col < OFF_ZA, 1.0, 0.5)).astype(f32)
        bsc_s[...] = bin_ref[...] * col_scale
        jobs = (_WeightCopy(win_hbm, stage_in, sem_in, win_ref, col_scale),
                _WeightCopy(wout_hbm, stage_out, sem_out, wout_ref, None))
        n_chunks = D_MODEL // W_CHUNK

        def chunk_copy(job, c):
            return pltpu.make_async_copy(job.hbm.at[pl.ds(c * W_CHUNK, W_CHUNK), :],
                                         job.stage.at[c % W_SLOTS], job.sem.at[c % W_SLOTS])

        for job in jobs:
            for c in range(W_SLOTS):
                chunk_copy(job, c).start(priority=c % 2)
        for job in jobs:
            for c in range(n_chunks):
                chunk_copy(job, c).wait()
                w = job.stage[c % W_SLOTS]
                if job.col_scale is not None:
                    w = w * job.col_scale
                job.dst[c * W_CHUNK:(c + 1) * W_CHUNK, :] = w.astype(bf16)
                if c + W_SLOTS < n_chunks:
                    chunk_copy(job, c + W_SLOTS).start(priority=c % 2)

    @pl.when(s_idx == 0)
    def _():
        for hh in range(N_KV_HEADS):
            kd_s[hh, 0:BLOCK, :] = jnp.zeros((BLOCK, KV_WIDTH), bf16)
            vd_s[hh, 0:BLOCK, :] = jnp.zeros((BLOCK, KV_WIDTH), bf16)

    @pl.when(s_idx > 0)
    def _():
        for hh in range(N_KV_HEADS):
            kd_s[hh, 0:BLOCK, :] = kd_s[hh, seq_tile:seq_tile + BLOCK, :]
            vd_s[hh, 0:BLOCK, :] = vd_s[hh, seq_tile:seq_tile + BLOCK, :]

    h_chunks = []
    for r0 in range(0, seq_tile, ROW_CHUNK):
        x = x_ref[0, r0:r0 + ROW_CHUNK, :]
        ms = jnp.mean(x * x, axis=-1, keepdims=True)
        hc = ((x * lax.rsqrt(ms + NORM_EPS)) * ng_ref[...]).astype(bf16)
        h_chunks.append(hc)
        q = jnp.dot(hc, win_ref[:, OFF_Q:OFF_Q + ATTN_WIDTH], preferred_element_type=f32)
        q_s[r0:r0 + ROW_CHUNK, :] = (q + bsc_s[:, OFF_Q:OFF_Q + ATTN_WIDTH]).astype(bf16)
    h = jnp.concatenate(h_chunks, axis=0)
    half_t = seq_tile // 2

    def proj_tile(off):
        cols = slice(off, off + MXU_TILE)
        return jnp.concatenate(
            [jnp.dot(h[i * half_t:(i + 1) * half_t], win_ref[:, cols], preferred_element_type=f32)
             for i in range(2)], axis=0) + bsc_s[:, cols]

    kv = proj_tile(OFF_K)
    half_mask = lax.broadcasted_iota(jnp.int32, (seq_tile, KV_WIDTH), 1) < HEAD_DIM
    for dup_ref, off in ((kd_s, 0), (vd_s, KV_WIDTH)):
        t = kv[:, off:off + KV_WIDTH]
        t_sw = pltpu.roll(t, HEAD_DIM, axis=1)
        dup_ref[0, BLOCK:BLOCK + seq_tile, :] = jnp.where(half_mask, t, t_sw).astype(bf16)
        dup_ref[1, BLOCK:BLOCK + seq_tile, :] = jnp.where(half_mask, t_sw, t).astype(bf16)

    ga_t = [_silu_of_half(proj_tile(OFF_ZA))]

    row = lax.broadcasted_iota(jnp.int32, (BLOCK, BLOCK), 0)
    prev_side = lane > row
    prev_side4 = jnp.concatenate([prev_side] * Q_PER_KV, axis=0)
    has_prev = s_idx > 0
    units = [(j, hh) for j in range(n_sub) for hh in range(N_KV_HEADS)]
    scores = []
    for j, hh in units:
        rows = slice(j * BLOCK, (j + 1) * BLOCK)
        parts = []
        for g in range(Q_PER_KV):
            c = hh * (Q_PER_KV // 2) + g // 2
            qc = q_s[rows, c * BLOCK:(c + 1) * BLOCK]
            keep = low_half if g % 2 == 0 else jnp.logical_not(low_half)
            parts.append(jnp.where(keep, qc, jnp.zeros_like(qc)))
        qst = jnp.concatenate(parts, axis=0)
        kb = kd_s[hh, j * BLOCK:(j + 2) * BLOCK, :]
        sc = lax.dot_general(qst, kb, (((1,), (1,)), ((), ())),
                             preferred_element_type=f32)
        s_prev = sc[:, 0:BLOCK]
        if j == 0:
            s_prev = jnp.where(has_prev, s_prev, NEG_INF)
        scores.append(jnp.where(prev_side4, s_prev, sc[:, BLOCK:2 * BLOCK]))

    def attend(u):
        j, hh = units[u]
        sc = scores[u]
        probs, inv_denoms = [], []
        for g in range(Q_PER_KV):
            sg = sc[g * BLOCK:(g + 1) * BLOCK, :]
            sink = sinks_ref[hh * Q_PER_KV + g]
            m = jnp.maximum(jnp.max(sg, axis=-1, keepdims=True), sink)
            p = jnp.exp(sg - m)
            denom = jnp.sum(p, axis=-1, keepdims=True) + jnp.exp(sink - m)
            inv_denoms.append(1.0 / denom)
            pb = p.astype(bf16)
            zero = jnp.zeros_like(pb)
            probs.append(jnp.concatenate([jnp.where(prev_side, pb, zero),
                                          jnp.where(prev_side, zero, pb)], axis=1))
        probs = jnp.concatenate(probs, axis=0)
        vb = vd_s[hh, j * BLOCK:(j + 2) * BLOCK, :]
        o = jnp.dot(probs, vb, preferred_element_type=f32)
        o = [o[g * BLOCK:(g + 1) * BLOCK, :] * inv_denoms[g] for g in range(Q_PER_KV)]
        return [jnp.where(low_half, o[2 * c2], o[2 * c2 + 1]) for c2 in range(Q_PER_KV // 2)]

    attn_out = []
    n_slots = (IN_WIDTH - OFF_ZA) // MXU_TILE - 1
    units_after = [2, 2, 3, 2, 2, 2, 3]
    assert sum(units_after) == len(units) and len(units_after) == n_slots
    slot = iter(units_after)

    def attend_some():
        for _ in range(next(slot)):
            attn_out.append(attend(len(attn_out)))

    for n in range(1, ATTN_WIDTH // MXU_TILE):
        ga_t.append(_silu_of_half(proj_tile(OFF_ZA + n * MXU_TILE)))
        attend_some()
    for n in range(SGU_WIDTH // MXU_TILE):
        u_s[:, n * MXU_TILE:(n + 1) * MXU_TILE] = _gelu_of_half(proj_tile(OFF_U + n * MXU_TILE))
        attend_some()
    vg = []
    for n in range(SGU_WIDTH // MXU_TILE):
        vg.append(_gelu_of_half(proj_tile(OFF_VS + n * MXU_TILE)))
        attend_some()
    vg = jnp.concatenate(vg, axis=1)
    mu = jnp.mean(vg, axis=-1, keepdims=True)
    vc = vg - mu
    var = jnp.mean(vc * vc, axis=-1, keepdims=True)
    vl_s[...] = ((vc * lax.rsqrt(var + NORM_EPS)) * lng_ref[...] + lnb_ref[...]).astype(bf16)
    for n in range(SGU_WIDTH // MXU_TILE):
        gs_s[:, n * MXU_TILE:(n + 1) * MXU_TILE] = _silu_of_half(proj_tile(OFF_ZS + n * MXU_TILE))
        attend_some()
    assert len(attn_out) == len(units)

    ga = jnp.concatenate(ga_t, axis=1)
    for u, (j, hh) in enumerate(units):
        rows = slice(j * BLOCK, (j + 1) * BLOCK)
        for c2 in range(Q_PER_KV // 2):
            cols = slice((hh * (Q_PER_KV // 2) + c2) * BLOCK, (hh * (Q_PER_KV // 2) + c2 + 1) * BLOCK)
            mix_s[rows, cols] = (attn_out[u][c2] * ga[rows, cols]).astype(bf16)

    for j in range(n_sub):
        rows = slice(j * BLOCK, (j + 1) * BLOCK)
        for p_idx in range(N_SGU_HEADS // 2):
            cols = slice(p_idx * BLOCK, (p_idx + 1) * BLOCK)
            vp = vl_s[rows, cols]
            zero = jnp.zeros_like(vp)
            rhs = jnp.concatenate([jnp.where(low_half, vp, zero),
                                   jnp.where(low_half, zero, vp)], axis=0)
            mixed = jnp.dot(wp_s[p_idx], rhs, preferred_element_type=f32) + sb_s[:, cols]
            mix_s[rows, ATTN_WIDTH + p_idx * BLOCK:ATTN_WIDTH + (p_idx + 1) * BLOCK] = (
                (u_s[rows, cols] * mixed) * gs_s[rows, cols]).astype(bf16)

    y = x_ref[0] + jnp.dot(mix_s[...], wout_ref[...], preferred_element_type=f32) + bout_ref[...]
    ms2 = jnp.mean(y * y, axis=-1, keepdims=True)
    o_ref[0] = (y * lax.rsqrt(ms2 + NORM_EPS)) * fg_ref[...]


def _layer_call(x, sinks, norm_g, w_in, b_in, ln_g, ln_b, sgu_w, sgu_b, w_out, b_out, final_g,
                *, seq_tile=SEQ_TILE):
    batch, seq, d_model = x.shape
    assert d_model == D_MODEL and seq % seq_tile == 0 and seq_tile % (2 * BLOCK) == 0
    f32, bf16 = jnp.float32, jnp.bfloat16

    def full(shape):
        return pl.BlockSpec(shape, lambda b, s: (0,) * len(shape))

    tile_spec = pl.BlockSpec((1, seq_tile, D_MODEL), lambda b, s: (b, s, 0))
    return pl.pallas_call(
        functools.partial(_layer_kernel, seq_tile=seq_tile),
        grid=(batch, seq // seq_tile),
        in_specs=[
            pl.BlockSpec(memory_space=pltpu.SMEM),
            tile_spec,
            full((1, D_MODEL)),
            pl.BlockSpec(memory_space=pl.ANY),
            full((1, IN_WIDTH)),
            full((1, SGU_WIDTH)),
            full((1, SGU_WIDTH)),
            full((N_SGU_HEADS, BLOCK, BLOCK)),
            full((N_SGU_HEADS, BLOCK)),
            pl.BlockSpec(memory_space=pl.ANY),
            full((1, D_MODEL)),
            full((1, D_MODEL)),
        ],
        out_specs=tile_spec,
        out_shape=jax.ShapeDtypeStruct(x.shape, x.dtype),
        scratch_shapes=[
            pltpu.VMEM((D_MODEL, IN_WIDTH), bf16),
            pltpu.VMEM((D_MODEL, D_MODEL), bf16),
            pltpu.VMEM((1, IN_WIDTH), f32),
            pltpu.VMEM((W_SLOTS, W_CHUNK, IN_WIDTH), f32),
            pltpu.VMEM((W_SLOTS, W_CHUNK, D_MODEL), f32),
            pltpu.SemaphoreType.DMA((W_SLOTS,)),
            pltpu.SemaphoreType.DMA((W_SLOTS,)),
            pltpu.VMEM((seq_tile, ATTN_WIDTH), bf16),
            pltpu.VMEM((N_KV_HEADS, BLOCK + seq_tile, KV_WIDTH), bf16),
            pltpu.VMEM((N_KV_HEADS, BLOCK + seq_tile, KV_WIDTH), bf16),
            pltpu.VMEM((seq_tile, SGU_WIDTH), f32),
            pltpu.VMEM((seq_tile, SGU_WIDTH), bf16),
            pltpu.VMEM((seq_tile, SGU_WIDTH), f32),
            pltpu.VMEM((seq_tile, D_MODEL), bf16),
            pltpu.VMEM((N_SGU_HEADS // 2, BLOCK, 2 * BLOCK), bf16),
            pltpu.VMEM((BLOCK, SGU_WIDTH), f32),
        ],
        compiler_params=pltpu.CompilerParams(
            dimension_semantics=("arbitrary", "arbitrary"),
            vmem_limit_bytes=V7X_VMEM_LIMIT_BYTES),
        name="hybrid_layer",
    )(sinks, x, norm_g, w_in, b_in, ln_g, ln_b, sgu_w, sgu_b, w_out, b_out, final_g)


def kernel(x, norm_g, w_in, b_in, attn_sinks, sgu_ln_g, sgu_ln_b, sgu_w, sgu_b, w_out, b_out, final_norm_g):
    depth = norm_g.shape[0]
    for l in range(depth):
        last = l == depth - 1
        assert last, "the fused call applies the final norm; only depth 1 is supported"
        x = _layer_call(
            x, attn_sinks[l], norm_g[l][None, :], w_in[l], b_in[l][None, :],
            sgu_ln_g[l][None, :], sgu_ln_b[l][None, :], sgu_w[l], sgu_b[l],
            w_out[l], b_out[l][None, :], final_norm_g[None, :])
    return x
```

```python
import collections
import functools

import jax
import jax.numpy as jnp
from jax import lax
from jax.experimental import pallas as pl
from jax.experimental.pallas import tpu as pltpu

D_MODEL = 1024
HEAD_DIM = 64
ATTN_WIDTH = 512
KV_WIDTH = 128
SGU_WIDTH = 512
N_KV_HEADS = 2
Q_PER_KV = 4
N_SGU_HEADS = 8
BLOCK = 128
NORM_EPS = 1e-5
NEG_INF = -1e30
ATTN_SCALE = HEAD_DIM ** -0.5

OFF_Q = 0
OFF_K = OFF_Q + ATTN_WIDTH
OFF_V = OFF_K + KV_WIDTH
OFF_ZA = OFF_V + KV_WIDTH
OFF_U = OFF_ZA + ATTN_WIDTH
OFF_VS = OFF_U + SGU_WIDTH
OFF_ZS = OFF_VS + SGU_WIDTH
IN_WIDTH = OFF_ZS + SGU_WIDTH

SEQ_TILE = 1024
MXU_TILE = 256
ROW_CHUNK = 256
W_CHUNK = 64
W_SLOTS = 8
V7X_VMEM_LIMIT_BYTES = 56 * 1024 * 1024

_SQRT_TWO = 1.4142135623730951

_WeightCopy = collections.namedtuple("_WeightCopy", "hbm stage sem dst col_scale")


def _silu_of_half(hz):
    return hz * (1.0 + jnp.tanh(hz))


def _gelu_of_half(hz):
    return hz * (1.0 + lax.erf(hz * _SQRT_TWO))


def _layer_kernel(sinks_ref, x_ref, ng_ref, win_hbm, bin_ref, lng_ref, lnb_ref,
                  sw_ref, sb_ref, wout_hbm, bout_ref, fg_ref, o_ref,
                  win_ref, wout_ref, bsc_s, stage_in, stage_out, sem_in, sem_out,
                  q_s, kd_s, vd_s, u_s, vl_s, gs_s, mix_s, wp_s, sb_s, *, seq_tile):
    f32, bf16 = jnp.float32, jnp.bfloat16
    n_sub = seq_tile // BLOCK
    b_idx = pl.program_id(0)
    s_idx = pl.program_id(1)

    lane = lax.broadcasted_iota(jnp.int32, (BLOCK, BLOCK), 1)
    low_half = lane < HEAD_DIM

    @pl.when((b_idx == 0) & (s_idx == 0))
    def _():
        col = lax.broadcasted_iota(jnp.int32, (1, IN_WIDTH), 1)
        col_scale = jnp.where(col < OFF_K, ATTN_SCALE, jnp.where(col < OFF_ZA, 1.0, 0.5)).astype(f32)
        bsc_s[...] = bin_ref[...] * col_scale
        jobs = (_WeightCopy(win_hbm, stage_in, sem_in, win_ref, col_scale),
                _WeightCopy(wout_hbm, stage_out, sem_out, wout_ref, None))
        n_chunks = D_MODEL // W_CHUNK

        def chunk_copy(job, c):
            return pltpu.make_async_copy(job.hbm.at[pl.ds(c * W_CHUNK, W_CHUNK), :],
                                         job.stage.at[c % W_SLOTS], job.sem.at[c % W_SLOTS])

        for job in jobs:
            for c in range(W_SLOTS):
                chunk_copy(job, c).start(priority=c % 2)
        row = lax.broadcasted_iota(jnp.int32, (BLOCK, BLOCK), 0)
        for p in range(N_SGU_HEADS // 2):
            for half in range(2):
                w = jnp.where(row >= lane, sw_ref[2 * p + half], 0.0)
                wp_s[p, :, half * BLOCK:(half + 1) * BLOCK] = w.astype(bf16)
        sbt = sb_ref[...].T
        for hh in range(N_SGU_HEADS):
            sb_s[:, hh * HEAD_DIM:(hh + 1) * HEAD_DIM] = jnp.broadcast_to(
                sbt[:, hh:hh + 1], (BLOCK, HEAD_DIM))
        for job in jobs:
            for c in range(n_chunks):
                chunk_copy(job, c).wait()
                w = job.stage[c % W_SLOTS]
                if job.col_scale is not None:
                    w = w * job.col_scale
                job.dst[c * W_CHUNK:(c + 1) * W_CHUNK, :] = w.astype(bf16)
                if c + W_SLOTS < n_chunks:
                    chunk_copy(job, c + W_SLOTS).start(priority=c % 2)

    @pl.when(s_idx == 0)
    def _():
        for hh in range(N_KV_HEADS):
            kd_s[hh, 0:BLOCK, :] = jnp.zeros((BLOCK, KV_WIDTH), bf16)
            vd_s[hh, 0:BLOCK, :] = jnp.zeros((BLOCK, KV_WIDTH), bf16)

    @pl.when(s_idx > 0)
    def _():
        for hh in range(N_KV_HEADS):
            kd_s[hh, 0:BLOCK, :] = kd_s[hh, seq_tile:seq_tile + BLOCK, :]
            vd_s[hh, 0:BLOCK, :] = vd_s[hh, seq_tile:seq_tile + BLOCK, :]

    h_chunks = []
    for r0 in range(0, seq_tile, ROW_CHUNK):
        x = x_ref[0, r0:r0 + ROW_CHUNK, :]
        ms = jnp.mean(x * x, axis=-1, keepdims=True)
        hc = ((x * lax.rsqrt(ms + NORM_EPS)) * ng_ref[...]).astype(bf16)
        h_chunks.append(hc)
        q = jnp.dot(hc, win_ref[:, OFF_Q:OFF_Q + ATTN_WIDTH], preferred_element_type=f32)
        q_s[r0:r0 + ROW_CHUNK, :] = (q + bsc_s[:, OFF_Q:OFF_Q + ATTN_WIDTH]).astype(bf16)
    h = jnp.concatenate(h_chunks, axis=0)
    half_t = seq_tile // 2

    def proj_tile(off):
        cols = slice(off, off + MXU_TILE)
        return jnp.concatenate(
            [jnp.dot(h[i * half_t:(i + 1) * half_t], win_ref[:, cols], preferred_element_type=f32)
             for i in range(2)], axis=0) + bsc_s[:, cols]

    kv = proj_tile(OFF_K)
    half_mask = lax.broadcasted_iota(jnp.int32, (seq_tile, KV_WIDTH), 1) < HEAD_DIM
    for dup_ref, off in ((kd_s, 0), (vd_s, KV_WIDTH)):
        t = kv[:, off:off + KV_WIDTH]
        t_sw = pltpu.roll(t, HEAD_DIM, axis=1)
        dup_ref[0, BLOCK:BLOCK + seq_tile, :] = jnp.where(half_mask, t, t_sw).astype(bf16)
        dup_ref[1, BLOCK:BLOCK + seq_tile, :] = jnp.where(half_mask, t_sw, t).astype(bf16)

    ga_t = [_silu_of_half(proj_tile(OFF_ZA))]

    row = lax.broadcasted_iota(jnp.int32, (BLOCK, BLOCK), 0)
    prev_side = lane > row
    prev_side4 = jnp.concatenate([prev_side] * Q_PER_KV, axis=0)
    has_prev = s_idx > 0
    units = [(j, hh) for j in range(n_sub) for hh in range(N_KV_HEADS)]
    scores = []
    for j, hh in units:
        rows = slice(j * BLOCK, (j + 1) * BLOCK)
        parts = []
        for g in range(Q_PER_KV):
            c = hh * (Q_PER_KV // 2) + g // 2
            qc = q_s[rows, c * BLOCK:(c + 1) * BLOCK]
            keep = low_half if g % 2 == 0 else jnp.logical_not(low_half)
            parts.append(jnp.where(keep, qc, jnp.zeros_like(qc)))
        qst = jnp.concatenate(parts, axis=0)
        kb = kd_s[hh, j * BLOCK:(j + 2) * BLOCK, :]
        sc = lax.dot_general(qst, kb, (((1,), (1,)), ((), ())),
                             preferred_element_type=f32)
        s_prev = sc[:, 0:BLOCK]
        if j == 0:
            s_prev = jnp.where(has_prev, s_prev, NEG_INF)
        scores.append(jnp.where(prev_side4, s_prev, sc[:, BLOCK:2 * BLOCK]))

    def attend(u):
        j, hh = units[u]
        sc = scores[u]
        probs, inv_denoms = [], []
        for g in range(Q_PER_KV):
            sg = sc[g * BLOCK:(g + 1) * BLOCK, :]
            sink = sinks_ref[hh * Q_PER_KV + g]
            m = jnp.maximum(jnp.max(sg, axis=-1, keepdims=True), sink)
            p = jnp.exp(sg - m)
            denom = jnp.sum(p, axis=-1, keepdims=True) + jnp.exp(sink - m)
            inv_denoms.append(1.0 / denom)
            pb = p.astype(bf16)
            zero = jnp.zeros_like(pb)
            probs.append(jnp.concatenate([jnp.where(prev_side, pb, zero),
                                          jnp.where(prev_side, zero, pb)], axis=1))
        probs = jnp.concatenate(probs, axis=0)
        vb = vd_s[hh, j * BLOCK:(j + 2) * BLOCK, :]
        o = jnp.dot(probs, vb, preferred_element_type=f32)
        o = [o[g * BLOCK:(g + 1) * BLOCK, :] * inv_denoms[g] for g in range(Q_PER_KV)]
        return [jnp.where(low_half, o[2 * c2], o[2 * c2 + 1]) for c2 in range(Q_PER_KV // 2)]

    attn_out = []
    n_slots = (IN_WIDTH - OFF_ZA) // MXU_TILE - 1
    units_after = [len(units) * (i + 1) // n_slots - len(units) * i // n_slots for i in range(n_slots)]
    slot = iter(units_after)

    def attend_some():
        for _ in range(next(slot)):
            attn_out.append(attend(len(attn_out)))

    for n in range(1, ATTN_WIDTH // MXU_TILE):
        ga_t.append(_silu_of_half(proj_tile(OFF_ZA + n * MXU_TILE)))
        attend_some()
    for n in range(SGU_WIDTH // MXU_TILE):
        u_s[:, n * MXU_TILE:(n + 1) * MXU_TILE] = _gelu_of_half(proj_tile(OFF_U + n * MXU_TILE))
        attend_some()
    vg = []
    for n in range(SGU_WIDTH // MXU_TILE):
        vg.append(_gelu_of_half(proj_tile(OFF_VS + n * MXU_TILE)))
        attend_some()
    vg = jnp.concatenate(vg, axis=1)
    mu = jnp.mean(vg, axis=-1, keepdims=True)
    vc = vg - mu
    var = jnp.mean(vc * vc, axis=-1, keepdims=True)
    vl_s[...] = ((vc * lax.rsqrt(var + NORM_EPS)) * lng_ref[...] + lnb_ref[...]).astype(bf16)
    for n in range(SGU_WIDTH // MXU_TILE):
        gs_s[:, n * MXU_TILE:(n + 1) * MXU_TILE] = _silu_of_half(proj_tile(OFF_ZS + n * MXU_TILE))
        attend_some()
    assert len(attn_out) == len(units)

    ga = jnp.concatenate(ga_t, axis=1)
    for u, (j, hh) in enumerate(units):
        rows = slice(j * BLOCK, (j + 1) * BLOCK)
        for c2 in range(Q_PER_KV // 2):
            cols = slice((hh * (Q_PER_KV // 2) + c2) * BLOCK, (hh * (Q_PER_KV // 2) + c2 + 1) * BLOCK)
            mix_s[rows, cols] = (attn_out[u][c2] * ga[rows, cols]).astype(bf16)

    for j in range(n_sub):
        rows = slice(j * BLOCK, (j + 1) * BLOCK)
        for p_idx in range(N_SGU_HEADS // 2):
            cols = slice(p_idx * BLOCK, (p_idx + 1) * BLOCK)
            vp = vl_s[rows, cols]
            zero = jnp.zeros_like(vp)
            rhs = jnp.concatenate([jnp.where(low_half, vp, zero),
                                   jnp.where(low_half, zero, vp)], axis=0)
            mixed = jnp.dot(wp_s[p_idx], rhs, preferred_element_type=f32) + sb_s[:, cols]
            mix_s[rows, ATTN_WIDTH + p_idx * BLOCK:ATTN_WIDTH + (p_idx + 1) * BLOCK] = (
                (u_s[rows, cols] * mixed) * gs_s[rows, cols]).astype(bf16)

    y = x_ref[0] + jnp.dot(mix_s[...], wout_ref[...], preferred_element_type=f32) + bout_ref[...]
    ms2 = jnp.mean(y * y, axis=-1, keepdims=True)
    o_ref[0] = (y * lax.rsqrt(ms2 + NORM_EPS)) * fg_ref[...]


def _layer_call(x, sinks, norm_g, w_in, b_in, ln_g, ln_b, sgu_w, sgu_b, w_out, b_out, final_g,
                *, seq_tile=SEQ_TILE):
    batch, seq, d_model = x.shape
    assert d_model == D_MODEL and seq % seq_tile == 0 and seq_tile % (2 * BLOCK) == 0
    f32, bf16 = jnp.float32, jnp.bfloat16

    def full(shape):
        return pl.BlockSpec(shape, lambda b, s: (0,) * len(shape))

    tile_spec = pl.BlockSpec((1, seq_tile, D_MODEL), lambda b, s: (b, s, 0))
    return pl.pallas_call(
        functools.partial(_layer_kernel, seq_tile=seq_tile),
        grid=(batch, seq // seq_tile),
        in_specs=[
            pl.BlockSpec(memory_space=pltpu.SMEM),
            tile_spec,
            full((1, D_MODEL)),
            pl.BlockSpec(memory_space=pl.ANY),
            full((1, IN_WIDTH)),
            full((1, SGU_WIDTH)),
            full((1, SGU_WIDTH)),
            full((N_SGU_HEADS, BLOCK, BLOCK)),
            full((N_SGU_HEADS, BLOCK)),
            pl.BlockSpec(memory_space=pl.ANY),
            full((1, D_MODEL)),
            full((1, D_MODEL)),
        ],
        out_specs=tile_spec,
        out_shape=jax.ShapeDtypeStruct(x.shape, x.dtype),
        scratch_shapes=[
            pltpu.VMEM((D_MODEL, IN_WIDTH), bf16),
            pltpu.VMEM((D_MODEL, D_MODEL), bf16),
            pltpu.VMEM((1, IN_WIDTH), f32),
            pltpu.VMEM((W_SLOTS, W_CHUNK, IN_WIDTH), f32),
            pltpu.VMEM((W_SLOTS, W_CHUNK, D_MODEL), f32),
            pltpu.SemaphoreType.DMA((W_SLOTS,)),
            pltpu.SemaphoreType.DMA((W_SLOTS,)),
            pltpu.VMEM((seq_tile, ATTN_WIDTH), bf16),
            pltpu.VMEM((N_KV_HEADS, BLOCK + seq_tile, KV_WIDTH), bf16),
            pltpu.VMEM((N_KV_HEADS, BLOCK + seq_tile, KV_WIDTH), bf16),
            pltpu.VMEM((seq_tile, SGU_WIDTH), f32),
            pltpu.VMEM((seq_tile, SGU_WIDTH), bf16),
            pltpu.VMEM((seq_tile, SGU_WIDTH), f32),
            pltpu.VMEM((seq_tile, D_MODEL), bf16),
            pltpu.VMEM((N_SGU_HEADS // 2, BLOCK, 2 * BLOCK), bf16),
            pltpu.VMEM((BLOCK, SGU_WIDTH), f32),
        ],
        compiler_params=pltpu.CompilerParams(
            dimension_semantics=("arbitrary", "arbitrary"),
            vmem_limit_bytes=V7X_VMEM_LIMIT_BYTES),
        name="hybrid_layer",
    )(sinks, x, norm_g, w_in, b_in, ln_g, ln_b, sgu_w, sgu_b, w_out, b_out, final_g)


def kernel(x, norm_g, w_in, b_in, attn_sinks, sgu_ln_g, sgu_ln_b, sgu_w, sgu_b, w_out, b_out, final_norm_g):
    depth = norm_g.shape[0]
    for l in range(depth):
        last = l == depth - 1
        assert last, "the fused call applies the final norm; only depth 1 is supported"
        x = _layer_call(
            x, attn_sinks[l], norm_g[l][None, :], w_in[l], b_in[l][None, :],
            sgu_ln_g[l][None, :], sgu_ln_b[l][None, :], sgu_w[l], sgu_b[l],
            w_out[l], b_out[l][None, :], final_norm_g[None, :])
    return x
```
